```python
import math
import jax
import jax.numpy as jnp
from jax import lax
import numpy as np

D_MODEL = 2048
BATCH = 4
SEQ = 2048
DEPTH = 4
DEC_BATCH = 128
DEC_SEQ = 4
PAST_LEN = 16384
PAGE_SIZE = 128

F32 = jnp.float32
N_META = 16
N_MIXERS = 3
EXPAND = 2
D_INNER = EXPAND * D_MODEL
CHUNK = 128
EPS = 1e-6

SSD_HEAD_DIM = 64
SSD_HEADS = D_INNER // SSD_HEAD_DIM
SSD_GROUPS = 8
SSD_HPG = SSD_HEADS // SSD_GROUPS
SSD_STATE = 128
SSD_CONV = 4
SSD_CONV_DIM = D_INNER + 2 * SSD_GROUPS * SSD_STATE
SSD_IN = D_INNER + SSD_CONV_DIM + SSD_HEADS

RWKV_HEAD_DIM = 64
RWKV_HEADS = D_INNER // RWKV_HEAD_DIM
RWKV_LORA = 96
RWKV_LN_EPS = 1e-5 * RWKV_HEAD_DIM

RET_HEADS = 8
RET_QK_DIM = D_MODEL // RET_HEADS
RET_V_DIM = D_INNER // RET_HEADS
RET_IN = 2 * D_MODEL + 2 * D_INNER
RET_THETA_BASE = 10000.0

N_SSD_LAYERS = len(range(0, DEPTH, N_MIXERS))
N_RWKV_LAYERS = len(range(1, DEPTH, N_MIXERS))
N_RET_LAYERS = len(range(2, DEPTH, N_MIXERS))

kernel_name = 'meta_ssd_rwkv7_retention_hybrid_step'


def rms_norm(x, g):
    xf = x.astype(F32)
    xf = xf * lax.rsqrt(jnp.mean(xf * xf, axis=-1, keepdims=True) + EPS)
    return (xf * g.astype(F32)).astype(x.dtype)


def group_rms_norm(x, g, groups):
    shp = x.shape
    xf = x.astype(F32).reshape(shp[:-1] + (groups, shp[-1] // groups))
    xf = xf * lax.rsqrt(jnp.mean(xf * xf, axis=-1, keepdims=True) + EPS)
    return xf.reshape(shp) * g.astype(F32)


def causal_conv(u, buf, w, b):
    t = u.shape[1]
    full = jnp.concatenate([buf, u], axis=1)
    y = b + full[:, 0:t] * w[0]
    for i in range(1, SSD_CONV):
        y = y + full[:, i:i + t] * w[i]
    return y, full[:, t:]


def chunked_decay_scan(q, k, v, log_a, h0, chunk):
    bsz, t, g, n = q.shape
    r, p = v.shape[3], v.shape[4]
    nc = t // chunk
    q = q.reshape(bsz, nc, chunk, g, n)
    k = k.reshape(bsz, nc, chunk, g, n)
    v = v.reshape(bsz, nc, chunk, g, r, p)
    acum = jnp.cumsum(log_a.reshape(bsz, nc, chunk, g, r), axis=2)
    causal = jnp.tril(jnp.ones((chunk, chunk), dtype=bool))[:, :, None, None]
    seg = acum[:, :, :, None] - acum[:, :, None, :]
    lmat = jnp.exp(jnp.where(causal, seg, -jnp.inf))
    scores = jnp.einsum('bcign,bcjgn->bcijg', q, k)[..., None] * lmat
    y = jnp.einsum('bcijgr,bcjgrp->bcigrp', scores, v)
    v_end = v * jnp.exp(acum[:, :, -1:] - acum)[..., None]
    chunk_states = jnp.einsum('bcjgn,bcjgrp->bcgrpn', k, v_end)
    chunk_decay = jnp.exp(acum[:, :, -1])

    def step(h, inp):
        s, d = inp
        return h * d[..., None, None] + s, h

    h_last, h_in = lax.scan(step, h0, (jnp.moveaxis(chunk_states, 1, 0), jnp.moveaxis(chunk_decay, 1, 0)))
    h_in = jnp.moveaxis(h_in, 0, 1)
    y = y + jnp.einsum('bcign,bcgrpn->bcigrp', q, h_in) * jnp.exp(acum)[..., None]
    return y.reshape(bsz, t, g, r, p), h_last


def run_segments(q, k, v, log_a, h0, segs):
    ys = []
    h = h0
    for start, length, chunk in segs:
        sl = slice(start, start + length)
        y, h = chunked_decay_scan(q[:, sl], k[:, sl], v[:, sl], log_a[:, sl], h, chunk)
        ys.append(y)
    return jnp.concatenate(ys, axis=1), h


def rotary(x, pos):
    half = x.shape[-1] // 2
    inv_freq = 1.0 / (RET_THETA_BASE ** jnp.linspace(0.0, 1.0, half, dtype=F32))
    ang = pos.astype(F32)[:, None] * inv_freq
    cos = jnp.cos(ang)[None, :, None, :]
    sin = jnp.sin(ang)[None, :, None, :]
    x1, x2 = x[..., :half], x[..., half:]
    return jnp.concatenate([x1 * cos - x2 * sin, x1 * sin + x2 * cos], axis=-1)


def mamba2_mixer(u, conv_buf, ssm_state, segs, w_in, conv_w, conv_b, dt_bias, a_log, d_skip, norm_g, w_out):
    bsz, t, _ = u.shape
    proj = (u @ w_in).astype(F32)
    z = proj[..., :D_INNER]
    xbc = proj[..., D_INNER:D_INNER + SSD_CONV_DIM]
    dt = proj[..., D_INNER + SSD_CONV_DIM:]
    xbc, new_buf = causal_conv(xbc, conv_buf.astype(F32), conv_w.astype(F32), conv_b.astype(F32))
    xbc = jax.nn.silu(xbc)
    gn = SSD_GROUPS * SSD_STATE
    xs = xbc[..., :D_INNER].reshape(bsz, t, SSD_GROUPS, SSD_HPG, SSD_HEAD_DIM)
    bmat = xbc[..., D_INNER:D_INNER + gn].reshape(bsz, t, SSD_GROUPS, SSD_STATE)
    cmat = xbc[..., D_INNER + gn:].reshape(bsz, t, SSD_GROUPS, SSD_STATE)
    dt = jax.nn.softplus(dt + dt_bias.astype(F32)).reshape(bsz, t, SSD_GROUPS, SSD_HPG)
    log_a = dt * (-jnp.exp(a_log.astype(F32))).reshape(SSD_GROUPS, SSD_HPG)
    h0 = ssm_state.astype(F32).reshape(bsz, SSD_GROUPS, SSD_HPG, SSD_HEAD_DIM, SSD_STATE)
    y, h_last = run_segments(cmat, bmat, xs * dt[..., None], log_a, h0, segs)
    y = y + xs * d_skip.astype(F32).reshape(SSD_GROUPS, SSD_HPG)[:, :, None]
    y = y.reshape(bsz, t, D_INNER) * jax.nn.silu(z)
    y = group_rms_norm(y, norm_g, SSD_GROUPS)
    out = y.astype(u.dtype) @ w_out
    return (out, new_buf.astype(u.dtype),
            h_last.reshape(bsz, SSD_HEADS, SSD_HEAD_DIM, SSD_STATE).astype(u.dtype))


def rwkv7_mixer(u, shift_buf, wkv_state, mu, w_rkvg, w0, w_lora_a, w_lora_b, a0, a_lora_a, a_lora_b,
                k_k, k_a, r_k, ln_g, ln_b, w_out):
    bsz, t, _ = u.shape
    prev = jnp.concatenate([shift_buf[:, None].astype(u.dtype), u[:, :-1]], axis=1)
    xx = prev - u
    xmix = u[None] + xx[None] * mu[:, None, None, :].astype(u.dtype)
    rkvg = jnp.einsum('sbtd,sde->sbte', xmix[:4], w_rkvg).astype(F32)
    r, k, v, g = rkvg[0], rkvg[1], rkvg[2], rkvg[3]
    w_raw = (w0 + jnp.tanh(xmix[4] @ w_lora_a) @ w_lora_b).astype(F32)
    decay = jnp.exp(-jnp.exp(-jax.nn.softplus(-w_raw) - 0.5))
    a = jax.nn.sigmoid((a0 + (xmix[5] @ a_lora_a) @ a_lora_b).astype(F32))

    def heads(z):
        return z.reshape(bsz, t, RWKV_HEADS, RWKV_HEAD_DIM)

    kk = heads(k * k_k.astype(F32))
    kk = kk / jnp.maximum(jnp.sqrt(jnp.sum(kk * kk, axis=-1, keepdims=True)), 1e-12)
    k = heads(k * (1.0 + (a - 1.0) * k_a.astype(F32)))
    r, v, a, decay = heads(r), heads(v), heads(a), heads(decay)

    def step(s, inp):
        r_t, w_t, k_t, v_t, kk_t, a_t = inp
        sa = jnp.einsum('bhvk,bhk->bhv', s, -kk_t)
        s = s * w_t[:, :, None, :] + sa[..., None] * (kk_t * a_t)[:, :, None, :] + v_t[..., None] * k_t[:, :, None, :]
        return s, jnp.einsum('bhvk,bhk->bhv', s, r_t)

    def tm(z):
        return jnp.moveaxis(z, 1, 0)

    s_last, y = lax.scan(step, wkv_state.astype(F32), (tm(r), tm(decay), tm(k), tm(v), tm(kk), tm(a)))
    y = jnp.moveaxis(y, 0, 1)
    yc = y - jnp.mean(y, axis=-1, keepdims=True)
    y = yc * lax.rsqrt(jnp.mean(yc * yc, axis=-1, keepdims=True) + RWKV_LN_EPS)
    y = y.reshape(bsz, t, D_INNER) * ln_g.astype(F32) + ln_b.astype(F32)
    bonus = jnp.sum(r * k * r_k.astype(F32), axis=-1, keepdims=True) * v
    y = (y + bonus.reshape(bsz, t, D_INNER)) * jax.nn.silu(g)
    return y.astype(u.dtype) @ w_out, u[:, -1], s_last.astype(u.dtype)


def retention_mixer(u, ret_state, segs, pos, w_in, norm_g, w_out):
    bsz, t, _ = u.shape
    proj = (u @ w_in).astype(F32)
    q = proj[..., :D_MODEL].reshape(bsz, t, RET_HEADS, RET_QK_DIM)
    k = proj[..., D_MODEL:2 * D_MODEL].reshape(bsz, t, RET_HEADS, RET_QK_DIM)
    v = proj[..., 2 * D_MODEL:2 * D_MODEL + D_INNER].reshape(bsz, t, RET_HEADS, 1, RET_V_DIM)
    g = proj[..., 2 * D_MODEL + D_INNER:]
    q = rotary(q, pos)
    k = rotary(k, pos) * (RET_QK_DIM ** -0.5)
    log_gamma = jnp.log1p(-jnp.exp2(-5.0 - jnp.arange(RET_HEADS, dtype=F32)))
    log_a = jnp.broadcast_to(log_gamma[:, None], (bsz, t, RET_HEADS, 1))
    h0 = ret_state.astype(F32).reshape(bsz, RET_HEADS, 1, RET_V_DIM, RET_QK_DIM)
    y, h_last = run_segments(q, k, v, log_a, h0, segs)
    y = group_rms_norm(y.reshape(bsz, t, D_INNER), norm_g, RET_HEADS) * jax.nn.silu(g)
    out = y.astype(u.dtype) @ w_out
    return out, h_last.reshape(bsz, RET_HEADS, RET_V_DIM, RET_QK_DIM).astype(u.dtype)


def run_trunk(h, conv_st, ssd_st, shift_st, wkv_st, ret_st, segs, pos, norm_g, final_norm_g, ssd_w, rwkv_w, ret_w):
    new_conv, new_ssd, new_shift, new_wkv, new_ret = [], [], [], [], []
    for layer in range(DEPTH):
        kind, j = layer % N_MIXERS, layer // N_MIXERS
        u = rms_norm(h, norm_g[layer])
        if kind == 0:
            out, c, s = mamba2_mixer(u, conv_st[j], ssd_st[j], segs, *[p[j] for p in ssd_w])
            new_conv.append(c)
            new_ssd.append(s)
        elif kind == 1:
            out, sh, s = rwkv7_mixer(u, shift_st[j], wkv_st[j], *[p[j] for p in rwkv_w])
            new_shift.append(sh)
            new_wkv.append(s)
        else:
            out, s = retention_mixer(u, ret_st[j], segs, pos, *[p[j] for p in ret_w])
            new_ret.append(s)
        h = h + out
    return (rms_norm(h, final_norm_g), jnp.stack(new_conv), jnp.stack(new_ssd), jnp.stack(new_shift),
            jnp.stack(new_wkv), jnp.stack(new_ret))


def setup_inputs(seed: int = 0) -> dict:
    key = jax.random.key(seed)
    ks = iter(jax.random.split(key, 64))

    def nrm(shape, scale=1.0):
        return scale * jax.random.normal(next(ks), shape, F32)

    def unif(shape, lo, hi):
        return jax.random.uniform(next(ks), shape, F32, lo, hi)

    ns, nr, nt = N_SSD_LAYERS, N_RWKV_LAYERS, N_RET_LAYERS
    dt0 = jnp.exp(unif((ns, SSD_HEADS), math.log(1e-3), math.log(1e-1)))
    return {
        'x_prompt': nrm((BATCH, SEQ, D_MODEL)),
        'x_sample': nrm((DEC_BATCH, DEC_SEQ, D_MODEL)),
        'state_ssd_conv': nrm((ns, DEC_BATCH, SSD_CONV - 1, SSD_CONV_DIM)),
        'state_ssd': nrm((ns, DEC_BATCH, SSD_HEADS, SSD_HEAD_DIM, SSD_STATE), 0.1),
        'state_rwkv_shift': nrm((nr, DEC_BATCH, D_MODEL)),
        'state_rwkv_wkv': nrm((nr, DEC_BATCH, RWKV_HEADS, RWKV_HEAD_DIM, RWKV_HEAD_DIM), 0.1),
        'state_ret': nrm((nt, DEC_BATCH, RET_HEADS, RET_V_DIM, RET_QK_DIM), 0.1),
        'meta_tokens': nrm((N_META, D_MODEL)),
        'norm_g': 1.0 + nrm((DEPTH, D_MODEL), 0.02),
        'final_norm_g': 1.0 + nrm((D_MODEL,), 0.02),
        'ssd_w_in': nrm((ns, D_MODEL, SSD_IN), D_MODEL ** -0.5),
        'ssd_conv_w': nrm((ns, SSD_CONV, SSD_CONV_DIM), SSD_CONV ** -0.5),
        'ssd_conv_b': nrm((ns, SSD_CONV_DIM), 0.02),
        'ssd_dt_bias': dt0 + jnp.log(-jnp.expm1(-dt0)),
        'ssd_a_log': jnp.log(unif((ns, SSD_HEADS), 1.0, 16.0)),
        'ssd_d': 1.0 + nrm((ns, SSD_HEADS), 0.02),
        'ssd_norm_g': 1.0 + nrm((ns, D_INNER), 0.02),
        'ssd_w_out': nrm((ns, D_INNER, D_MODEL), D_INNER ** -0.5),
        'rwkv_mu': unif((nr, 6, D_MODEL), 0.0, 1.0),
        'rwkv_w_rkvg': nrm((nr, 4, D_MODEL, D_INNER), D_MODEL ** -0.5),
        'rwkv_w0': jnp.linspace(-6.0, -1.0, D_INNER, dtype=F32)[None] + nrm((nr, D_INNER), 0.1),
        'rwkv_w_lora_a': nrm((nr, D_MODEL, RWKV_LORA), D_MODEL ** -0.5),
        'rwkv_w_lora_b': nrm((nr, RWKV_LORA, D_INNER), 0.1 * RWKV_LORA ** -0.5),
        'rwkv_a0': nrm((nr, D_INNER), 0.1),
        'rwkv_a_lora_a': nrm((nr, D_MODEL, RWKV_LORA), D_MODEL ** -0.5),
        'rwkv_a_lora_b': nrm((nr, RWKV_LORA, D_INNER), 0.1 * RWKV_LORA ** -0.5),
        'rwkv_k_k': 0.85 + nrm((nr, D_INNER), 0.02),
        'rwkv_k_a': 1.0 + nrm((nr, D_INNER), 0.02),
        'rwkv_r_k': nrm((nr, RWKV_HEADS, RWKV_HEAD_DIM), 0.1),
        'rwkv_ln_g': 1.0 + nrm((nr, D_INNER), 0.02),
        'rwkv_ln_b': nrm((nr, D_INNER), 0.02),
        'rwkv_w_out': nrm((nr, D_INNER, D_MODEL), D_INNER ** -0.5),
        'ret_w_in': nrm((nt, D_MODEL, RET_IN), D_MODEL ** -0.5),
        'ret_norm_g': 1.0 + nrm((nt, D_INNER), 0.02),
        'ret_w_out': nrm((nt, D_INNER, D_MODEL), D_INNER ** -0.5),
    }


def reference(x_prompt, x_sample, state_ssd_conv, state_ssd, state_rwkv_shift, state_rwkv_wkv, state_ret,
              meta_tokens, norm_g, final_norm_g,
              ssd_w_in, ssd_conv_w, ssd_conv_b, ssd_dt_bias, ssd_a_log, ssd_d, ssd_norm_g, ssd_w_out,
              rwkv_mu, rwkv_w_rkvg, rwkv_w0, rwkv_w_lora_a, rwkv_w_lora_b, rwkv_a0, rwkv_a_lora_a,
              rwkv_a_lora_b, rwkv_k_k, rwkv_k_a, rwkv_r_k, rwkv_ln_g, rwkv_ln_b, rwkv_w_out,
              ret_w_in, ret_norm_g, ret_w_out):
    ssd_w = (ssd_w_in, ssd_conv_w, ssd_conv_b, ssd_dt_bias, ssd_a_log, ssd_d, ssd_norm_g, ssd_w_out)
    rwkv_w = (rwkv_mu, rwkv_w_rkvg, rwkv_w0, rwkv_w_lora_a, rwkv_w_lora_b, rwkv_a0, rwkv_a_lora_a,
              rwkv_a_lora_b, rwkv_k_k, rwkv_k_a, rwkv_r_k, rwkv_ln_g, rwkv_ln_b, rwkv_w_out)
    ret_w = (ret_w_in, ret_norm_g, ret_w_out)
    dt = x_prompt.dtype

    bp, seq = x_prompt.shape[0], x_prompt.shape[1]
    h_p = jnp.concatenate([jnp.broadcast_to(meta_tokens.astype(dt)[None], (bp, N_META, D_MODEL)), x_prompt], axis=1)
    zc = jnp.zeros((N_SSD_LAYERS, bp, SSD_CONV - 1, SSD_CONV_DIM), dt)
    zs = jnp.zeros((N_SSD_LAYERS, bp, SSD_HEADS, SSD_HEAD_DIM, SSD_STATE), dt)
    zsh = jnp.zeros((N_RWKV_LAYERS, bp, D_MODEL), dt)
    zw = jnp.zeros((N_RWKV_LAYERS, bp, RWKV_HEADS, RWKV_HEAD_DIM, RWKV_HEAD_DIM), dt)
    zr = jnp.zeros((N_RET_LAYERS, bp, RET_HEADS, RET_V_DIM, RET_QK_DIM), dt)
    prompt_segs = ((0, N_META, N_META), (N_META, seq, CHUNK))
    prompt_pos = jnp.arange(N_META + seq)
    y_p, p_conv, p_ssd, p_shift, p_wkv, p_ret = run_trunk(
        h_p, zc, zs, zsh, zw, zr, prompt_segs, prompt_pos, norm_g, final_norm_g, ssd_w, rwkv_w, ret_w)
    y_prompt = y_p[:, N_META:]

    ds = x_sample.shape[1]
    sample_segs = ((0, ds, ds),)
    sample_pos = PAST_LEN + jnp.arange(ds)
    y_sample, s_conv, s_ssd, s_shift, s_wkv, s_ret = run_trunk(
        x_sample, state_ssd_conv, state_ssd, state_rwkv_shift, state_rwkv_wkv, state_ret,
        sample_segs, sample_pos, norm_g, final_norm_g, ssd_w, rwkv_w, ret_w)

    return (y_prompt, y_sample, p_conv, p_ssd, p_shift, p_wkv, p_ret, s_conv, s_ssd, s_shift, s_wkv, s_ret)
```

```python
import functools
import math

import jax
import jax.numpy as jnp
from jax import lax
from jax.experimental import pallas as pl
from jax.experimental.pallas import tpu as pltpu

F32 = jnp.float32
BF16 = jnp.bfloat16

EPS = 1e-6
N_META = 16
HEAD_DIM = 64
SSD_STATE = 128
SSD_GROUPS = 8
SSD_HPG = 8
SSD_CONV = 4
RET_HEADS = 8
RET_QK = 256
RET_V = 512
RET_THETA_BASE = 10000.0
RWKV_LORA_PAD = 128
RWKV_CHUNK = 64
RWKV_HPB = 4
RWKV_LN_EPS = 1e-5 * HEAD_DIM
KPAD = 128
NEG = -1e30
ROW_TILE = 512
VMEM_LIMIT = 56 * 1024 * 1024


def _cparams(sem):
    return pltpu.CompilerParams(dimension_semantics=sem, vmem_limit_bytes=VMEM_LIMIT)


def _nt(a, b):
    return lax.dot_general(a, b, (((1,), (1,)), ((), ())), preferred_element_type=F32)


def _tn(a, b):
    return lax.dot_general(a, b, (((0,), (0,)), ((), ())), preferred_element_type=F32)


def _dot(a, b):
    return jnp.dot(a, b, preferred_element_type=F32)


def _split(x, terms):
    parts = []
    r = x
    for i in range(terms):
        p = r.astype(BF16)
        parts.append(p)
        if i + 1 < terms:
            r = r - p.astype(F32)
    return parts


def _dot_split_l(x, m, terms=3):
    acc = None
    for p in _split(x, terms):
        d = _dot(p, m)
        acc = d if acc is None else acc + d
    return acc


def _dot_split_r(m, x, terms=3):
    acc = None
    for p in _split(x, terms):
        d = _dot(m, p)
        acc = d if acc is None else acc + d
    return acc


def _pad_rows(x, rows):
    if x.shape[0] == rows:
        return x
    return jnp.concatenate([x, jnp.zeros((rows - x.shape[0], x.shape[1]), x.dtype)], axis=0)


def _idiv(x, n):
    return jnp.right_shift(x, int(math.log2(n)))


def _imod(x, n):
    return jnp.bitwise_and(x, n - 1)


def _silu(x):
    return x * jax.nn.sigmoid(x)


def _softplus(x):
    return jnp.maximum(x, 0.0) + jnp.log(1.0 + jnp.exp(-jnp.abs(x)))


def _rmsnorm_kernel(x_ref, g_ref, o_ref):
    x = x_ref[...]
    ms = jnp.mean(x * x, axis=-1, keepdims=True)
    o_ref[...] = x * lax.rsqrt(ms + EPS) * g_ref[...]


def _row_tile(m):
    tm = math.gcd(m, ROW_TILE)
    assert tm % 16 == 0, m
    return tm


def rmsnorm(x, g):
    m, d = x.shape
    tm = _row_tile(m)
    return pl.pallas_call(
        _rmsnorm_kernel,
        grid=(m // tm,),
        in_specs=[pl.BlockSpec((tm, d), lambda i: (i, 0)), pl.BlockSpec((1, d), lambda i: (0, 0))],
        out_specs=pl.BlockSpec((tm, d), lambda i: (i, 0)),
        out_shape=jax.ShapeDtypeStruct((m, d), F32),
        compiler_params=_cparams(("parallel",)),
        name="rmsnorm",
    )(x, g.reshape(1, d))


def _mm_kernel(a_ref, w_ref, *rest, has_res):
    if has_res:
        r_ref, o_ref, abf_ref = rest
    else:
        o_ref, abf_ref = rest

    @pl.when(pl.program_id(1) == 0)
    def _():
        abf_ref[...] = a_ref[...].astype(BF16)

    acc = _dot(abf_ref[...], w_ref[...])
    if has_res:
        acc = r_ref[...] + acc
    o_ref[...] = acc


def matmul(a, w, res=None, tn=512):
    m, k = a.shape
    n = w.shape[1]
    tn = min(tn, n)
    tm = _row_tile(m)
    in_specs = [pl.BlockSpec((tm, k), lambda i, j: (i, 0)), pl.BlockSpec((k, tn), lambda i, j: (0, j))]
    args = [a, w]
    if res is not None:
        in_specs.append(pl.BlockSpec((tm, tn), lambda i, j: (i, j)))
        args.append(res)
    return pl.pallas_call(
        functools.partial(_mm_kernel, has_res=res is not None),
        grid=(m // tm, n // tn),
        in_specs=in_specs,
        out_specs=pl.BlockSpec((tm, tn), lambda i, j: (i, j)),
        out_shape=jax.ShapeDtypeStruct((m, n), F32),
        scratch_shapes=[pltpu.VMEM((tm, k), BF16)],
        compiler_params=_cparams(("parallel", "arbitrary")),
        name="matmul_res" if res is not None else "matmul",
    )(*args)


def _mixmm_kernel(u_ref, p_ref, mu_ref, w_ref, o_ref, xm_ref):
    @pl.when(pl.program_id(2) == 0)
    def _():
        u = u_ref[...]
        xm_ref[...] = (u + (p_ref[...] - u) * mu_ref[0]).astype(BF16)

    o_ref[0] = _dot(xm_ref[...], w_ref[0])


def mix_matmul(u, prev, mu, w, tn=512):
    m, k = u.shape
    s, _, n = w.shape
    tn = min(tn, n)
    tm = _row_tile(m)
    return pl.pallas_call(
        _mixmm_kernel,
        grid=(m // tm, s, n // tn),
        in_specs=[
            pl.BlockSpec((tm, k), lambda i, si, j: (i, 0)),
            pl.BlockSpec((tm, k), lambda i, si, j: (i, 0)),
            pl.BlockSpec((1, 1, k), lambda i, si, j: (si, 0, 0)),
            pl.BlockSpec((1, k, tn), lambda i, si, j: (si, 0, j)),
        ],
        out_specs=pl.BlockSpec((1, tm, tn), lambda i, si, j: (si, i, j)),
        out_shape=jax.ShapeDtypeStruct((s, m, n), F32),
        scratch_shapes=[pltpu.VMEM((tm, k), BF16)],
        compiler_params=_cparams(("parallel", "arbitrary", "arbitrary")),
        name="mix_matmul",
    )(u, prev, mu.reshape(s, 1, k), w)


def _conv_silu(cur, car_ref, w_ref, b_ref, lq):
    car = car_ref[...]
    w = w_ref[...]
    n = cur.shape[1]
    rows8 = lax.broadcasted_iota(jnp.int32, (8, n), 0)
    acc = b_ref[...] + cur * w[SSD_CONV - 1:SSD_CONV]
    for s in range(1, SSD_CONV):
        rolled = pltpu.roll(cur, s, 0)
        first = jnp.where(rows8 < s, pltpu.roll(car, s, 0), rolled[:8])
        sh = first if lq == 8 else jnp.concatenate([first, rolled[8:]], axis=0)
        acc = acc + sh * w[SSD_CONV - 1 - s:SSD_CONV - s]
    car_ref[...] = cur[lq - 8:]
    return _silu(acc)


def _ssd_kernel(z_ref, x_ref, b_ref, c_ref, dt_ref, cix_ref, cib_ref, cic_ref, s0_ref,
                cwx_ref, cwb_ref, cwc_ref, cbx_ref, cbb_ref, cbc_ref, dtb_ref, alog_ref, d_ref, ng_ref,
                y_ref, sout_ref, carx, carb, carc, st, *, lq, lo, hi, nchunks):
    c = pl.program_id(2)
    gw = SSD_HPG * HEAD_DIM

    @pl.when(c == 0)
    def _():
        carx[...] = cix_ref[0]
        carb[...] = cib_ref[0]
        carc[...] = cic_ref[0]
        st[...] = s0_ref[0].reshape(gw, SSD_STATE)

    xc = _conv_silu(x_ref[...], carx, cwx_ref, cbx_ref, lq)
    bc = _conv_silu(b_ref[...], carb, cwb_ref, cbb_ref, lq)
    cc = _conv_silu(c_ref[...], carc, cwc_ref, cbc_ref, lq)

    pos = c * lq + lax.broadcasted_iota(jnp.int32, (lq, 1), 0)
    valid = (pos >= lo) & (pos < hi)
    lane = lax.broadcasted_iota(jnp.int32, (lq, 128), 1)
    dt = _softplus(dt_ref[...] + dtb_ref[0])
    dt = jnp.where(valid & (lane < SSD_HPG), dt, 0.0)
    la = dt * (-jnp.exp(alog_ref[0]))

    la_pad = _pad_rows(la, KPAD)
    ri = lax.broadcasted_iota(jnp.int32, (KPAD, KPAD), 0)
    ci = lax.broadcasted_iota(jnp.int32, (KPAD, KPAD), 1)
    tril = (ri >= ci).astype(BF16)
    triu = (ri <= ci).astype(BF16)
    acum_full = _dot_split_r(tril, la_pad)
    acum_t = _dot_split_l(la_pad.T, triu)
    acum = acum_full[:lq]
    a_end = acum_full[KPAD - 1:KPAD]
    dec_end = jnp.exp(a_end - acum_full)
    e_in = jnp.exp(acum)
    cd = jnp.exp(a_end)

    bcp = _pad_rows(bc, KPAD).astype(BF16)
    ccb = cc.astype(BF16)
    g_sc = _nt(ccb, bcp)
    st_old = st[...]
    y_in = _nt(ccb, st_old.astype(BF16))

    qi = lax.broadcasted_iota(jnp.int32, (lq, KPAD), 0)
    kj = lax.broadcasted_iota(jnp.int32, (lq, KPAD), 1)
    causal = kj <= qi
    lane_q = lax.broadcasted_iota(jnp.int32, (lq, 128), 1) < HEAD_DIM
    lane_k = lax.broadcasted_iota(jnp.int32, (KPAD, 128), 1) < HEAD_DIM
    row_k = lax.broadcasted_iota(jnp.int32, (128, 1), 0) < HEAD_DIM
    dvec = d_ref[0]

    ys = []
    for p in range(SSD_HPG // 2):
        h0, h1 = 2 * p, 2 * p + 1
        xp = xc[:, 128 * p:128 * (p + 1)]
        vp = xp * jnp.where(lane_q, dt[:, h0:h0 + 1], dt[:, h1:h1 + 1])
        vpp = _pad_rows(vp, KPAD)
        yp = y_in[:, 128 * p:128 * (p + 1)] * jnp.where(lane_q, e_in[:, h0:h0 + 1], e_in[:, h1:h1 + 1])
        yp = yp + xp * jnp.where(lane_q, dvec[:, h0:h0 + 1], dvec[:, h1:h1 + 1])
        for hh, h in ((0, h0), (1, h1)):
            seg = acum[:, h:h + 1] - acum_t[h:h + 1, :]
            lm = jnp.exp(jnp.where(causal, seg, NEG))
            pm = (g_sc * lm).astype(BF16)
            vm = jnp.where(lane_k if hh == 0 else jnp.logical_not(lane_k), vpp, 0.0).astype(BF16)
            yp = yp + _dot(pm, vm)
        ys.append(yp)
        vend = vpp * jnp.where(lane_k, dec_end[:, h0:h0 + 1], dec_end[:, h1:h1 + 1])
        upd = _tn(vend.astype(BF16), bcp)
        cdp = jnp.where(row_k, cd[:, h0:h0 + 1], cd[:, h1:h1 + 1])
        st[128 * p:128 * (p + 1), :] = st_old[128 * p:128 * (p + 1), :] * cdp + upd

    y = jnp.concatenate(ys, axis=1)
    y = y * _silu(z_ref[...])
    ms = jnp.mean(y * y, axis=-1, keepdims=True)
    y = y * lax.rsqrt(ms + EPS) * ng_ref[...]
    y_ref[...] = jnp.where(valid, y, 0.0).astype(BF16)

    @pl.when(c == nchunks - 1)
    def _():
        sout_ref[0] = st[...].reshape(SSD_HPG, HEAD_DIM, SSD_STATE)


def ssd_scan(proj, dt_raw, conv_init, s0, conv_w, conv_b, dtb, alog, dskip, norm_g, *, bsz, tp, lq, lo, hi):
    m = proj.shape[0]
    nch = tp // lq
    gw = SSD_HPG * HEAD_DIM
    d_inner = SSD_GROUPS * gw
    xoff = d_inner // gw
    boff = 2 * d_inner // SSD_STATE
    coff = boff + SSD_GROUPS
    cboff = d_inner // SSD_STATE
    row = lambda b, g, c: b * nch + c
    kern = functools.partial(_ssd_kernel, lq=lq, lo=lo, hi=hi, nchunks=nch)
    conv_b2 = conv_b.reshape(1, -1)
    return pl.pallas_call(
        kern,
        grid=(bsz, SSD_GROUPS, nch),
        in_specs=[
            pl.BlockSpec((lq, gw), lambda b, g, c: (row(b, g, c), g)),
            pl.BlockSpec((lq, gw), lambda b, g, c: (row(b, g, c), xoff + g)),
            pl.BlockSpec((lq, SSD_STATE), lambda b, g, c: (row(b, g, c), boff + g)),
            pl.BlockSpec((lq, SSD_STATE), lambda b, g, c: (row(b, g, c), coff + g)),
            pl.BlockSpec((lq, 128), lambda b, g, c: (row(b, g, c), g)),
            pl.BlockSpec((1, 8, gw), lambda b, g, c: (b, 0, g)),
            pl.BlockSpec((1, 8, SSD_STATE), lambda b, g, c: (b, 0, cboff + g)),
            pl.BlockSpec((1, 8, SSD_STATE), lambda b, g, c: (b, 0, cboff + SSD_GROUPS + g)),
            pl.BlockSpec((1, SSD_HPG, HEAD_DIM, SSD_STATE), lambda b, g, c: (b, g, 0, 0)),
            pl.BlockSpec((SSD_CONV, gw), lambda b, g, c: (0, g)),
            pl.BlockSpec((SSD_CONV, SSD_STATE), lambda b, g, c: (0, cboff + g)),
            pl.BlockSpec((SSD_CONV, SSD_STATE), lambda b, g, c: (0, cboff + SSD_GROUPS + g)),
            pl.BlockSpec((1, gw), lambda b, g, c: (0, g)),
            pl.BlockSpec((1, SSD_STATE), lambda b, g, c: (0, cboff + g)),
            pl.BlockSpec((1, SSD_STATE), lambda b, g, c: (0, cboff + SSD_GROUPS + g)),
            pl.BlockSpec((1, 1, 128), lambda b, g, c: (g, 0, 0)),
            pl.BlockSpec((1, 1, 128), lambda b, g, c: (g, 0, 0)),
            pl.BlockSpec((1, 1, 128), lambda b, g, c: (g, 0, 0)),
            pl.BlockSpec((1, gw), lambda b, g, c: (0, g)),
        ],
        out_specs=[
            pl.BlockSpec((lq, gw), lambda b, g, c: (row(b, g, c), g)),
            pl.BlockSpec((1, SSD_HPG, HEAD_DIM, SSD_STATE), lambda b, g, c: (b, g, 0, 0)),
        ],
        out_shape=[
            jax.ShapeDtypeStruct((m, d_inner), BF16),
            jax.ShapeDtypeStruct((bsz, SSD_GROUPS * SSD_HPG, HEAD_DIM, SSD_STATE), F32),
        ],
        scratch_shapes=[
            pltpu.VMEM((8, gw), F32), pltpu.VMEM((8, SSD_STATE), F32), pltpu.VMEM((8, SSD_STATE), F32),
            pltpu.VMEM((gw, SSD_STATE), F32),
        ],
        compiler_params=_cparams(("parallel", "parallel", "arbitrary")),
        name="ssd_scan",
    )(proj, proj, proj, proj, dt_raw, conv_init, conv_init, conv_init, s0,
      conv_w, conv_w, conv_w, conv_b2, conv_b2, conv_b2, dtb, alog, dskip, norm_g.reshape(1, -1))


def _ret_kernel(q_ref, k_ref, v_ref, g_ref, cos_ref, sin_ref, lg_ref, s0_ref, ng_ref,
                y_ref, sout_ref, st, *, lq, lo, hi, nchunks):
    c = pl.program_id(2)

    @pl.when(c == 0)
    def _():
        st[...] = s0_ref[0, 0]

    lg = lg_ref[0][0:1, 0:1]
    nv = float(hi - lo)

    def count(p):
        return jnp.clip((p + 1 - lo).astype(F32), 0.0, nv)

    base = c * lq
    pos_i = base + lax.broadcasted_iota(jnp.int32, (lq, 1), 0)
    valid = (pos_i >= lo) & (pos_i < hi)
    cnt_i = count(pos_i)
    cnt_j = count(base + lax.broadcasted_iota(jnp.int32, (1, KPAD), 1))
    cnt_jc = count(base + lax.broadcasted_iota(jnp.int32, (KPAD, 1), 0))
    cnt0 = count(base - 1 + jnp.zeros((1, 1), jnp.int32))
    cnt_end = count(base + lq - 1 + jnp.zeros((1, 1), jnp.int32))

    cos = cos_ref[...]
    sin = sin_ref[...]
    half = RET_QK // 2

    def rot(x):
        x1, x2 = x[:, :half], x[:, half:]
        return jnp.concatenate([x1 * cos - x2 * sin, x1 * sin + x2 * cos], axis=1)

    qr = rot(q_ref[...]).astype(BF16)
    kr = jnp.where(valid, rot(k_ref[...]) * (RET_QK ** -0.5), 0.0)
    v = jnp.where(valid, v_ref[...], 0.0)
    krp = _pad_rows(kr, KPAD).astype(BF16)
    vp = _pad_rows(v, KPAD)

    sc = _nt(qr, krp)
    qi = lax.broadcasted_iota(jnp.int32, (lq, KPAD), 0)
    kj = lax.broadcasted_iota(jnp.int32, (lq, KPAD), 1)
    dm = jnp.exp(jnp.where(kj <= qi, lg * (cnt_i - cnt_j), NEG))
    y = _dot((sc * dm).astype(BF16), vp.astype(BF16))
    s_old = st[...]
    y = y + _nt(qr, s_old.astype(BF16)) * jnp.exp(lg * (cnt_i - cnt0))
    vend = vp * jnp.exp(lg * (cnt_end - cnt_jc))
    st[...] = s_old * jnp.exp(lg * (cnt_end - cnt0)) + _tn(vend.astype(BF16), krp)

    ms = jnp.mean(y * y, axis=-1, keepdims=True)
    y = y * lax.rsqrt(ms + EPS) * ng_ref[...] * _silu(g_ref[...])
    y_ref[...] = jnp.where(valid, y, 0.0).astype(BF16)

    @pl.when(c == nchunks - 1)
    def _():
        sout_ref[0, 0] = st[...]


def ret_scan(proj, cos, sin, lg, s0, norm_g, *, bsz, tp, lq, lo, hi):
    m = proj.shape[0]
    nch = tp // lq
    d_inner = RET_HEADS * RET_V
    koff = RET_HEADS
    voff = 2 * RET_HEADS * RET_QK // RET_V
    goff = voff + RET_HEADS
    row = lambda b, h, c: b * nch + c
    kern = functools.partial(_ret_kernel, lq=lq, lo=lo, hi=hi, nchunks=nch)
    return pl.pallas_call(
        kern,
        grid=(bsz, RET_HEADS, nch),
        in_specs=[
            pl.BlockSpec((lq, RET_QK), lambda b, h, c: (row(b, h, c), h)),
            pl.BlockSpec((lq, RET_QK), lambda b, h, c: (row(b, h, c), koff + h)),
            pl.BlockSpec((lq, RET_V), lambda b, h, c: (row(b, h, c), voff + h)),
            pl.BlockSpec((lq, RET_V), lambda b, h, c: (row(b, h, c), goff + h)),
            pl.BlockSpec((lq, RET_QK // 2), lambda b, h, c: (c, 0)),
            pl.BlockSpec((lq, RET_QK // 2), lambda b, h, c: (c, 0)),
            pl.BlockSpec((1, 8, 128), lambda b, h, c: (h, 0, 0)),
            pl.BlockSpec((1, 1, RET_V, RET_QK), lambda b, h, c: (b, h, 0, 0)),
            pl.BlockSpec((1, RET_V), lambda b, h, c: (0, h)),
        ],
        out_specs=[
            pl.BlockSpec((lq, RET_V), lambda b, h, c: (row(b, h, c), h)),
            pl.BlockSpec((1, 1, RET_V, RET_QK), lambda b, h, c: (b, h, 0, 0)),
        ],
        out_shape=[
            jax.ShapeDtypeStruct((m, d_inner), BF16),
            jax.ShapeDtypeStruct((bsz, RET_HEADS, RET_V, RET_QK), F32),
        ],
        scratch_shapes=[pltpu.VMEM((RET_V, RET_QK), F32)],
        compiler_params=_cparams(("parallel", "parallel", "arbitrary")),
        name="ret_scan",
    )(proj, proj, proj, proj, cos, sin, lg, s0, norm_g.reshape(1, -1))


def _rwkv_kernel(r_ref, k_ref, v_ref, g_ref, lw_ref, la_ref, bw_ref, ba_ref, par_ref, s0_ref,
                 y_ref, sout_ref, st, *, cd, lo, hi, nchunks):
    c = pl.program_id(2)
    cs = RWKV_CHUNK
    w4 = RWKV_HPB * HEAD_DIM

    r2 = lax.broadcasted_iota(jnp.int32, (w4, w4), 0)
    c2 = lax.broadcasted_iota(jnp.int32, (w4, w4), 1)
    blk = _idiv(r2, HEAD_DIM) == _idiv(c2, HEAD_DIM)
    ones_bd = blk.astype(BF16)

    @pl.when(c == 0)
    def _():
        s0 = s0_ref[0].reshape(w4, HEAD_DIM)
        tile = (lax.broadcasted_iota(jnp.int32, (HEAD_DIM, w4), 0)
                == _imod(lax.broadcasted_iota(jnp.int32, (HEAD_DIM, w4), 1), HEAD_DIM)).astype(BF16)
        st[...] = jnp.where(blk, _dot_split_l(s0, tile), 0.0)

    def segsum(x):
        return _dot_split_l(x, ones_bd, terms=2)

    par = par_ref[...]
    w0, a0, k_k, k_a, r_k, ln_g, ln_b = (par[i:i + 1] for i in range(7))

    r = _pad_rows(r_ref[0], cs)
    k = _pad_rows(k_ref[0], cs)
    v = _pad_rows(v_ref[0], cs)
    g = _pad_rows(g_ref[0], cs)
    w_raw = w0 + _dot(jnp.tanh(_pad_rows(lw_ref[0], cs)).astype(BF16), bw_ref[0])
    a = jax.nn.sigmoid(a0 + _dot(_pad_rows(la_ref[0], cs).astype(BF16), ba_ref[0]))

    ti = lax.broadcasted_iota(jnp.int32, (cs, 1), 0)
    pos = c * cd + ti
    valid = (ti < cd) & (pos >= lo) & (pos < hi)

    lw = -jnp.exp(-_softplus(-w_raw) - 0.5)
    kk = k * k_k
    kk = kk / jnp.maximum(jnp.sqrt(segsum(kk * kk)), 1e-12)
    kp = k * (1.0 + (a - 1.0) * k_a)
    lw = jnp.where(valid, lw, 0.0)
    kk = jnp.where(valid, kk, 0.0)
    kp = jnp.where(valid, kp, 0.0)
    vm = jnp.where(valid, v, 0.0)

    tri = (lax.broadcasted_iota(jnp.int32, (cs, cs), 0) >= lax.broadcasted_iota(jnp.int32, (cs, cs), 1)).astype(BF16)
    cw = _dot_split_r(tri, lw)
    cwl = cw[cs - 1:cs]
    wt = jnp.exp(cw)
    wi = jnp.exp(-cw)
    wend = jnp.exp(cwl - cw)
    b = kk * a
    at = -kk * jnp.exp(cw - lw)
    rt = r * wt

    def bd(x):
        return jnp.where(blk, jnp.concatenate([x] * RWKV_HPB, axis=0), 0.0).astype(BF16)

    lhs = jnp.concatenate([bd(at), bd(rt)], axis=0)
    rhs = jnp.concatenate([bd(b * wi), bd(kp * wi)], axis=0)
    sc = _nt(lhs, rhs)
    tt = _imod(r2, cs)
    jj = _imod(c2, cs)
    strict = blk & (tt > jj)
    incl = blk & (tt >= jj)
    mab = jnp.where(strict, sc[:w4, :w4], 0.0)
    mak = jnp.where(strict, sc[:w4, w4:], 0.0)
    nrb = jnp.where(incl, sc[w4:, :w4], 0.0)
    nrk = jnp.where(incl, sc[w4:, w4:], 0.0)

    tinv = (r2 == c2).astype(F32) + jnp.where(
        (_idiv(r2, 2) == _idiv(c2, 2)) & (_imod(tt, 2) == 1) & (_imod(jj, 2) == 0), mab, 0.0)
    msz = 2
    while msz < cs:
        off = ((_idiv(r2, 2 * msz) == _idiv(c2, 2 * msz)) & (_imod(tt, 2 * msz) >= msz)
               & (_imod(jj, 2 * msz) < msz))
        tb = tinv.astype(BF16)
        tinv = tinv + _dot(_dot(tb, jnp.where(off, mab, 0.0).astype(BF16)).astype(BF16), tb)
        msz *= 2

    s_old = st[...]
    x = _nt(lhs, s_old.astype(BF16))
    vbd = bd(vm)
    u = _dot(tinv.astype(BF16), (x[:w4] + _dot(mak.astype(BF16), vbd)).astype(BF16))
    ub = u.astype(BF16)
    yb = x[w4:] + _dot(nrb.astype(BF16), ub) + _dot(nrk.astype(BF16), vbd)
    s_new = s_old * wt[cs - 1:cs] + _tn(jnp.concatenate([ub, vbd], axis=0),
                                        jnp.concatenate([bd(b * wend), bd(kp * wend)], axis=0))
    st[...] = s_new

    y = yb[0:cs] + yb[cs:2 * cs] + yb[2 * cs:3 * cs] + yb[3 * cs:4 * cs]
    inv = 1.0 / HEAD_DIM
    yc = y - segsum(y) * inv
    y = yc * lax.rsqrt(segsum(yc * yc) * inv + RWKV_LN_EPS) * ln_g + ln_b
    y = (y + segsum(r * kp * r_k) * v) * _silu(g)
    y_ref[...] = jnp.where(valid, y, 0.0)[:cd].astype(BF16)

    @pl.when(c == nchunks - 1)
    def _():
        tile_t = (_imod(lax.broadcasted_iota(jnp.int32, (w4, HEAD_DIM), 0), HEAD_DIM)
                  == lax.broadcasted_iota(jnp.int32, (w4, HEAD_DIM), 1)).astype(BF16)
        sout_ref[0] = _dot_split_l(s_new, tile_t).reshape(RWKV_HPB, HEAD_DIM, HEAD_DIM)


def rwkv_scan(rkvg, lora1, lora_b, par, s0, *, bsz, tp, cd, lo, hi):
    _, m, e = rkvg.shape
    nch = tp // cd
    heads = e // HEAD_DIM
    w4 = RWKV_HPB * HEAD_DIM
    row = lambda b, h, c: b * nch + c
    kern = functools.partial(_rwkv_kernel, cd=cd, lo=lo, hi=hi, nchunks=nch)
    proj_spec = lambda s: pl.BlockSpec((1, cd, w4), lambda b, h, c: (s, row(b, h, c), h))
    return pl.pallas_call(
        kern,
        grid=(bsz, heads // RWKV_HPB, nch),
        in_specs=[
            proj_spec(0), proj_spec(1), proj_spec(2), proj_spec(3),
            pl.BlockSpec((1, cd, RWKV_LORA_PAD), lambda b, h, c: (0, row(b, h, c), 0)),
            pl.BlockSpec((1, cd, RWKV_LORA_PAD), lambda b, h, c: (1, row(b, h, c), 0)),
            pl.BlockSpec((1, RWKV_LORA_PAD, w4), lambda b, h, c: (0, 0, h)),
            pl.BlockSpec((1, RWKV_LORA_PAD, w4), lambda b, h, c: (1, 0, h)),
            pl.BlockSpec((8, w4), lambda b, h, c: (0, h)),
            pl.BlockSpec((1, RWKV_HPB, HEAD_DIM, HEAD_DIM), lambda b, h, c: (b, h, 0, 0)),
        ],
        out_specs=[
            pl.BlockSpec((cd, w4), lambda b, h, c: (row(b, h, c), h)),
            pl.BlockSpec((1, RWKV_HPB, HEAD_DIM, HEAD_DIM), lambda b, h, c: (b, h, 0, 0)),
        ],
        out_shape=[
            jax.ShapeDtypeStruct((m, e), BF16),
            jax.ShapeDtypeStruct((bsz, heads, HEAD_DIM, HEAD_DIM), F32),
        ],
        scratch_shapes=[pltpu.VMEM((w4, w4), F32)],
        compiler_params=_cparams(("parallel", "parallel", "arbitrary")),
        name="rwkv_scan",
    )(rkvg, rkvg, rkvg, rkvg, lora1, lora1, lora_b, lora_b, par, s0)


def _prep_weights(norm_g, ssd_w_in, ssd_dt_bias, ssd_a_log, ssd_d, ssd_w_out,
                  rwkv_w_rkvg, rwkv_w_lora_a, rwkv_w_lora_b, rwkv_a_lora_a, rwkv_a_lora_b,
                  rwkv_w0, rwkv_a0, rwkv_k_k, rwkv_k_a, rwkv_r_k, rwkv_ln_g, rwkv_ln_b, rwkv_w_out,
                  ret_w_in, ret_w_out):
    d_inner = ssd_w_out.shape[1]
    n_main = d_inner + d_inner + 2 * SSD_GROUPS * SSD_STATE
    ns = ssd_w_in.shape[0]
    d_model = ssd_w_in.shape[1]

    def head_lanes(p):
        p = p.reshape(ns, SSD_GROUPS, 1, SSD_HPG)
        return jnp.pad(p, ((0, 0), (0, 0), (0, 0), (0, 128 - SSD_HPG)))

    w_dt = ssd_w_in[:, :, n_main:].reshape(ns, d_model, SSD_GROUPS, SSD_HPG)
    w_dt = jnp.pad(w_dt, ((0, 0), (0, 0), (0, 0), (0, 128 - SSD_HPG))).reshape(ns, d_model, SSD_GROUPS * 128)
    rank = rwkv_w_lora_a.shape[2]
    lora_a = jnp.stack([rwkv_w_lora_a, rwkv_a_lora_a], axis=1)
    lora_a = jnp.pad(lora_a, ((0, 0), (0, 0), (0, 0), (0, RWKV_LORA_PAD - rank)))
    lora_b = jnp.stack([rwkv_w_lora_b, rwkv_a_lora_b], axis=1)
    lora_b = jnp.pad(lora_b, ((0, 0), (0, 0), (0, RWKV_LORA_PAD - rank), (0, 0)))
    nr = rwkv_w0.shape[0]
    par = jnp.stack([rwkv_w0, rwkv_a0, rwkv_k_k, rwkv_k_a, rwkv_r_k.reshape(nr, -1), rwkv_ln_g, rwkv_ln_b,
                     jnp.zeros_like(rwkv_w0)], axis=1)
    return dict(
        ssd_w_main=ssd_w_in[:, :, :n_main].astype(BF16), ssd_w_dt=w_dt.astype(BF16),
        ssd_dtb=head_lanes(ssd_dt_bias), ssd_alog=head_lanes(ssd_a_log), ssd_dskip=head_lanes(ssd_d),
        ssd_w_out=ssd_w_out.astype(BF16),
        rwkv_w=rwkv_w_rkvg.astype(BF16), rwkv_lora_a=lora_a.astype(BF16), rwkv_lora_b=lora_b.astype(BF16),
        rwkv_par=par, rwkv_w_out=rwkv_w_out.astype(BF16),
        ret_w_in=ret_w_in.astype(BF16), ret_w_out=ret_w_out.astype(BF16),
    )


def _trunk(h, conv_st, ssd_st, shift_st, wkv_st, ret_st, pos, *, bsz, tp, lq, cd, lo, hi, depth,
           norm_g, final_norm_g, wts, ssd_conv_w, ssd_conv_b, ssd_norm_g, rwkv_mu, ret_norm_g):
    d_model = h.shape[1]
    geo = dict(bsz=bsz, tp=tp, lo=lo, hi=hi)
    new_conv, new_ssd, new_shift, new_wkv, new_ret = [], [], [], [], []

    half = RET_QK // 2
    inv_freq = 1.0 / (RET_THETA_BASE ** jnp.linspace(0.0, 1.0, half, dtype=F32))
    ang = pos.astype(F32)[:, None] * inv_freq
    cos, sin = jnp.cos(ang), jnp.sin(ang)
    log_gamma = jnp.log1p(-jnp.exp2(-5.0 - jnp.arange(RET_HEADS, dtype=F32)))
    lg = jnp.broadcast_to(log_gamma[:, None, None], (RET_HEADS, 8, 128))

    for layer in range(depth):
        kind, j = layer % 3, layer // 3
        u = rmsnorm(h, norm_g[layer])
        if kind == 0:
            proj = matmul(u, wts["ssd_w_main"][j])
            dt_raw = matmul(u, wts["ssd_w_dt"][j])
            conv_init = jnp.pad(conv_st[j], ((0, 0), (8 - (SSD_CONV - 1), 0), (0, 0)))
            y, s_new = ssd_scan(proj, dt_raw, conv_init, ssd_st[j], ssd_conv_w[j], ssd_conv_b[j],
                                wts["ssd_dtb"][j], wts["ssd_alog"][j], wts["ssd_dskip"][j], ssd_norm_g[j],
                                lq=lq, **geo)
            d_inner = y.shape[1]
            xbc = proj.reshape(bsz, tp, -1)[:, hi - (SSD_CONV - 1):hi, d_inner:]
            new_conv.append(xbc)
            new_ssd.append(s_new)
            h = matmul(y, wts["ssd_w_out"][j], res=h)
        elif kind == 1:
            u3 = u.reshape(bsz, tp, d_model)
            prev = jnp.concatenate([shift_st[j][:, None, :], u3[:, :-1]], axis=1).reshape(bsz * tp, d_model)
            rkvg = mix_matmul(u, prev, rwkv_mu[j][:4], wts["rwkv_w"][j])
            lora1 = mix_matmul(u, prev, rwkv_mu[j][4:], wts["rwkv_lora_a"][j])
            y, s_new = rwkv_scan(rkvg, lora1, wts["rwkv_lora_b"][j], wts["rwkv_par"][j], wkv_st[j], cd=cd, **geo)
            new_shift.append(u3[:, hi - 1])
            new_wkv.append(s_new)
            h = matmul(y, wts["rwkv_w_out"][j], res=h)
        else:
            proj = matmul(u, wts["ret_w_in"][j])
            y, s_new = ret_scan(proj, cos, sin, lg, ret_st[j], ret_norm_g[j], lq=lq, **geo)
            new_ret.append(s_new)
            h = matmul(y, wts["ret_w_out"][j], res=h)
    y = rmsnorm(h, final_norm_g)
    return (y, jnp.stack(new_conv), jnp.stack(new_ssd), jnp.stack(new_shift), jnp.stack(new_wkv), jnp.stack(new_ret))


def kernel(x_prompt, x_sample, state_ssd_conv, state_ssd, state_rwkv_shift, state_rwkv_wkv, state_ret, meta_tokens, norm_g, final_norm_g, ssd_w_in, ssd_conv_w, ssd_conv_b, ssd_dt_bias, ssd_a_log, ssd_d, ssd_norm_g, ssd_w_out, rwkv_mu, rwkv_w_rkvg, rwkv_w0, rwkv_w_lora_a, rwkv_w_lora_b, rwkv_a0, rwkv_a_lora_a, rwkv_a_lora_b, rwkv_k_k, rwkv_k_a, rwkv_r_k, rwkv_ln_g, rwkv_ln_b, rwkv_w_out, ret_w_in, ret_norm_g, ret_w_out):
    depth = norm_g.shape[0]
    d_model = x_prompt.shape[2]
    wts = _prep_weights(norm_g, ssd_w_in, ssd_dt_bias, ssd_a_log, ssd_d, ssd_w_out,
                        rwkv_w_rkvg, rwkv_w_lora_a, rwkv_w_lora_b, rwkv_a_lora_a, rwkv_a_lora_b,
                        rwkv_w0, rwkv_a0, rwkv_k_k, rwkv_k_a, rwkv_r_k, rwkv_ln_g, rwkv_ln_b, rwkv_w_out,
                        ret_w_in, ret_w_out)
    common = dict(depth=depth, norm_g=norm_g, final_norm_g=final_norm_g, wts=wts, ssd_conv_w=ssd_conv_w,
                  ssd_conv_b=ssd_conv_b, ssd_norm_g=ssd_norm_g, rwkv_mu=rwkv_mu, ret_norm_g=ret_norm_g)

    bp, seq, _ = x_prompt.shape
    lq_p = 128
    lo_p = lq_p - N_META
    tp_p = lo_p + N_META + seq
    h_p = jnp.concatenate([jnp.zeros((bp, lo_p, d_model), F32),
                           jnp.broadcast_to(meta_tokens[None], (bp, N_META, d_model)), x_prompt], axis=1)
    zeros_like_b = lambda s: jnp.zeros((s.shape[0], bp) + s.shape[2:], F32)
    pos_p = jnp.maximum(jnp.arange(tp_p) - lo_p, 0)
    outs_p = _trunk(h_p.reshape(bp * tp_p, d_model), zeros_like_b(state_ssd_conv), zeros_like_b(state_ssd),
                    zeros_like_b(state_rwkv_shift), zeros_like_b(state_rwkv_wkv), zeros_like_b(state_ret), pos_p,
                    bsz=bp, tp=tp_p, lq=lq_p, cd=RWKV_CHUNK, lo=lo_p, hi=tp_p, **common)
    y_prompt = outs_p[0].reshape(bp, tp_p, d_model)[:, lo_p + N_META:]

    bs, ds, _ = x_sample.shape
    tp_s = 16
    past_len = 16384
    h_s = jnp.concatenate([x_sample, jnp.zeros((bs, tp_s - ds, d_model), F32)], axis=1)
    pos_s = past_len + jnp.arange(tp_s)
    outs_s = _trunk(h_s.reshape(bs * tp_s, d_model), state_ssd_conv, state_ssd, state_rwkv_shift, state_rwkv_wkv,
                    state_ret, pos_s, bsz=bs, tp=tp_s, lq=tp_s, cd=tp_s, lo=0, hi=ds, **common)
    y_sample = outs_s[0].reshape(bs, tp_s, d_model)[:, :ds]

    return (y_prompt, y_sample) + tuple(outs_p[1:]) + tuple(outs_s[1:])
```

```python
import functools
import math

import jax
import jax.numpy as jnp
from jax import lax
from jax.experimental import pallas as pl
from jax.experimental.pallas import tpu as pltpu

F32 = jnp.float32
BF16 = jnp.bfloat16

EPS = 1e-6
N_META = 16
HEAD_DIM = 64
SSD_STATE = 128
SSD_GROUPS = 8
SSD_HPG = 8
SSD_CONV = 4
RET_HEADS = 8
RET_QK = 256
RET_V = 512
RET_THETA_BASE = 10000.0
RWKV_LORA_PAD = 128
RWKV_CHUNK = 64
RWKV_HPB = 4
RWKV_GROUPS_PER_STEP = 4
RWKV_SHORT_HEADS = 16
PAST_LEN = 16384
RWKV_LN_EPS = 1e-5 * HEAD_DIM
KPAD = 128
NEG = -1e30
ROW_TILE = 512
VMEM_LIMIT = 56 * 1024 * 1024


def _cparams(sem):
    return pltpu.CompilerParams(dimension_semantics=sem, vmem_limit_bytes=VMEM_LIMIT)


def _nt(a, b):
    return lax.dot_general(a, b, (((1,), (1,)), ((), ())), preferred_element_type=F32)


def _tn(a, b):
    return lax.dot_general(a, b, (((0,), (0,)), ((), ())), preferred_element_type=F32)


def _dot(a, b):
    return jnp.dot(a, b, preferred_element_type=F32)


def _split(x, terms):
    parts = []
    r = x
    for i in range(terms):
        p = r.astype(BF16)
        parts.append(p)
        if i + 1 < terms:
            r = r - p.astype(F32)
    return parts


def _dot_split_l(x, m, terms=3):
    acc = None
    for p in _split(x, terms):
        d = _dot(p, m)
        acc = d if acc is None else acc + d
    return acc


def _dot_split_r(m, x, terms=3):
    acc = None
    for p in _split(x, terms):
        d = _dot(m, p)
        acc = d if acc is None else acc + d
    return acc


def _pad_rows(x, rows):
    if x.shape[0] == rows:
        return x
    return jnp.concatenate([x, jnp.zeros((rows - x.shape[0], x.shape[1]), x.dtype)], axis=0)


def _idiv(x, n):
    return jnp.right_shift(x, int(math.log2(n)))


def _imod(x, n):
    return jnp.bitwise_and(x, n - 1)


def _silu(x):
    return x * jax.nn.sigmoid(x)


def _softplus(x):
    return jnp.maximum(x, 0.0) + jnp.log(1.0 + jnp.exp(-jnp.abs(x)))


def _rmsnorm_kernel(x_ref, g_ref, o_ref):
    x = x_ref[...]
    ms = jnp.mean(x * x, axis=-1, keepdims=True)
    o_ref[...] = x * lax.rsqrt(ms + EPS) * g_ref[...]


def _row_tile(m):
    tm = math.gcd(m, ROW_TILE)
    assert tm % 16 == 0, m
    return tm


def rmsnorm(x, g):
    m, d = x.shape
    tm = _row_tile(m)
    return pl.pallas_call(
        _rmsnorm_kernel,
        grid=(m // tm,),
        in_specs=[pl.BlockSpec((tm, d), lambda i: (i, 0)), pl.BlockSpec((1, d), lambda i: (0, 0))],
        out_specs=pl.BlockSpec((tm, d), lambda i: (i, 0)),
        out_shape=jax.ShapeDtypeStruct((m, d), F32),
        compiler_params=_cparams(("parallel",)),
        name="rmsnorm",
    )(x, g.reshape(1, d))


def _mm_kernel(a_ref, w_ref, *rest, has_res):
    if has_res:
        r_ref, o_ref, abf_ref = rest
    else:
        o_ref, abf_ref = rest

    @pl.when(pl.program_id(1) == 0)
    def _():
        abf_ref[...] = a_ref[...].astype(BF16)

    acc = _dot(abf_ref[...], w_ref[...])
    if has_res:
        acc = r_ref[...] + acc
    o_ref[...] = acc


def matmul(a, w, res=None, tn=512):
    m, k = a.shape
    n = w.shape[1]
    tn = min(tn, n)
    tm = _row_tile(m)
    in_specs = [pl.BlockSpec((tm, k), lambda i, j: (i, 0)), pl.BlockSpec((k, tn), lambda i, j: (0, j))]
    args = [a, w]
    if res is not None:
        in_specs.append(pl.BlockSpec((tm, tn), lambda i, j: (i, j)))
        args.append(res)
    return pl.pallas_call(
        functools.partial(_mm_kernel, has_res=res is not None),
        grid=(m // tm, n // tn),
        in_specs=in_specs,
        out_specs=pl.BlockSpec((tm, tn), lambda i, j: (i, j)),
        out_shape=jax.ShapeDtypeStruct((m, n), F32),
        scratch_shapes=[pltpu.VMEM((tm, k), BF16)],
        compiler_params=_cparams(("parallel", "arbitrary")),
        name="matmul_res" if res is not None else "matmul",
    )(*args)


def _mixmm_kernel(u_ref, p_ref, mu_ref, w_ref, o_ref, xm_ref):
    @pl.when(pl.program_id(2) == 0)
    def _():
        u = u_ref[...]
        xm_ref[...] = (u + (p_ref[...] - u) * mu_ref[0]).astype(BF16)

    o_ref[0] = _dot(xm_ref[...], w_ref[0])


def mix_matmul(u, prev, mu, w, tn=512):
    m, k = u.shape
    s, _, n = w.shape
    tn = min(tn, n)
    tm = _row_tile(m)
    return pl.pallas_call(
        _mixmm_kernel,
        grid=(m // tm, s, n // tn),
        in_specs=[
            pl.BlockSpec((tm, k), lambda i, si, j: (i, 0)),
            pl.BlockSpec((tm, k), lambda i, si, j: (i, 0)),
            pl.BlockSpec((1, 1, k), lambda i, si, j: (si, 0, 0)),
            pl.BlockSpec((1, k, tn), lambda i, si, j: (si, 0, j)),
        ],
        out_specs=pl.BlockSpec((1, tm, tn), lambda i, si, j: (si, i, j)),
        out_shape=jax.ShapeDtypeStruct((s, m, n), F32),
        scratch_shapes=[pltpu.VMEM((tm, k), BF16)],
        compiler_params=_cparams(("parallel", "arbitrary", "arbitrary")),
        name="mix_matmul",
    )(u, prev, mu.reshape(s, 1, k), w)


def _conv_silu(cur, car_ref, w_ref, b_ref, lq):
    car = car_ref[...]
    w = w_ref[...]
    n = cur.shape[1]
    rows8 = lax.broadcasted_iota(jnp.int32, (8, n), 0)
    acc = b_ref[...] + cur * w[SSD_CONV - 1:SSD_CONV]
    for s in range(1, SSD_CONV):
        rolled = pltpu.roll(cur, s, 0)
        first = jnp.where(rows8 < s, pltpu.roll(car, s, 0), rolled[:8])
        sh = first if lq == 8 else jnp.concatenate([first, rolled[8:]], axis=0)
        acc = acc + sh * w[SSD_CONV - 1 - s:SSD_CONV - s]
    car_ref[...] = cur[lq - 8:]
    return _silu(acc)


def _ssd_kernel(z_ref, x_ref, b_ref, c_ref, dt_ref, cix_ref, cib_ref, cic_ref, s0_ref,
                cwx_ref, cwb_ref, cwc_ref, cbx_ref, cbb_ref, cbc_ref, dtb_ref, alog_ref, d_ref, ng_ref,
                *rest, lq, lo, hi, nchunks):
    y_ref, sout_ref, carx, carb, carc, st = rest[-6:]
    c = pl.program_id(2)
    gw = SSD_HPG * HEAD_DIM

    @pl.when(c == 0)
    def _():
        carx[...] = cix_ref[0]
        carb[...] = cib_ref[0]
        carc[...] = cic_ref[0]
        st[...] = s0_ref[0, 0].reshape(gw, SSD_STATE)

    xc = _conv_silu(x_ref[...], carx, cwx_ref, cbx_ref, lq)
    bc = _conv_silu(b_ref[...], carb, cwb_ref, cbb_ref, lq)
    cc = _conv_silu(c_ref[...], carc, cwc_ref, cbc_ref, lq)

    pos = c * lq + lax.broadcasted_iota(jnp.int32, (lq, 1), 0)
    valid = (pos >= lo) & (pos < hi)
    lane = lax.broadcasted_iota(jnp.int32, (lq, 128), 1)
    dt = _softplus(dt_ref[...] + dtb_ref[0])
    dt = jnp.where(valid & (lane < SSD_HPG), dt, 0.0)
    la = dt * (-jnp.exp(alog_ref[0]))

    la_pad = _pad_rows(la, KPAD)
    ri = lax.broadcasted_iota(jnp.int32, (KPAD, KPAD), 0)
    ci = lax.broadcasted_iota(jnp.int32, (KPAD, KPAD), 1)
    tril = (ri >= ci).astype(BF16)
    triu = (ri <= ci).astype(BF16)
    acum_full = _dot_split_r(tril, la_pad)
    acum_t = _dot_split_l(la_pad.T, triu)
    acum = acum_full[:lq]
    a_end = acum_full[KPAD - 1:KPAD]
    dec_end = jnp.exp(a_end - acum_full)
    e_in = jnp.exp(acum)
    cd = jnp.exp(a_end)

    bcp = _pad_rows(bc, KPAD).astype(BF16)
    ccb = cc.astype(BF16)
    g_sc = _nt(ccb, bcp)
    st_old = st[...]
    y_in = _nt(ccb, st_old.astype(BF16))

    qi = lax.broadcasted_iota(jnp.int32, (lq, KPAD), 0)
    kj = lax.broadcasted_iota(jnp.int32, (lq, KPAD), 1)
    causal = kj <= qi
    lane_q = lax.broadcasted_iota(jnp.int32, (lq, 128), 1) < HEAD_DIM
    lane_k = lax.broadcasted_iota(jnp.int32, (KPAD, 128), 1) < HEAD_DIM
    row_k = lax.broadcasted_iota(jnp.int32, (128, 1), 0) < HEAD_DIM
    dvec = d_ref[0]

    ys = []
    for p in range(SSD_HPG // 2):
        h0, h1 = 2 * p, 2 * p + 1
        xp = xc[:, 128 * p:128 * (p + 1)]
        vp = xp * jnp.where(lane_q, dt[:, h0:h0 + 1], dt[:, h1:h1 + 1])
        vpp = _pad_rows(vp, KPAD)
        yp = y_in[:, 128 * p:128 * (p + 1)] * jnp.where(lane_q, e_in[:, h0:h0 + 1], e_in[:, h1:h1 + 1])
        yp = yp + xp * jnp.where(lane_q, dvec[:, h0:h0 + 1], dvec[:, h1:h1 + 1])
        for hh, h in ((0, h0), (1, h1)):
            seg = acum[:, h:h + 1] - acum_t[h:h + 1, :]
            lm = jnp.exp(jnp.where(causal, seg, NEG))
            pm = (g_sc * lm).astype(BF16)
            vm = jnp.where(lane_k if hh == 0 else jnp.logical_not(lane_k), vpp, 0.0).astype(BF16)
            yp = yp + _dot(pm, vm)
        ys.append(yp)
        vend = vpp * jnp.where(lane_k, dec_end[:, h0:h0 + 1], dec_end[:, h1:h1 + 1])
        upd = _tn(vend.astype(BF16), bcp)
        cdp = jnp.where(row_k, cd[:, h0:h0 + 1], cd[:, h1:h1 + 1])
        st[128 * p:128 * (p + 1), :] = st_old[128 * p:128 * (p + 1), :] * cdp + upd

    y = jnp.concatenate(ys, axis=1)
    y = y * _silu(z_ref[...])
    ms = jnp.mean(y * y, axis=-1, keepdims=True)
    y = y * lax.rsqrt(ms + EPS) * ng_ref[...]
    y_ref[...] = jnp.where(valid, y, 0.0).astype(BF16)

    @pl.when(c == nchunks - 1)
    def _():
        sout_ref[0, 0] = st[...].reshape(SSD_HPG, HEAD_DIM, SSD_STATE)


def ssd_scan(proj, dt_raw, conv_init, s0_all, layer, s_buf, conv_w, conv_b, dtb, alog, dskip, norm_g,
             *, bsz, tp, lq, lo, hi):
    m = proj.shape[0]
    state_spec = pl.BlockSpec((1, 1, SSD_HPG, HEAD_DIM, SSD_STATE), lambda b, g, c: (layer, b, g, 0, 0))
    extra_specs, extra_args, aliases = [], [], {}
    if s_buf is not None:
        extra_specs, extra_args, aliases = [pl.BlockSpec(memory_space=pl.ANY)], [s_buf], {19: 1}
    nch = tp // lq
    gw = SSD_HPG * HEAD_DIM
    d_inner = SSD_GROUPS * gw
    xoff = d_inner // gw
    boff = 2 * d_inner // SSD_STATE
    coff = boff + SSD_GROUPS
    cboff = d_inner // SSD_STATE
    row = lambda b, g, c: b * nch + c
    kern = functools.partial(_ssd_kernel, lq=lq, lo=lo, hi=hi, nchunks=nch)
    conv_b2 = conv_b.reshape(1, -1)
    return pl.pallas_call(
        kern,
        grid=(bsz, SSD_GROUPS, nch),
        in_specs=[
            pl.BlockSpec((lq, gw), lambda b, g, c: (row(b, g, c), g)),
            pl.BlockSpec((lq, gw), lambda b, g, c: (row(b, g, c), xoff + g)),
            pl.BlockSpec((lq, SSD_STATE), lambda b, g, c: (row(b, g, c), boff + g)),
            pl.BlockSpec((lq, SSD_STATE), lambda b, g, c: (row(b, g, c), coff + g)),
            pl.BlockSpec((lq, 128), lambda b, g, c: (row(b, g, c), g)),
            pl.BlockSpec((1, 8, gw), lambda b, g, c: (b, 0, g)),
            pl.BlockSpec((1, 8, SSD_STATE), lambda b, g, c: (b, 0, cboff + g)),
            pl.BlockSpec((1, 8, SSD_STATE), lambda b, g, c: (b, 0, cboff + SSD_GROUPS + g)),
            state_spec,
            pl.BlockSpec((SSD_CONV, gw), lambda b, g, c: (0, g)),
            pl.BlockSpec((SSD_CONV, SSD_STATE), lambda b, g, c: (0, cboff + g)),
            pl.BlockSpec((SSD_CONV, SSD_STATE), lambda b, g, c: (0, cboff + SSD_GROUPS + g)),
            pl.BlockSpec((1, gw), lambda b, g, c: (0, g)),
            pl.BlockSpec((1, SSD_STATE), lambda b, g, c: (0, cboff + g)),
            pl.BlockSpec((1, SSD_STATE), lambda b, g, c: (0, cboff + SSD_GROUPS + g)),
            pl.BlockSpec((1, 1, 128), lambda b, g, c: (g, 0, 0)),
            pl.BlockSpec((1, 1, 128), lambda b, g, c: (g, 0, 0)),
            pl.BlockSpec((1, 1, 128), lambda b, g, c: (g, 0, 0)),
            pl.BlockSpec((1, gw), lambda b, g, c: (0, g)),
        ] + extra_specs,
        out_specs=[
            pl.BlockSpec((lq, gw), lambda b, g, c: (row(b, g, c), g)),
            state_spec,
        ],
        out_shape=[
            jax.ShapeDtypeStruct((m, d_inner), BF16),
            jax.ShapeDtypeStruct(s0_all.shape, F32),
        ],
        scratch_shapes=[
            pltpu.VMEM((8, gw), F32), pltpu.VMEM((8, SSD_STATE), F32), pltpu.VMEM((8, SSD_STATE), F32),
            pltpu.VMEM((gw, SSD_STATE), F32),
        ],
        input_output_aliases=aliases,
        compiler_params=_cparams(("parallel", "parallel", "arbitrary")),
        name="ssd_scan",
    )(proj, proj, proj, proj, dt_raw, conv_init, conv_init, conv_init, s0_all,
      conv_w, conv_w, conv_w, conv_b2, conv_b2, conv_b2, dtb, alog, dskip, norm_g.reshape(1, -1), *extra_args)


def _ret_kernel(q_ref, k_ref, v_ref, g_ref, cos_ref, sin_ref, lg_ref, s0_ref, ng_ref,
                y_ref, sout_ref, st, *, lq, lo, hi, nchunks):
    c = pl.program_id(2)

    @pl.when(c == 0)
    def _():
        st[...] = s0_ref[0, 0]

    lg = lg_ref[0][0:1, 0:1]
    nv = float(hi - lo)

    def count(p):
        return jnp.clip((p + 1 - lo).astype(F32), 0.0, nv)

    base = c * lq
    pos_i = base + lax.broadcasted_iota(jnp.int32, (lq, 1), 0)
    valid = (pos_i >= lo) & (pos_i < hi)
    cnt_i = count(pos_i)
    cnt_j = count(base + lax.broadcasted_iota(jnp.int32, (1, KPAD), 1))
    cnt_jc = count(base + lax.broadcasted_iota(jnp.int32, (KPAD, 1), 0))
    cnt0 = count(base - 1 + jnp.zeros((1, 1), jnp.int32))
    cnt_end = count(base + lq - 1 + jnp.zeros((1, 1), jnp.int32))

    cos = cos_ref[...]
    sin = sin_ref[...]
    half = RET_QK // 2

    def rot(x):
        x1, x2 = x[:, :half], x[:, half:]
        return jnp.concatenate([x1 * cos - x2 * sin, x1 * sin + x2 * cos], axis=1)

    qr = rot(q_ref[...]).astype(BF16)
    kr = jnp.where(valid, rot(k_ref[...]) * (RET_QK ** -0.5), 0.0)
    v = jnp.where(valid, v_ref[...], 0.0)
    krp = _pad_rows(kr, KPAD).astype(BF16)
    vp = _pad_rows(v, KPAD)

    sc = _nt(qr, krp)
    qi = lax.broadcasted_iota(jnp.int32, (lq, KPAD), 0)
    kj = lax.broadcasted_iota(jnp.int32, (lq, KPAD), 1)
    dm = jnp.exp(jnp.where(kj <= qi, lg * (cnt_i - cnt_j), NEG))
    y = _dot((sc * dm).astype(BF16), vp.astype(BF16))
    s_old = st[...]
    y = y + _nt(qr, s_old.astype(BF16)) * jnp.exp(lg * (cnt_i - cnt0))
    vend = vp * jnp.exp(lg * (cnt_end - cnt_jc))
    st[...] = s_old * jnp.exp(lg * (cnt_end - cnt0)) + _tn(vend.astype(BF16), krp)

    ms = jnp.mean(y * y, axis=-1, keepdims=True)
    y = y * lax.rsqrt(ms + EPS) * ng_ref[...] * _silu(g_ref[...])
    y_ref[...] = jnp.where(valid, y, 0.0).astype(BF16)

    @pl.when(c == nchunks - 1)
    def _():
        sout_ref[0, 0] = st[...]


def ret_scan(proj, cos, sin, lg, s0, norm_g, *, bsz, tp, lq, lo, hi):
    m = proj.shape[0]
    nch = tp // lq
    d_inner = RET_HEADS * RET_V
    koff = RET_HEADS
    voff = 2 * RET_HEADS * RET_QK // RET_V
    goff = voff + RET_HEADS
    row = lambda b, h, c: b * nch + c
    kern = functools.partial(_ret_kernel, lq=lq, lo=lo, hi=hi, nchunks=nch)
    return pl.pallas_call(
        kern,
        grid=(bsz, RET_HEADS, nch),
        in_specs=[
            pl.BlockSpec((lq, RET_QK), lambda b, h, c: (row(b, h, c), h)),
            pl.BlockSpec((lq, RET_QK), lambda b, h, c: (row(b, h, c), koff + h)),
            pl.BlockSpec((lq, RET_V), lambda b, h, c: (row(b, h, c), voff + h)),
            pl.BlockSpec((lq, RET_V), lambda b, h, c: (row(b, h, c), goff + h)),
            pl.BlockSpec((lq, RET_QK // 2), lambda b, h, c: (c, 0)),
            pl.BlockSpec((lq, RET_QK // 2), lambda b, h, c: (c, 0)),
            pl.BlockSpec((1, 8, 128), lambda b, h, c: (h, 0, 0)),
            pl.BlockSpec((1, 1, RET_V, RET_QK), lambda b, h, c: (b, h, 0, 0)),
            pl.BlockSpec((1, RET_V), lambda b, h, c: (0, h)),
        ],
        out_specs=[
            pl.BlockSpec((lq, RET_V), lambda b, h, c: (row(b, h, c), h)),
            pl.BlockSpec((1, 1, RET_V, RET_QK), lambda b, h, c: (b, h, 0, 0)),
        ],
        out_shape=[
            jax.ShapeDtypeStruct((m, d_inner), BF16),
            jax.ShapeDtypeStruct((bsz, RET_HEADS, RET_V, RET_QK), F32),
        ],
        scratch_shapes=[pltpu.VMEM((RET_V, RET_QK), F32)],
        compiler_params=_cparams(("parallel", "parallel", "arbitrary")),
        name="ret_scan",
    )(proj, proj, proj, proj, cos, sin, lg, s0, norm_g.reshape(1, -1))


def _rwkv_kernel(r_ref, k_ref, v_ref, g_ref, lw_ref, la_ref, bw_ref, ba_ref, par_ref, s0_ref,
                 y_ref, sout_ref, st, *, lo, hi, nchunks, ngrp):
    c = pl.program_id(2)
    w4 = RWKV_HPB * HEAD_DIM
    r2 = lax.broadcasted_iota(jnp.int32, (w4, w4), 0)
    c2 = lax.broadcasted_iota(jnp.int32, (w4, w4), 1)
    blk = _idiv(r2, HEAD_DIM) == _idiv(c2, HEAD_DIM)

    @pl.when(c == 0)
    def _():
        tile = (lax.broadcasted_iota(jnp.int32, (HEAD_DIM, w4), 0)
                == _imod(lax.broadcasted_iota(jnp.int32, (HEAD_DIM, w4), 1), HEAD_DIM)).astype(BF16)
        for gi in range(ngrp):
            s0 = s0_ref[0, gi * RWKV_HPB:(gi + 1) * RWKV_HPB].reshape(w4, HEAD_DIM)
            st[gi] = jnp.where(blk, _dot_split_l(s0, tile), 0.0)

    s_new = [None] * ngrp
    gens = [_rwkv_group(gi, r_ref, k_ref, v_ref, g_ref, lw_ref, la_ref, bw_ref, ba_ref, par_ref,
                        y_ref, st, s_new, lo=lo, hi=hi) for gi in range(ngrp)]
    for _ in zip(*gens):
        pass

    @pl.when(c == nchunks - 1)
    def _():
        tile_t = (_imod(lax.broadcasted_iota(jnp.int32, (w4, HEAD_DIM), 0), HEAD_DIM)
                  == lax.broadcasted_iota(jnp.int32, (w4, HEAD_DIM), 1)).astype(BF16)
        for gi in range(ngrp):
            sout_ref[0, gi * RWKV_HPB:(gi + 1) * RWKV_HPB] = _dot_split_l(s_new[gi], tile_t).reshape(
                RWKV_HPB, HEAD_DIM, HEAD_DIM)


def _rwkv_group(gi, r_ref, k_ref, v_ref, g_ref, lw_ref, la_ref, bw_ref, ba_ref, par_ref, y_ref, st, out,
                *, lo, hi):
    c = pl.program_id(2)
    cs = RWKV_CHUNK
    w4 = RWKV_HPB * HEAD_DIM
    sl = slice(gi * w4, (gi + 1) * w4)

    r2 = lax.broadcasted_iota(jnp.int32, (w4, w4), 0)
    c2 = lax.broadcasted_iota(jnp.int32, (w4, w4), 1)
    blk = _idiv(r2, HEAD_DIM) == _idiv(c2, HEAD_DIM)
    ones_bd = blk.astype(BF16)

    def segsum(x):
        return _dot_split_l(x, ones_bd, terms=2)

    par = par_ref[:, sl]
    w0, a0, k_k, k_a, r_k, ln_g, ln_b = (par[i:i + 1] for i in range(7))

    r = r_ref[0, :, sl]
    k = k_ref[0, :, sl]
    v = v_ref[0, :, sl]
    g = g_ref[0, :, sl]
    w_raw = w0 + _dot(jnp.tanh(lw_ref[0]).astype(BF16), bw_ref[0, :, sl])
    a = jax.nn.sigmoid(a0 + _dot(la_ref[0].astype(BF16), ba_ref[0, :, sl]))

    ti = lax.broadcasted_iota(jnp.int32, (cs, 1), 0)
    pos = c * cs + ti
    valid = (pos >= lo) & (pos < hi)

    lw = -jnp.exp(-_softplus(-w_raw) - 0.5)
    kk = k * k_k
    kk = kk / jnp.maximum(jnp.sqrt(segsum(kk * kk)), 1e-12)
    yield
    kp = k * (1.0 + (a - 1.0) * k_a)
    lw = jnp.where(valid, lw, 0.0)
    kk = jnp.where(valid, kk, 0.0)
    kp = jnp.where(valid, kp, 0.0)
    vm = jnp.where(valid, v, 0.0)

    tri = (lax.broadcasted_iota(jnp.int32, (cs, cs), 0) >= lax.broadcasted_iota(jnp.int32, (cs, cs), 1)).astype(BF16)
    cw = _dot_split_r(tri, lw)
    yield
    cwl = cw[cs - 1:cs]
    wt = jnp.exp(cw)
    wi = jnp.exp(-cw)
    wend = jnp.exp(cwl - cw)
    b = kk * a
    at = -kk * jnp.exp(cw - lw)
    rt = r * wt

    def bd(x):
        return jnp.where(blk, jnp.concatenate([x] * RWKV_HPB, axis=0), 0.0).astype(BF16)

    lhs = jnp.concatenate([bd(at), bd(rt)], axis=0)
    rhs = jnp.concatenate([bd(b * wi), bd(kp * wi)], axis=0)
    sc = _nt(lhs, rhs)
    yield
    tt = _imod(r2, cs)
    jj = _imod(c2, cs)
    strict = blk & (tt > jj)
    incl = blk & (tt >= jj)
    mab = jnp.where(strict, sc[:w4, :w4], 0.0)
    mak = jnp.where(strict, sc[:w4, w4:], 0.0)
    nrb = jnp.where(incl, sc[w4:, :w4], 0.0)
    nrk = jnp.where(incl, sc[w4:, w4:], 0.0)

    tinv = (r2 == c2).astype(F32) + jnp.where(
        (_idiv(r2, 2) == _idiv(c2, 2)) & (_imod(tt, 2) == 1) & (_imod(jj, 2) == 0), mab, 0.0)
    msz = 2
    while msz < cs:
        off = ((_idiv(r2, 2 * msz) == _idiv(c2, 2 * msz)) & (_imod(tt, 2 * msz) >= msz)
               & (_imod(jj, 2 * msz) < msz))
        tb = tinv.astype(BF16)
        tno = _dot(tb, jnp.where(off, mab, 0.0).astype(BF16)).astype(BF16)
        yield
        tinv = tinv + _dot(tno, tb)
        yield
        msz *= 2

    s_old = st[gi]
    x = _nt(lhs, s_old.astype(BF16))
    yield
    vbd = bd(vm)
    u = _dot(tinv.astype(BF16), (x[:w4] + _dot(mak.astype(BF16), vbd)).astype(BF16))
    yield
    ub = u.astype(BF16)
    yb = x[w4:] + _dot(nrb.astype(BF16), ub) + _dot(nrk.astype(BF16), vbd)
    yield
    s_new = s_old * wt[cs - 1:cs] + _tn(jnp.concatenate([ub, vbd], axis=0),
                                        jnp.concatenate([bd(b * wend), bd(kp * wend)], axis=0))
    st[gi] = s_new
    out[gi] = s_new
    yield

    y = yb[0:cs] + yb[cs:2 * cs] + yb[2 * cs:3 * cs] + yb[3 * cs:4 * cs]
    inv = 1.0 / HEAD_DIM
    yc = y - segsum(y) * inv
    yield
    y = yc * lax.rsqrt(segsum(yc * yc) * inv + RWKV_LN_EPS) * ln_g + ln_b
    y = (y + segsum(r * kp * r_k) * v) * _silu(g)
    y_ref[:, sl] = jnp.where(valid, y, 0.0).astype(BF16)
    yield


def rwkv_scan(rkvg, lora1, lora_b, par, s0, *, bsz, tp, lo, hi):
    _, m, e = rkvg.shape
    cs = RWKV_CHUNK
    nch = tp // cs
    heads = e // HEAD_DIM
    ngrp = RWKV_GROUPS_PER_STEP
    hb = RWKV_HPB * ngrp
    wb = hb * HEAD_DIM
    row = lambda b, h, c: b * nch + c
    kern = functools.partial(_rwkv_kernel, lo=lo, hi=hi, nchunks=nch, ngrp=ngrp)
    proj_spec = lambda s: pl.BlockSpec((1, cs, wb), lambda b, h, c: (s, row(b, h, c), h))
    return pl.pallas_call(
        kern,
        grid=(bsz, heads // hb, nch),
        in_specs=[
            proj_spec(0), proj_spec(1), proj_spec(2), proj_spec(3),
            pl.BlockSpec((1, cs, RWKV_LORA_PAD), lambda b, h, c: (0, row(b, h, c), 0)),
            pl.BlockSpec((1, cs, RWKV_LORA_PAD), lambda b, h, c: (1, row(b, h, c), 0)),
            pl.BlockSpec((1, RWKV_LORA_PAD, wb), lambda b, h, c: (0, 0, h)),
            pl.BlockSpec((1, RWKV_LORA_PAD, wb), lambda b, h, c: (1, 0, h)),
            pl.BlockSpec((8, wb), lambda b, h, c: (0, h)),
            pl.BlockSpec((1, hb, HEAD_DIM, HEAD_DIM), lambda b, h, c: (b, h, 0, 0)),
        ],
        out_specs=[
            pl.BlockSpec((cs, wb), lambda b, h, c: (row(b, h, c), h)),
            pl.BlockSpec((1, hb, HEAD_DIM, HEAD_DIM), lambda b, h, c: (b, h, 0, 0)),
        ],
        out_shape=[
            jax.ShapeDtypeStruct((m, e), BF16),
            jax.ShapeDtypeStruct((bsz, heads, HEAD_DIM, HEAD_DIM), F32),
        ],
        scratch_shapes=[pltpu.VMEM((ngrp, RWKV_HPB * HEAD_DIM, RWKV_HPB * HEAD_DIM), F32)],
        compiler_params=_cparams(("parallel", "parallel", "arbitrary")),
        name="rwkv_scan",
    )(rkvg, rkvg, rkvg, rkvg, lora1, lora1, lora_b, lora_b, par, s0)


def _rwkv_short_kernel(r_ref, k_ref, v_ref, g_ref, lw_ref, la_ref, bw_ref, ba_ref, par_ref, s0_ref,
                       ones_ref, tile_ref, y_ref, sout_ref, *, nvalid, rows_out):
    ct = 8
    hp = RWKV_SHORT_HEADS
    wl = hp * HEAD_DIM
    nr = hp * ct
    ones_bd = ones_ref[...]
    tile_t = tile_ref[...]

    def segsum(x):
        return jnp.concatenate(
            [_dot_split_l(x[:, 256 * j:256 * (j + 1)], ones_bd, terms=2) for j in range(wl // 256)], axis=1)

    par = par_ref[...]
    w0, a0, k_k, k_a, r_k, ln_g, ln_b = (par[i:i + 1] for i in range(7))
    r, k, v, g = r_ref[0], k_ref[0], v_ref[0], g_ref[0]
    w_raw = w0 + _dot(jnp.tanh(lw_ref[0]).astype(BF16), bw_ref[0])
    a = jax.nn.sigmoid(a0 + _dot(la_ref[0].astype(BF16), ba_ref[0]))

    ti = lax.broadcasted_iota(jnp.int32, (ct, 1), 0)
    valid = ti < nvalid
    lw = jnp.where(valid, -jnp.exp(-_softplus(-w_raw) - 0.5), 0.0)
    kk = k * k_k
    kk = jnp.where(valid, kk / jnp.maximum(jnp.sqrt(segsum(kk * kk)), 1e-12), 0.0)
    kp = jnp.where(valid, k * (1.0 + (a - 1.0) * k_a), 0.0)
    vm = jnp.where(valid, v, 0.0)

    cw = lw
    for s in (1, 2, 4):
        cw = cw + jnp.where(ti >= s, pltpu.roll(cw, s, 0), 0.0)
    cwl = cw[ct - 1:ct]
    wend = jnp.exp(cwl - cw)
    wi = jnp.exp(-cw)
    b = kk * a
    at = -kk * jnp.exp(cw - lw)
    rt = r * jnp.exp(cw)

    rr = lax.broadcasted_iota(jnp.int32, (nr, wl), 0)
    cc = lax.broadcasted_iota(jnp.int32, (nr, wl), 1)
    blk = _idiv(rr, ct) == _idiv(cc, HEAD_DIM)

    def bd(x):
        return jnp.where(blk, jnp.concatenate([x] * hp, axis=0), 0.0).astype(BF16)

    lhs = jnp.concatenate([bd(at), bd(rt)], axis=0)
    rhs = jnp.concatenate([bd(b * wi), bd(kp * wi)], axis=0)
    sc = _nt(lhs, rhs)
    ri = lax.broadcasted_iota(jnp.int32, (2 * nr, 2 * nr), 0)
    ci = lax.broadcasted_iota(jnp.int32, (2 * nr, 2 * nr), 1)
    same = _idiv(_imod(ri, nr), ct) == _idiv(_imod(ci, nr), ct)
    tt = _imod(ri, ct)
    jj = _imod(ci, ct)
    sc = jnp.where(same & (tt + (ri >= nr).astype(jnp.int32) > jj), sc, 0.0)
    mab, mak, nrb, nrk = sc[:nr, :nr], sc[:nr, nr:], sc[nr:, :nr], sc[nr:, nr:]

    r1 = lax.broadcasted_iota(jnp.int32, (nr, nr), 0)
    c1 = lax.broadcasted_iota(jnp.int32, (nr, nr), 1)
    t1 = _imod(r1, ct)
    j1 = _imod(c1, ct)
    tinv = (r1 == c1).astype(F32) + jnp.where(
        (_idiv(r1, 2) == _idiv(c1, 2)) & (_imod(t1, 2) == 1) & (_imod(j1, 2) == 0), mab, 0.0)
    msz = 2
    while msz < nvalid:
        off = ((_idiv(r1, 2 * msz) == _idiv(c1, 2 * msz)) & (_imod(t1, 2 * msz) >= msz)
               & (_imod(j1, 2 * msz) < msz))
        tb = tinv.astype(BF16)
        tinv = tinv + _dot(_dot(tb, jnp.where(off, mab, 0.0).astype(BF16)).astype(BF16), tb)
        msz *= 2

    s_old = s0_ref[0].reshape(wl, HEAD_DIM)
    lhs_rows = _dot(lhs, tile_t).astype(BF16)
    x = _nt(lhs_rows, s_old.astype(BF16))
    xa = jnp.where(blk, x[:nr], 0.0)
    xr = jnp.where(blk, x[nr:], 0.0)
    vbd = bd(vm)
    u = _dot(tinv.astype(BF16), (xa + _dot(mak.astype(BF16), vbd)).astype(BF16))
    uv = jnp.concatenate([u.astype(BF16), vbd], axis=0)
    yb = xr + _dot(jnp.concatenate([nrb, nrk], axis=1).astype(BF16), uv)
    bk_rows = _dot(jnp.concatenate([bd(b * wend), bd(kp * wend)], axis=0), tile_t).astype(BF16)
    ds = _tn(uv, bk_rows)

    r16 = lax.broadcasted_iota(jnp.int32, (hp, wl), 0)
    c16 = lax.broadcasted_iota(jnp.int32, (hp, wl), 1)
    wc = jnp.exp(cwl)
    wc_rows = _dot_split_l(jnp.where(r16 == _idiv(c16, HEAD_DIM), wc, 0.0), tile_t)
    rsel = lax.broadcasted_iota(jnp.int32, (hp, HEAD_DIM), 0)
    for h in range(hp):
        rs = slice(h * HEAD_DIM, (h + 1) * HEAD_DIM)
        wc_h = jnp.sum(jnp.where(rsel == h, wc_rows, 0.0), axis=0, keepdims=True)
        sout_ref[0, h] = s_old[rs] * wc_h + ds[rs]

    y = yb[0:ct]
    for h in range(1, hp):
        y = y + yb[h * ct:(h + 1) * ct]
    inv = 1.0 / HEAD_DIM
    yc = y - segsum(y) * inv
    y = yc * lax.rsqrt(segsum(yc * yc) * inv + RWKV_LN_EPS) * ln_g + ln_b
    y = (y + segsum(r * kp * r_k) * v) * _silu(g)
    y = jnp.where(valid, y, 0.0)
    y_ref[...] = _pad_rows(y, rows_out).astype(BF16)


def rwkv_short(rkvg, lora1, lora_b, par, s0, *, bsz, tp, hi):
    _, m, e = rkvg.shape
    heads = e // HEAD_DIM
    hp = RWKV_SHORT_HEADS
    wl = hp * HEAD_DIM
    rb = tp // 8
    w4 = RWKV_HPB * HEAD_DIM
    ones_bd = (jnp.arange(w4)[:, None] // HEAD_DIM == jnp.arange(w4)[None, :] // HEAD_DIM).astype(BF16)
    tile_t = (jnp.arange(wl)[:, None] % HEAD_DIM == jnp.arange(HEAD_DIM)[None, :]).astype(BF16)
    kern = functools.partial(_rwkv_short_kernel, nvalid=hi, rows_out=tp)
    proj_spec = lambda s: pl.BlockSpec((1, 8, wl), lambda b, h: (s, b * rb, h))
    return pl.pallas_call(
        kern,
        grid=(bsz, heads // hp),
        in_specs=[
            proj_spec(0), proj_spec(1), proj_spec(2), proj_spec(3),
            pl.BlockSpec((1, 8, RWKV_LORA_PAD), lambda b, h: (0, b * rb, 0)),
            pl.BlockSpec((1, 8, RWKV_LORA_PAD), lambda b, h: (1, b * rb, 0)),
            pl.BlockSpec((1, RWKV_LORA_PAD, wl), lambda b, h: (0, 0, h)),
            pl.BlockSpec((1, RWKV_LORA_PAD, wl), lambda b, h: (1, 0, h)),
            pl.BlockSpec((8, wl), lambda b, h: (0, h)),
            pl.BlockSpec((1, hp, HEAD_DIM, HEAD_DIM), lambda b, h: (b, h, 0, 0)),
            pl.BlockSpec((w4, w4), lambda b, h: (0, 0)),
            pl.BlockSpec((wl, HEAD_DIM), lambda b, h: (0, 0)),
        ],
        out_specs=[
            pl.BlockSpec((tp, wl), lambda b, h: (b, h)),
            pl.BlockSpec((1, hp, HEAD_DIM, HEAD_DIM), lambda b, h: (b, h, 0, 0)),
        ],
        out_shape=[
            jax.ShapeDtypeStruct((m, e), BF16),
            jax.ShapeDtypeStruct((bsz, heads, HEAD_DIM, HEAD_DIM), F32),
        ],
        compiler_params=_cparams(("parallel", "parallel")),
        name="rwkv_short",
    )(rkvg, rkvg, rkvg, rkvg, lora1, lora1, lora_b, lora_b, par, s0, ones_bd, tile_t)


def _prep_weights(norm_g, ssd_w_in, ssd_dt_bias, ssd_a_log, ssd_d, ssd_w_out,
                  rwkv_w_rkvg, rwkv_w_lora_a, rwkv_w_lora_b, rwkv_a_lora_a, rwkv_a_lora_b,
                  rwkv_w0, rwkv_a0, rwkv_k_k, rwkv_k_a, rwkv_r_k, rwkv_ln_g, rwkv_ln_b, rwkv_w_out,
                  ret_w_in, ret_w_out):
    d_inner = ssd_w_out.shape[1]
    n_main = d_inner + d_inner + 2 * SSD_GROUPS * SSD_STATE
    ns = ssd_w_in.shape[0]
    d_model = ssd_w_in.shape[1]

    def head_lanes(p):
        p = p.reshape(ns, SSD_GROUPS, 1, SSD_HPG)
        return jnp.pad(p, ((0, 0), (0, 0), (0, 0), (0, 128 - SSD_HPG)))

    w_dt = ssd_w_in[:, :, n_main:].reshape(ns, d_model, SSD_GROUPS, SSD_HPG)
    w_dt = jnp.pad(w_dt, ((0, 0), (0, 0), (0, 0), (0, 128 - SSD_HPG))).reshape(ns, d_model, SSD_GROUPS * 128)
    rank = rwkv_w_lora_a.shape[2]
    lora_a = jnp.stack([rwkv_w_lora_a, rwkv_a_lora_a], axis=1)
    lora_a = jnp.pad(lora_a, ((0, 0), (0, 0), (0, 0), (0, RWKV_LORA_PAD - rank)))
    lora_b = jnp.stack([rwkv_w_lora_b, rwkv_a_lora_b], axis=1)
    lora_b = jnp.pad(lora_b, ((0, 0), (0, 0), (0, RWKV_LORA_PAD - rank), (0, 0)))
    nr = rwkv_w0.shape[0]
    par = jnp.stack([rwkv_w0, rwkv_a0, rwkv_k_k, rwkv_k_a, rwkv_r_k.reshape(nr, -1), rwkv_ln_g, rwkv_ln_b,
                     jnp.zeros_like(rwkv_w0)], axis=1)
    return dict(
        ssd_w_main=ssd_w_in[:, :, :n_main].astype(BF16), ssd_w_dt=w_dt.astype(BF16),
        ssd_dtb=head_lanes(ssd_dt_bias), ssd_alog=head_lanes(ssd_a_log), ssd_dskip=head_lanes(ssd_d),
        ssd_w_out=ssd_w_out.astype(BF16),
        rwkv_w=rwkv_w_rkvg.astype(BF16), rwkv_lora_a=lora_a.astype(BF16), rwkv_lora_b=lora_b.astype(BF16),
        rwkv_par=par, rwkv_w_out=rwkv_w_out.astype(BF16),
        ret_w_in=ret_w_in.astype(BF16), ret_w_out=ret_w_out.astype(BF16),
    )


def _trunk(h, conv_st, ssd_st, shift_st, wkv_st, ret_st, pos, *, bsz, tp, lq, lo, hi, depth,
           norm_g, final_norm_g, wts, ssd_conv_w, ssd_conv_b, ssd_norm_g, rwkv_mu, ret_norm_g):
    d_model = h.shape[1]
    geo = dict(bsz=bsz, tp=tp, lo=lo, hi=hi)
    new_conv, new_shift, new_wkv, new_ret = [], [], [], []
    new_ssd = None

    half = RET_QK // 2
    inv_freq = 1.0 / (RET_THETA_BASE ** jnp.linspace(0.0, 1.0, half, dtype=F32))
    ang = pos.astype(F32)[:, None] * inv_freq
    cos, sin = jnp.cos(ang), jnp.sin(ang)
    log_gamma = jnp.log1p(-jnp.exp2(-5.0 - jnp.arange(RET_HEADS, dtype=F32)))
    lg = jnp.broadcast_to(log_gamma[:, None, None], (RET_HEADS, 8, 128))

    for layer in range(depth):
        kind, j = layer % 3, layer // 3
        u = rmsnorm(h, norm_g[layer])
        if kind == 0:
            proj = matmul(u, wts["ssd_w_main"][j])
            dt_raw = matmul(u, wts["ssd_w_dt"][j])
            conv_init = jnp.pad(conv_st[j], ((0, 0), (8 - (SSD_CONV - 1), 0), (0, 0)))
            y, new_ssd = ssd_scan(proj, dt_raw, conv_init, ssd_st, j, new_ssd, ssd_conv_w[j], ssd_conv_b[j],
                                  wts["ssd_dtb"][j], wts["ssd_alog"][j], wts["ssd_dskip"][j], ssd_norm_g[j],
                                  lq=lq, **geo)
            d_inner = y.shape[1]
            xbc = proj.reshape(bsz, tp, -1)[:, hi - (SSD_CONV - 1):hi, d_inner:]
            new_conv.append(xbc)
            h = matmul(y, wts["ssd_w_out"][j], res=h)
        elif kind == 1:
            u3 = u.reshape(bsz, tp, d_model)
            prev = jnp.concatenate([shift_st[j][:, None, :], u3[:, :-1]], axis=1).reshape(bsz * tp, d_model)
            rkvg = mix_matmul(u, prev, rwkv_mu[j][:4], wts["rwkv_w"][j])
            lora1 = mix_matmul(u, prev, rwkv_mu[j][4:], wts["rwkv_lora_a"][j])
            if lo == 0 and hi <= 8 and tp <= 16:
                y, s_new = rwkv_short(rkvg, lora1, wts["rwkv_lora_b"][j], wts["rwkv_par"][j], wkv_st[j],
                                      bsz=bsz, tp=tp, hi=hi)
            else:
                y, s_new = rwkv_scan(rkvg, lora1, wts["rwkv_lora_b"][j], wts["rwkv_par"][j], wkv_st[j], **geo)
            new_shift.append(u3[:, hi - 1])
            new_wkv.append(s_new)
            h = matmul(y, wts["rwkv_w_out"][j], res=h)
        else:
            proj = matmul(u, wts["ret_w_in"][j])
            y, s_new = ret_scan(proj, cos, sin, lg, ret_st[j], ret_norm_g[j], lq=lq, **geo)
            new_ret.append(s_new)
            h = matmul(y, wts["ret_w_out"][j], res=h)
    y = rmsnorm(h, final_norm_g)
    return (y, jnp.stack(new_conv), new_ssd, jnp.stack(new_shift), jnp.stack(new_wkv), jnp.stack(new_ret))


def kernel(x_prompt, x_sample, state_ssd_conv, state_ssd, state_rwkv_shift, state_rwkv_wkv, state_ret, meta_tokens, norm_g, final_norm_g, ssd_w_in, ssd_conv_w, ssd_conv_b, ssd_dt_bias, ssd_a_log, ssd_d, ssd_norm_g, ssd_w_out, rwkv_mu, rwkv_w_rkvg, rwkv_w0, rwkv_w_lora_a, rwkv_w_lora_b, rwkv_a0, rwkv_a_lora_a, rwkv_a_lora_b, rwkv_k_k, rwkv_k_a, rwkv_r_k, rwkv_ln_g, rwkv_ln_b, rwkv_w_out, ret_w_in, ret_norm_g, ret_w_out):
    depth = norm_g.shape[0]
    d_model = x_prompt.shape[2]
    wts = _prep_weights(norm_g, ssd_w_in, ssd_dt_bias, ssd_a_log, ssd_d, ssd_w_out,
                        rwkv_w_rkvg, rwkv_w_lora_a, rwkv_w_lora_b, rwkv_a_lora_a, rwkv_a_lora_b,
                        rwkv_w0, rwkv_a0, rwkv_k_k, rwkv_k_a, rwkv_r_k, rwkv_ln_g, rwkv_ln_b, rwkv_w_out,
                        ret_w_in, ret_w_out)
    common = dict(depth=depth, norm_g=norm_g, final_norm_g=final_norm_g, wts=wts, ssd_conv_w=ssd_conv_w,
                  ssd_conv_b=ssd_conv_b, ssd_norm_g=ssd_norm_g, rwkv_mu=rwkv_mu, ret_norm_g=ret_norm_g)

    bp, seq, _ = x_prompt.shape
    lq_p = 128
    lo_p = lq_p - N_META
    tp_p = lo_p + N_META + seq
    h_p = jnp.concatenate([jnp.zeros((bp, lo_p, d_model), F32),
                           jnp.broadcast_to(meta_tokens[None], (bp, N_META, d_model)), x_prompt], axis=1)
    zeros_like_b = lambda s: jnp.zeros((s.shape[0], bp) + s.shape[2:], F32)
    pos_p = jnp.maximum(jnp.arange(tp_p) - lo_p, 0)
    outs_p = _trunk(h_p.reshape(bp * tp_p, d_model), zeros_like_b(state_ssd_conv), zeros_like_b(state_ssd),
                    zeros_like_b(state_rwkv_shift), zeros_like_b(state_rwkv_wkv), zeros_like_b(state_ret), pos_p,
                    bsz=bp, tp=tp_p, lq=lq_p, lo=lo_p, hi=tp_p, **common)
    y_prompt = outs_p[0].reshape(bp, tp_p, d_model)[:, lo_p + N_META:]

    bs, ds, _ = x_sample.shape
    tp_s = 16
    h_s = jnp.concatenate([x_sample, jnp.zeros((bs, tp_s - ds, d_model), F32)], axis=1)
    pos_s = PAST_LEN + jnp.arange(tp_s)
    outs_s = _trunk(h_s.reshape(bs * tp_s, d_model), state_ssd_conv, state_ssd, state_rwkv_shift, state_rwkv_wkv,
                    state_ret, pos_s, bsz=bs, tp=tp_s, lq=tp_s, lo=0, hi=ds, **common)
    y_sample = outs_s[0].reshape(bs, tp_s, d_model)[:, :ds]

    return (y_prompt, y_sample) + tuple(outs_p[1:]) + tuple(outs_s[1:])
```

```python
import functools
import math

import jax
import jax.numpy as jnp
from jax import lax
from jax.experimental import pallas as pl
from jax.experimental.pallas import tpu as pltpu

F32 = jnp.float32
BF16 = jnp.bfloat16

EPS = 1e-6
N_META = 16
HEAD_DIM = 64
SSD_STATE = 128
SSD_GROUPS = 8
SSD_HPG = 8
SSD_CONV = 4
RET_HEADS = 8
RET_QK = 256
RET_V = 512
RET_THETA_BASE = 10000.0
RWKV_LORA_PAD = 128
RWKV_CHUNK = 64
RWKV_HPB = 4
RWKV_GROUPS_PER_STEP = 4
RWKV_SHORT_HEADS = 16
RWKV_LANE_HEADS = 2
LANES = 128
PAST_LEN = 16384
RWKV_LN_EPS = 1e-5 * HEAD_DIM
KPAD = 128
NEG = -1e30
ROW_TILE = 512
VMEM_LIMIT = 56 * 1024 * 1024


def _cparams(sem):
    return pltpu.CompilerParams(dimension_semantics=sem, vmem_limit_bytes=VMEM_LIMIT)


def _nt(a, b):
    return lax.dot_general(a, b, (((1,), (1,)), ((), ())), preferred_element_type=F32)


def _tn(a, b):
    return lax.dot_general(a, b, (((0,), (0,)), ((), ())), preferred_element_type=F32)


def _dot(a, b):
    return jnp.dot(a, b, preferred_element_type=F32)


def _split(x, terms):
    parts = []
    r = x
    for i in range(terms):
        p = r.astype(BF16)
        parts.append(p)
        if i + 1 < terms:
            r = r - p.astype(F32)
    return parts


def _dot_split_l(x, m, terms=3):
    acc = None
    for p in _split(x, terms):
        d = _dot(p, m)
        acc = d if acc is None else acc + d
    return acc


def _dot_split_r(m, x, terms=3):
    acc = None
    for p in _split(x, terms):
        d = _dot(m, p)
        acc = d if acc is None else acc + d
    return acc


def _pad_rows(x, rows):
    if x.shape[0] == rows:
        return x
    return jnp.concatenate([x, jnp.zeros((rows - x.shape[0], x.shape[1]), x.dtype)], axis=0)


def _idiv(x, n):
    return jnp.right_shift(x, int(math.log2(n)))


def _imod(x, n):
    return jnp.bitwise_and(x, n - 1)


def _silu(x):
    return x * jax.nn.sigmoid(x)


def _softplus(x):
    return jnp.maximum(x, 0.0) + jnp.log(1.0 + jnp.exp(-jnp.abs(x)))


def _rmsnorm_kernel(x_ref, g_ref, o_ref):
    x = x_ref[...]
    ms = jnp.mean(x * x, axis=-1, keepdims=True)
    o_ref[...] = x * lax.rsqrt(ms + EPS) * g_ref[...]


def _row_tile(m):
    tm = math.gcd(m, ROW_TILE)
    assert tm % 16 == 0, m
    return tm


def rmsnorm(x, g):
    m, d = x.shape
    tm = _row_tile(m)
    return pl.pallas_call(
        _rmsnorm_kernel,
        grid=(m // tm,),
        in_specs=[pl.BlockSpec((tm, d), lambda i: (i, 0)), pl.BlockSpec((1, d), lambda i: (0, 0))],
        out_specs=pl.BlockSpec((tm, d), lambda i: (i, 0)),
        out_shape=jax.ShapeDtypeStruct((m, d), F32),
        compiler_params=_cparams(("parallel",)),
        name="rmsnorm",
    )(x, g.reshape(1, d))


def _mm_kernel(a_ref, w_ref, *rest, has_res):
    if has_res:
        r_ref, o_ref, abf_ref = rest
    else:
        o_ref, abf_ref = rest

    @pl.when(pl.program_id(1) == 0)
    def _():
        abf_ref[...] = a_ref[...].astype(BF16)

    acc = _dot(abf_ref[...], w_ref[...])
    if has_res:
        acc = r_ref[...] + acc
    o_ref[...] = acc


def matmul(a, w, res=None, tn=512):
    m, k = a.shape
    n = w.shape[1]
    tn = min(tn, n)
    tm = _row_tile(m)
    in_specs = [pl.BlockSpec((tm, k), lambda i, j: (i, 0)), pl.BlockSpec((k, tn), lambda i, j: (0, j))]
    args = [a, w]
    if res is not None:
        in_specs.append(pl.BlockSpec((tm, tn), lambda i, j: (i, j)))
        args.append(res)
    return pl.pallas_call(
        functools.partial(_mm_kernel, has_res=res is not None),
        grid=(m // tm, n // tn),
        in_specs=in_specs,
        out_specs=pl.BlockSpec((tm, tn), lambda i, j: (i, j)),
        out_shape=jax.ShapeDtypeStruct((m, n), F32),
        scratch_shapes=[pltpu.VMEM((tm, k), BF16)],
        compiler_params=_cparams(("parallel", "arbitrary")),
        name="matmul_res" if res is not None else "matmul",
    )(*args)


def _mixmm_kernel(u_ref, p_ref, mu_ref, w_ref, o_ref, xm_ref):
    @pl.when(pl.program_id(2) == 0)
    def _():
        u = u_ref[...]
        xm_ref[...] = (u + (p_ref[...] - u) * mu_ref[0]).astype(BF16)

    o_ref[0] = _dot(xm_ref[...], w_ref[0])


def mix_matmul(u, prev, mu, w, tn=512):
    m, k = u.shape
    s, _, n = w.shape
    tn = min(tn, n)
    tm = _row_tile(m)
    return pl.pallas_call(
        _mixmm_kernel,
        grid=(m // tm, s, n // tn),
        in_specs=[
            pl.BlockSpec((tm, k), lambda i, si, j: (i, 0)),
            pl.BlockSpec((tm, k), lambda i, si, j: (i, 0)),
            pl.BlockSpec((1, 1, k), lambda i, si, j: (si, 0, 0)),
            pl.BlockSpec((1, k, tn), lambda i, si, j: (si, 0, j)),
        ],
        out_specs=pl.BlockSpec((1, tm, tn), lambda i, si, j: (si, i, j)),
        out_shape=jax.ShapeDtypeStruct((s, m, n), F32),
        scratch_shapes=[pltpu.VMEM((tm, k), BF16)],
        compiler_params=_cparams(("parallel", "arbitrary", "arbitrary")),
        name="mix_matmul",
    )(u, prev, mu.reshape(s, 1, k), w)


def _conv_silu(cur, car_ref, cols, w, b, lq):
    car = car_ref[:, cols]
    n = cur.shape[1]
    rows8 = lax.broadcasted_iota(jnp.int32, (8, n), 0)
    acc = b + cur * w[SSD_CONV - 1:SSD_CONV]
    for s in range(1, SSD_CONV):
        rolled = pltpu.roll(cur, s, 0)
        first = jnp.where(rows8 < s, pltpu.roll(car, s, 0), rolled[:8])
        sh = first if lq == 8 else jnp.concatenate([first, rolled[8:]], axis=0)
        acc = acc + sh * w[SSD_CONV - 1 - s:SSD_CONV - s]
    car_ref[:, cols] = cur[lq - 8:]
    return _silu(acc)


def _ssd_kernel(z_ref, x_ref, b_ref, c_ref, dt_ref, ci_ref, s0_ref, cw_ref, cb_ref, dtb_ref, alog_ref,
                d_ref, ng_ref, *rest, lq, lo, hi, nchunks):
    y_ref, sout_ref, car, st = rest[-4:]
    c = pl.program_id(1)
    gw = SSD_HPG * HEAD_DIM

    @pl.when(c == 0)
    def _():
        car[...] = ci_ref[0]
        st[...] = s0_ref[0, 0].reshape(SSD_GROUPS * gw, SSD_STATE)

    gens = [_ssd_group(g, z_ref, x_ref, b_ref, c_ref, dt_ref, cw_ref, cb_ref, dtb_ref, alog_ref, d_ref,
                       ng_ref, y_ref, car, st, lq=lq, lo=lo, hi=hi) for g in range(SSD_GROUPS)]
    for _ in zip(*gens):
        pass

    @pl.when(c == nchunks - 1)
    def _():
        sout_ref[0, 0] = st[...].reshape(SSD_GROUPS * SSD_HPG, HEAD_DIM, SSD_STATE)


def _ssd_group(g, z_ref, x_ref, b_ref, c_ref, dt_ref, cw_ref, cb_ref, dtb_ref, alog_ref, d_ref, ng_ref,
               y_ref, car, st, *, lq, lo, hi):
    c = pl.program_id(1)
    gw = SSD_HPG * HEAD_DIM
    d_inner = SSD_GROUPS * gw
    xs = slice(g * gw, (g + 1) * gw)
    ns = slice(g * SSD_STATE, (g + 1) * SSD_STATE)
    bs = slice(d_inner + g * SSD_STATE, d_inner + (g + 1) * SSD_STATE)
    cs_ = slice(d_inner + (SSD_GROUPS + g) * SSD_STATE, d_inner + (SSD_GROUPS + g + 1) * SSD_STATE)

    xc = _conv_silu(x_ref[:, xs], car, xs, cw_ref[:, xs], cb_ref[:, xs], lq)
    bc = _conv_silu(b_ref[:, ns], car, bs, cw_ref[:, bs], cb_ref[:, bs], lq)
    cc = _conv_silu(c_ref[:, ns], car, cs_, cw_ref[:, cs_], cb_ref[:, cs_], lq)

    pos = c * lq + lax.broadcasted_iota(jnp.int32, (lq, 1), 0)
    valid = (pos >= lo) & (pos < hi)
    lane = lax.broadcasted_iota(jnp.int32, (lq, 128), 1)
    dt = _softplus(dt_ref[:, ns] + dtb_ref[:, ns])
    dt = jnp.where(valid & (lane < SSD_HPG), dt, 0.0)
    la = dt * (-jnp.exp(alog_ref[:, ns]))
    yield

    la_pad = _pad_rows(la, KPAD)
    ri = lax.broadcasted_iota(jnp.int32, (KPAD, KPAD), 0)
    ci = lax.broadcasted_iota(jnp.int32, (KPAD, KPAD), 1)
    tril = (ri >= ci).astype(BF16)
    triu = (ri <= ci).astype(BF16)
    acum_full = _dot_split_r(tril, la_pad)
    acum_t = _dot_split_l(la_pad.T, triu)
    yield
    acum = acum_full[:lq]
    a_end = acum_full[KPAD - 1:KPAD]
    dec_end = jnp.exp(a_end - acum_full)
    e_in = jnp.exp(acum)
    cd = jnp.exp(a_end)

    bcp = _pad_rows(bc, KPAD).astype(BF16)
    ccb = cc.astype(BF16)
    g_sc = _nt(ccb, bcp)
    st_old = st[xs, :]
    y_in = _nt(ccb, st_old.astype(BF16))
    yield

    qi = lax.broadcasted_iota(jnp.int32, (lq, KPAD), 0)
    kj = lax.broadcasted_iota(jnp.int32, (lq, KPAD), 1)
    causal = kj <= qi
    lane_q = lax.broadcasted_iota(jnp.int32, (lq, 128), 1) < HEAD_DIM
    lane_k = lax.broadcasted_iota(jnp.int32, (KPAD, 128), 1) < HEAD_DIM
    row_k = lax.broadcasted_iota(jnp.int32, (128, 1), 0) < HEAD_DIM
    dvec = d_ref[:, ns]

    ys = []
    for p in range(SSD_HPG // 2):
        h0, h1 = 2 * p, 2 * p + 1
        xp = xc[:, 128 * p:128 * (p + 1)]
        vp = xp * jnp.where(lane_q, dt[:, h0:h0 + 1], dt[:, h1:h1 + 1])
        vpp = _pad_rows(vp, KPAD)
        yp = y_in[:, 128 * p:128 * (p + 1)] * jnp.where(lane_q, e_in[:, h0:h0 + 1], e_in[:, h1:h1 + 1])
        yp = yp + xp * jnp.where(lane_q, dvec[:, h0:h0 + 1], dvec[:, h1:h1 + 1])
        for hh, h in ((0, h0), (1, h1)):
            seg = acum[:, h:h + 1] - acum_t[h:h + 1, :]
            lm = jnp.exp(jnp.where(causal, seg, NEG))
            pm = (g_sc * lm).astype(BF16)
            vm = jnp.where(lane_k if hh == 0 else jnp.logical_not(lane_k), vpp, 0.0).astype(BF16)
            yp = yp + _dot(pm, vm)
        ys.append(yp)
        vend = vpp * jnp.where(lane_k, dec_end[:, h0:h0 + 1], dec_end[:, h1:h1 + 1])
        upd = _tn(vend.astype(BF16), bcp)
        cdp = jnp.where(row_k, cd[:, h0:h0 + 1], cd[:, h1:h1 + 1])
        st[g * gw + 128 * p:g * gw + 128 * (p + 1), :] = st_old[128 * p:128 * (p + 1), :] * cdp + upd
        yield

    y = jnp.concatenate(ys, axis=1)
    y = y * _silu(z_ref[:, xs])
    ms = jnp.mean(y * y, axis=-1, keepdims=True)
    y = y * lax.rsqrt(ms + EPS) * ng_ref[:, xs]
    y_ref[:, xs] = jnp.where(valid, y, 0.0).astype(BF16)
    yield


def ssd_scan(proj, dt_raw, conv_init, s0_all, layer, s_buf, conv_w, conv_b, dtb, alog, dskip, norm_g,
             *, bsz, tp, lq, lo, hi):
    m = proj.shape[0]
    heads = SSD_GROUPS * SSD_HPG
    state_spec = pl.BlockSpec((1, 1, heads, HEAD_DIM, SSD_STATE), lambda b, c: (layer, b, 0, 0, 0))
    extra_specs, extra_args, aliases = [], [], {}
    if s_buf is not None:
        extra_specs, extra_args, aliases = [pl.BlockSpec(memory_space=pl.ANY)], [s_buf], {13: 1}
    nch = tp // lq
    d_inner = heads * HEAD_DIM
    gn = SSD_GROUPS * SSD_STATE
    conv_dim = d_inner + 2 * gn
    row = lambda b, c: b * nch + c
    const = lambda b, c: (0, 0)
    kern = functools.partial(_ssd_kernel, lq=lq, lo=lo, hi=hi, nchunks=nch)
    return pl.pallas_call(
        kern,
        grid=(bsz, nch),
        in_specs=[
            pl.BlockSpec((lq, d_inner), lambda b, c: (row(b, c), 0)),
            pl.BlockSpec((lq, d_inner), lambda b, c: (row(b, c), 1)),
            pl.BlockSpec((lq, gn), lambda b, c: (row(b, c), 2 * d_inner // gn)),
            pl.BlockSpec((lq, gn), lambda b, c: (row(b, c), 2 * d_inner // gn + 1)),
            pl.BlockSpec((lq, gn), lambda b, c: (row(b, c), 0)),
            pl.BlockSpec((1, 8, conv_dim), lambda b, c: (b, 0, 0)),
            state_spec,
            pl.BlockSpec((SSD_CONV, conv_dim), const),
            pl.BlockSpec((1, conv_dim), const),
            pl.BlockSpec((1, gn), const),
            pl.BlockSpec((1, gn), const),
            pl.BlockSpec((1, gn), const),
            pl.BlockSpec((1, d_inner), const),
        ] + extra_specs,
        out_specs=[
            pl.BlockSpec((lq, d_inner), lambda b, c: (row(b, c), 0)),
            state_spec,
        ],
        out_shape=[
            jax.ShapeDtypeStruct((m, d_inner), BF16),
            jax.ShapeDtypeStruct(s0_all.shape, F32),
        ],
        scratch_shapes=[pltpu.VMEM((8, conv_dim), F32), pltpu.VMEM((d_inner, SSD_STATE), F32)],
        input_output_aliases=aliases,
        compiler_params=_cparams(("parallel", "arbitrary")),
        name="ssd_scan",
    )(proj, proj, proj, proj, dt_raw, conv_init, s0_all, conv_w, conv_b.reshape(1, -1),
      dtb.reshape(1, -1), alog.reshape(1, -1), dskip.reshape(1, -1), norm_g.reshape(1, -1), *extra_args)


def _ret_kernel(q_ref, k_ref, v_ref, g_ref, cos_ref, sin_ref, lg_ref, s0_ref, ng_ref,
                y_ref, sout_ref, *, lq, lo, hi):
    c = pl.program_id(1)

    @pl.when(c == 0)
    def _():
        sout_ref[...] = s0_ref[...]

    gens = [_ret_head(h, q_ref, k_ref, v_ref, g_ref, cos_ref, sin_ref, lg_ref, ng_ref, y_ref, sout_ref,
                      lq=lq, lo=lo, hi=hi) for h in range(RET_HEADS)]
    for _ in zip(*gens):
        pass


def _ret_head(h, q_ref, k_ref, v_ref, g_ref, cos_ref, sin_ref, lg_ref, ng_ref, y_ref, st_ref, *, lq, lo, hi):
    c = pl.program_id(1)
    qs = slice(h * RET_QK, (h + 1) * RET_QK)
    vs = slice(h * RET_V, (h + 1) * RET_V)
    lg = lg_ref[h][0:1, 0:1]
    nv = float(hi - lo)

    def count(p):
        return jnp.clip((p + 1 - lo).astype(F32), 0.0, nv)

    base = c * lq
    pos_i = base + lax.broadcasted_iota(jnp.int32, (lq, 1), 0)
    valid = (pos_i >= lo) & (pos_i < hi)
    cnt_i = count(pos_i)
    cnt_j = count(base + lax.broadcasted_iota(jnp.int32, (1, KPAD), 1))
    cnt_jc = count(base + lax.broadcasted_iota(jnp.int32, (KPAD, 1), 0))
    cnt0 = count(base - 1 + jnp.zeros((1, 1), jnp.int32))
    cnt_end = count(base + lq - 1 + jnp.zeros((1, 1), jnp.int32))

    cos = cos_ref[...]
    sin = sin_ref[...]
    half = RET_QK // 2

    def rot(x):
        x1, x2 = x[:, :half], x[:, half:]
        return jnp.concatenate([x1 * cos - x2 * sin, x1 * sin + x2 * cos], axis=1)

    qr = rot(q_ref[:, qs]).astype(BF16)
    kr = jnp.where(valid, rot(k_ref[:, qs]) * (RET_QK ** -0.5), 0.0)
    v = jnp.where(valid, v_ref[:, vs], 0.0)
    krp = _pad_rows(kr, KPAD).astype(BF16)
    vp = _pad_rows(v, KPAD)
    yield

    sc = _nt(qr, krp)
    s_old = st_ref[0, h]
    y_in = _nt(qr, s_old.astype(BF16))
    yield
    qi = lax.broadcasted_iota(jnp.int32, (lq, KPAD), 0)
    kj = lax.broadcasted_iota(jnp.int32, (lq, KPAD), 1)
    dm = jnp.exp(jnp.where(kj <= qi, lg * (cnt_i - cnt_j), NEG))
    y = _dot((sc * dm).astype(BF16), vp.astype(BF16))
    y = y + y_in * jnp.exp(lg * (cnt_i - cnt0))
    vend = vp * jnp.exp(lg * (cnt_end - cnt_jc))
    st_ref[0, h] = s_old * jnp.exp(lg * (cnt_end - cnt0)) + _tn(vend.astype(BF16), krp)
    yield

    ms = jnp.mean(y * y, axis=-1, keepdims=True)
    y = y * lax.rsqrt(ms + EPS) * ng_ref[:, vs] * _silu(g_ref[:, vs])
    y_ref[:, vs] = jnp.where(valid, y, 0.0).astype(BF16)
    yield


def ret_scan(proj, cos, sin, lg, s0, norm_g, *, bsz, tp, lq, lo, hi):
    m = proj.shape[0]
    nch = tp // lq
    d_inner = RET_HEADS * RET_V
    d_qk = RET_HEADS * RET_QK
    row = lambda b, c: b * nch + c
    state_spec = pl.BlockSpec((1, RET_HEADS, RET_V, RET_QK), lambda b, c: (b, 0, 0, 0))
    kern = functools.partial(_ret_kernel, lq=lq, lo=lo, hi=hi)
    return pl.pallas_call(
        kern,
        grid=(bsz, nch),
        in_specs=[
            pl.BlockSpec((lq, d_qk), lambda b, c: (row(b, c), 0)),
            pl.BlockSpec((lq, d_qk), lambda b, c: (row(b, c), 1)),
            pl.BlockSpec((lq, d_inner), lambda b, c: (row(b, c), 2 * d_qk // d_inner)),
            pl.BlockSpec((lq, d_inner), lambda b, c: (row(b, c), 2 * d_qk // d_inner + 1)),
            pl.BlockSpec((lq, RET_QK // 2), lambda b, c: (c, 0)),
            pl.BlockSpec((lq, RET_QK // 2), lambda b, c: (c, 0)),
            pl.BlockSpec((RET_HEADS, 8, 128), lambda b, c: (0, 0, 0)),
            state_spec,
            pl.BlockSpec((1, d_inner), lambda b, c: (0, 0)),
        ],
        out_specs=[
            pl.BlockSpec((lq, d_inner), lambda b, c: (row(b, c), 0)),
            state_spec,
        ],
        out_shape=[
            jax.ShapeDtypeStruct((m, d_inner), BF16),
            jax.ShapeDtypeStruct((bsz, RET_HEADS, RET_V, RET_QK), F32),
        ],
        compiler_params=_cparams(("parallel", "arbitrary")),
        name="ret_scan",
    )(proj, proj, proj, proj, cos, sin, lg, s0, norm_g.reshape(1, -1))


def _rwkv_kernel(r_ref, k_ref, v_ref, g_ref, lw_ref, la_ref, bw_ref, ba_ref, par_ref, s0_ref,
                 y_ref, sout_ref, st, *, lo, hi, nchunks, ngrp):
    c = pl.program_id(2)
    w4 = RWKV_HPB * HEAD_DIM
    r2 = lax.broadcasted_iota(jnp.int32, (w4, w4), 0)
    c2 = lax.broadcasted_iota(jnp.int32, (w4, w4), 1)
    blk = _idiv(r2, HEAD_DIM) == _idiv(c2, HEAD_DIM)

    @pl.when(c == 0)
    def _():
        tile = (lax.broadcasted_iota(jnp.int32, (HEAD_DIM, w4), 0)
                == _imod(lax.broadcasted_iota(jnp.int32, (HEAD_DIM, w4), 1), HEAD_DIM)).astype(BF16)
        for gi in range(ngrp):
            s0 = s0_ref[0, gi * RWKV_HPB:(gi + 1) * RWKV_HPB].reshape(w4, HEAD_DIM)
            st[gi] = jnp.where(blk, _dot_split_l(s0, tile), 0.0)

    s_new = [None] * ngrp
    gens = [_rwkv_group(gi, r_ref, k_ref, v_ref, g_ref, lw_ref, la_ref, bw_ref, ba_ref, par_ref,
                        y_ref, st, s_new, lo=lo, hi=hi) for gi in range(ngrp)]
    for _ in zip(*gens):
        pass

    @pl.when(c == nchunks - 1)
    def _():
        tile_t = (_imod(lax.broadcasted_iota(jnp.int32, (w4, HEAD_DIM), 0), HEAD_DIM)
                  == lax.broadcasted_iota(jnp.int32, (w4, HEAD_DIM), 1)).astype(BF16)
        for gi in range(ngrp):
            sout_ref[0, gi * RWKV_HPB:(gi + 1) * RWKV_HPB] = _dot_split_l(s_new[gi], tile_t).reshape(
                RWKV_HPB, HEAD_DIM, HEAD_DIM)


def _rwkv_group(gi, r_ref, k_ref, v_ref, g_ref, lw_ref, la_ref, bw_ref, ba_ref, par_ref, y_ref, st, out,
                *, lo, hi):
    c = pl.program_id(2)
    cs = RWKV_CHUNK
    w4 = RWKV_HPB * HEAD_DIM
    sl = slice(gi * w4, (gi + 1) * w4)

    r2 = lax.broadcasted_iota(jnp.int32, (w4, w4), 0)
    c2 = lax.broadcasted_iota(jnp.int32, (w4, w4), 1)
    blk = _idiv(r2, HEAD_DIM) == _idiv(c2, HEAD_DIM)
    ones_bd = blk.astype(BF16)

    def segsum(x):
        return _dot_split_l(x, ones_bd, terms=2)

    par = par_ref[:, sl]
    w0, a0, k_k, k_a, r_k, ln_g, ln_b = (par[i:i + 1] for i in range(7))

    r = r_ref[0, :, sl]
    k = k_ref[0, :, sl]
    v = v_ref[0, :, sl]
    g = g_ref[0, :, sl]
    w_raw = w0 + _dot(jnp.tanh(lw_ref[0]).astype(BF16), bw_ref[0, :, sl])
    a = jax.nn.sigmoid(a0 + _dot(la_ref[0].astype(BF16), ba_ref[0, :, sl]))

    ti = lax.broadcasted_iota(jnp.int32, (cs, 1), 0)
    pos = c * cs + ti
    valid = (pos >= lo) & (pos < hi)

    lw = -jnp.exp(-_softplus(-w_raw) - 0.5)
    kk = k * k_k
    kk = kk / jnp.maximum(jnp.sqrt(segsum(kk * kk)), 1e-12)
    yield
    kp = k * (1.0 + (a - 1.0) * k_a)
    lw = jnp.where(valid, lw, 0.0)
    kk = jnp.where(valid, kk, 0.0)
    kp = jnp.where(valid, kp, 0.0)
    vm = jnp.where(valid, v, 0.0)

    tri = (lax.broadcasted_iota(jnp.int32, (cs, cs), 0) >= lax.broadcasted_iota(jnp.int32, (cs, cs), 1)).astype(BF16)
    cw = _dot_split_r(tri, lw)
    yield
    cwl = cw[cs - 1:cs]
    wt = jnp.exp(cw)
    wi = jnp.exp(-cw)
    wend = jnp.exp(cwl - cw)
    b = kk * a
    at = -kk * jnp.exp(cw - lw)
    rt = r * wt

    def bd(x):
        return jnp.where(blk, jnp.concatenate([x] * RWKV_HPB, axis=0), 0.0).astype(BF16)

    lhs = jnp.concatenate([bd(at), bd(rt)], axis=0)
    rhs = jnp.concatenate([bd(b * wi), bd(kp * wi)], axis=0)
    sc = _nt(lhs, rhs)
    yield
    tt = _imod(r2, cs)
    jj = _imod(c2, cs)
    strict = blk & (tt > jj)
    incl = blk & (tt >= jj)
    mab = jnp.where(strict, sc[:w4, :w4], 0.0)
    mak = jnp.where(strict, sc[:w4, w4:], 0.0)
    nrb = jnp.where(incl, sc[w4:, :w4], 0.0)
    nrk = jnp.where(incl, sc[w4:, w4:], 0.0)

    tinv = (r2 == c2).astype(F32) + jnp.where(
        (_idiv(r2, 2) == _idiv(c2, 2)) & (_imod(tt, 2) == 1) & (_imod(jj, 2) == 0), mab, 0.0)
    msz = 2
    while msz < cs:
        off = ((_idiv(r2, 2 * msz) == _idiv(c2, 2 * msz)) & (_imod(tt, 2 * msz) >= msz)
               & (_imod(jj, 2 * msz) < msz))
        tb = tinv.astype(BF16)
        tno = _dot(tb, jnp.where(off, mab, 0.0).astype(BF16)).astype(BF16)
        yield
        tinv = tinv + _dot(tno, tb)
        yield
        msz *= 2

    s_old = st[gi]
    x = _nt(lhs, s_old.astype(BF16))
    yield
    vbd = bd(vm)
    u = _dot(tinv.astype(BF16), (x[:w4] + _dot(mak.astype(BF16), vbd)).astype(BF16))
    yield
    ub = u.astype(BF16)
    yb = x[w4:] + _dot(nrb.astype(BF16), ub) + _dot(nrk.astype(BF16), vbd)
    yield
    s_new = s_old * wt[cs - 1:cs] + _tn(jnp.concatenate([ub, vbd], axis=0),
                                        jnp.concatenate([bd(b * wend), bd(kp * wend)], axis=0))
    st[gi] = s_new
    out[gi] = s_new
    yield

    y = yb[0:cs] + yb[cs:2 * cs] + yb[2 * cs:3 * cs] + yb[3 * cs:4 * cs]
    inv = 1.0 / HEAD_DIM
    yc = y - segsum(y) * inv
    yield
    y = yc * lax.rsqrt(segsum(yc * yc) * inv + RWKV_LN_EPS) * ln_g + ln_b
    y = (y + segsum(r * kp * r_k) * v) * _silu(g)
    y_ref[:, sl] = jnp.where(valid, y, 0.0).astype(BF16)
    yield


def rwkv_scan(rkvg, lora1, lora_b, par, s0, *, bsz, tp, lo, hi):
    _, m, e = rkvg.shape
    cs = RWKV_CHUNK
    nch = tp // cs
    heads = e // HEAD_DIM
    ngrp = RWKV_GROUPS_PER_STEP
    hb = RWKV_HPB * ngrp
    wb = hb * HEAD_DIM
    row = lambda b, h, c: b * nch + c
    kern = functools.partial(_rwkv_kernel, lo=lo, hi=hi, nchunks=nch, ngrp=ngrp)
    proj_spec = lambda s: pl.BlockSpec((1, cs, wb), lambda b, h, c: (s, row(b, h, c), h))
    return pl.pallas_call(
        kern,
        grid=(bsz, heads // hb, nch),
        in_specs=[
            proj_spec(0), proj_spec(1), proj_spec(2), proj_spec(3),
            pl.BlockSpec((1, cs, RWKV_LORA_PAD), lambda b, h, c: (0, row(b, h, c), 0)),
            pl.BlockSpec((1, cs, RWKV_LORA_PAD), lambda b, h, c: (1, row(b, h, c), 0)),
            pl.BlockSpec((1, RWKV_LORA_PAD, wb), lambda b, h, c: (0, 0, h)),
            pl.BlockSpec((1, RWKV_LORA_PAD, wb), lambda b, h, c: (1, 0, h)),
            pl.BlockSpec((8, wb), lambda b, h, c: (0, h)),
            pl.BlockSpec((1, hb, HEAD_DIM, HEAD_DIM), lambda b, h, c: (b, h, 0, 0)),
        ],
        out_specs=[
            pl.BlockSpec((cs, wb), lambda b, h, c: (row(b, h, c), h)),
            pl.BlockSpec((1, hb, HEAD_DIM, HEAD_DIM), lambda b, h, c: (b, h, 0, 0)),
        ],
        out_shape=[
            jax.ShapeDtypeStruct((m, e), BF16),
            jax.ShapeDtypeStruct((bsz, heads, HEAD_DIM, HEAD_DIM), F32),
        ],
        scratch_shapes=[pltpu.VMEM((ngrp, RWKV_HPB * HEAD_DIM, RWKV_HPB * HEAD_DIM), F32)],
        compiler_params=_cparams(("parallel", "parallel", "arbitrary")),
        name="rwkv_scan",
    )(rkvg, rkvg, rkvg, rkvg, lora1, lora1, lora_b, lora_b, par, s0)


def _rwkv_short_kernel(r_ref, k_ref, v_ref, g_ref, lw_ref, la_ref, bw_ref, ba_ref, par_ref, s0_ref,
                       ones_ref, tile_ref, y_ref, sout_ref, *, nvalid, rows_out):
    ct = 8
    hp = RWKV_SHORT_HEADS
    wl = hp * HEAD_DIM
    nr = hp * ct
    ones_bd = ones_ref[...]
    tile_t = tile_ref[...]

    def segsum(x):
        return jnp.concatenate(
            [_dot_split_l(x[:, 256 * j:256 * (j + 1)], ones_bd, terms=2) for j in range(wl // 256)], axis=1)

    par = par_ref[...]
    w0, a0, k_k, k_a, r_k, ln_g, ln_b = (par[i:i + 1] for i in range(7))
    r, k, v, g = r_ref[0], k_ref[0], v_ref[0], g_ref[0]
    w_raw = w0 + _dot(jnp.tanh(lw_ref[0]).astype(BF16), bw_ref[0])
    a = jax.nn.sigmoid(a0 + _dot(la_ref[0].astype(BF16), ba_ref[0]))

    ti = lax.broadcasted_iota(jnp.int32, (ct, 1), 0)
    valid = ti < nvalid
    lw = jnp.where(valid, -jnp.exp(-_softplus(-w_raw) - 0.5), 0.0)
    kk = k * k_k
    kk = jnp.where(valid, kk / jnp.maximum(jnp.sqrt(segsum(kk * kk)), 1e-12), 0.0)
    kp = jnp.where(valid, k * (1.0 + (a - 1.0) * k_a), 0.0)
    vm = jnp.where(valid, v, 0.0)

    cw = lw
    for s in (1, 2, 4):
        cw = cw + jnp.where(ti >= s, pltpu.roll(cw, s, 0), 0.0)
    cwl = cw[ct - 1:ct]
    wend = jnp.exp(cwl - cw)
    wi = jnp.exp(-cw)
    b = kk * a
    at = -kk * jnp.exp(cw - lw)
    rt = r * jnp.exp(cw)

    rr = lax.broadcasted_iota(jnp.int32, (nr, wl), 0)
    cc = lax.broadcasted_iota(jnp.int32, (nr, wl), 1)
    blk = _idiv(rr, ct) == _idiv(cc, HEAD_DIM)

    def bd(x):
        return jnp.where(blk, jnp.concatenate([x] * hp, axis=0), 0.0).astype(BF16)

    lhs = jnp.concatenate([bd(at), bd(rt)], axis=0)
    rhs = jnp.concatenate([bd(b * wi), bd(kp * wi)], axis=0)
    sc = _nt(lhs, rhs)
    ri = lax.broadcasted_iota(jnp.int32, (2 * nr, 2 * nr), 0)
    ci = lax.broadcasted_iota(jnp.int32, (2 * nr, 2 * nr), 1)
    same = _idiv(_imod(ri, nr), ct) == _idiv(_imod(ci, nr), ct)
    tt = _imod(ri, ct)
    jj = _imod(ci, ct)
    sc = jnp.where(same & (tt + (ri >= nr).astype(jnp.int32) > jj), sc, 0.0)
    mab, mak, nrb, nrk = sc[:nr, :nr], sc[:nr, nr:], sc[nr:, :nr], sc[nr:, nr:]

    r1 = lax.broadcasted_iota(jnp.int32, (nr, nr), 0)
    c1 = lax.broadcasted_iota(jnp.int32, (nr, nr), 1)
    t1 = _imod(r1, ct)
    j1 = _imod(c1, ct)
    tinv = (r1 == c1).astype(F32) + jnp.where(
        (_idiv(r1, 2) == _idiv(c1, 2)) & (_imod(t1, 2) == 1) & (_imod(j1, 2) == 0), mab, 0.0)
    msz = 2
    while msz < nvalid:
        off = ((_idiv(r1, 2 * msz) == _idiv(c1, 2 * msz)) & (_imod(t1, 2 * msz) >= msz)
               & (_imod(j1, 2 * msz) < msz))
        tb = tinv.astype(BF16)
        tinv = tinv + _dot(_dot(tb, jnp.where(off, mab, 0.0).astype(BF16)).astype(BF16), tb)
        msz *= 2

    s_old = s0_ref[0].reshape(wl, HEAD_DIM)
    lhs_rows = _dot(lhs, tile_t).astype(BF16)
    x = _nt(lhs_rows, s_old.astype(BF16))
    xa = jnp.where(blk, x[:nr], 0.0)
    xr = jnp.where(blk, x[nr:], 0.0)
    vbd = bd(vm)
    u = _dot(tinv.astype(BF16), (xa + _dot(mak.astype(BF16), vbd)).astype(BF16))
    uv = jnp.concatenate([u.astype(BF16), vbd], axis=0)
    yb = xr + _dot(jnp.concatenate([nrb, nrk], axis=1).astype(BF16), uv)
    bk_rows = _dot(jnp.concatenate([bd(b * wend), bd(kp * wend)], axis=0), tile_t).astype(BF16)
    ds = _tn(uv, bk_rows)

    r16 = lax.broadcasted_iota(jnp.int32, (hp, wl), 0)
    c16 = lax.broadcasted_iota(jnp.int32, (hp, wl), 1)
    wc = jnp.exp(cwl)
    wc_rows = _dot_split_l(jnp.where(r16 == _idiv(c16, HEAD_DIM), wc, 0.0), tile_t)
    rsel = lax.broadcasted_iota(jnp.int32, (hp, HEAD_DIM), 0)
    for h in range(hp):
        rs = slice(h * HEAD_DIM, (h + 1) * HEAD_DIM)
        wc_h = jnp.sum(jnp.where(rsel == h, wc_rows, 0.0), axis=0, keepdims=True)
        sout_ref[0, h] = s_old[rs] * wc_h + ds[rs]

    y = yb[0:ct]
    for h in range(1, hp):
        y = y + yb[h * ct:(h + 1) * ct]
    inv = 1.0 / HEAD_DIM
    yc = y - segsum(y) * inv
    y = yc * lax.rsqrt(segsum(yc * yc) * inv + RWKV_LN_EPS) * ln_g + ln_b
    y = (y + segsum(r * kp * r_k) * v) * _silu(g)
    y = jnp.where(valid, y, 0.0)
    y_ref[...] = _pad_rows(y, rows_out).astype(BF16)


def rwkv_short(rkvg, lora1, lora_b, par, s0, *, bsz, tp, hi):
    _, m, e = rkvg.shape
    heads = e // HEAD_DIM
    hp = RWKV_SHORT_HEADS
    wl = hp * HEAD_DIM
    rb = tp // 8
    w4 = RWKV_HPB * HEAD_DIM
    ones_bd = (jnp.arange(w4)[:, None] // HEAD_DIM == jnp.arange(w4)[None, :] // HEAD_DIM).astype(BF16)
    tile_t = (jnp.arange(wl)[:, None] % HEAD_DIM == jnp.arange(HEAD_DIM)[None, :]).astype(BF16)
    kern = functools.partial(_rwkv_short_kernel, nvalid=hi, rows_out=tp)
    proj_spec = lambda s: pl.BlockSpec((1, 8, wl), lambda b, h: (s, b * rb, h))
    return pl.pallas_call(
        kern,
        grid=(bsz, heads // hp),
        in_specs=[
            proj_spec(0), proj_spec(1), proj_spec(2), proj_spec(3),
            pl.BlockSpec((1, 8, RWKV_LORA_PAD), lambda b, h: (0, b * rb, 0)),
            pl.BlockSpec((1, 8, RWKV_LORA_PAD), lambda b, h: (1, b * rb, 0)),
            pl.BlockSpec((1, RWKV_LORA_PAD, wl), lambda b, h: (0, 0, h)),
            pl.BlockSpec((1, RWKV_LORA_PAD, wl), lambda b, h: (1, 0, h)),
            pl.BlockSpec((8, wl), lambda b, h: (0, h)),
            pl.BlockSpec((1, hp, HEAD_DIM, HEAD_DIM), lambda b, h: (b, h, 0, 0)),
            pl.BlockSpec((w4, w4), lambda b, h: (0, 0)),
            pl.BlockSpec((wl, HEAD_DIM), lambda b, h: (0, 0)),
        ],
        out_specs=[
            pl.BlockSpec((tp, wl), lambda b, h: (b, h)),
            pl.BlockSpec((1, hp, HEAD_DIM, HEAD_DIM), lambda b, h: (b, h, 0, 0)),
        ],
        out_shape=[
            jax.ShapeDtypeStruct((m, e), BF16),
            jax.ShapeDtypeStruct((bsz, heads, HEAD_DIM, HEAD_DIM), F32),
        ],
        compiler_params=_cparams(("parallel", "parallel")),
        name="rwkv_short",
    )(rkvg, rkvg, rkvg, rkvg, lora1, lora1, lora_b, lora_b, par, s0, ones_bd, tile_t)


def _rwkv_lanes_kernel(r_ref, k_ref, v_ref, g_ref, lw_ref, la_ref, bw_ref, ba_ref, par_ref, s0_ref,
                       y_ref, sout_ref, tk, tw, tb, tq, tr, tv, ty, *, nt, nb):
    hp = RWKV_LANE_HEADS
    wl = hp * HEAD_DIM
    ri = lax.broadcasted_iota(jnp.int32, (wl, wl), 0)
    ci = lax.broadcasted_iota(jnp.int32, (wl, wl), 1)
    ones_bd = (_idiv(ri, HEAD_DIM) == _idiv(ci, HEAD_DIM)).astype(BF16)

    def segsum(x):
        return _dot_split_l(x, ones_bd, terms=2)

    par = par_ref[...]
    w0, a0, k_k, k_a, r_k, ln_g, ln_b = (par[i:i + 1] for i in range(7))
    r, k, v, g = r_ref[0], k_ref[0], v_ref[0], g_ref[0]
    w_raw = w0 + _dot(jnp.tanh(lw_ref[0]).astype(BF16), bw_ref[0])
    a = jax.nn.sigmoid(a0 + _dot(la_ref[0].astype(BF16), ba_ref[0]))
    decay = jnp.exp(-jnp.exp(-_softplus(-w_raw) - 0.5))
    kk = k * k_k
    kk = kk / jnp.maximum(jnp.sqrt(segsum(kk * kk)), 1e-12)
    kp = k * (1.0 + (a - 1.0) * k_a)
    bb = kk * a

    for t in range(nt):
        rows = slice(t * nb, (t + 1) * nb)
        tk[t] = (-kk[rows]).T
        tw[t] = decay[rows].T
        tb[t] = bb[rows].T
        tq[t] = kp[rows].T
        tr[t] = r[rows].T
        tv[t] = v[rows].T

    for hh in range(hp):
        ks = slice(hh * HEAD_DIM, (hh + 1) * HEAD_DIM)

        def body(vi, carry, hh=hh, ks=ks):
            sv = s0_ref[hh, vi]
            row = hh * HEAD_DIM + vi
            for t in range(nt):
                sa = jnp.sum(sv * tk[t, ks, :], axis=0, keepdims=True)
                sv = sv * tw[t, ks, :] + sa * tb[t, ks, :] + tv[t, pl.ds(row, 1), :] * tq[t, ks, :]
                ty[t, pl.ds(row, 1), :] = jnp.sum(sv * tr[t, ks, :], axis=0, keepdims=True)
            sout_ref[hh, vi] = sv
            return carry

        lax.fori_loop(0, HEAD_DIM, body, 0)

    inv = 1.0 / HEAD_DIM
    for t in range(nt):
        rows = slice(t * nb, (t + 1) * nb)
        y = ty[t].T
        yc = y - segsum(y) * inv
        y = yc * lax.rsqrt(segsum(yc * yc) * inv + RWKV_LN_EPS) * ln_g + ln_b
        y = (y + segsum(r[rows] * kp[rows] * r_k) * v[rows]) * _silu(g[rows])
        y_ref[rows, :] = y.astype(BF16)


def rwkv_lanes(rkvg, lora1, lora_b, par, s0t, *, nt, nb):
    _, m, e = rkvg.shape
    heads = e // HEAD_DIM
    hp = RWKV_LANE_HEADS
    wl = hp * HEAD_DIM
    kern = functools.partial(_rwkv_lanes_kernel, nt=nt, nb=nb)
    proj_spec = lambda s: pl.BlockSpec((1, m, wl), lambda h: (s, 0, h))
    state_spec = pl.BlockSpec((hp, HEAD_DIM, HEAD_DIM, nb), lambda h: (h, 0, 0, 0))
    tile = pltpu.VMEM((nt, wl, nb), F32)
    return pl.pallas_call(
        kern,
        grid=(heads // hp,),
        in_specs=[
            proj_spec(0), proj_spec(1), proj_spec(2), proj_spec(3),
            pl.BlockSpec((1, m, RWKV_LORA_PAD), lambda h: (0, 0, 0)),
            pl.BlockSpec((1, m, RWKV_LORA_PAD), lambda h: (1, 0, 0)),
            pl.BlockSpec((1, RWKV_LORA_PAD, wl), lambda h: (0, 0, h)),
            pl.BlockSpec((1, RWKV_LORA_PAD, wl), lambda h: (1, 0, h)),
            pl.BlockSpec((8, wl), lambda h: (0, h)),
            state_spec,
        ],
        out_specs=[pl.BlockSpec((m, wl), lambda h: (0, h)), state_spec],
        out_shape=[jax.ShapeDtypeStruct((m, e), BF16), jax.ShapeDtypeStruct(s0t.shape, F32)],
        scratch_shapes=[tile] * 7,
        compiler_params=_cparams(("parallel",)),
        name="rwkv_lanes",
    )(rkvg, rkvg, rkvg, rkvg, lora1, lora1, lora_b, lora_b, par, s0t)


def _prep_weights(norm_g, ssd_w_in, ssd_dt_bias, ssd_a_log, ssd_d, ssd_w_out,
                  rwkv_w_rkvg, rwkv_w_lora_a, rwkv_w_lora_b, rwkv_a_lora_a, rwkv_a_lora_b,
                  rwkv_w0, rwkv_a0, rwkv_k_k, rwkv_k_a, rwkv_r_k, rwkv_ln_g, rwkv_ln_b, rwkv_w_out,
                  ret_w_in, ret_w_out):
    d_inner = ssd_w_out.shape[1]
    n_main = d_inner + d_inner + 2 * SSD_GROUPS * SSD_STATE
    ns = ssd_w_in.shape[0]
    d_model = ssd_w_in.shape[1]

    def head_lanes(p):
        p = p.reshape(ns, SSD_GROUPS, 1, SSD_HPG)
        return jnp.pad(p, ((0, 0), (0, 0), (0, 0), (0, 128 - SSD_HPG)))

    w_dt = ssd_w_in[:, :, n_main:].reshape(ns, d_model, SSD_GROUPS, SSD_HPG)
    w_dt = jnp.pad(w_dt, ((0, 0), (0, 0), (0, 0), (0, 128 - SSD_HPG))).reshape(ns, d_model, SSD_GROUPS * 128)
    rank = rwkv_w_lora_a.shape[2]
    lora_a = jnp.stack([rwkv_w_lora_a, rwkv_a_lora_a], axis=1)
    lora_a = jnp.pad(lora_a, ((0, 0), (0, 0), (0, 0), (0, RWKV_LORA_PAD - rank)))
    lora_b = jnp.stack([rwkv_w_lora_b, rwkv_a_lora_b], axis=1)
    lora_b = jnp.pad(lora_b, ((0, 0), (0, 0), (0, RWKV_LORA_PAD - rank), (0, 0)))
    nr = rwkv_w0.shape[0]
    par = jnp.stack([rwkv_w0, rwkv_a0, rwkv_k_k, rwkv_k_a, rwkv_r_k.reshape(nr, -1), rwkv_ln_g, rwkv_ln_b,
                     jnp.zeros_like(rwkv_w0)], axis=1)
    return dict(
        ssd_w_main=ssd_w_in[:, :, :n_main].astype(BF16), ssd_w_dt=w_dt.astype(BF16),
        ssd_dtb=head_lanes(ssd_dt_bias), ssd_alog=head_lanes(ssd_a_log), ssd_dskip=head_lanes(ssd_d),
        ssd_w_out=ssd_w_out.astype(BF16),
        rwkv_w=rwkv_w_rkvg.astype(BF16), rwkv_lora_a=lora_a.astype(BF16), rwkv_lora_b=lora_b.astype(BF16),
        rwkv_par=par, rwkv_w_out=rwkv_w_out.astype(BF16),
        ret_w_in=ret_w_in.astype(BF16), ret_w_out=ret_w_out.astype(BF16),
    )


def _trunk(h, conv_st, ssd_st, shift_st, wkv_st, ret_st, pos, *, bsz, tp, lq, lo, hi, depth,
           norm_g, final_norm_g, wts, ssd_conv_w, ssd_conv_b, ssd_norm_g, rwkv_mu, ret_norm_g):
    d_model = h.shape[1]
    geo = dict(bsz=bsz, tp=tp, lo=lo, hi=hi)
    new_conv, new_shift, new_wkv, new_ret = [], [], [], []
    new_ssd = None

    half = RET_QK // 2
    inv_freq = 1.0 / (RET_THETA_BASE ** jnp.linspace(0.0, 1.0, half, dtype=F32))
    ang = pos.astype(F32)[:, None] * inv_freq
    cos, sin = jnp.cos(ang), jnp.sin(ang)
    log_gamma = jnp.log1p(-jnp.exp2(-5.0 - jnp.arange(RET_HEADS, dtype=F32)))
    lg = jnp.broadcast_to(log_gamma[:, None, None], (RET_HEADS, 8, 128))

    for layer in range(depth):
        kind, j = layer % 3, layer // 3
        u = rmsnorm(h, norm_g[layer])
        if kind == 0:
            proj = matmul(u, wts["ssd_w_main"][j])
            dt_raw = matmul(u, wts["ssd_w_dt"][j])
            conv_init = jnp.pad(conv_st[j], ((0, 0), (8 - (SSD_CONV - 1), 0), (0, 0)))
            y, new_ssd = ssd_scan(proj, dt_raw, conv_init, ssd_st, j, new_ssd, ssd_conv_w[j], ssd_conv_b[j],
                                  wts["ssd_dtb"][j], wts["ssd_alog"][j], wts["ssd_dskip"][j], ssd_norm_g[j],
                                  lq=lq, **geo)
            d_inner = y.shape[1]
            xbc = proj.reshape(bsz, tp, -1)[:, hi - (SSD_CONV - 1):hi, d_inner:]
            new_conv.append(xbc)
            h = matmul(y, wts["ssd_w_out"][j], res=h)
        elif kind == 1 and lo == 0 and hi <= 8 and bsz % LANES == 0:
            tmajor = lambda x: jnp.swapaxes(x.reshape(bsz, tp, -1)[:, :hi], 0, 1)
            uc = tmajor(u)
            prev = jnp.concatenate([shift_st[j][None], uc[:-1]], axis=0).reshape(hi * bsz, d_model)
            uc = uc.reshape(hi * bsz, d_model)
            rkvg = mix_matmul(uc, prev, rwkv_mu[j][:4], wts["rwkv_w"][j])
            lora1 = mix_matmul(uc, prev, rwkv_mu[j][4:], wts["rwkv_lora_a"][j])
            y, s_t = rwkv_lanes(rkvg, lora1, wts["rwkv_lora_b"][j], wts["rwkv_par"][j],
                                jnp.transpose(wkv_st[j], (1, 2, 3, 0)), nt=hi, nb=bsz)
            new_shift.append(u.reshape(bsz, tp, d_model)[:, hi - 1])
            new_wkv.append(jnp.transpose(s_t, (3, 0, 1, 2)))
            hc = matmul(y, wts["rwkv_w_out"][j], res=tmajor(h).reshape(hi * bsz, d_model))
            hc = jnp.swapaxes(hc.reshape(hi, bsz, d_model), 0, 1)
            h = jnp.pad(hc, ((0, 0), (0, tp - hi), (0, 0))).reshape(bsz * tp, d_model)
        elif kind == 1:
            u3 = u.reshape(bsz, tp, d_model)
            prev = jnp.concatenate([shift_st[j][:, None, :], u3[:, :-1]], axis=1).reshape(bsz * tp, d_model)
            rkvg = mix_matmul(u, prev, rwkv_mu[j][:4], wts["rwkv_w"][j])
            lora1 = mix_matmul(u, prev, rwkv_mu[j][4:], wts["rwkv_lora_a"][j])
            if lo == 0 and hi <= 8 and tp <= 16:
                y, s_new = rwkv_short(rkvg, lora1, wts["rwkv_lora_b"][j], wts["rwkv_par"][j], wkv_st[j],
                                      bsz=bsz, tp=tp, hi=hi)
            else:
                y, s_new = rwkv_scan(rkvg, lora1, wts["rwkv_lora_b"][j], wts["rwkv_par"][j], wkv_st[j], **geo)
            new_shift.append(u3[:, hi - 1])
            new_wkv.append(s_new)
            h = matmul(y, wts["rwkv_w_out"][j], res=h)
        else:
            proj = matmul(u, wts["ret_w_in"][j])
            y, s_new = ret_scan(proj, cos, sin, lg, ret_st[j], ret_norm_g[j], lq=lq, **geo)
            new_ret.append(s_new)
            h = matmul(y, wts["ret_w_out"][j], res=h)
    y = rmsnorm(h, final_norm_g)
    return (y, jnp.stack(new_conv), new_ssd, jnp.stack(new_shift), jnp.stack(new_wkv), jnp.stack(new_ret))


def kernel(x_prompt, x_sample, state_ssd_conv, state_ssd, state_rwkv_shift, state_rwkv_wkv, state_ret, meta_tokens, norm_g, final_norm_g, ssd_w_in, ssd_conv_w, ssd_conv_b, ssd_dt_bias, ssd_a_log, ssd_d, ssd_norm_g, ssd_w_out, rwkv_mu, rwkv_w_rkvg, rwkv_w0, rwkv_w_lora_a, rwkv_w_lora_b, rwkv_a0, rwkv_a_lora_a, rwkv_a_lora_b, rwkv_k_k, rwkv_k_a, rwkv_r_k, rwkv_ln_g, rwkv_ln_b, rwkv_w_out, ret_w_in, ret_norm_g, ret_w_out):
    depth = norm_g.shape[0]
    d_model = x_prompt.shape[2]
    wts = _prep_weights(norm_g, ssd_w_in, ssd_dt_bias, ssd_a_log, ssd_d, ssd_w_out,
                        rwkv_w_rkvg, rwkv_w_lora_a, rwkv_w_lora_b, rwkv_a_lora_a, rwkv_a_lora_b,
                        rwkv_w0, rwkv_a0, rwkv_k_k, rwkv_k_a, rwkv_r_k, rwkv_ln_g, rwkv_ln_b, rwkv_w_out,
                        ret_w_in, ret_w_out)
    common = dict(depth=depth, norm_g=norm_g, final_norm_g=final_norm_g, wts=wts, ssd_conv_w=ssd_conv_w,
                  ssd_conv_b=ssd_conv_b, ssd_norm_g=ssd_norm_g, rwkv_mu=rwkv_mu, ret_norm_g=ret_norm_g)

    bp, seq, _ = x_prompt.shape
    lq_p = 128
    lo_p = lq_p - N_META
    tp_p = lo_p + N_META + seq
    h_p = jnp.concatenate([jnp.zeros((bp, lo_p, d_model), F32),
                           jnp.broadcast_to(meta_tokens[None], (bp, N_META, d_model)), x_prompt], axis=1)
    zeros_like_b = lambda s: jnp.zeros((s.shape[0], bp) + s.shape[2:], F32)
    pos_p = jnp.maximum(jnp.arange(tp_p) - lo_p, 0)
    outs_p = _trunk(h_p.reshape(bp * tp_p, d_model), zeros_like_b(state_ssd_conv), zeros_like_b(state_ssd),
                    zeros_like_b(state_rwkv_shift), zeros_like_b(state_rwkv_wkv), zeros_like_b(state_ret), pos_p,
                    bsz=bp, tp=tp_p, lq=lq_p, lo=lo_p, hi=tp_p, **common)
    y_prompt = outs_p[0].reshape(bp, tp_p, d_model)[:, lo_p + N_META:]

    bs, ds, _ = x_sample.shape
    tp_s = 16
    h_s = jnp.concatenate([x_sample, jnp.zeros((bs, tp_s - ds, d_model), F32)], axis=1)
    pos_s = PAST_LEN + jnp.arange(tp_s)
    outs_s = _trunk(h_s.reshape(bs * tp_s, d_model), state_ssd_conv, state_ssd, state_rwkv_shift, state_rwkv_wkv,
                    state_ret, pos_s, bsz=bs, tp=tp_s, lq=tp_s, lo=0, hi=ds, **common)
    y_sample = outs_s[0].reshape(bs, tp_s, d_model)[:, :ds]

    return (y_prompt, y_sample) + tuple(outs_p[1:]) + tuple(outs_s[1:])
```

```python
import functools
import math

import jax
import jax.numpy as jnp
from jax import lax
from jax.experimental import pallas as pl
from jax.experimental.pallas import tpu as pltpu

F32 = jnp.float32
BF16 = jnp.bfloat16

EPS = 1e-6
N_META = 16
HEAD_DIM = 64
SSD_STATE = 128
SSD_GROUPS = 8
SSD_HPG = 8
SSD_CONV = 4
RET_HEADS = 8
RET_QK = 256
RET_V = 512
RET_THETA_BASE = 10000.0
RWKV_LORA_PAD = 128
RWKV_CHUNK = 64
RWKV_HPB = 4
RWKV_GROUPS_PER_STEP = 4
RWKV_SHORT_HEADS = 16
RWKV_LANE_HEADS = 2
LANES = 128
PAST_LEN = 16384
RWKV_LN_EPS = 1e-5 * HEAD_DIM
KPAD = 128
NEG = -1e30
ROW_TILE = 512
MATMUL_VMEM_BUDGET = 40 * 1024 * 1024
VMEM_LIMIT = 56 * 1024 * 1024


def _cparams(sem):
    return pltpu.CompilerParams(dimension_semantics=sem, vmem_limit_bytes=VMEM_LIMIT)


def _nt(a, b):
    return lax.dot_general(a, b, (((1,), (1,)), ((), ())), preferred_element_type=F32)


def _tn(a, b):
    return lax.dot_general(a, b, (((0,), (0,)), ((), ())), preferred_element_type=F32)


def _dot(a, b):
    return jnp.dot(a, b, preferred_element_type=F32)


def _split(x, terms):
    parts = []
    r = x
    for i in range(terms):
        p = r.astype(BF16)
        parts.append(p)
        if i + 1 < terms:
            r = r - p.astype(F32)
    return parts


def _dot_split_l(x, m, terms=3):
    acc = None
    for p in _split(x, terms):
        d = _dot(p, m)
        acc = d if acc is None else acc + d
    return acc


def _dot_split_r(m, x, terms=3):
    acc = None
    for p in _split(x, terms):
        d = _dot(m, p)
        acc = d if acc is None else acc + d
    return acc


def _pad_rows(x, rows):
    if x.shape[0] == rows:
        return x
    return jnp.concatenate([x, jnp.zeros((rows - x.shape[0], x.shape[1]), x.dtype)], axis=0)


def _idiv(x, n):
    return jnp.right_shift(x, int(math.log2(n)))


def _imod(x, n):
    return jnp.bitwise_and(x, n - 1)


def _silu(x):
    return x * jax.nn.sigmoid(x)


def _softplus(x):
    return jnp.maximum(x, 0.0) + jnp.log(1.0 + jnp.exp(-jnp.abs(x)))


def _rmsnorm_kernel(x_ref, g_ref, o_ref):
    x = x_ref[...]
    ms = jnp.mean(x * x, axis=-1, keepdims=True)
    o_ref[...] = (x * lax.rsqrt(ms + EPS) * g_ref[...]).astype(o_ref.dtype)


def _row_tile(m):
    tm = math.gcd(m, ROW_TILE)
    assert tm % 16 == 0, m
    return tm


def _matmul_tiles(m, k, n, *, a_bytes, n_a, cast, n_out):
    best = None
    for tn in (t for t in (1024, 512, 256, 128) if n % t == 0):
        for tm in (t for t in range(16, m + 1, 16) if m % t == 0):
            need = (2 * n_a * tm * k * a_bytes + (tm * k * 2 if cast else 0)
                    + 2 * k * tn * 2 + 2 * n_out * tm * tn * 4)
            if need <= MATMUL_VMEM_BUDGET and (best is None or (tm * tn, tm) > (best[0] * best[1], best[0])):
                best = (tm, tn)
    assert best is not None, (m, k, n)
    return best


def rmsnorm(x, g, out_dtype=F32):
    m, d = x.shape
    tm = _row_tile(m)
    return pl.pallas_call(
        _rmsnorm_kernel,
        grid=(m // tm,),
        in_specs=[pl.BlockSpec((tm, d), lambda i: (i, 0)), pl.BlockSpec((1, d), lambda i: (0, 0))],
        out_specs=pl.BlockSpec((tm, d), lambda i: (i, 0)),
        out_shape=jax.ShapeDtypeStruct((m, d), out_dtype),
        compiler_params=_cparams(("parallel",)),
        name="rmsnorm",
    )(x, g.reshape(1, d))


def _mm_kernel(a_ref, w_ref, *rest, has_res, cast):
    rest = list(rest)
    abf_ref = rest.pop() if cast else a_ref
    o_ref = rest.pop()

    if cast:
        @pl.when(pl.program_id(1) == 0)
        def _():
            abf_ref[...] = a_ref[...].astype(BF16)

    acc = _dot(abf_ref[...], w_ref[...])
    if has_res:
        acc = rest[0][...] + acc
    o_ref[...] = acc


def matmul(a, w, res=None):
    m, k = a.shape
    n = w.shape[1]
    cast = a.dtype != BF16
    tm, tn = _matmul_tiles(m, k, n, a_bytes=a.dtype.itemsize, n_a=1, cast=cast, n_out=2 if res is not None else 1)
    in_specs = [pl.BlockSpec((tm, k), lambda i, j: (i, 0)), pl.BlockSpec((k, tn), lambda i, j: (0, j))]
    args = [a, w]
    if res is not None:
        in_specs.append(pl.BlockSpec((tm, tn), lambda i, j: (i, j)))
        args.append(res)
    return pl.pallas_call(
        functools.partial(_mm_kernel, has_res=res is not None, cast=cast),
        grid=(m // tm, n // tn),
        in_specs=in_specs,
        out_specs=pl.BlockSpec((tm, tn), lambda i, j: (i, j)),
        out_shape=jax.ShapeDtypeStruct((m, n), F32),
        scratch_shapes=[pltpu.VMEM((tm, k), BF16)] if cast else [],
        compiler_params=_cparams(("parallel", "arbitrary")),
        name="matmul_res" if res is not None else "matmul",
    )(*args)


def _mixmm_kernel(u_ref, p_ref, mu_ref, w_ref, o_ref, xm_ref):
    @pl.when(pl.program_id(2) == 0)
    def _():
        u = u_ref[...]
        xm_ref[...] = (u + (p_ref[...] - u) * mu_ref[0]).astype(BF16)

    o_ref[0] = _dot(xm_ref[...], w_ref[0])


def mix_matmul(u, prev, mu, w):
    m, k = u.shape
    s, _, n = w.shape
    tm, tn = _matmul_tiles(m, k, n, a_bytes=4, n_a=2, cast=True, n_out=1)
    return pl.pallas_call(
        _mixmm_kernel,
        grid=(m // tm, s, n // tn),
        in_specs=[
            pl.BlockSpec((tm, k), lambda i, si, j: (i, 0)),
            pl.BlockSpec((tm, k), lambda i, si, j: (i, 0)),
            pl.BlockSpec((1, 1, k), lambda i, si, j: (si, 0, 0)),
            pl.BlockSpec((1, k, tn), lambda i, si, j: (si, 0, j)),
        ],
        out_specs=pl.BlockSpec((1, tm, tn), lambda i, si, j: (si, i, j)),
        out_shape=jax.ShapeDtypeStruct((s, m, n), F32),
        scratch_shapes=[pltpu.VMEM((tm, k), BF16)],
        compiler_params=_cparams(("parallel", "arbitrary", "arbitrary")),
        name="mix_matmul",
    )(u, prev, mu.reshape(s, 1, k), w)


def _conv_silu(cur, car_ref, cols, w, b, lq):
    car = car_ref[:, cols]
    n = cur.shape[1]
    rows8 = lax.broadcasted_iota(jnp.int32, (8, n), 0)
    acc = b + cur * w[SSD_CONV - 1:SSD_CONV]
    for s in range(1, SSD_CONV):
        rolled = pltpu.roll(cur, s, 0)
        first = jnp.where(rows8 < s, pltpu.roll(car, s, 0), rolled[:8])
        sh = first if lq == 8 else jnp.concatenate([first, rolled[8:]], axis=0)
        acc = acc + sh * w[SSD_CONV - 1 - s:SSD_CONV - s]
    car_ref[:, cols] = cur[lq - 8:]
    return _silu(acc)


def _ssd_kernel(z_ref, x_ref, b_ref, c_ref, dt_ref, ci_ref, s0_ref, cw_ref, cb_ref, dtb_ref, alog_ref,
                d_ref, ng_ref, *rest, lq, lo, hi, nchunks):
    y_ref, sout_ref, car, st = rest[-4:]
    c = pl.program_id(1)
    gw = SSD_HPG * HEAD_DIM

    @pl.when(c == 0)
    def _():
        car[...] = ci_ref[0]
        st[...] = s0_ref[0, 0].reshape(SSD_GROUPS * gw, SSD_STATE)

    gens = [_ssd_group(g, z_ref, x_ref, b_ref, c_ref, dt_ref, cw_ref, cb_ref, dtb_ref, alog_ref, d_ref,
                       ng_ref, y_ref, car, st, lq=lq, lo=lo, hi=hi) for g in range(SSD_GROUPS)]
    for _ in zip(*gens):
        pass

    @pl.when(c == nchunks - 1)
    def _():
        sout_ref[0, 0] = st[...].reshape(SSD_GROUPS * SSD_HPG, HEAD_DIM, SSD_STATE)


def _ssd_group(g, z_ref, x_ref, b_ref, c_ref, dt_ref, cw_ref, cb_ref, dtb_ref, alog_ref, d_ref, ng_ref,
               y_ref, car, st, *, lq, lo, hi):
    c = pl.program_id(1)
    gw = SSD_HPG * HEAD_DIM
    d_inner = SSD_GROUPS * gw
    xs = slice(g * gw, (g + 1) * gw)
    ns = slice(g * SSD_STATE, (g + 1) * SSD_STATE)
    bs = slice(d_inner + g * SSD_STATE, d_inner + (g + 1) * SSD_STATE)
    cs_ = slice(d_inner + (SSD_GROUPS + g) * SSD_STATE, d_inner + (SSD_GROUPS + g + 1) * SSD_STATE)

    xc = _conv_silu(x_ref[:, xs], car, xs, cw_ref[:, xs], cb_ref[:, xs], lq)
    bc = _conv_silu(b_ref[:, ns], car, bs, cw_ref[:, bs], cb_ref[:, bs], lq)
    cc = _conv_silu(c_ref[:, ns], car, cs_, cw_ref[:, cs_], cb_ref[:, cs_], lq)

    pos = c * lq + lax.broadcasted_iota(jnp.int32, (lq, 1), 0)
    valid = (pos >= lo) & (pos < hi)
    lane = lax.broadcasted_iota(jnp.int32, (lq, 128), 1)
    dt = _softplus(dt_ref[:, ns] + dtb_ref[:, ns])
    dt = jnp.where(valid & (lane < SSD_HPG), dt, 0.0)
    la = dt * (-jnp.exp(alog_ref[:, ns]))
    yield

    la_pad = _pad_rows(la, KPAD)
    ri = lax.broadcasted_iota(jnp.int32, (KPAD, KPAD), 0)
    ci = lax.broadcasted_iota(jnp.int32, (KPAD, KPAD), 1)
    tril = (ri >= ci).astype(BF16)
    triu = (ri <= ci).astype(BF16)
    acum_full = _dot_split_r(tril, la_pad)
    acum_t = _dot_split_l(la_pad.T, triu)
    yield
    acum = acum_full[:lq]
    a_end = acum_full[KPAD - 1:KPAD]
    dec_end = jnp.exp(a_end - acum_full)
    e_in = jnp.exp(acum)
    cd = jnp.exp(a_end)

    bcp = _pad_rows(bc, KPAD).astype(BF16)
    ccb = cc.astype(BF16)
    g_sc = _nt(ccb, bcp)
    st_old = st[xs, :]
    y_in = _nt(ccb, st_old.astype(BF16))
    yield

    qi = lax.broadcasted_iota(jnp.int32, (lq, KPAD), 0)
    kj = lax.broadcasted_iota(jnp.int32, (lq, KPAD), 1)
    causal = kj <= qi
    lane_q = lax.broadcasted_iota(jnp.int32, (lq, 128), 1) < HEAD_DIM
    lane_k = lax.broadcasted_iota(jnp.int32, (KPAD, 128), 1) < HEAD_DIM
    row_k = lax.broadcasted_iota(jnp.int32, (128, 1), 0) < HEAD_DIM
    dvec = d_ref[:, ns]

    ys = []
    for p in range(SSD_HPG // 2):
        h0, h1 = 2 * p, 2 * p + 1
        xp = xc[:, 128 * p:128 * (p + 1)]
        vp = xp * jnp.where(lane_q, dt[:, h0:h0 + 1], dt[:, h1:h1 + 1])
        vpp = _pad_rows(vp, KPAD)
        yp = y_in[:, 128 * p:128 * (p + 1)] * jnp.where(lane_q, e_in[:, h0:h0 + 1], e_in[:, h1:h1 + 1])
        yp = yp + xp * jnp.where(lane_q, dvec[:, h0:h0 + 1], dvec[:, h1:h1 + 1])
        for hh, h in ((0, h0), (1, h1)):
            seg = acum[:, h:h + 1] - acum_t[h:h + 1, :]
            lm = jnp.exp(jnp.where(causal, seg, NEG))
            pm = (g_sc * lm).astype(BF16)
            vm = jnp.where(lane_k if hh == 0 else jnp.logical_not(lane_k), vpp, 0.0).astype(BF16)
            yp = yp + _dot(pm, vm)
        ys.append(yp)
        vend = vpp * jnp.where(lane_k, dec_end[:, h0:h0 + 1], dec_end[:, h1:h1 + 1])
        upd = _tn(vend.astype(BF16), bcp)
        cdp = jnp.where(row_k, cd[:, h0:h0 + 1], cd[:, h1:h1 + 1])
        st[g * gw + 128 * p:g * gw + 128 * (p + 1), :] = st_old[128 * p:128 * (p + 1), :] * cdp + upd
        yield

    y = jnp.concatenate(ys, axis=1)
    y = y * _silu(z_ref[:, xs])
    ms = jnp.mean(y * y, axis=-1, keepdims=True)
    y = y * lax.rsqrt(ms + EPS) * ng_ref[:, xs]
    y_ref[:, xs] = jnp.where(valid, y, 0.0).astype(BF16)
    yield


def ssd_scan(proj, dt_raw, conv_init, s0_all, layer, s_buf, conv_w, conv_b, dtb, alog, dskip, norm_g,
             *, bsz, tp, lq, lo, hi):
    m = proj.shape[0]
    heads = SSD_GROUPS * SSD_HPG
    state_spec = pl.BlockSpec((1, 1, heads, HEAD_DIM, SSD_STATE), lambda b, c: (layer, b, 0, 0, 0))
    extra_specs, extra_args, aliases = [], [], {}
    if s_buf is not None:
        extra_specs, extra_args, aliases = [pl.BlockSpec(memory_space=pl.ANY)], [s_buf], {13: 1}
    nch = tp // lq
    d_inner = heads * HEAD_DIM
    gn = SSD_GROUPS * SSD_STATE
    conv_dim = d_inner + 2 * gn
    row = lambda b, c: b * nch + c
    const = lambda b, c: (0, 0)
    kern = functools.partial(_ssd_kernel, lq=lq, lo=lo, hi=hi, nchunks=nch)
    return pl.pallas_call(
        kern,
        grid=(bsz, nch),
        in_specs=[
            pl.BlockSpec((lq, d_inner), lambda b, c: (row(b, c), 0)),
            pl.BlockSpec((lq, d_inner), lambda b, c: (row(b, c), 1)),
            pl.BlockSpec((lq, gn), lambda b, c: (row(b, c), 2 * d_inner // gn)),
            pl.BlockSpec((lq, gn), lambda b, c: (row(b, c), 2 * d_inner // gn + 1)),
            pl.BlockSpec((lq, gn), lambda b, c: (row(b, c), 0)),
            pl.BlockSpec((1, 8, conv_dim), lambda b, c: (b, 0, 0)),
            state_spec,
            pl.BlockSpec((SSD_CONV, conv_dim), const),
            pl.BlockSpec((1, conv_dim), const),
            pl.BlockSpec((1, gn), const),
            pl.BlockSpec((1, gn), const),
            pl.BlockSpec((1, gn), const),
            pl.BlockSpec((1, d_inner), const),
        ] + extra_specs,
        out_specs=[
            pl.BlockSpec((lq, d_inner), lambda b, c: (row(b, c), 0)),
            state_spec,
        ],
        out_shape=[
            jax.ShapeDtypeStruct((m, d_inner), BF16),
            jax.ShapeDtypeStruct(s0_all.shape, F32),
        ],
        scratch_shapes=[pltpu.VMEM((8, conv_dim), F32), pltpu.VMEM((d_inner, SSD_STATE), F32)],
        input_output_aliases=aliases,
        compiler_params=_cparams(("parallel", "arbitrary")),
        name="ssd_scan",
    )(proj, proj, proj, proj, dt_raw, conv_init, s0_all, conv_w, conv_b.reshape(1, -1),
      dtb.reshape(1, -1), alog.reshape(1, -1), dskip.reshape(1, -1), norm_g.reshape(1, -1), *extra_args)


def _ret_kernel(q_ref, k_ref, v_ref, g_ref, cos_ref, sin_ref, lg_ref, s0_ref, ng_ref,
                y_ref, sout_ref, *, lq, lo, hi):
    c = pl.program_id(1)

    @pl.when(c == 0)
    def _():
        sout_ref[...] = s0_ref[...]

    gens = [_ret_head(h, q_ref, k_ref, v_ref, g_ref, cos_ref, sin_ref, lg_ref, ng_ref, y_ref, sout_ref,
                      lq=lq, lo=lo, hi=hi) for h in range(RET_HEADS)]
    for _ in zip(*gens):
        pass


def _ret_head(h, q_ref, k_ref, v_ref, g_ref, cos_ref, sin_ref, lg_ref, ng_ref, y_ref, st_ref, *, lq, lo, hi):
    c = pl.program_id(1)
    qs = slice(h * RET_QK, (h + 1) * RET_QK)
    vs = slice(h * RET_V, (h + 1) * RET_V)
    lg = lg_ref[h][0:1, 0:1]
    nv = float(hi - lo)

    def count(p):
        return jnp.clip((p + 1 - lo).astype(F32), 0.0, nv)

    base = c * lq
    pos_i = base + lax.broadcasted_iota(jnp.int32, (lq, 1), 0)
    valid = (pos_i >= lo) & (pos_i < hi)
    cnt_i = count(pos_i)
    cnt_j = count(base + lax.broadcasted_iota(jnp.int32, (1, KPAD), 1))
    cnt_jc = count(base + lax.broadcasted_iota(jnp.int32, (KPAD, 1), 0))
    cnt0 = count(base - 1 + jnp.zeros((1, 1), jnp.int32))
    cnt_end = count(base + lq - 1 + jnp.zeros((1, 1), jnp.int32))

    cos = cos_ref[...]
    sin = sin_ref[...]
    half = RET_QK // 2

    def rot(x):
        x1, x2 = x[:, :half], x[:, half:]
        return jnp.concatenate([x1 * cos - x2 * sin, x1 * sin + x2 * cos], axis=1)

    qr = rot(q_ref[:, qs]).astype(BF16)
    kr = jnp.where(valid, rot(k_ref[:, qs]) * (RET_QK ** -0.5), 0.0)
    v = jnp.where(valid, v_ref[:, vs], 0.0)
    krp = _pad_rows(kr, KPAD).astype(BF16)
    vp = _pad_rows(v, KPAD)
    yield

    sc = _nt(qr, krp)
    s_old = st_ref[0, h]
    y_in = _nt(qr, s_old.astype(BF16))
    yield
    qi = lax.broadcasted_iota(jnp.int32, (lq, KPAD), 0)
    kj = lax.broadcasted_iota(jnp.int32, (lq, KPAD), 1)
    dm = jnp.exp(jnp.where(kj <= qi, lg * (cnt_i - cnt_j), NEG))
    y = _dot((sc * dm).astype(BF16), vp.astype(BF16))
    y = y + y_in * jnp.exp(lg * (cnt_i - cnt0))
    vend = vp * jnp.exp(lg * (cnt_end - cnt_jc))
    st_ref[0, h] = s_old * jnp.exp(lg * (cnt_end - cnt0)) + _tn(vend.astype(BF16), krp)
    yield

    ms = jnp.mean(y * y, axis=-1, keepdims=True)
    y = y * lax.rsqrt(ms + EPS) * ng_ref[:, vs] * _silu(g_ref[:, vs])
    y_ref[:, vs] = jnp.where(valid, y, 0.0).astype(BF16)
    yield


def ret_scan(proj, cos, sin, lg, s0, norm_g, *, bsz, tp, lq, lo, hi):
    m = proj.shape[0]
    nch = tp // lq
    d_inner = RET_HEADS * RET_V
    d_qk = RET_HEADS * RET_QK
    row = lambda b, c: b * nch + c
    state_spec = pl.BlockSpec((1, RET_HEADS, RET_V, RET_QK), lambda b, c: (b, 0, 0, 0))
    kern = functools.partial(_ret_kernel, lq=lq, lo=lo, hi=hi)
    return pl.pallas_call(
        kern,
        grid=(bsz, nch),
        in_specs=[
            pl.BlockSpec((lq, d_qk), lambda b, c: (row(b, c), 0)),
            pl.BlockSpec((lq, d_qk), lambda b, c: (row(b, c), 1)),
            pl.BlockSpec((lq, d_inner), lambda b, c: (row(b, c), 2 * d_qk // d_inner)),
            pl.BlockSpec((lq, d_inner), lambda b, c: (row(b, c), 2 * d_qk // d_inner + 1)),
            pl.BlockSpec((lq, RET_QK // 2), lambda b, c: (c, 0)),
            pl.BlockSpec((lq, RET_QK // 2), lambda b, c: (c, 0)),
            pl.BlockSpec((RET_HEADS, 8, 128), lambda b, c: (0, 0, 0)),
            state_spec,
            pl.BlockSpec((1, d_inner), lambda b, c: (0, 0)),
        ],
        out_specs=[
            pl.BlockSpec((lq, d_inner), lambda b, c: (row(b, c), 0)),
            state_spec,
        ],
        out_shape=[
            jax.ShapeDtypeStruct((m, d_inner), BF16),
            jax.ShapeDtypeStruct((bsz, RET_HEADS, RET_V, RET_QK), F32),
        ],
        compiler_params=_cparams(("parallel", "arbitrary")),
        name="ret_scan",
    )(proj, proj, proj, proj, cos, sin, lg, s0, norm_g.reshape(1, -1))


def _rwkv_kernel(r_ref, k_ref, v_ref, g_ref, lw_ref, la_ref, bw_ref, ba_ref, par_ref, s0_ref,
                 y_ref, sout_ref, st, *, lo, hi, nchunks, ngrp):
    c = pl.program_id(2)
    w4 = RWKV_HPB * HEAD_DIM
    r2 = lax.broadcasted_iota(jnp.int32, (w4, w4), 0)
    c2 = lax.broadcasted_iota(jnp.int32, (w4, w4), 1)
    blk = _idiv(r2, HEAD_DIM) == _idiv(c2, HEAD_DIM)

    @pl.when(c == 0)
    def _():
        tile = (lax.broadcasted_iota(jnp.int32, (HEAD_DIM, w4), 0)
                == _imod(lax.broadcasted_iota(jnp.int32, (HEAD_DIM, w4), 1), HEAD_DIM)).astype(BF16)
        for gi in range(ngrp):
            s0 = s0_ref[0, gi * RWKV_HPB:(gi + 1) * RWKV_HPB].reshape(w4, HEAD_DIM)
            st[gi] = jnp.where(blk, _dot_split_l(s0, tile), 0.0)

    s_new = [None] * ngrp
    gens = [_rwkv_group(gi, r_ref, k_ref, v_ref, g_ref, lw_ref, la_ref, bw_ref, ba_ref, par_ref,
                        y_ref, st, s_new, lo=lo, hi=hi) for gi in range(ngrp)]
    for _ in zip(*gens):
        pass

    @pl.when(c == nchunks - 1)
    def _():
        tile_t = (_imod(lax.broadcasted_iota(jnp.int32, (w4, HEAD_DIM), 0), HEAD_DIM)
                  == lax.broadcasted_iota(jnp.int32, (w4, HEAD_DIM), 1)).astype(BF16)
        for gi in range(ngrp):
            sout_ref[0, gi * RWKV_HPB:(gi + 1) * RWKV_HPB] = _dot_split_l(s_new[gi], tile_t).reshape(
                RWKV_HPB, HEAD_DIM, HEAD_DIM)


def _rwkv_group(gi, r_ref, k_ref, v_ref, g_ref, lw_ref, la_ref, bw_ref, ba_ref, par_ref, y_ref, st, out,
                *, lo, hi):
    c = pl.program_id(2)
    cs = RWKV_CHUNK
    w4 = RWKV_HPB * HEAD_DIM
    sl = slice(gi * w4, (gi + 1) * w4)

    r2 = lax.broadcasted_iota(jnp.int32, (w4, w4), 0)
    c2 = lax.broadcasted_iota(jnp.int32, (w4, w4), 1)
    blk = _idiv(r2, HEAD_DIM) == _idiv(c2, HEAD_DIM)
    ones_bd = blk.astype(BF16)

    def segsum(x):
        return _dot_split_l(x, ones_bd, terms=2)

    par = par_ref[:, sl]
    w0, a0, k_k, k_a, r_k, ln_g, ln_b = (par[i:i + 1] for i in range(7))

    r = r_ref[0, :, sl]
    k = k_ref[0, :, sl]
    v = v_ref[0, :, sl]
    g = g_ref[0, :, sl]
    w_raw = w0 + _dot(jnp.tanh(lw_ref[0]).astype(BF16), bw_ref[0, :, sl])
    a = jax.nn.sigmoid(a0 + _dot(la_ref[0].astype(BF16), ba_ref[0, :, sl]))

    ti = lax.broadcasted_iota(jnp.int32, (cs, 1), 0)
    pos = c * cs + ti
    valid = (pos >= lo) & (pos < hi)

    lw = -jnp.exp(-_softplus(-w_raw) - 0.5)
    kk = k * k_k
    kk = kk / jnp.maximum(jnp.sqrt(segsum(kk * kk)), 1e-12)
    yield
    kp = k * (1.0 + (a - 1.0) * k_a)
    lw = jnp.where(valid, lw, 0.0)
    kk = jnp.where(valid, kk, 0.0)
    kp = jnp.where(valid, kp, 0.0)
    vm = jnp.where(valid, v, 0.0)

    tri = (lax.broadcasted_iota(jnp.int32, (cs, cs), 0) >= lax.broadcasted_iota(jnp.int32, (cs, cs), 1)).astype(BF16)
    cw = _dot_split_r(tri, lw)
    yield
    cwl = cw[cs - 1:cs]
    wt = jnp.exp(cw)
    wi = jnp.exp(-cw)
    wend = jnp.exp(cwl - cw)
    b = kk * a
    at = -kk * jnp.exp(cw - lw)
    rt = r * wt

    def bd(x):
        return jnp.where(blk, jnp.concatenate([x] * RWKV_HPB, axis=0), 0.0).astype(BF16)

    lhs = jnp.concatenate([bd(at), bd(rt)], axis=0)
    rhs = jnp.concatenate([bd(b * wi), bd(kp * wi)], axis=0)
    sc = _nt(lhs, rhs)
    yield
    tt = _imod(r2, cs)
    jj = _imod(c2, cs)
    strict = blk & (tt > jj)
    incl = blk & (tt >= jj)
    mab = jnp.where(strict, sc[:w4, :w4], 0.0)
    mak = jnp.where(strict, sc[:w4, w4:], 0.0)
    nrb = jnp.where(incl, sc[w4:, :w4], 0.0)
    nrk = jnp.where(incl, sc[w4:, w4:], 0.0)

    tinv = (r2 == c2).astype(F32) + jnp.where(
        (_idiv(r2, 2) == _idiv(c2, 2)) & (_imod(tt, 2) == 1) & (_imod(jj, 2) == 0), mab, 0.0)
    msz = 2
    while msz < cs:
        off = ((_idiv(r2, 2 * msz) == _idiv(c2, 2 * msz)) & (_imod(tt, 2 * msz) >= msz)
               & (_imod(jj, 2 * msz) < msz))
        tb = tinv.astype(BF16)
        tno = _dot(tb, jnp.where(off, mab, 0.0).astype(BF16)).astype(BF16)
        yield
        tinv = tinv + _dot(tno, tb)
        yield
        msz *= 2

    s_old = st[gi]
    x = _nt(lhs, s_old.astype(BF16))
    yield
    vbd = bd(vm)
    u = _dot(tinv.astype(BF16), (x[:w4] + _dot(mak.astype(BF16), vbd)).astype(BF16))
    yield
    ub = u.astype(BF16)
    yb = x[w4:] + _dot(nrb.astype(BF16), ub) + _dot(nrk.astype(BF16), vbd)
    yield
    s_new = s_old * wt[cs - 1:cs] + _tn(jnp.concatenate([ub, vbd], axis=0),
                                        jnp.concatenate([bd(b * wend), bd(kp * wend)], axis=0))
    st[gi] = s_new
    out[gi] = s_new
    yield

    y = yb[0:cs] + yb[cs:2 * cs] + yb[2 * cs:3 * cs] + yb[3 * cs:4 * cs]
    inv = 1.0 / HEAD_DIM
    yc = y - segsum(y) * inv
    yield
    y = yc * lax.rsqrt(segsum(yc * yc) * inv + RWKV_LN_EPS) * ln_g + ln_b
    y = (y + segsum(r * kp * r_k) * v) * _silu(g)
    y_ref[:, sl] = jnp.where(valid, y, 0.0).astype(BF16)
    yield


def rwkv_scan(rkvg, lora1, lora_b, par, s0, *, bsz, tp, lo, hi):
    _, m, e = rkvg.shape
    cs = RWKV_CHUNK
    nch = tp // cs
    heads = e // HEAD_DIM
    ngrp = RWKV_GROUPS_PER_STEP
    hb = RWKV_HPB * ngrp
    wb = hb * HEAD_DIM
    row = lambda b, h, c: b * nch + c
    kern = functools.partial(_rwkv_kernel, lo=lo, hi=hi, nchunks=nch, ngrp=ngrp)
    proj_spec = lambda s: pl.BlockSpec((1, cs, wb), lambda b, h, c: (s, row(b, h, c), h))
    return pl.pallas_call(
        kern,
        grid=(bsz, heads // hb, nch),
        in_specs=[
            proj_spec(0), proj_spec(1), proj_spec(2), proj_spec(3),
            pl.BlockSpec((1, cs, RWKV_LORA_PAD), lambda b, h, c: (0, row(b, h, c), 0)),
            pl.BlockSpec((1, cs, RWKV_LORA_PAD), lambda b, h, c: (1, row(b, h, c), 0)),
            pl.BlockSpec((1, RWKV_LORA_PAD, wb), lambda b, h, c: (0, 0, h)),
            pl.BlockSpec((1, RWKV_LORA_PAD, wb), lambda b, h, c: (1, 0, h)),
            pl.BlockSpec((8, wb), lambda b, h, c: (0, h)),
            pl.BlockSpec((1, hb, HEAD_DIM, HEAD_DIM), lambda b, h, c: (b, h, 0, 0)),
        ],
        out_specs=[
            pl.BlockSpec((cs, wb), lambda b, h, c: (row(b, h, c), h)),
            pl.BlockSpec((1, hb, HEAD_DIM, HEAD_DIM), lambda b, h, c: (b, h, 0, 0)),
        ],
        out_shape=[
            jax.ShapeDtypeStruct((m, e), BF16),
            jax.ShapeDtypeStruct((bsz, heads, HEAD_DIM, HEAD_DIM), F32),
        ],
        scratch_shapes=[pltpu.VMEM((ngrp, RWKV_HPB * HEAD_DIM, RWKV_HPB * HEAD_DIM), F32)],
        compiler_params=_cparams(("parallel", "parallel", "arbitrary")),
        name="rwkv_scan",
    )(rkvg, rkvg, rkvg, rkvg, lora1, lora1, lora_b, lora_b, par, s0)


def _rwkv_short_kernel(r_ref, k_ref, v_ref, g_ref, lw_ref, la_ref, bw_ref, ba_ref, par_ref, s0_ref,
                       ones_ref, tile_ref, y_ref, sout_ref, *, nvalid, rows_out):
    ct = 8
    hp = RWKV_SHORT_HEADS
    wl = hp * HEAD_DIM
    nr = hp * ct
    ones_bd = ones_ref[...]
    tile_t = tile_ref[...]

    def segsum(x):
        return jnp.concatenate(
            [_dot_split_l(x[:, 256 * j:256 * (j + 1)], ones_bd, terms=2) for j in range(wl // 256)], axis=1)

    par = par_ref[...]
    w0, a0, k_k, k_a, r_k, ln_g, ln_b = (par[i:i + 1] for i in range(7))
    r, k, v, g = r_ref[0], k_ref[0], v_ref[0], g_ref[0]
    w_raw = w0 + _dot(jnp.tanh(lw_ref[0]).astype(BF16), bw_ref[0])
    a = jax.nn.sigmoid(a0 + _dot(la_ref[0].astype(BF16), ba_ref[0]))

    ti = lax.broadcasted_iota(jnp.int32, (ct, 1), 0)
    valid = ti < nvalid
    lw = jnp.where(valid, -jnp.exp(-_softplus(-w_raw) - 0.5), 0.0)
    kk = k * k_k
    kk = jnp.where(valid, kk / jnp.maximum(jnp.sqrt(segsum(kk * kk)), 1e-12), 0.0)
    kp = jnp.where(valid, k * (1.0 + (a - 1.0) * k_a), 0.0)
    vm = jnp.where(valid, v, 0.0)

    cw = lw
    for s in (1, 2, 4):
        cw = cw + jnp.where(ti >= s, pltpu.roll(cw, s, 0), 0.0)
    cwl = cw[ct - 1:ct]
    wend = jnp.exp(cwl - cw)
    wi = jnp.exp(-cw)
    b = kk * a
    at = -kk * jnp.exp(cw - lw)
    rt = r * jnp.exp(cw)

    rr = lax.broadcasted_iota(jnp.int32, (nr, wl), 0)
    cc = lax.broadcasted_iota(jnp.int32, (nr, wl), 1)
    blk = _idiv(rr, ct) == _idiv(cc, HEAD_DIM)

    def bd(x):
        return jnp.where(blk, jnp.concatenate([x] * hp, axis=0), 0.0).astype(BF16)

    lhs = jnp.concatenate([bd(at), bd(rt)], axis=0)
    rhs = jnp.concatenate([bd(b * wi), bd(kp * wi)], axis=0)
    sc = _nt(lhs, rhs)
    ri = lax.broadcasted_iota(jnp.int32, (2 * nr, 2 * nr), 0)
    ci = lax.broadcasted_iota(jnp.int32, (2 * nr, 2 * nr), 1)
    same = _idiv(_imod(ri, nr), ct) == _idiv(_imod(ci, nr), ct)
    tt = _imod(ri, ct)
    jj = _imod(ci, ct)
    sc = jnp.where(same & (tt + (ri >= nr).astype(jnp.int32) > jj), sc, 0.0)
    mab, mak, nrb, nrk = sc[:nr, :nr], sc[:nr, nr:], sc[nr:, :nr], sc[nr:, nr:]

    r1 = lax.broadcasted_iota(jnp.int32, (nr, nr), 0)
    c1 = lax.broadcasted_iota(jnp.int32, (nr, nr), 1)
    t1 = _imod(r1, ct)
    j1 = _imod(c1, ct)
    tinv = (r1 == c1).astype(F32) + jnp.where(
        (_idiv(r1, 2) == _idiv(c1, 2)) & (_imod(t1, 2) == 1) & (_imod(j1, 2) == 0), mab, 0.0)
    msz = 2
    while msz < nvalid:
        off = ((_idiv(r1, 2 * msz) == _idiv(c1, 2 * msz)) & (_imod(t1, 2 * msz) >= msz)
               & (_imod(j1, 2 * msz) < msz))
        tb = tinv.astype(BF16)
        tinv = tinv + _dot(_dot(tb, jnp.where(off, mab, 0.0).astype(BF16)).astype(BF16), tb)
        msz *= 2

    s_old = s0_ref[0].reshape(wl, HEAD_DIM)
    lhs_rows = _dot(lhs, tile_t).astype(BF16)
    x = _nt(lhs_rows, s_old.astype(BF16))
    xa = jnp.where(blk, x[:nr], 0.0)
    xr = jnp.where(blk, x[nr:], 0.0)
    vbd = bd(vm)
    u = _dot(tinv.astype(BF16), (xa + _dot(mak.astype(BF16), vbd)).astype(BF16))
    uv = jnp.concatenate([u.astype(BF16), vbd], axis=0)
    yb = xr + _dot(jnp.concatenate([nrb, nrk], axis=1).astype(BF16), uv)
    bk_rows = _dot(jnp.concatenate([bd(b * wend), bd(kp * wend)], axis=0), tile_t).astype(BF16)
    ds = _tn(uv, bk_rows)

    r16 = lax.broadcasted_iota(jnp.int32, (hp, wl), 0)
    c16 = lax.broadcasted_iota(jnp.int32, (hp, wl), 1)
    wc = jnp.exp(cwl)
    wc_rows = _dot_split_l(jnp.where(r16 == _idiv(c16, HEAD_DIM), wc, 0.0), tile_t)
    rsel = lax.broadcasted_iota(jnp.int32, (hp, HEAD_DIM), 0)
    for h in range(hp):
        rs = slice(h * HEAD_DIM, (h + 1) * HEAD_DIM)
        wc_h = jnp.sum(jnp.where(rsel == h, wc_rows, 0.0), axis=0, keepdims=True)
        sout_ref[0, h] = s_old[rs] * wc_h + ds[rs]

    y = yb[0:ct]
    for h in range(1, hp):
        y = y + yb[h * ct:(h + 1) * ct]
    inv = 1.0 / HEAD_DIM
    yc = y - segsum(y) * inv
    y = yc * lax.rsqrt(segsum(yc * yc) * inv + RWKV_LN_EPS) * ln_g + ln_b
    y = (y + segsum(r * kp * r_k) * v) * _silu(g)
    y = jnp.where(valid, y, 0.0)
    y_ref[...] = _pad_rows(y, rows_out).astype(BF16)


def rwkv_short(rkvg, lora1, lora_b, par, s0, *, bsz, tp, hi):
    _, m, e = rkvg.shape
    heads = e // HEAD_DIM
    hp = RWKV_SHORT_HEADS
    wl = hp * HEAD_DIM
    rb = tp // 8
    w4 = RWKV_HPB * HEAD_DIM
    ones_bd = (jnp.arange(w4)[:, None] // HEAD_DIM == jnp.arange(w4)[None, :] // HEAD_DIM).astype(BF16)
    tile_t = (jnp.arange(wl)[:, None] % HEAD_DIM == jnp.arange(HEAD_DIM)[None, :]).astype(BF16)
    kern = functools.partial(_rwkv_short_kernel, nvalid=hi, rows_out=tp)
    proj_spec = lambda s: pl.BlockSpec((1, 8, wl), lambda b, h: (s, b * rb, h))
    return pl.pallas_call(
        kern,
        grid=(bsz, heads // hp),
        in_specs=[
            proj_spec(0), proj_spec(1), proj_spec(2), proj_spec(3),
            pl.BlockSpec((1, 8, RWKV_LORA_PAD), lambda b, h: (0, b * rb, 0)),
            pl.BlockSpec((1, 8, RWKV_LORA_PAD), lambda b, h: (1, b * rb, 0)),
            pl.BlockSpec((1, RWKV_LORA_PAD, wl), lambda b, h: (0, 0, h)),
            pl.BlockSpec((1, RWKV_LORA_PAD, wl), lambda b, h: (1, 0, h)),
            pl.BlockSpec((8, wl), lambda b, h: (0, h)),
            pl.BlockSpec((1, hp, HEAD_DIM, HEAD_DIM), lambda b, h: (b, h, 0, 0)),
            pl.BlockSpec((w4, w4), lambda b, h: (0, 0)),
            pl.BlockSpec((wl, HEAD_DIM), lambda b, h: (0, 0)),
        ],
        out_specs=[
            pl.BlockSpec((tp, wl), lambda b, h: (b, h)),
            pl.BlockSpec((1, hp, HEAD_DIM, HEAD_DIM), lambda b, h: (b, h, 0, 0)),
        ],
        out_shape=[
            jax.ShapeDtypeStruct((m, e), BF16),
            jax.ShapeDtypeStruct((bsz, heads, HEAD_DIM, HEAD_DIM), F32),
        ],
        compiler_params=_cparams(("parallel", "parallel")),
        name="rwkv_short",
    )(rkvg, rkvg, rkvg, rkvg, lora1, lora1, lora_b, lora_b, par, s0, ones_bd, tile_t)


def _rwkv_lanes_kernel(r_ref, k_ref, v_ref, g_ref, lw_ref, la_ref, bw_ref, ba_ref, par_ref, s0_ref,
                       y_ref, sout_ref, tk, tw, tb, tq, tr, tv, ty, *, nt, nb):
    hp = RWKV_LANE_HEADS
    wl = hp * HEAD_DIM
    ri = lax.broadcasted_iota(jnp.int32, (wl, wl), 0)
    ci = lax.broadcasted_iota(jnp.int32, (wl, wl), 1)
    ones_bd = (_idiv(ri, HEAD_DIM) == _idiv(ci, HEAD_DIM)).astype(BF16)

    def segsum(x):
        return _dot_split_l(x, ones_bd, terms=2)

    par = par_ref[...]
    w0, a0, k_k, k_a, r_k, ln_g, ln_b = (par[i:i + 1] for i in range(7))
    r, k, v, g = r_ref[0], k_ref[0], v_ref[0], g_ref[0]
    w_raw = w0 + _dot(jnp.tanh(lw_ref[0]).astype(BF16), bw_ref[0])
    a = jax.nn.sigmoid(a0 + _dot(la_ref[0].astype(BF16), ba_ref[0]))
    decay = jnp.exp(-jnp.exp(-_softplus(-w_raw) - 0.5))
    kk = k * k_k
    kk = kk / jnp.maximum(jnp.sqrt(segsum(kk * kk)), 1e-12)
    kp = k * (1.0 + (a - 1.0) * k_a)
    bb = kk * a

    for t in range(nt):
        rows = slice(t * nb, (t + 1) * nb)
        tk[t] = (-kk[rows]).T
        tw[t] = decay[rows].T
        tb[t] = bb[rows].T
        tq[t] = kp[rows].T
        tr[t] = r[rows].T
        tv[t] = v[rows].T

    for hh in range(hp):
        ks = slice(hh * HEAD_DIM, (hh + 1) * HEAD_DIM)

        def body(vi, carry, hh=hh, ks=ks):
            sv = s0_ref[hh, vi]
            row = hh * HEAD_DIM + vi
            for t in range(nt):
                sa = jnp.sum(sv * tk[t, ks, :], axis=0, keepdims=True)
                sv = sv * tw[t, ks, :] + sa * tb[t, ks, :] + tv[t, pl.ds(row, 1), :] * tq[t, ks, :]
                ty[t, pl.ds(row, 1), :] = jnp.sum(sv * tr[t, ks, :], axis=0, keepdims=True)
            sout_ref[hh, vi] = sv
            return carry

        lax.fori_loop(0, HEAD_DIM, body, 0)

    inv = 1.0 / HEAD_DIM
    for t in range(nt):
        rows = slice(t * nb, (t + 1) * nb)
        y = ty[t].T
        yc = y - segsum(y) * inv
        y = yc * lax.rsqrt(segsum(yc * yc) * inv + RWKV_LN_EPS) * ln_g + ln_b
        y = (y + segsum(r[rows] * kp[rows] * r_k) * v[rows]) * _silu(g[rows])
        y_ref[rows, :] = y.astype(BF16)


def rwkv_lanes(rkvg, lora1, lora_b, par, s0t, *, nt, nb):
    _, m, e = rkvg.shape
    heads = e // HEAD_DIM
    hp = RWKV_LANE_HEADS
    wl = hp * HEAD_DIM
    kern = functools.partial(_rwkv_lanes_kernel, nt=nt, nb=nb)
    proj_spec = lambda s: pl.BlockSpec((1, m, wl), lambda h: (s, 0, h))
    state_spec = pl.BlockSpec((hp, HEAD_DIM, HEAD_DIM, nb), lambda h: (h, 0, 0, 0))
    tile = pltpu.VMEM((nt, wl, nb), F32)
    return pl.pallas_call(
        kern,
        grid=(heads // hp,),
        in_specs=[
            proj_spec(0), proj_spec(1), proj_spec(2), proj_spec(3),
            pl.BlockSpec((1, m, RWKV_LORA_PAD), lambda h: (0, 0, 0)),
            pl.BlockSpec((1, m, RWKV_LORA_PAD), lambda h: (1, 0, 0)),
            pl.BlockSpec((1, RWKV_LORA_PAD, wl), lambda h: (0, 0, h)),
            pl.BlockSpec((1, RWKV_LORA_PAD, wl), lambda h: (1, 0, h)),
            pl.BlockSpec((8, wl), lambda h: (0, h)),
            state_spec,
        ],
        out_specs=[pl.BlockSpec((m, wl), lambda h: (0, h)), state_spec],
        out_shape=[jax.ShapeDtypeStruct((m, e), BF16), jax.ShapeDtypeStruct(s0t.shape, F32)],
        scratch_shapes=[tile] * 7,
        compiler_params=_cparams(("parallel",)),
        name="rwkv_lanes",
    )(rkvg, rkvg, rkvg, rkvg, lora1, lora1, lora_b, lora_b, par, s0t)


def _prep_weights(norm_g, ssd_w_in, ssd_dt_bias, ssd_a_log, ssd_d, ssd_w_out,
                  rwkv_w_rkvg, rwkv_w_lora_a, rwkv_w_lora_b, rwkv_a_lora_a, rwkv_a_lora_b,
                  rwkv_w0, rwkv_a0, rwkv_k_k, rwkv_k_a, rwkv_r_k, rwkv_ln_g, rwkv_ln_b, rwkv_w_out,
                  ret_w_in, ret_w_out):
    d_inner = ssd_w_out.shape[1]
    n_main = d_inner + d_inner + 2 * SSD_GROUPS * SSD_STATE
    ns = ssd_w_in.shape[0]
    d_model = ssd_w_in.shape[1]

    def head_lanes(p):
        p = p.reshape(ns, SSD_GROUPS, 1, SSD_HPG)
        return jnp.pad(p, ((0, 0), (0, 0), (0, 0), (0, 128 - SSD_HPG)))

    w_dt = ssd_w_in[:, :, n_main:].reshape(ns, d_model, SSD_GROUPS, SSD_HPG)
    w_dt = jnp.pad(w_dt, ((0, 0), (0, 0), (0, 0), (0, 128 - SSD_HPG))).reshape(ns, d_model, SSD_GROUPS * 128)
    rank = rwkv_w_lora_a.shape[2]
    lora_a = jnp.stack([rwkv_w_lora_a, rwkv_a_lora_a], axis=1)
    lora_a = jnp.pad(lora_a, ((0, 0), (0, 0), (0, 0), (0, RWKV_LORA_PAD - rank)))
    lora_b = jnp.stack([rwkv_w_lora_b, rwkv_a_lora_b], axis=1)
    lora_b = jnp.pad(lora_b, ((0, 0), (0, 0), (0, RWKV_LORA_PAD - rank), (0, 0)))
    nr = rwkv_w0.shape[0]
    par = jnp.stack([rwkv_w0, rwkv_a0, rwkv_k_k, rwkv_k_a, rwkv_r_k.reshape(nr, -1), rwkv_ln_g, rwkv_ln_b,
                     jnp.zeros_like(rwkv_w0)], axis=1)
    return dict(
        ssd_w_main=ssd_w_in[:, :, :n_main].astype(BF16), ssd_w_dt=w_dt.astype(BF16),
        ssd_dtb=head_lanes(ssd_dt_bias), ssd_alog=head_lanes(ssd_a_log), ssd_dskip=head_lanes(ssd_d),
        ssd_w_out=ssd_w_out.astype(BF16),
        rwkv_w=rwkv_w_rkvg.astype(BF16), rwkv_lora_a=lora_a.astype(BF16), rwkv_lora_b=lora_b.astype(BF16),
        rwkv_par=par, rwkv_w_out=rwkv_w_out.astype(BF16),
        ret_w_in=ret_w_in.astype(BF16), ret_w_out=ret_w_out.astype(BF16),
    )


def _trunk(h, conv_st, ssd_st, shift_st, wkv_st, ret_st, pos, *, bsz, tp, lq, lo, hi, depth,
           norm_g, final_norm_g, wts, ssd_conv_w, ssd_conv_b, ssd_norm_g, rwkv_mu, ret_norm_g):
    d_model = h.shape[1]
    geo = dict(bsz=bsz, tp=tp, lo=lo, hi=hi)
    new_conv, new_shift, new_wkv, new_ret = [], [], [], []
    new_ssd = None

    half = RET_QK // 2
    inv_freq = 1.0 / (RET_THETA_BASE ** jnp.linspace(0.0, 1.0, half, dtype=F32))
    ang = pos.astype(F32)[:, None] * inv_freq
    cos, sin = jnp.cos(ang), jnp.sin(ang)
    log_gamma = jnp.log1p(-jnp.exp2(-5.0 - jnp.arange(RET_HEADS, dtype=F32)))
    lg = jnp.broadcast_to(log_gamma[:, None, None], (RET_HEADS, 8, 128))

    for layer in range(depth):
        kind, j = layer % 3, layer // 3
        u = rmsnorm(h, norm_g[layer], F32 if kind == 1 else BF16)
        if kind == 0:
            proj = matmul(u, wts["ssd_w_main"][j])
            dt_raw = matmul(u, wts["ssd_w_dt"][j])
            conv_init = jnp.pad(conv_st[j], ((0, 0), (8 - (SSD_CONV - 1), 0), (0, 0)))
            y, new_ssd = ssd_scan(proj, dt_raw, conv_init, ssd_st, j, new_ssd, ssd_conv_w[j], ssd_conv_b[j],
                                  wts["ssd_dtb"][j], wts["ssd_alog"][j], wts["ssd_dskip"][j], ssd_norm_g[j],
                                  lq=lq, **geo)
            d_inner = y.shape[1]
            xbc = proj.reshape(bsz, tp, -1)[:, hi - (SSD_CONV - 1):hi, d_inner:]
            new_conv.append(xbc)
            h = matmul(y, wts["ssd_w_out"][j], res=h)
        elif kind == 1 and lo == 0 and hi <= 8 and bsz % LANES == 0:
            tmajor = lambda x: jnp.swapaxes(x.reshape(bsz, tp, -1)[:, :hi], 0, 1)
            uc = tmajor(u)
            prev = jnp.concatenate([shift_st[j][None], uc[:-1]], axis=0).reshape(hi * bsz, d_model)
            uc = uc.reshape(hi * bsz, d_model)
            rkvg = mix_matmul(uc, prev, rwkv_mu[j][:4], wts["rwkv_w"][j])
            lora1 = mix_matmul(uc, prev, rwkv_mu[j][4:], wts["rwkv_lora_a"][j])
            y, s_t = rwkv_lanes(rkvg, lora1, wts["rwkv_lora_b"][j], wts["rwkv_par"][j],
                                jnp.transpose(wkv_st[j], (1, 2, 3, 0)), nt=hi, nb=bsz)
            new_shift.append(u.reshape(bsz, tp, d_model)[:, hi - 1])
            new_wkv.append(jnp.transpose(s_t, (3, 0, 1, 2)))
            hc = matmul(y, wts["rwkv_w_out"][j], res=tmajor(h).reshape(hi * bsz, d_model))
            hc = jnp.swapaxes(hc.reshape(hi, bsz, d_model), 0, 1)
            h = jnp.pad(hc, ((0, 0), (0, tp - hi), (0, 0))).reshape(bsz * tp, d_model)
        elif kind == 1:
            u3 = u.reshape(bsz, tp, d_model)
            prev = jnp.concatenate([shift_st[j][:, None, :], u3[:, :-1]], axis=1).reshape(bsz * tp, d_model)
            rkvg = mix_matmul(u, prev, rwkv_mu[j][:4], wts["rwkv_w"][j])
            lora1 = mix_matmul(u, prev, rwkv_mu[j][4:], wts["rwkv_lora_a"][j])
            if lo == 0 and hi <= 8 and tp <= 16:
                y, s_new = rwkv_short(rkvg, lora1, wts["rwkv_lora_b"][j], wts["rwkv_par"][j], wkv_st[j],
                                      bsz=bsz, tp=tp, hi=hi)
            else:
                y, s_new = rwkv_scan(rkvg, lora1, wts["rwkv_lora_b"][j], wts["rwkv_par"][j], wkv_st[j], **geo)
            new_shift.append(u3[:, hi - 1])
            new_wkv.append(s_new)
            h = matmul(y, wts["rwkv_w_out"][j], res=h)
        else:
            proj = matmul(u, wts["ret_w_in"][j])
            y, s_new = ret_scan(proj, cos, sin, lg, ret_st[j], ret_norm_g[j], lq=lq, **geo)
            new_ret.append(s_new)
            h = matmul(y, wts["ret_w_out"][j], res=h)
    y = rmsnorm(h, final_norm_g)
    return (y, jnp.stack(new_conv), new_ssd, jnp.stack(new_shift), jnp.stack(new_wkv), jnp.stack(new_ret))


def kernel(x_prompt, x_sample, state_ssd_conv, state_ssd, state_rwkv_shift, state_rwkv_wkv, state_ret, meta_tokens, norm_g, final_norm_g, ssd_w_in, ssd_conv_w, ssd_conv_b, ssd_dt_bias, ssd_a_log, ssd_d, ssd_norm_g, ssd_w_out, rwkv_mu, rwkv_w_rkvg, rwkv_w0, rwkv_w_lora_a, rwkv_w_lora_b, rwkv_a0, rwkv_a_lora_a, rwkv_a_lora_b, rwkv_k_k, rwkv_k_a, rwkv_r_k, rwkv_ln_g, rwkv_ln_b, rwkv_w_out, ret_w_in, ret_norm_g, ret_w_out):
    depth = norm_g.shape[0]
    d_model = x_prompt.shape[2]
    wts = _prep_weights(norm_g, ssd_w_in, ssd_dt_bias, ssd_a_log, ssd_d, ssd_w_out,
                        rwkv_w_rkvg, rwkv_w_lora_a, rwkv_w_lora_b, rwkv_a_lora_a, rwkv_a_lora_b,
                        rwkv_w0, rwkv_a0, rwkv_k_k, rwkv_k_a, rwkv_r_k, rwkv_ln_g, rwkv_ln_b, rwkv_w_out,
                        ret_w_in, ret_w_out)
    common = dict(depth=depth, norm_g=norm_g, final_norm_g=final_norm_g, wts=wts, ssd_conv_w=ssd_conv_w,
                  ssd_conv_b=ssd_conv_b, ssd_norm_g=ssd_norm_g, rwkv_mu=rwkv_mu, ret_norm_g=ret_norm_g)

    bp, seq, _ = x_prompt.shape
    lq_p = 128
    lo_p = lq_p - N_META
    tp_p = lo_p + N_META + seq
    h_p = jnp.concatenate([jnp.zeros((bp, lo_p, d_model), F32),
                           jnp.broadcast_to(meta_tokens[None], (bp, N_META, d_model)), x_prompt], axis=1)
    zeros_like_b = lambda s: jnp.zeros((s.shape[0], bp) + s.shape[2:], F32)
    pos_p = jnp.maximum(jnp.arange(tp_p) - lo_p, 0)
    outs_p = _trunk(h_p.reshape(bp * tp_p, d_model), zeros_like_b(state_ssd_conv), zeros_like_b(state_ssd),
                    zeros_like_b(state_rwkv_shift), zeros_like_b(state_rwkv_wkv), zeros_like_b(state_ret), pos_p,
                    bsz=bp, tp=tp_p, lq=lq_p, lo=lo_p, hi=tp_p, **common)
    y_prompt = outs_p[0].reshape(bp, tp_p, d_model)[:, lo_p + N_META:]

    bs, ds, _ = x_sample.shape
    tp_s = 16
    h_s = jnp.concatenate([x_sample, jnp.zeros((bs, tp_s - ds, d_model), F32)], axis=1)
    pos_s = PAST_LEN + jnp.arange(tp_s)
    outs_s = _trunk(h_s.reshape(bs * tp_s, d_model), state_ssd_conv, state_ssd, state_rwkv_shift, state_rwkv_wkv,
                    state_ret, pos_s, bsz=bs, tp=tp_s, lq=tp_s, lo=0, hi=ds, **common)
    y_sample = outs_s[0].reshape(bs, tp_s, d_model)[:, :ds]

    return (y_prompt, y_sample) + tuple(outs_p[1:]) + tuple(outs_s[1:])
```

```python
import functools
import math

import jax
import jax.numpy as jnp
from jax import lax
from jax.experimental import pallas as pl
from jax.experimental.pallas import tpu as pltpu

F32 = jnp.float32
BF16 = jnp.bfloat16

EPS = 1e-6
N_META = 16
HEAD_DIM = 64
SSD_STATE = 128
SSD_GROUPS = 8
SSD_HPG = 8
SSD_CONV = 4
RET_HEADS = 8
RET_QK = 256
RET_V = 512
RET_THETA_BASE = 10000.0
RWKV_LORA_PAD = 128
RWKV_CHUNK = 64
RWKV_HPB = 4
RWKV_GROUPS_PER_STEP = 8
RWKV_SHORT_HEADS = 16
RWKV_LANE_HEADS = 2
LANES = 128
PAST_LEN = 16384
RWKV_LN_EPS = 1e-5 * HEAD_DIM
KPAD = 128
NEG = -1e30
ROW_TILE = 512
MATMUL_VMEM_BUDGET = 40 * 1024 * 1024
VMEM_LIMIT = 56 * 1024 * 1024


def _cparams(sem):
    return pltpu.CompilerParams(dimension_semantics=sem, vmem_limit_bytes=VMEM_LIMIT)


def _nt(a, b):
    return lax.dot_general(a, b, (((1,), (1,)), ((), ())), preferred_element_type=F32)


def _tn(a, b):
    return lax.dot_general(a, b, (((0,), (0,)), ((), ())), preferred_element_type=F32)


def _dot(a, b):
    return jnp.dot(a, b, preferred_element_type=F32)


def _split(x, terms):
    parts = []
    r = x
    for i in range(terms):
        p = r.astype(BF16)
        parts.append(p)
        if i + 1 < terms:
            r = r - p.astype(F32)
    return parts


def _dot_split_l(x, m, terms=3):
    acc = None
    for p in _split(x, terms):
        d = _dot(p, m)
        acc = d if acc is None else acc + d
    return acc


def _dot_split_r(m, x, terms=3):
    acc = None
    for p in _split(x, terms):
        d = _dot(m, p)
        acc = d if acc is None else acc + d
    return acc


def _pad_rows(x, rows):
    if x.shape[0] == rows:
        return x
    return jnp.concatenate([x, jnp.zeros((rows - x.shape[0], x.shape[1]), x.dtype)], axis=0)


def _idiv(x, n):
    return jnp.right_shift(x, int(math.log2(n)))


def _imod(x, n):
    return jnp.bitwise_and(x, n - 1)


def _silu(x):
    h = 0.5 * x
    return h + h * jnp.tanh(h)


def _softplus(x):
    return jnp.maximum(x, 0.0) + jnp.log(1.0 + jnp.exp(-jnp.abs(x)))


def _rmsnorm_kernel(x_ref, g_ref, o_ref):
    x = x_ref[...]
    ms = jnp.mean(x * x, axis=-1, keepdims=True)
    o_ref[...] = (x * lax.rsqrt(ms + EPS) * g_ref[...]).astype(o_ref.dtype)


def _row_tile(m):
    tm = math.gcd(m, ROW_TILE)
    assert tm % 16 == 0, m
    return tm


def _matmul_tiles(m, k, n, *, a_bytes, n_a, cast, n_out):
    best = None
    for tn in (t for t in (1024, 512, 256, 128) if n % t == 0):
        for tm in (t for t in range(16, m + 1, 16) if m % t == 0):
            need = (2 * n_a * tm * k * a_bytes + (tm * k * 2 if cast else 0)
                    + 2 * k * tn * 2 + 2 * n_out * tm * tn * 4)
            if need <= MATMUL_VMEM_BUDGET and (best is None or (tm * tn, tm) > (best[0] * best[1], best[0])):
                best = (tm, tn)
    assert best is not None, (m, k, n)
    return best


def rmsnorm(x, g, out_dtype=F32):
    m, d = x.shape
    tm = _row_tile(m)
    return pl.pallas_call(
        _rmsnorm_kernel,
        grid=(m // tm,),
        in_specs=[pl.BlockSpec((tm, d), lambda i: (i, 0)), pl.BlockSpec((1, d), lambda i: (0, 0))],
        out_specs=pl.BlockSpec((tm, d), lambda i: (i, 0)),
        out_shape=jax.ShapeDtypeStruct((m, d), out_dtype),
        compiler_params=_cparams(("parallel",)),
        name="rmsnorm",
    )(x, g.reshape(1, d))


def _mm_kernel(a_ref, w_ref, *rest, has_res, cast):
    rest = list(rest)
    abf_ref = rest.pop() if cast else a_ref
    o_ref = rest.pop()

    if cast:
        @pl.when(pl.program_id(1) == 0)
        def _():
            abf_ref[...] = a_ref[...].astype(BF16)

    acc = _dot(abf_ref[...], w_ref[...])
    if has_res:
        acc = rest[0][...] + acc
    o_ref[...] = acc.astype(o_ref.dtype)


def matmul(a, w, res=None, out_dtype=F32):
    m, k = a.shape
    n = w.shape[1]
    cast = a.dtype != BF16
    tm, tn = _matmul_tiles(m, k, n, a_bytes=a.dtype.itemsize, n_a=1, cast=cast, n_out=2 if res is not None else 1)
    in_specs = [pl.BlockSpec((tm, k), lambda i, j: (i, 0)), pl.BlockSpec((k, tn), lambda i, j: (0, j))]
    args = [a, w]
    if res is not None:
        in_specs.append(pl.BlockSpec((tm, tn), lambda i, j: (i, j)))
        args.append(res)
    return pl.pallas_call(
        functools.partial(_mm_kernel, has_res=res is not None, cast=cast),
        grid=(m // tm, n // tn),
        in_specs=in_specs,
        out_specs=pl.BlockSpec((tm, tn), lambda i, j: (i, j)),
        out_shape=jax.ShapeDtypeStruct((m, n), out_dtype),
        scratch_shapes=[pltpu.VMEM((tm, k), BF16)] if cast else [],
        compiler_params=_cparams(("parallel", "arbitrary")),
        name="matmul_res" if res is not None else "matmul",
    )(*args)


def _mixmm_kernel(u_ref, p_ref, mu_ref, w_ref, o_ref, xm_ref):
    @pl.when(pl.program_id(2) == 0)
    def _():
        u = u_ref[...]
        xm_ref[...] = (u + (p_ref[...] - u) * mu_ref[0]).astype(BF16)

    o_ref[0] = _dot(xm_ref[...], w_ref[0]).astype(o_ref.dtype)


def mix_matmul(u, prev, mu, w, out_dtype=F32):
    m, k = u.shape
    s, _, n = w.shape
    tm, tn = _matmul_tiles(m, k, n, a_bytes=4, n_a=2, cast=True, n_out=1)
    return pl.pallas_call(
        _mixmm_kernel,
        grid=(m // tm, s, n // tn),
        in_specs=[
            pl.BlockSpec((tm, k), lambda i, si, j: (i, 0)),
            pl.BlockSpec((tm, k), lambda i, si, j: (i, 0)),
            pl.BlockSpec((1, 1, k), lambda i, si, j: (si, 0, 0)),
            pl.BlockSpec((1, k, tn), lambda i, si, j: (si, 0, j)),
        ],
        out_specs=pl.BlockSpec((1, tm, tn), lambda i, si, j: (si, i, j)),
        out_shape=jax.ShapeDtypeStruct((s, m, n), out_dtype),
        scratch_shapes=[pltpu.VMEM((tm, k), BF16)],
        compiler_params=_cparams(("parallel", "arbitrary", "arbitrary")),
        name="mix_matmul",
    )(u, prev, mu.reshape(s, 1, k), w)


def _conv_silu(cur, car_ref, cols, w, b, lq):
    car_ref[8:8 + lq, cols] = cur
    acc = b + cur * w[SSD_CONV - 1:SSD_CONV]
    for s in range(1, SSD_CONV):
        acc = acc + car_ref[8 - s:8 - s + lq, cols] * w[SSD_CONV - 1 - s:SSD_CONV - s]
    car_ref[0:8, cols] = cur[lq - 8:]
    return _silu(acc)


def _ssd_kernel(z_ref, x_ref, b_ref, c_ref, dt_ref, ci_ref, s0_ref, cw_ref, cb_ref, dtb_ref, alog_ref,
                d_ref, ng_ref, *rest, lq, lo, hi, nchunks):
    y_ref, sout_ref, car, st = rest[-4:]
    c = pl.program_id(1)
    gw = SSD_HPG * HEAD_DIM

    @pl.when(c == 0)
    def _():
        car[0:8, :] = ci_ref[0]
        st[...] = s0_ref[0, 0].reshape(SSD_GROUPS * gw, SSD_STATE)

    gens = [_ssd_group(g, z_ref, x_ref, b_ref, c_ref, dt_ref, cw_ref, cb_ref, dtb_ref, alog_ref, d_ref,
                       ng_ref, y_ref, car, st, lq=lq, lo=lo, hi=hi) for g in range(SSD_GROUPS)]
    for _ in zip(*gens):
        pass

    @pl.when(c == nchunks - 1)
    def _():
        sout_ref[0, 0] = st[...].reshape(SSD_GROUPS * SSD_HPG, HEAD_DIM, SSD_STATE)


def _ssd_group(g, z_ref, x_ref, b_ref, c_ref, dt_ref, cw_ref, cb_ref, dtb_ref, alog_ref, d_ref, ng_ref,
               y_ref, car, st, *, lq, lo, hi):
    c = pl.program_id(1)
    gw = SSD_HPG * HEAD_DIM
    d_inner = SSD_GROUPS * gw
    xs = slice(g * gw, (g + 1) * gw)
    ns = slice(g * SSD_STATE, (g + 1) * SSD_STATE)
    bs = slice(d_inner + g * SSD_STATE, d_inner + (g + 1) * SSD_STATE)
    cs_ = slice(d_inner + (SSD_GROUPS + g) * SSD_STATE, d_inner + (SSD_GROUPS + g + 1) * SSD_STATE)

    f32 = lambda x: x.astype(F32)
    xc = _conv_silu(f32(x_ref[:, xs]), car, xs, cw_ref[:, xs], cb_ref[:, xs], lq)
    bc = _conv_silu(f32(b_ref[:, ns]), car, bs, cw_ref[:, bs], cb_ref[:, bs], lq)
    cc = _conv_silu(f32(c_ref[:, ns]), car, cs_, cw_ref[:, cs_], cb_ref[:, cs_], lq)

    pos = c * lq + lax.broadcasted_iota(jnp.int32, (lq, 1), 0)
    valid = (pos >= lo) & (pos < hi)
    lane = lax.broadcasted_iota(jnp.int32, (lq, 128), 1)
    dt = _softplus(dt_ref[:, ns] + dtb_ref[:, ns])
    dt = jnp.where(valid & (lane < SSD_HPG), dt, 0.0)
    la = dt * (-jnp.exp(alog_ref[:, ns]))
    yield

    la_pad = _pad_rows(la, KPAD)
    ri = lax.broadcasted_iota(jnp.int32, (KPAD, KPAD), 0)
    ci = lax.broadcasted_iota(jnp.int32, (KPAD, KPAD), 1)
    tril = (ri >= ci).astype(BF16)
    triu = (ri <= ci).astype(BF16)
    acum_full = _dot_split_r(tril, la_pad)
    acum_t = _dot_split_l(la_pad.T, triu)
    yield
    acum = acum_full[:lq]
    a_end = acum_full[KPAD - 1:KPAD]
    dec_end = jnp.exp(a_end - acum_full)
    e_in = jnp.exp(acum)
    cd = jnp.exp(a_end)

    bcp = _pad_rows(bc, KPAD).astype(BF16)
    ccb = cc.astype(BF16)
    g_sc = _nt(ccb, bcp)
    st_old = st[xs, :]
    y_in = _nt(ccb, st_old.astype(BF16))
    yield

    qi = lax.broadcasted_iota(jnp.int32, (lq, KPAD), 0)
    kj = lax.broadcasted_iota(jnp.int32, (lq, KPAD), 1)
    causal = kj <= qi
    lane_q = lax.broadcasted_iota(jnp.int32, (lq, 128), 1) < HEAD_DIM
    lane_k = lax.broadcasted_iota(jnp.int32, (KPAD, 128), 1) < HEAD_DIM
    row_k = lax.broadcasted_iota(jnp.int32, (128, 1), 0) < HEAD_DIM
    dvec = d_ref[:, ns]

    ys = []
    for p in range(SSD_HPG // 2):
        h0, h1 = 2 * p, 2 * p + 1
        xp = xc[:, 128 * p:128 * (p + 1)]
        vp = xp * jnp.where(lane_q, dt[:, h0:h0 + 1], dt[:, h1:h1 + 1])
        vpp = _pad_rows(vp, KPAD)
        yp = y_in[:, 128 * p:128 * (p + 1)] * jnp.where(lane_q, e_in[:, h0:h0 + 1], e_in[:, h1:h1 + 1])
        yp = yp + xp * jnp.where(lane_q, dvec[:, h0:h0 + 1], dvec[:, h1:h1 + 1])
        for hh, h in ((0, h0), (1, h1)):
            seg = acum[:, h:h + 1] - acum_t[h:h + 1, :]
            lm = jnp.exp(jnp.where(causal, seg, NEG))
            pm = (g_sc * lm).astype(BF16)
            vm = jnp.where(lane_k if hh == 0 else jnp.logical_not(lane_k), vpp, 0.0).astype(BF16)
            yp = yp + _dot(pm, vm)
        ys.append(yp)
        vend = vpp * jnp.where(lane_k, dec_end[:, h0:h0 + 1], dec_end[:, h1:h1 + 1])
        upd = _tn(vend.astype(BF16), bcp)
        cdp = jnp.where(row_k, cd[:, h0:h0 + 1], cd[:, h1:h1 + 1])
        st[g * gw + 128 * p:g * gw + 128 * (p + 1), :] = st_old[128 * p:128 * (p + 1), :] * cdp + upd
        yield

    y = jnp.concatenate(ys, axis=1)
    y = y * _silu(f32(z_ref[:, xs]))
    ms = jnp.mean(y * y, axis=-1, keepdims=True)
    y = y * lax.rsqrt(ms + EPS) * ng_ref[:, xs]
    y_ref[:, xs] = jnp.where(valid, y, 0.0).astype(BF16)
    yield


def ssd_scan(proj, dt_raw, conv_init, s0_all, layer, s_buf, conv_w, conv_b, dtb, alog, dskip, norm_g,
             *, bsz, tp, lq, lo, hi):
    m = proj.shape[0]
    heads = SSD_GROUPS * SSD_HPG
    state_spec = pl.BlockSpec((1, 1, heads, HEAD_DIM, SSD_STATE), lambda b, c: (layer, b, 0, 0, 0))
    extra_specs, extra_args, aliases = [], [], {}
    if s_buf is not None:
        extra_specs, extra_args, aliases = [pl.BlockSpec(memory_space=pl.ANY)], [s_buf], {13: 1}
    nch = tp // lq
    d_inner = heads * HEAD_DIM
    gn = SSD_GROUPS * SSD_STATE
    conv_dim = d_inner + 2 * gn
    row = lambda b, c: b * nch + c
    const = lambda b, c: (0, 0)
    kern = functools.partial(_ssd_kernel, lq=lq, lo=lo, hi=hi, nchunks=nch)
    return pl.pallas_call(
        kern,
        grid=(bsz, nch),
        in_specs=[
            pl.BlockSpec((lq, d_inner), lambda b, c: (row(b, c), 0)),
            pl.BlockSpec((lq, d_inner), lambda b, c: (row(b, c), 1)),
            pl.BlockSpec((lq, gn), lambda b, c: (row(b, c), 2 * d_inner // gn)),
            pl.BlockSpec((lq, gn), lambda b, c: (row(b, c), 2 * d_inner // gn + 1)),
            pl.BlockSpec((lq, gn), lambda b, c: (row(b, c), 0)),
            pl.BlockSpec((1, 8, conv_dim), lambda b, c: (b, 0, 0)),
            state_spec,
            pl.BlockSpec((SSD_CONV, conv_dim), const),
            pl.BlockSpec((1, conv_dim), const),
            pl.BlockSpec((1, gn), const),
            pl.BlockSpec((1, gn), const),
            pl.BlockSpec((1, gn), const),
            pl.BlockSpec((1, d_inner), const),
        ] + extra_specs,
        out_specs=[
            pl.BlockSpec((lq, d_inner), lambda b, c: (row(b, c), 0)),
            state_spec,
        ],
        out_shape=[
            jax.ShapeDtypeStruct((m, d_inner), BF16),
            jax.ShapeDtypeStruct(s0_all.shape, F32),
        ],
        scratch_shapes=[pltpu.VMEM((8 + lq, conv_dim), F32), pltpu.VMEM((d_inner, SSD_STATE), F32)],
        input_output_aliases=aliases,
        compiler_params=_cparams(("parallel", "arbitrary")),
        name="ssd_scan",
    )(proj, proj, proj, proj, dt_raw, conv_init, s0_all, conv_w, conv_b.reshape(1, -1),
      dtb.reshape(1, -1), alog.reshape(1, -1), dskip.reshape(1, -1), norm_g.reshape(1, -1), *extra_args)


def _ret_kernel(q_ref, k_ref, v_ref, g_ref, cos_ref, sin_ref, lg_ref, s0_ref, ng_ref,
                y_ref, sout_ref, *, lq, lo, hi):
    c = pl.program_id(1)

    @pl.when(c == 0)
    def _():
        sout_ref[...] = s0_ref[...]

    gens = [_ret_head(h, q_ref, k_ref, v_ref, g_ref, cos_ref, sin_ref, lg_ref, ng_ref, y_ref, sout_ref,
                      lq=lq, lo=lo, hi=hi) for h in range(RET_HEADS)]
    for _ in zip(*gens):
        pass


def _ret_head(h, q_ref, k_ref, v_ref, g_ref, cos_ref, sin_ref, lg_ref, ng_ref, y_ref, st_ref, *, lq, lo, hi):
    c = pl.program_id(1)
    qs = slice(h * RET_QK, (h + 1) * RET_QK)
    vs = slice(h * RET_V, (h + 1) * RET_V)
    lg = lg_ref[h][0:1, 0:1]
    nv = float(hi - lo)

    def count(p):
        return jnp.clip((p + 1 - lo).astype(F32), 0.0, nv)

    base = c * lq
    pos_i = base + lax.broadcasted_iota(jnp.int32, (lq, 1), 0)
    valid = (pos_i >= lo) & (pos_i < hi)
    cnt_i = count(pos_i)
    cnt_j = count(base + lax.broadcasted_iota(jnp.int32, (1, KPAD), 1))
    cnt_jc = count(base + lax.broadcasted_iota(jnp.int32, (KPAD, 1), 0))
    cnt0 = count(base - 1 + jnp.zeros((1, 1), jnp.int32))
    cnt_end = count(base + lq - 1 + jnp.zeros((1, 1), jnp.int32))

    cos = cos_ref[...]
    sin = sin_ref[...]
    half = RET_QK // 2

    def rot(x):
        x1, x2 = x[:, :half], x[:, half:]
        return jnp.concatenate([x1 * cos - x2 * sin, x1 * sin + x2 * cos], axis=1)

    qr = rot(q_ref[:, qs].astype(F32)).astype(BF16)
    kr = jnp.where(valid, rot(k_ref[:, qs].astype(F32)) * (RET_QK ** -0.5), 0.0)
    v = jnp.where(valid, v_ref[:, vs].astype(F32), 0.0)
    krp = _pad_rows(kr, KPAD).astype(BF16)
    vp = _pad_rows(v, KPAD)
    yield

    sc = _nt(qr, krp)
    s_old = st_ref[0, h]
    y_in = _nt(qr, s_old.astype(BF16))
    yield
    qi = lax.broadcasted_iota(jnp.int32, (lq, KPAD), 0)
    kj = lax.broadcasted_iota(jnp.int32, (lq, KPAD), 1)
    dm = jnp.exp(jnp.where(kj <= qi, lg * (cnt_i - cnt_j), NEG))
    y = _dot((sc * dm).astype(BF16), vp.astype(BF16))
    y = y + y_in * jnp.exp(lg * (cnt_i - cnt0))
    vend = vp * jnp.exp(lg * (cnt_end - cnt_jc))
    st_ref[0, h] = s_old * jnp.exp(lg * (cnt_end - cnt0)) + _tn(vend.astype(BF16), krp)
    yield

    ms = jnp.mean(y * y, axis=-1, keepdims=True)
    y = y * lax.rsqrt(ms + EPS) * ng_ref[:, vs] * _silu(g_ref[:, vs].astype(F32))
    y_ref[:, vs] = jnp.where(valid, y, 0.0).astype(BF16)
    yield


def ret_scan(proj, cos, sin, lg, s0, norm_g, *, bsz, tp, lq, lo, hi):
    m = proj.shape[0]
    nch = tp // lq
    d_inner = RET_HEADS * RET_V
    d_qk = RET_HEADS * RET_QK
    row = lambda b, c: b * nch + c
    state_spec = pl.BlockSpec((1, RET_HEADS, RET_V, RET_QK), lambda b, c: (b, 0, 0, 0))
    kern = functools.partial(_ret_kernel, lq=lq, lo=lo, hi=hi)
    return pl.pallas_call(
        kern,
        grid=(bsz, nch),
        in_specs=[
            pl.BlockSpec((lq, d_qk), lambda b, c: (row(b, c), 0)),
            pl.BlockSpec((lq, d_qk), lambda b, c: (row(b, c), 1)),
            pl.BlockSpec((lq, d_inner), lambda b, c: (row(b, c), 2 * d_qk // d_inner)),
            pl.BlockSpec((lq, d_inner), lambda b, c: (row(b, c), 2 * d_qk // d_inner + 1)),
            pl.BlockSpec((lq, RET_QK // 2), lambda b, c: (c, 0)),
            pl.BlockSpec((lq, RET_QK // 2), lambda b, c: (c, 0)),
            pl.BlockSpec((RET_HEADS, 8, 128), lambda b, c: (0, 0, 0)),
            state_spec,
            pl.BlockSpec((1, d_inner), lambda b, c: (0, 0)),
        ],
        out_specs=[
            pl.BlockSpec((lq, d_inner), lambda b, c: (row(b, c), 0)),
            state_spec,
        ],
        out_shape=[
            jax.ShapeDtypeStruct((m, d_inner), BF16),
            jax.ShapeDtypeStruct((bsz, RET_HEADS, RET_V, RET_QK), F32),
        ],
        compiler_params=_cparams(("parallel", "arbitrary")),
        name="ret_scan",
    )(proj, proj, proj, proj, cos, sin, lg, s0, norm_g.reshape(1, -1))


def _rwkv_kernel(r_ref, k_ref, v_ref, g_ref, lw_ref, la_ref, bw_ref, ba_ref, par_ref, s0_ref,
                 y_ref, sout_ref, st, *, lo, hi, nchunks, ngrp):
    c = pl.program_id(2)
    w4 = RWKV_HPB * HEAD_DIM
    r2 = lax.broadcasted_iota(jnp.int32, (w4, w4), 0)
    c2 = lax.broadcasted_iota(jnp.int32, (w4, w4), 1)
    blk = _idiv(r2, HEAD_DIM) == _idiv(c2, HEAD_DIM)

    @pl.when(c == 0)
    def _():
        tile = (lax.broadcasted_iota(jnp.int32, (HEAD_DIM, w4), 0)
                == _imod(lax.broadcasted_iota(jnp.int32, (HEAD_DIM, w4), 1), HEAD_DIM)).astype(BF16)
        for gi in range(ngrp):
            s0 = s0_ref[0, gi * RWKV_HPB:(gi + 1) * RWKV_HPB].reshape(w4, HEAD_DIM)
            st[gi] = jnp.where(blk, _dot_split_l(s0, tile), 0.0)

    s_new = [None] * ngrp
    gens = [_rwkv_group(gi, r_ref, k_ref, v_ref, g_ref, lw_ref, la_ref, bw_ref, ba_ref, par_ref,
                        y_ref, st, s_new, lo=lo, hi=hi) for gi in range(ngrp)]
    for _ in zip(*gens):
        pass

    @pl.when(c == nchunks - 1)
    def _():
        tile_t = (_imod(lax.broadcasted_iota(jnp.int32, (w4, HEAD_DIM), 0), HEAD_DIM)
                  == lax.broadcasted_iota(jnp.int32, (w4, HEAD_DIM), 1)).astype(BF16)
        for gi in range(ngrp):
            sout_ref[0, gi * RWKV_HPB:(gi + 1) * RWKV_HPB] = _dot_split_l(s_new[gi], tile_t).reshape(
                RWKV_HPB, HEAD_DIM, HEAD_DIM)


def _rwkv_group(gi, r_ref, k_ref, v_ref, g_ref, lw_ref, la_ref, bw_ref, ba_ref, par_ref, y_ref, st, out,
                *, lo, hi):
    c = pl.program_id(2)
    cs = RWKV_CHUNK
    w4 = RWKV_HPB * HEAD_DIM
    sl = slice(gi * w4, (gi + 1) * w4)

    r2 = lax.broadcasted_iota(jnp.int32, (w4, w4), 0)
    c2 = lax.broadcasted_iota(jnp.int32, (w4, w4), 1)
    blk = _idiv(r2, HEAD_DIM) == _idiv(c2, HEAD_DIM)
    ones_bd = blk.astype(BF16)

    def segsum(x):
        return _dot_split_l(x, ones_bd, terms=2)

    par = par_ref[:, sl]
    w0, a0, k_k, k_a, r_k, ln_g, ln_b = (par[i:i + 1] for i in range(7))

    r = r_ref[0, :, sl].astype(F32)
    k = k_ref[0, :, sl].astype(F32)
    v = v_ref[0, :, sl].astype(F32)
    g = g_ref[0, :, sl].astype(F32)
    w_raw = w0 + _dot(jnp.tanh(lw_ref[0]).astype(BF16), bw_ref[0, :, sl])
    a = jax.nn.sigmoid(a0 + _dot(la_ref[0].astype(BF16), ba_ref[0, :, sl]))

    ti = lax.broadcasted_iota(jnp.int32, (cs, 1), 0)
    pos = c * cs + ti
    valid = (pos >= lo) & (pos < hi)

    lw = -jnp.exp(-_softplus(-w_raw) - 0.5)
    kk = k * k_k
    kk = kk / jnp.maximum(jnp.sqrt(segsum(kk * kk)), 1e-12)
    yield
    kp = k * (1.0 + (a - 1.0) * k_a)
    lw = jnp.where(valid, lw, 0.0)
    kk = jnp.where(valid, kk, 0.0)
    kp = jnp.where(valid, kp, 0.0)
    vm = jnp.where(valid, v, 0.0)

    tri = (lax.broadcasted_iota(jnp.int32, (cs, cs), 0) >= lax.broadcasted_iota(jnp.int32, (cs, cs), 1)).astype(BF16)
    cw = _dot_split_r(tri, lw)
    yield
    cwl = cw[cs - 1:cs]
    wt = jnp.exp(cw)
    wi = jnp.exp(-cw)
    wend = jnp.exp(cwl - cw)
    b = kk * a
    at = -kk * jnp.exp(cw - lw)
    rt = r * wt

    def bd(x):
        return jnp.where(blk, jnp.concatenate([x] * RWKV_HPB, axis=0), 0.0).astype(BF16)

    lhs = jnp.concatenate([bd(at), bd(rt)], axis=0)
    rhs = jnp.concatenate([bd(b * wi), bd(kp * wi)], axis=0)
    sc = _nt(lhs, rhs)
    yield
    tt = _imod(r2, cs)
    jj = _imod(c2, cs)
    strict = blk & (tt > jj)
    incl = blk & (tt >= jj)
    mab = jnp.where(strict, sc[:w4, :w4], 0.0)
    mak = jnp.where(strict, sc[:w4, w4:], 0.0)
    nrb = jnp.where(incl, sc[w4:, :w4], 0.0)
    nrk = jnp.where(incl, sc[w4:, w4:], 0.0)

    tinv = (r2 == c2).astype(F32) + jnp.where(
        (_idiv(r2, 2) == _idiv(c2, 2)) & (_imod(tt, 2) == 1) & (_imod(jj, 2) == 0), mab, 0.0)
    msz = 2
    while msz < cs:
        off = ((_idiv(r2, 2 * msz) == _idiv(c2, 2 * msz)) & (_imod(tt, 2 * msz) >= msz)
               & (_imod(jj, 2 * msz) < msz))
        tb = tinv.astype(BF16)
        tno = _dot(tb, jnp.where(off, mab, 0.0).astype(BF16)).astype(BF16)
        yield
        tinv = tinv + _dot(tno, tb)
        yield
        msz *= 2

    s_old = st[gi]
    x = _nt(lhs, s_old.astype(BF16))
    yield
    vbd = bd(vm)
    u = _dot(tinv.astype(BF16), (x[:w4] + _dot(mak.astype(BF16), vbd)).astype(BF16))
    yield
    ub = u.astype(BF16)
    yb = x[w4:] + _dot(nrb.astype(BF16), ub) + _dot(nrk.astype(BF16), vbd)
    yield
    s_new = s_old * wt[cs - 1:cs] + _tn(jnp.concatenate([ub, vbd], axis=0),
                                        jnp.concatenate([bd(b * wend), bd(kp * wend)], axis=0))
    st[gi] = s_new
    out[gi] = s_new
    yield

    y = yb[0:cs] + yb[cs:2 * cs] + yb[2 * cs:3 * cs] + yb[3 * cs:4 * cs]
    inv = 1.0 / HEAD_DIM
    yc = y - segsum(y) * inv
    yield
    y = yc * lax.rsqrt(segsum(yc * yc) * inv + RWKV_LN_EPS) * ln_g + ln_b
    y = (y + segsum(r * kp * r_k) * v) * _silu(g)
    y_ref[:, sl] = jnp.where(valid, y, 0.0).astype(BF16)
    yield


def rwkv_scan(rkvg, lora1, lora_b, par, s0, *, bsz, tp, lo, hi):
    _, m, e = rkvg.shape
    cs = RWKV_CHUNK
    nch = tp // cs
    heads = e // HEAD_DIM
    ngrp = RWKV_GROUPS_PER_STEP
    hb = RWKV_HPB * ngrp
    wb = hb * HEAD_DIM
    row = lambda b, h, c: b * nch + c
    kern = functools.partial(_rwkv_kernel, lo=lo, hi=hi, nchunks=nch, ngrp=ngrp)
    proj_spec = lambda s: pl.BlockSpec((1, cs, wb), lambda b, h, c: (s, row(b, h, c), h))
    return pl.pallas_call(
        kern,
        grid=(bsz, heads // hb, nch),
        in_specs=[
            proj_spec(0), proj_spec(1), proj_spec(2), proj_spec(3),
            pl.BlockSpec((1, cs, RWKV_LORA_PAD), lambda b, h, c: (0, row(b, h, c), 0)),
            pl.BlockSpec((1, cs, RWKV_LORA_PAD), lambda b, h, c: (1, row(b, h, c), 0)),
            pl.BlockSpec((1, RWKV_LORA_PAD, wb), lambda b, h, c: (0, 0, h)),
            pl.BlockSpec((1, RWKV_LORA_PAD, wb), lambda b, h, c: (1, 0, h)),
            pl.BlockSpec((8, wb), lambda b, h, c: (0, h)),
            pl.BlockSpec((1, hb, HEAD_DIM, HEAD_DIM), lambda b, h, c: (b, h, 0, 0)),
        ],
        out_specs=[
            pl.BlockSpec((cs, wb), lambda b, h, c: (row(b, h, c), h)),
            pl.BlockSpec((1, hb, HEAD_DIM, HEAD_DIM), lambda b, h, c: (b, h, 0, 0)),
        ],
        out_shape=[
            jax.ShapeDtypeStruct((m, e), BF16),
            jax.ShapeDtypeStruct((bsz, heads, HEAD_DIM, HEAD_DIM), F32),
        ],
        scratch_shapes=[pltpu.VMEM((ngrp, RWKV_HPB * HEAD_DIM, RWKV_HPB * HEAD_DIM), F32)],
        compiler_params=_cparams(("parallel", "parallel", "arbitrary")),
        name="rwkv_scan",
    )(rkvg, rkvg, rkvg, rkvg, lora1, lora1, lora_b, lora_b, par, s0)


def _rwkv_short_kernel(r_ref, k_ref, v_ref, g_ref, lw_ref, la_ref, bw_ref, ba_ref, par_ref, s0_ref,
                       ones_ref, tile_ref, y_ref, sout_ref, *, nvalid, rows_out):
    ct = 8
    hp = RWKV_SHORT_HEADS
    wl = hp * HEAD_DIM
    nr = hp * ct
    ones_bd = ones_ref[...]
    tile_t = tile_ref[...]

    def segsum(x):
        return jnp.concatenate(
            [_dot_split_l(x[:, 256 * j:256 * (j + 1)], ones_bd, terms=2) for j in range(wl // 256)], axis=1)

    par = par_ref[...]
    w0, a0, k_k, k_a, r_k, ln_g, ln_b = (par[i:i + 1] for i in range(7))
    r, k, v, g = r_ref[0], k_ref[0], v_ref[0], g_ref[0]
    w_raw = w0 + _dot(jnp.tanh(lw_ref[0]).astype(BF16), bw_ref[0])
    a = jax.nn.sigmoid(a0 + _dot(la_ref[0].astype(BF16), ba_ref[0]))

    ti = lax.broadcasted_iota(jnp.int32, (ct, 1), 0)
    valid = ti < nvalid
    lw = jnp.where(valid, -jnp.exp(-_softplus(-w_raw) - 0.5), 0.0)
    kk = k * k_k
    kk = jnp.where(valid, kk / jnp.maximum(jnp.sqrt(segsum(kk * kk)), 1e-12), 0.0)
    kp = jnp.where(valid, k * (1.0 + (a - 1.0) * k_a), 0.0)
    vm = jnp.where(valid, v, 0.0)

    cw = lw
    for s in (1, 2, 4):
        cw = cw + jnp.where(ti >= s, pltpu.roll(cw, s, 0), 0.0)
    cwl = cw[ct - 1:ct]
    wend = jnp.exp(cwl - cw)
    wi = jnp.exp(-cw)
    b = kk * a
    at = -kk * jnp.exp(cw - lw)
    rt = r * jnp.exp(cw)

    rr = lax.broadcasted_iota(jnp.int32, (nr, wl), 0)
    cc = lax.broadcasted_iota(jnp.int32, (nr, wl), 1)
    blk = _idiv(rr, ct) == _idiv(cc, HEAD_DIM)

    def bd(x):
        return jnp.where(blk, jnp.concatenate([x] * hp, axis=0), 0.0).astype(BF16)

    lhs = jnp.concatenate([bd(at), bd(rt)], axis=0)
    rhs = jnp.concatenate([bd(b * wi), bd(kp * wi)], axis=0)
    sc = _nt(lhs, rhs)
    ri = lax.broadcasted_iota(jnp.int32, (2 * nr, 2 * nr), 0)
    ci = lax.broadcasted_iota(jnp.int32, (2 * nr, 2 * nr), 1)
    same = _idiv(_imod(ri, nr), ct) == _idiv(_imod(ci, nr), ct)
    tt = _imod(ri, ct)
    jj = _imod(ci, ct)
    sc = jnp.where(same & (tt + (ri >= nr).astype(jnp.int32) > jj), sc, 0.0)
    mab, mak, nrb, nrk = sc[:nr, :nr], sc[:nr, nr:], sc[nr:, :nr], sc[nr:, nr:]

    r1 = lax.broadcasted_iota(jnp.int32, (nr, nr), 0)
    c1 = lax.broadcasted_iota(jnp.int32, (nr, nr), 1)
    t1 = _imod(r1, ct)
    j1 = _imod(c1, ct)
    tinv = (r1 == c1).astype(F32) + jnp.where(
        (_idiv(r1, 2) == _idiv(c1, 2)) & (_imod(t1, 2) == 1) & (_imod(j1, 2) == 0), mab, 0.0)
    msz = 2
    while msz < nvalid:
        off = ((_idiv(r1, 2 * msz) == _idiv(c1, 2 * msz)) & (_imod(t1, 2 * msz) >= msz)
               & (_imod(j1, 2 * msz) < msz))
        tb = tinv.astype(BF16)
        tinv = tinv + _dot(_dot(tb, jnp.where(off, mab, 0.0).astype(BF16)).astype(BF16), tb)
        msz *= 2

    s_old = s0_ref[0].reshape(wl, HEAD_DIM)
    lhs_rows = _dot(lhs, tile_t).astype(BF16)
    x = _nt(lhs_rows, s_old.astype(BF16))
    xa = jnp.where(blk, x[:nr], 0.0)
    xr = jnp.where(blk, x[nr:], 0.0)
    vbd = bd(vm)
    u = _dot(tinv.astype(BF16), (xa + _dot(mak.astype(BF16), vbd)).astype(BF16))
    uv = jnp.concatenate([u.astype(BF16), vbd], axis=0)
    yb = xr + _dot(jnp.concatenate([nrb, nrk], axis=1).astype(BF16), uv)
    bk_rows = _dot(jnp.concatenate([bd(b * wend), bd(kp * wend)], axis=0), tile_t).astype(BF16)
    ds = _tn(uv, bk_rows)

    r16 = lax.broadcasted_iota(jnp.int32, (hp, wl), 0)
    c16 = lax.broadcasted_iota(jnp.int32, (hp, wl), 1)
    wc = jnp.exp(cwl)
    wc_rows = _dot_split_l(jnp.where(r16 == _idiv(c16, HEAD_DIM), wc, 0.0), tile_t)
    rsel = lax.broadcasted_iota(jnp.int32, (hp, HEAD_DIM), 0)
    for h in range(hp):
        rs = slice(h * HEAD_DIM, (h + 1) * HEAD_DIM)
        wc_h = jnp.sum(jnp.where(rsel == h, wc_rows, 0.0), axis=0, keepdims=True)
        sout_ref[0, h] = s_old[rs] * wc_h + ds[rs]

    y = yb[0:ct]
    for h in range(1, hp):
        y = y + yb[h * ct:(h + 1) * ct]
    inv = 1.0 / HEAD_DIM
    yc = y - segsum(y) * inv
    y = yc * lax.rsqrt(segsum(yc * yc) * inv + RWKV_LN_EPS) * ln_g + ln_b
    y = (y + segsum(r * kp * r_k) * v) * _silu(g)
    y = jnp.where(valid, y, 0.0)
    y_ref[...] = _pad_rows(y, rows_out).astype(BF16)


def rwkv_short(rkvg, lora1, lora_b, par, s0, *, bsz, tp, hi):
    _, m, e = rkvg.shape
    heads = e // HEAD_DIM
    hp = RWKV_SHORT_HEADS
    wl = hp * HEAD_DIM
    rb = tp // 8
    w4 = RWKV_HPB * HEAD_DIM
    ones_bd = (jnp.arange(w4)[:, None] // HEAD_DIM == jnp.arange(w4)[None, :] // HEAD_DIM).astype(BF16)
    tile_t = (jnp.arange(wl)[:, None] % HEAD_DIM == jnp.arange(HEAD_DIM)[None, :]).astype(BF16)
    kern = functools.partial(_rwkv_short_kernel, nvalid=hi, rows_out=tp)
    proj_spec = lambda s: pl.BlockSpec((1, 8, wl), lambda b, h: (s, b * rb, h))
    return pl.pallas_call(
        kern,
        grid=(bsz, heads // hp),
        in_specs=[
            proj_spec(0), proj_spec(1), proj_spec(2), proj_spec(3),
            pl.BlockSpec((1, 8, RWKV_LORA_PAD), lambda b, h: (0, b * rb, 0)),
            pl.BlockSpec((1, 8, RWKV_LORA_PAD), lambda b, h: (1, b * rb, 0)),
            pl.BlockSpec((1, RWKV_LORA_PAD, wl), lambda b, h: (0, 0, h)),
            pl.BlockSpec((1, RWKV_LORA_PAD, wl), lambda b, h: (1, 0, h)),
            pl.BlockSpec((8, wl), lambda b, h: (0, h)),
            pl.BlockSpec((1, hp, HEAD_DIM, HEAD_DIM), lambda b, h: (b, h, 0, 0)),
            pl.BlockSpec((w4, w4), lambda b, h: (0, 0)),
            pl.BlockSpec((wl, HEAD_DIM), lambda b, h: (0, 0)),
        ],
        out_specs=[
            pl.BlockSpec((tp, wl), lambda b, h: (b, h)),
            pl.BlockSpec((1, hp, HEAD_DIM, HEAD_DIM), lambda b, h: (b, h, 0, 0)),
        ],
        out_shape=[
            jax.ShapeDtypeStruct((m, e), BF16),
            jax.ShapeDtypeStruct((bsz, heads, HEAD_DIM, HEAD_DIM), F32),
        ],
        compiler_params=_cparams(("parallel", "parallel")),
        name="rwkv_short",
    )(rkvg, rkvg, rkvg, rkvg, lora1, lora1, lora_b, lora_b, par, s0, ones_bd, tile_t)


def _rwkv_lanes_kernel(r_ref, k_ref, v_ref, g_ref, lw_ref, la_ref, bw_ref, ba_ref, par_ref, s0_ref,
                       y_ref, sout_ref, tk, tw, tb, tq, tr, tv, ty, *, nt, nb):
    hp = RWKV_LANE_HEADS
    wl = hp * HEAD_DIM
    ri = lax.broadcasted_iota(jnp.int32, (wl, wl), 0)
    ci = lax.broadcasted_iota(jnp.int32, (wl, wl), 1)
    ones_bd = (_idiv(ri, HEAD_DIM) == _idiv(ci, HEAD_DIM)).astype(BF16)

    def segsum(x):
        return _dot_split_l(x, ones_bd, terms=2)

    par = par_ref[...]
    w0, a0, k_k, k_a, r_k, ln_g, ln_b = (par[i:i + 1] for i in range(7))
    r, k, v, g = (x[0].astype(F32) for x in (r_ref, k_ref, v_ref, g_ref))
    w_raw = w0 + _dot(jnp.tanh(lw_ref[0]).astype(BF16), bw_ref[0])
    a = jax.nn.sigmoid(a0 + _dot(la_ref[0].astype(BF16), ba_ref[0]))
    decay = jnp.exp(-jnp.exp(-_softplus(-w_raw) - 0.5))
    kk = k * k_k
    kk = kk / jnp.maximum(jnp.sqrt(segsum(kk * kk)), 1e-12)
    kp = k * (1.0 + (a - 1.0) * k_a)
    bb = kk * a

    for t in range(nt):
        rows = slice(t * nb, (t + 1) * nb)
        tk[t] = (-kk[rows]).T
        tw[t] = decay[rows].T
        tb[t] = bb[rows].T
        tq[t] = kp[rows].T
        tr[t] = r[rows].T
        tv[t] = v[rows].T

    for hh in range(hp):
        ks = slice(hh * HEAD_DIM, (hh + 1) * HEAD_DIM)

        def body(vi, carry, hh=hh, ks=ks):
            sv = s0_ref[hh, vi]
            row = hh * HEAD_DIM + vi
            for t in range(nt):
                sa = jnp.sum(sv * tk[t, ks, :], axis=0, keepdims=True)
                sv = sv * tw[t, ks, :] + sa * tb[t, ks, :] + tv[t, pl.ds(row, 1), :] * tq[t, ks, :]
                ty[t, pl.ds(row, 1), :] = jnp.sum(sv * tr[t, ks, :], axis=0, keepdims=True)
            sout_ref[hh, vi] = sv
            return carry

        lax.fori_loop(0, HEAD_DIM, body, 0, unroll=4)

    inv = 1.0 / HEAD_DIM
    for t in range(nt):
        rows = slice(t * nb, (t + 1) * nb)
        y = ty[t].T
        yc = y - segsum(y) * inv
        y = yc * lax.rsqrt(segsum(yc * yc) * inv + RWKV_LN_EPS) * ln_g + ln_b
        y = (y + segsum(r[rows] * kp[rows] * r_k) * v[rows]) * _silu(g[rows])
        y_ref[rows, :] = y.astype(BF16)


def rwkv_lanes(rkvg, lora1, lora_b, par, s0t, *, nt, nb):
    _, m, e = rkvg.shape
    heads = e // HEAD_DIM
    hp = RWKV_LANE_HEADS
    wl = hp * HEAD_DIM
    kern = functools.partial(_rwkv_lanes_kernel, nt=nt, nb=nb)
    proj_spec = lambda s: pl.BlockSpec((1, m, wl), lambda h: (s, 0, h))
    state_spec = pl.BlockSpec((hp, HEAD_DIM, HEAD_DIM, nb), lambda h: (h, 0, 0, 0))
    tile = pltpu.VMEM((nt, wl, nb), F32)
    return pl.pallas_call(
        kern,
        grid=(heads // hp,),
        in_specs=[
            proj_spec(0), proj_spec(1), proj_spec(2), proj_spec(3),
            pl.BlockSpec((1, m, RWKV_LORA_PAD), lambda h: (0, 0, 0)),
            pl.BlockSpec((1, m, RWKV_LORA_PAD), lambda h: (1, 0, 0)),
            pl.BlockSpec((1, RWKV_LORA_PAD, wl), lambda h: (0, 0, h)),
            pl.BlockSpec((1, RWKV_LORA_PAD, wl), lambda h: (1, 0, h)),
            pl.BlockSpec((8, wl), lambda h: (0, h)),
            state_spec,
        ],
        out_specs=[pl.BlockSpec((m, wl), lambda h: (0, h)), state_spec],
        out_shape=[jax.ShapeDtypeStruct((m, e), BF16), jax.ShapeDtypeStruct(s0t.shape, F32)],
        scratch_shapes=[tile] * 7,
        compiler_params=_cparams(("parallel",)),
        name="rwkv_lanes",
    )(rkvg, rkvg, rkvg, rkvg, lora1, lora1, lora_b, lora_b, par, s0t)


def _prep_weights(norm_g, ssd_w_in, ssd_dt_bias, ssd_a_log, ssd_d, ssd_w_out,
                  rwkv_w_rkvg, rwkv_w_lora_a, rwkv_w_lora_b, rwkv_a_lora_a, rwkv_a_lora_b,
                  rwkv_w0, rwkv_a0, rwkv_k_k, rwkv_k_a, rwkv_r_k, rwkv_ln_g, rwkv_ln_b, rwkv_w_out,
                  ret_w_in, ret_w_out):
    d_inner = ssd_w_out.shape[1]
    n_main = d_inner + d_inner + 2 * SSD_GROUPS * SSD_STATE
    ns = ssd_w_in.shape[0]
    d_model = ssd_w_in.shape[1]

    def head_lanes(p):
        p = p.reshape(ns, SSD_GROUPS, 1, SSD_HPG)
        return jnp.pad(p, ((0, 0), (0, 0), (0, 0), (0, 128 - SSD_HPG)))

    w_dt = ssd_w_in[:, :, n_main:].reshape(ns, d_model, SSD_GROUPS, SSD_HPG)
    w_dt = jnp.pad(w_dt, ((0, 0), (0, 0), (0, 0), (0, 128 - SSD_HPG))).reshape(ns, d_model, SSD_GROUPS * 128)
    rank = rwkv_w_lora_a.shape[2]
    lora_a = jnp.stack([rwkv_w_lora_a, rwkv_a_lora_a], axis=1)
    lora_a = jnp.pad(lora_a, ((0, 0), (0, 0), (0, 0), (0, RWKV_LORA_PAD - rank)))
    lora_b = jnp.stack([rwkv_w_lora_b, rwkv_a_lora_b], axis=1)
    lora_b = jnp.pad(lora_b, ((0, 0), (0, 0), (0, RWKV_LORA_PAD - rank), (0, 0)))
    nr = rwkv_w0.shape[0]
    par = jnp.stack([rwkv_w0, rwkv_a0, rwkv_k_k, rwkv_k_a, rwkv_r_k.reshape(nr, -1), rwkv_ln_g, rwkv_ln_b,
                     jnp.zeros_like(rwkv_w0)], axis=1)
    return dict(
        ssd_w_main=ssd_w_in[:, :, :n_main].astype(BF16), ssd_w_dt=w_dt.astype(BF16),
        ssd_dtb=head_lanes(ssd_dt_bias), ssd_alog=head_lanes(ssd_a_log), ssd_dskip=head_lanes(ssd_d),
        ssd_w_out=ssd_w_out.astype(BF16),
        rwkv_w=rwkv_w_rkvg.astype(BF16), rwkv_lora_a=lora_a.astype(BF16), rwkv_lora_b=lora_b.astype(BF16),
        rwkv_par=par, rwkv_w_out=rwkv_w_out.astype(BF16),
        ret_w_in=ret_w_in.astype(BF16), ret_w_out=ret_w_out.astype(BF16),
    )


def _trunk(h, conv_st, ssd_st, shift_st, wkv_st, ret_st, pos, *, bsz, tp, lq, lo, hi, depth,
           norm_g, final_norm_g, wts, ssd_conv_w, ssd_conv_b, ssd_norm_g, rwkv_mu, ret_norm_g):
    d_model = h.shape[1]
    geo = dict(bsz=bsz, tp=tp, lo=lo, hi=hi)
    new_conv, new_shift, new_wkv, new_ret = [], [], [], []
    new_ssd = None

    half = RET_QK // 2
    inv_freq = 1.0 / (RET_THETA_BASE ** jnp.linspace(0.0, 1.0, half, dtype=F32))
    ang = pos.astype(F32)[:, None] * inv_freq
    cos, sin = jnp.cos(ang), jnp.sin(ang)
    log_gamma = jnp.log1p(-jnp.exp2(-5.0 - jnp.arange(RET_HEADS, dtype=F32)))
    lg = jnp.broadcast_to(log_gamma[:, None, None], (RET_HEADS, 8, 128))

    for layer in range(depth):
        kind, j = layer % 3, layer // 3
        u = rmsnorm(h, norm_g[layer], F32 if kind == 1 else BF16)
        if kind == 0:
            proj = matmul(u, wts["ssd_w_main"][j], out_dtype=BF16)
            dt_raw = matmul(u, wts["ssd_w_dt"][j])
            conv_init = jnp.pad(conv_st[j], ((0, 0), (8 - (SSD_CONV - 1), 0), (0, 0)))
            y, new_ssd = ssd_scan(proj, dt_raw, conv_init, ssd_st, j, new_ssd, ssd_conv_w[j], ssd_conv_b[j],
                                  wts["ssd_dtb"][j], wts["ssd_alog"][j], wts["ssd_dskip"][j], ssd_norm_g[j],
                                  lq=lq, **geo)
            nk = SSD_CONV - 1
            last = u.reshape(bsz, tp, d_model)[:, hi - nk:hi].reshape(bsz * nk, d_model)
            last = jnp.pad(last, ((0, -(bsz * nk) % 16), (0, 0)))
            xbc = matmul(last, wts["ssd_w_main"][j])[:bsz * nk, y.shape[1]:]
            new_conv.append(xbc.reshape(bsz, nk, -1))
            h = matmul(y, wts["ssd_w_out"][j], res=h)
        elif kind == 1 and lo == 0 and hi <= 8 and bsz % LANES == 0:
            tmajor = lambda x: jnp.swapaxes(x.reshape(bsz, tp, -1)[:, :hi], 0, 1)
            uc = tmajor(u)
            prev = jnp.concatenate([shift_st[j][None], uc[:-1]], axis=0).reshape(hi * bsz, d_model)
            uc = uc.reshape(hi * bsz, d_model)
            rkvg = mix_matmul(uc, prev, rwkv_mu[j][:4], wts["rwkv_w"][j], out_dtype=BF16)
            lora1 = mix_matmul(uc, prev, rwkv_mu[j][4:], wts["rwkv_lora_a"][j])
            y, s_t = rwkv_lanes(rkvg, lora1, wts["rwkv_lora_b"][j], wts["rwkv_par"][j],
                                jnp.transpose(wkv_st[j], (1, 2, 3, 0)), nt=hi, nb=bsz)
            new_shift.append(u.reshape(bsz, tp, d_model)[:, hi - 1])
            new_wkv.append(jnp.transpose(s_t, (3, 0, 1, 2)))
            hc = matmul(y, wts["rwkv_w_out"][j], res=tmajor(h).reshape(hi * bsz, d_model))
            hc = jnp.swapaxes(hc.reshape(hi, bsz, d_model), 0, 1)
            h = jnp.pad(hc, ((0, 0), (0, tp - hi), (0, 0))).reshape(bsz * tp, d_model)
        elif kind == 1:
            u3 = u.reshape(bsz, tp, d_model)
            prev = jnp.concatenate([shift_st[j][:, None, :], u3[:, :-1]], axis=1).reshape(bsz * tp, d_model)
            rkvg = mix_matmul(u, prev, rwkv_mu[j][:4], wts["rwkv_w"][j], out_dtype=BF16)
            lora1 = mix_matmul(u, prev, rwkv_mu[j][4:], wts["rwkv_lora_a"][j])
            if lo == 0 and hi <= 8 and tp <= 16:
                y, s_new = rwkv_short(rkvg.astype(F32), lora1, wts["rwkv_lora_b"][j], wts["rwkv_par"][j], wkv_st[j],
                                      bsz=bsz, tp=tp, hi=hi)
            else:
                y, s_new = rwkv_scan(rkvg, lora1, wts["rwkv_lora_b"][j], wts["rwkv_par"][j], wkv_st[j], **geo)
            new_shift.append(u3[:, hi - 1])
            new_wkv.append(s_new)
            h = matmul(y, wts["rwkv_w_out"][j], res=h)
        else:
            proj = matmul(u, wts["ret_w_in"][j], out_dtype=BF16)
            y, s_new = ret_scan(proj, cos, sin, lg, ret_st[j], ret_norm_g[j], lq=lq, **geo)
            new_ret.append(s_new)
            h = matmul(y, wts["ret_w_out"][j], res=h)
    y = rmsnorm(h, final_norm_g)
    return (y, jnp.stack(new_conv), new_ssd, jnp.stack(new_shift), jnp.stack(new_wkv), jnp.stack(new_ret))


def kernel(x_prompt, x_sample, state_ssd_conv, state_ssd, state_rwkv_shift, state_rwkv_wkv, state_ret, meta_tokens, norm_g, final_norm_g, ssd_w_in, ssd_conv_w, ssd_conv_b, ssd_dt_bias, ssd_a_log, ssd_d, ssd_norm_g, ssd_w_out, rwkv_mu, rwkv_w_rkvg, rwkv_w0, rwkv_w_lora_a, rwkv_w_lora_b, rwkv_a0, rwkv_a_lora_a, rwkv_a_lora_b, rwkv_k_k, rwkv_k_a, rwkv_r_k, rwkv_ln_g, rwkv_ln_b, rwkv_w_out, ret_w_in, ret_norm_g, ret_w_out):
    depth = norm_g.shape[0]
    d_model = x_prompt.shape[2]
    wts = _prep_weights(norm_g, ssd_w_in, ssd_dt_bias, ssd_a_log, ssd_d, ssd_w_out,
                        rwkv_w_rkvg, rwkv_w_lora_a, rwkv_w_lora_b, rwkv_a_lora_a, rwkv_a_lora_b,
                        rwkv_w0, rwkv_a0, rwkv_k_k, rwkv_k_a, rwkv_r_k, rwkv_ln_g, rwkv_ln_b, rwkv_w_out,
                        ret_w_in, ret_w_out)
    common = dict(depth=depth, norm_g=norm_g, final_norm_g=final_norm_g, wts=wts, ssd_conv_w=ssd_conv_w,
                  ssd_conv_b=ssd_conv_b, ssd_norm_g=ssd_norm_g, rwkv_mu=rwkv_mu, ret_norm_g=ret_norm_g)

    bp, seq, _ = x_prompt.shape
    lq_p = 128
    lo_p = lq_p - N_META
    tp_p = lo_p + N_META + seq
    h_p = jnp.concatenate([jnp.zeros((bp, lo_p, d_model), F32),
                           jnp.broadcast_to(meta_tokens[None], (bp, N_META, d_model)), x_prompt], axis=1)
    zeros_like_b = lambda s: jnp.zeros((s.shape[0], bp) + s.shape[2:], F32)
    pos_p = jnp.maximum(jnp.arange(tp_p) - lo_p, 0)
    outs_p = _trunk(h_p.reshape(bp * tp_p, d_model), zeros_like_b(state_ssd_conv), zeros_like_b(state_ssd),
                    zeros_like_b(state_rwkv_shift), zeros_like_b(state_rwkv_wkv), zeros_like_b(state_ret), pos_p,
                    bsz=bp, tp=tp_p, lq=lq_p, lo=lo_p, hi=tp_p, **common)
    y_prompt = outs_p[0].reshape(bp, tp_p, d_model)[:, lo_p + N_META:]

    bs, ds, _ = x_sample.shape
    tp_s = 16
    h_s = jnp.concatenate([x_sample, jnp.zeros((bs, tp_s - ds, d_model), F32)], axis=1)
    pos_s = PAST_LEN + jnp.arange(tp_s)
    outs_s = _trunk(h_s.reshape(bs * tp_s, d_model), state_ssd_conv, state_ssd, state_rwkv_shift, state_rwkv_wkv,
                    state_ret, pos_s, bsz=bs, tp=tp_s, lq=tp_s, lo=0, hi=ds, **common)
    y_sample = outs_s[0].reshape(bs, tp_s, d_model)[:, :ds]

    return (y_prompt, y_sample) + tuple(outs_p[1:]) + tuple(outs_s[1:])
```

```python
import functools
import math

import jax
import jax.numpy as jnp
from jax import lax
from jax.experimental import pallas as pl
from jax.experimental.pallas import tpu as pltpu

F32 = jnp.float32
BF16 = jnp.bfloat16

EPS = 1e-6
N_META = 16
HEAD_DIM = 64
SSD_STATE = 128
SSD_GROUPS = 8
SSD_HPG = 8
SSD_CONV = 4
RET_HEADS = 8
RET_QK = 256
RET_V = 512
RET_THETA_BASE = 10000.0
RWKV_LORA_PAD = 128
RWKV_CHUNK = 64
RWKV_HPB = 4
RWKV_GROUPS_PER_STEP = 8
RWKV_SHORT_HEADS = 16
RWKV_LANE_HEADS = 2
LANES = 128
PAST_LEN = 16384
RWKV_LN_EPS = 1e-5 * HEAD_DIM
NEG = -1e30
ROW_TILE = 512
MATMUL_VMEM_BUDGET = 40 * 1024 * 1024
VMEM_LIMIT = 56 * 1024 * 1024


def _cparams(sem):
    return pltpu.CompilerParams(dimension_semantics=sem, vmem_limit_bytes=VMEM_LIMIT)


def _nt(a, b):
    return lax.dot_general(a, b, (((1,), (1,)), ((), ())), preferred_element_type=F32)


def _tn(a, b):
    return lax.dot_general(a, b, (((0,), (0,)), ((), ())), preferred_element_type=F32)


def _dot(a, b):
    return jnp.dot(a, b, preferred_element_type=F32)


def _split(x, terms):
    parts = []
    r = x
    for i in range(terms):
        p = r.astype(BF16)
        parts.append(p)
        if i + 1 < terms:
            r = r - p.astype(F32)
    return parts


def _dot_split_l(x, m, terms=3):
    acc = None
    for p in _split(x, terms):
        d = _dot(p, m)
        acc = d if acc is None else acc + d
    return acc


def _dot_split_r(m, x, terms=3):
    acc = None
    for p in _split(x, terms):
        d = _dot(m, p)
        acc = d if acc is None else acc + d
    return acc


def _pad_rows(x, rows):
    if x.shape[0] == rows:
        return x
    return jnp.concatenate([x, jnp.zeros((rows - x.shape[0], x.shape[1]), x.dtype)], axis=0)


def _idiv(x, n):
    return jnp.right_shift(x, int(math.log2(n)))


def _imod(x, n):
    return jnp.bitwise_and(x, n - 1)


def _silu(x):
    h = 0.5 * x
    return h + h * jnp.tanh(h)


def _softplus(x):
    return jnp.maximum(x, 0.0) + jnp.log(1.0 + jnp.exp(-jnp.abs(x)))


def _rmsnorm_kernel(x_ref, g_ref, o_ref):
    x = x_ref[...]
    ms = jnp.mean(x * x, axis=-1, keepdims=True)
    o_ref[...] = (x * lax.rsqrt(ms + EPS) * g_ref[...]).astype(o_ref.dtype)


def _row_tile(m):
    tm = math.gcd(m, ROW_TILE)
    assert tm % 16 == 0, m
    return tm


def _matmul_tiles(m, k, n, *, a_bytes, n_a, cast, n_out):
    best = None
    for tn in (t for t in (1024, 512, 256, 128) if n % t == 0):
        for tm in (t for t in range(16, m + 1, 16) if m % t == 0):
            need = (2 * n_a * tm * k * a_bytes + (tm * k * 2 if cast else 0)
                    + 2 * k * tn * 2 + 2 * n_out * tm * tn * 4)
            if need <= MATMUL_VMEM_BUDGET and (best is None or (tm * tn, tm) > (best[0] * best[1], best[0])):
                best = (tm, tn)
    assert best is not None, (m, k, n)
    return best


def rmsnorm(x, g, out_dtype=F32):
    m, d = x.shape
    tm = _row_tile(m)
    return pl.pallas_call(
        _rmsnorm_kernel,
        grid=(m // tm,),
        in_specs=[pl.BlockSpec((tm, d), lambda i: (i, 0)), pl.BlockSpec((1, d), lambda i: (0, 0))],
        out_specs=pl.BlockSpec((tm, d), lambda i: (i, 0)),
        out_shape=jax.ShapeDtypeStruct((m, d), out_dtype),
        compiler_params=_cparams(("parallel",)),
        name="rmsnorm",
    )(x, g.reshape(1, d))


def _mm_kernel(a_ref, w_ref, *rest, has_res, cast):
    rest = list(rest)
    abf_ref = rest.pop() if cast else a_ref
    o_ref = rest.pop()

    if cast:
        @pl.when(pl.program_id(1) == 0)
        def _():
            abf_ref[...] = a_ref[...].astype(BF16)

    acc = _dot(abf_ref[...], w_ref[...])
    if has_res:
        acc = rest[0][...] + acc
    o_ref[...] = acc.astype(o_ref.dtype)


def matmul(a, w, res=None, out_dtype=F32):
    m, k = a.shape
    n = w.shape[1]
    cast = a.dtype != BF16
    tm, tn = _matmul_tiles(m, k, n, a_bytes=a.dtype.itemsize, n_a=1, cast=cast, n_out=2 if res is not None else 1)
    in_specs = [pl.BlockSpec((tm, k), lambda i, j: (i, 0)), pl.BlockSpec((k, tn), lambda i, j: (0, j))]
    args = [a, w]
    if res is not None:
        in_specs.append(pl.BlockSpec((tm, tn), lambda i, j: (i, j)))
        args.append(res)
    return pl.pallas_call(
        functools.partial(_mm_kernel, has_res=res is not None, cast=cast),
        grid=(m // tm, n // tn),
        in_specs=in_specs,
        out_specs=pl.BlockSpec((tm, tn), lambda i, j: (i, j)),
        out_shape=jax.ShapeDtypeStruct((m, n), out_dtype),
        scratch_shapes=[pltpu.VMEM((tm, k), BF16)] if cast else [],
        compiler_params=_cparams(("parallel", "arbitrary")),
        name="matmul_res" if res is not None else "matmul",
    )(*args)


def _mixmm_kernel(u_ref, p_ref, mu_ref, w_ref, o_ref, xm_ref):
    @pl.when(pl.program_id(2) == 0)
    def _():
        u = u_ref[...]
        xm_ref[...] = (u + (p_ref[...] - u) * mu_ref[0]).astype(BF16)

    o_ref[0] = _dot(xm_ref[...], w_ref[0]).astype(o_ref.dtype)


def mix_matmul(u, prev, mu, w, out_dtype=F32):
    m, k = u.shape
    s, _, n = w.shape
    tm, tn = _matmul_tiles(m, k, n, a_bytes=4, n_a=2, cast=True, n_out=1)
    return pl.pallas_call(
        _mixmm_kernel,
        grid=(m // tm, s, n // tn),
        in_specs=[
            pl.BlockSpec((tm, k), lambda i, si, j: (i, 0)),
            pl.BlockSpec((tm, k), lambda i, si, j: (i, 0)),
            pl.BlockSpec((1, 1, k), lambda i, si, j: (si, 0, 0)),
            pl.BlockSpec((1, k, tn), lambda i, si, j: (si, 0, j)),
        ],
        out_specs=pl.BlockSpec((1, tm, tn), lambda i, si, j: (si, i, j)),
        out_shape=jax.ShapeDtypeStruct((s, m, n), out_dtype),
        scratch_shapes=[pltpu.VMEM((tm, k), BF16)],
        compiler_params=_cparams(("parallel", "arbitrary", "arbitrary")),
        name="mix_matmul",
    )(u, prev, mu.reshape(s, 1, k), w)


def _conv_silu(cur, car_ref, cols, w, b, lq):
    car_ref[8:8 + lq, cols] = cur
    acc = b + cur * w[SSD_CONV - 1:SSD_CONV]
    for s in range(1, SSD_CONV):
        acc = acc + car_ref[8 - s:8 - s + lq, cols] * w[SSD_CONV - 1 - s:SSD_CONV - s]
    car_ref[0:8, cols] = cur[lq - 8:]
    return _silu(acc)


def _ssd_kernel(z_ref, x_ref, b_ref, c_ref, dt_ref, ci_ref, s0_ref, cw_ref, cb_ref, dtb_ref, alog_ref,
                d_ref, ng_ref, *rest, lq, lo, hi, nchunks, out_layer):
    y_ref, sout_ref, car, st = rest[-4:]
    c = pl.program_id(1)
    gw = SSD_HPG * HEAD_DIM

    @pl.when(c == 0)
    def _():
        car[0:8, :] = ci_ref[0]
        st[...] = s0_ref[0, 0].reshape(SSD_GROUPS * gw, SSD_STATE)

    gens = [_ssd_group(g, z_ref, x_ref, b_ref, c_ref, dt_ref, cw_ref, cb_ref, dtb_ref, alog_ref, d_ref,
                       ng_ref, y_ref, car, st, lq=lq, lo=lo, hi=hi) for g in range(SSD_GROUPS)]
    for _ in zip(*gens):
        pass

    @pl.when(c == nchunks - 1)
    def _():
        for layer in range(sout_ref.shape[0]):
            if layer == out_layer:
                sout_ref[layer, 0] = st[...].reshape(SSD_GROUPS * SSD_HPG, HEAD_DIM, SSD_STATE)
            else:
                sout_ref[layer, 0] = jnp.zeros(sout_ref.shape[2:], F32)


def _ssd_group(g, z_ref, x_ref, b_ref, c_ref, dt_ref, cw_ref, cb_ref, dtb_ref, alog_ref, d_ref, ng_ref,
               y_ref, car, st, *, lq, lo, hi):
    c = pl.program_id(1)
    gw = SSD_HPG * HEAD_DIM
    d_inner = SSD_GROUPS * gw
    xs = slice(g * gw, (g + 1) * gw)
    ns = slice(g * SSD_STATE, (g + 1) * SSD_STATE)
    bs = slice(d_inner + g * SSD_STATE, d_inner + (g + 1) * SSD_STATE)
    cs_ = slice(d_inner + (SSD_GROUPS + g) * SSD_STATE, d_inner + (SSD_GROUPS + g + 1) * SSD_STATE)

    f32 = lambda x: x.astype(F32)
    xc = _conv_silu(f32(x_ref[:, xs]), car, xs, cw_ref[:, xs], cb_ref[:, xs], lq)
    bc = _conv_silu(f32(b_ref[:, ns]), car, bs, cw_ref[:, bs], cb_ref[:, bs], lq)
    cc = _conv_silu(f32(c_ref[:, ns]), car, cs_, cw_ref[:, cs_], cb_ref[:, cs_], lq)

    pos = c * lq + lax.broadcasted_iota(jnp.int32, (lq, 1), 0)
    valid = (pos >= lo) & (pos < hi)
    lane = lax.broadcasted_iota(jnp.int32, (lq, 128), 1)
    dt = _softplus(dt_ref[:, ns] + dtb_ref[:, ns])
    dt = jnp.where(valid & (lane < SSD_HPG), dt, 0.0)
    la = dt * (-jnp.exp(alog_ref[:, ns]))
    yield

    ri = lax.broadcasted_iota(jnp.int32, (lq, lq), 0)
    ci = lax.broadcasted_iota(jnp.int32, (lq, lq), 1)
    tril = (ri >= ci).astype(BF16)
    triu = (ri <= ci).astype(BF16)
    acum = _dot_split_r(tril, la)
    acum_t = sum(_tn(part, triu) for part in _split(la, 3))
    yield
    a_end = acum[lq - 1:lq]
    dec_end = jnp.exp(a_end - acum)
    e_in = jnp.exp(acum)
    cd = jnp.exp(a_end)

    bcp = bc.astype(BF16)
    ccb = cc.astype(BF16)
    g_sc = _nt(ccb, bcp)
    st_old = st[xs, :]
    y_in = _nt(ccb, st_old.astype(BF16))
    yield

    causal = ci <= ri
    lane_q = lax.broadcasted_iota(jnp.int32, (lq, 128), 1) < HEAD_DIM
    lane_k = lane_q
    row_k = lax.broadcasted_iota(jnp.int32, (128, 1), 0) < HEAD_DIM
    dvec = d_ref[:, ns]

    ys = []
    for p in range(SSD_HPG // 2):
        h0, h1 = 2 * p, 2 * p + 1
        xp = xc[:, 128 * p:128 * (p + 1)]
        vp = xp * jnp.where(lane_q, dt[:, h0:h0 + 1], dt[:, h1:h1 + 1])
        vpp = vp
        yp = y_in[:, 128 * p:128 * (p + 1)] * jnp.where(lane_q, e_in[:, h0:h0 + 1], e_in[:, h1:h1 + 1])
        yp = yp + xp * jnp.where(lane_q, dvec[:, h0:h0 + 1], dvec[:, h1:h1 + 1])
        for hh, h in ((0, h0), (1, h1)):
            seg = acum[:, h:h + 1] - acum_t[h:h + 1, :]
            lm = jnp.exp(jnp.where(causal, seg, NEG))
            pm = (g_sc * lm).astype(BF16)
            vm = jnp.where(lane_k if hh == 0 else jnp.logical_not(lane_k), vpp, 0.0).astype(BF16)
            yp = yp + _dot(pm, vm)
        ys.append(yp)
        vend = vpp * jnp.where(lane_k, dec_end[:, h0:h0 + 1], dec_end[:, h1:h1 + 1])
        upd = _tn(vend.astype(BF16), bcp)
        cdp = jnp.where(row_k, cd[:, h0:h0 + 1], cd[:, h1:h1 + 1])
        st[g * gw + 128 * p:g * gw + 128 * (p + 1), :] = st_old[128 * p:128 * (p + 1), :] * cdp + upd
        yield

    y = jnp.concatenate(ys, axis=1)
    y = y * _silu(f32(z_ref[:, xs]))
    ms = jnp.mean(y * y, axis=-1, keepdims=True)
    y = y * lax.rsqrt(ms + EPS) * ng_ref[:, xs]
    y_ref[:, xs] = jnp.where(valid, y, 0.0).astype(BF16)
    yield


def ssd_scan(proj, dt_raw, conv_init, s0_all, layer, s_buf, conv_w, conv_b, dtb, alog, dskip, norm_g,
             *, bsz, tp, lq, lo, hi):
    m = proj.shape[0]
    heads = SSD_GROUPS * SSD_HPG
    nlayers = s0_all.shape[0]
    state_spec = pl.BlockSpec((1, 1, heads, HEAD_DIM, SSD_STATE), lambda b, c: (layer, b, 0, 0, 0))
    if s_buf is not None:
        out_state_spec, out_layer = state_spec, 0
        extra_specs, extra_args, aliases = [pl.BlockSpec(memory_space=pl.ANY)], [s_buf], {13: 1}
    else:
        out_state_spec = pl.BlockSpec((nlayers, 1, heads, HEAD_DIM, SSD_STATE), lambda b, c: (0, b, 0, 0, 0))
        out_layer = layer
        extra_specs, extra_args, aliases = [], [], {}
    nch = tp // lq
    d_inner = heads * HEAD_DIM
    gn = SSD_GROUPS * SSD_STATE
    conv_dim = d_inner + 2 * gn
    row = lambda b, c: b * nch + c
    const = lambda b, c: (0, 0)
    kern = functools.partial(_ssd_kernel, lq=lq, lo=lo, hi=hi, nchunks=nch, out_layer=out_layer)
    return pl.pallas_call(
        kern,
        grid=(bsz, nch),
        in_specs=[
            pl.BlockSpec((lq, d_inner), lambda b, c: (row(b, c), 0)),
            pl.BlockSpec((lq, d_inner), lambda b, c: (row(b, c), 1)),
            pl.BlockSpec((lq, gn), lambda b, c: (row(b, c), 2 * d_inner // gn)),
            pl.BlockSpec((lq, gn), lambda b, c: (row(b, c), 2 * d_inner // gn + 1)),
            pl.BlockSpec((lq, gn), lambda b, c: (row(b, c), 0)),
            pl.BlockSpec((1, 8, conv_dim), lambda b, c: (b, 0, 0)),
            state_spec,
            pl.BlockSpec((SSD_CONV, conv_dim), const),
            pl.BlockSpec((1, conv_dim), const),
            pl.BlockSpec((1, gn), const),
            pl.BlockSpec((1, gn), const),
            pl.BlockSpec((1, gn), const),
            pl.BlockSpec((1, d_inner), const),
        ] + extra_specs,
        out_specs=[
            pl.BlockSpec((lq, d_inner), lambda b, c: (row(b, c), 0)),
            out_state_spec,
        ],
        out_shape=[
            jax.ShapeDtypeStruct((m, d_inner), BF16),
            jax.ShapeDtypeStruct(s0_all.shape, F32),
        ],
        scratch_shapes=[pltpu.VMEM((8 + lq, conv_dim), F32), pltpu.VMEM((d_inner, SSD_STATE), F32)],
        input_output_aliases=aliases,
        compiler_params=_cparams(("parallel", "arbitrary")),
        name="ssd_scan",
    )(proj, proj, proj, proj, dt_raw, conv_init, s0_all, conv_w, conv_b.reshape(1, -1),
      dtb.reshape(1, -1), alog.reshape(1, -1), dskip.reshape(1, -1), norm_g.reshape(1, -1), *extra_args)


def _ret_kernel(q_ref, k_ref, v_ref, g_ref, cos_ref, sin_ref, lg_ref, s0_ref, ng_ref,
                y_ref, sout_ref, *, lq, lo, hi):
    c = pl.program_id(1)

    @pl.when(c == 0)
    def _():
        sout_ref[...] = s0_ref[...]

    gens = [_ret_head(h, q_ref, k_ref, v_ref, g_ref, cos_ref, sin_ref, lg_ref, ng_ref, y_ref, sout_ref,
                      lq=lq, lo=lo, hi=hi) for h in range(RET_HEADS)]
    for _ in zip(*gens):
        pass


def _ret_head(h, q_ref, k_ref, v_ref, g_ref, cos_ref, sin_ref, lg_ref, ng_ref, y_ref, st_ref, *, lq, lo, hi):
    c = pl.program_id(1)
    qs = slice(h * RET_QK, (h + 1) * RET_QK)
    vs = slice(h * RET_V, (h + 1) * RET_V)
    lg = lg_ref[h][0:1, 0:1]
    nv = float(hi - lo)

    def count(p):
        return jnp.clip((p + 1 - lo).astype(F32), 0.0, nv)

    base = c * lq
    pos_i = base + lax.broadcasted_iota(jnp.int32, (lq, 1), 0)
    valid = (pos_i >= lo) & (pos_i < hi)
    cnt_i = count(pos_i)
    cnt_j = count(base + lax.broadcasted_iota(jnp.int32, (1, lq), 1))
    cnt_jc = cnt_i
    cnt0 = count(base - 1 + jnp.zeros((1, 1), jnp.int32))
    cnt_end = count(base + lq - 1 + jnp.zeros((1, 1), jnp.int32))

    cos = cos_ref[...]
    sin = sin_ref[...]
    half = RET_QK // 2

    def rot(x):
        x1, x2 = x[:, :half], x[:, half:]
        return jnp.concatenate([x1 * cos - x2 * sin, x1 * sin + x2 * cos], axis=1)

    qr = rot(q_ref[:, qs].astype(F32)).astype(BF16)
    kr = jnp.where(valid, rot(k_ref[:, qs].astype(F32)) * (RET_QK ** -0.5), 0.0)
    v = jnp.where(valid, v_ref[:, vs].astype(F32), 0.0)
    krp = kr.astype(BF16)
    vp = v
    yield

    sc = _nt(qr, krp)
    s_old = st_ref[0, h]
    y_in = _nt(qr, s_old.astype(BF16))
    yield
    qi = lax.broadcasted_iota(jnp.int32, (lq, lq), 0)
    kj = lax.broadcasted_iota(jnp.int32, (lq, lq), 1)
    dm = jnp.exp(jnp.where(kj <= qi, lg * (cnt_i - cnt_j), NEG))
    y = _dot((sc * dm).astype(BF16), vp.astype(BF16))
    y = y + y_in * jnp.exp(lg * (cnt_i - cnt0))
    vend = vp * jnp.exp(lg * (cnt_end - cnt_jc))
    st_ref[0, h] = s_old * jnp.exp(lg * (cnt_end - cnt0)) + _tn(vend.astype(BF16), krp)
    yield

    ms = jnp.mean(y * y, axis=-1, keepdims=True)
    y = y * lax.rsqrt(ms + EPS) * ng_ref[:, vs] * _silu(g_ref[:, vs].astype(F32))
    y_ref[:, vs] = jnp.where(valid, y, 0.0).astype(BF16)
    yield


def ret_scan(proj, cos, sin, lg, s0, norm_g, *, bsz, tp, lq, lo, hi):
    m = proj.shape[0]
    nch = tp // lq
    d_inner = RET_HEADS * RET_V
    d_qk = RET_HEADS * RET_QK
    row = lambda b, c: b * nch + c
    state_spec = pl.BlockSpec((1, RET_HEADS, RET_V, RET_QK), lambda b, c: (b, 0, 0, 0))
    kern = functools.partial(_ret_kernel, lq=lq, lo=lo, hi=hi)
    return pl.pallas_call(
        kern,
        grid=(bsz, nch),
        in_specs=[
            pl.BlockSpec((lq, d_qk), lambda b, c: (row(b, c), 0)),
            pl.BlockSpec((lq, d_qk), lambda b, c: (row(b, c), 1)),
            pl.BlockSpec((lq, d_inner), lambda b, c: (row(b, c), 2 * d_qk // d_inner)),
            pl.BlockSpec((lq, d_inner), lambda b, c: (row(b, c), 2 * d_qk // d_inner + 1)),
            pl.BlockSpec((lq, RET_QK // 2), lambda b, c: (c, 0)),
            pl.BlockSpec((lq, RET_QK // 2), lambda b, c: (c, 0)),
            pl.BlockSpec((RET_HEADS, 8, 128), lambda b, c: (0, 0, 0)),
            state_spec,
            pl.BlockSpec((1, d_inner), lambda b, c: (0, 0)),
        ],
        out_specs=[
            pl.BlockSpec((lq, d_inner), lambda b, c: (row(b, c), 0)),
            state_spec,
        ],
        out_shape=[
            jax.ShapeDtypeStruct((m, d_inner), BF16),
            jax.ShapeDtypeStruct((bsz, RET_HEADS, RET_V, RET_QK), F32),
        ],
        compiler_params=_cparams(("parallel", "arbitrary")),
        name="ret_scan",
    )(proj, proj, proj, proj, cos, sin, lg, s0, norm_g.reshape(1, -1))


def _rwkv_kernel(r_ref, k_ref, v_ref, g_ref, lw_ref, la_ref, bw_ref, ba_ref, par_ref, s0_ref,
                 y_ref, sout_ref, st, *, lo, hi, nchunks, ngrp):
    c = pl.program_id(2)
    w4 = RWKV_HPB * HEAD_DIM
    r2 = lax.broadcasted_iota(jnp.int32, (w4, w4), 0)
    c2 = lax.broadcasted_iota(jnp.int32, (w4, w4), 1)
    blk = _idiv(r2, HEAD_DIM) == _idiv(c2, HEAD_DIM)

    @pl.when(c == 0)
    def _():
        tile = (lax.broadcasted_iota(jnp.int32, (HEAD_DIM, w4), 0)
                == _imod(lax.broadcasted_iota(jnp.int32, (HEAD_DIM, w4), 1), HEAD_DIM)).astype(BF16)
        for gi in range(ngrp):
            s0 = s0_ref[0, gi * RWKV_HPB:(gi + 1) * RWKV_HPB].reshape(w4, HEAD_DIM)
            st[gi] = jnp.where(blk, _dot_split_l(s0, tile), 0.0)

    s_new = [None] * ngrp
    gens = [_rwkv_group(gi, r_ref, k_ref, v_ref, g_ref, lw_ref, la_ref, bw_ref, ba_ref, par_ref,
                        y_ref, st, s_new, lo=lo, hi=hi) for gi in range(ngrp)]
    for _ in zip(*gens):
        pass

    @pl.when(c == nchunks - 1)
    def _():
        tile_t = (_imod(lax.broadcasted_iota(jnp.int32, (w4, HEAD_DIM), 0), HEAD_DIM)
                  == lax.broadcasted_iota(jnp.int32, (w4, HEAD_DIM), 1)).astype(BF16)
        for gi in range(ngrp):
            sout_ref[0, gi * RWKV_HPB:(gi + 1) * RWKV_HPB] = _dot_split_l(s_new[gi], tile_t).reshape(
                RWKV_HPB, HEAD_DIM, HEAD_DIM)


def _rwkv_group(gi, r_ref, k_ref, v_ref, g_ref, lw_ref, la_ref, bw_ref, ba_ref, par_ref, y_ref, st, out,
                *, lo, hi):
    c = pl.program_id(2)
    cs = RWKV_CHUNK
    w4 = RWKV_HPB * HEAD_DIM
    sl = slice(gi * w4, (gi + 1) * w4)

    r2 = lax.broadcasted_iota(jnp.int32, (w4, w4), 0)
    c2 = lax.broadcasted_iota(jnp.int32, (w4, w4), 1)
    blk = _idiv(r2, HEAD_DIM) == _idiv(c2, HEAD_DIM)
    ones_bd = blk.astype(BF16)

    def segsum(x):
        return _dot_split_l(x, ones_bd, terms=2)

    par = par_ref[:, sl]
    w0, a0, k_k, k_a, r_k, ln_g, ln_b = (par[i:i + 1] for i in range(7))

    r = r_ref[0, :, sl].astype(F32)
    k = k_ref[0, :, sl].astype(F32)
    v = v_ref[0, :, sl].astype(F32)
    g = g_ref[0, :, sl].astype(F32)
    w_raw = w0 + _dot(jnp.tanh(lw_ref[0]).astype(BF16), bw_ref[0, :, sl])
    a = jax.nn.sigmoid(a0 + _dot(la_ref[0].astype(BF16), ba_ref[0, :, sl]))

    ti = lax.broadcasted_iota(jnp.int32, (cs, 1), 0)
    pos = c * cs + ti
    valid = (pos >= lo) & (pos < hi)

    lw = -jnp.exp(-_softplus(-w_raw) - 0.5)
    kk = k * k_k
    kk = kk / jnp.maximum(jnp.sqrt(segsum(kk * kk)), 1e-12)
    yield
    kp = k * (1.0 + (a - 1.0) * k_a)
    lw = jnp.where(valid, lw, 0.0)
    kk = jnp.where(valid, kk, 0.0)
    kp = jnp.where(valid, kp, 0.0)
    vm = jnp.where(valid, v, 0.0)

    tri = (lax.broadcasted_iota(jnp.int32, (cs, cs), 0) >= lax.broadcasted_iota(jnp.int32, (cs, cs), 1)).astype(BF16)
    cw = _dot_split_r(tri, lw)
    yield
    cwl = cw[cs - 1:cs]
    wt = jnp.exp(cw)
    wi = jnp.exp(-cw)
    wend = jnp.exp(cwl - cw)
    b = kk * a
    at = -kk * jnp.exp(cw - lw)
    rt = r * wt

    def bd(x):
        return jnp.where(blk, jnp.concatenate([x] * RWKV_HPB, axis=0), 0.0).astype(BF16)

    lhs = jnp.concatenate([bd(at), bd(rt)], axis=0)
    rhs = jnp.concatenate([bd(b * wi), bd(kp * wi)], axis=0)
    sc = _nt(lhs, rhs)
    yield
    tt = _imod(r2, cs)
    jj = _imod(c2, cs)
    strict = blk & (tt > jj)
    incl = blk & (tt >= jj)
    mab = jnp.where(strict, sc[:w4, :w4], 0.0)
    mak = jnp.where(strict, sc[:w4, w4:], 0.0)
    nrb = jnp.where(incl, sc[w4:, :w4], 0.0)
    nrk = jnp.where(incl, sc[w4:, w4:], 0.0)

    tinv = (r2 == c2).astype(F32) + jnp.where(
        (_idiv(r2, 2) == _idiv(c2, 2)) & (_imod(tt, 2) == 1) & (_imod(jj, 2) == 0), mab, 0.0)
    msz = 2
    while msz < cs:
        off = ((_idiv(r2, 2 * msz) == _idiv(c2, 2 * msz)) & (_imod(tt, 2 * msz) >= msz)
               & (_imod(jj, 2 * msz) < msz))
        tb = tinv.astype(BF16)
        tno = _dot(tb, jnp.where(off, mab, 0.0).astype(BF16)).astype(BF16)
        yield
        tinv = tinv + _dot(tno, tb)
        yield
        msz *= 2

    s_old = st[gi]
    x = _nt(lhs, s_old.astype(BF16))
    yield
    vbd = bd(vm)
    u = _dot(tinv.astype(BF16), (x[:w4] + _dot(mak.astype(BF16), vbd)).astype(BF16))
    yield
    ub = u.astype(BF16)
    yb = x[w4:] + _dot(nrb.astype(BF16), ub) + _dot(nrk.astype(BF16), vbd)
    yield
    s_new = s_old * wt[cs - 1:cs] + _tn(jnp.concatenate([ub, vbd], axis=0),
                                        jnp.concatenate([bd(b * wend), bd(kp * wend)], axis=0))
    st[gi] = s_new
    out[gi] = s_new
    yield

    y = yb[0:cs] + yb[cs:2 * cs] + yb[2 * cs:3 * cs] + yb[3 * cs:4 * cs]
    inv = 1.0 / HEAD_DIM
    yc = y - segsum(y) * inv
    yield
    y = yc * lax.rsqrt(segsum(yc * yc) * inv + RWKV_LN_EPS) * ln_g + ln_b
    y = (y + segsum(r * kp * r_k) * v) * _silu(g)
    y_ref[:, sl] = jnp.where(valid, y, 0.0).astype(BF16)
    yield


def rwkv_scan(rkvg, lora1, lora_b, par, s0, *, bsz, tp, lo, hi):
    _, m, e = rkvg.shape
    cs = RWKV_CHUNK
    nch = tp // cs
    heads = e // HEAD_DIM
    ngrp = RWKV_GROUPS_PER_STEP
    hb = RWKV_HPB * ngrp
    wb = hb * HEAD_DIM
    row = lambda b, h, c: b * nch + c
    kern = functools.partial(_rwkv_kernel, lo=lo, hi=hi, nchunks=nch, ngrp=ngrp)
    proj_spec = lambda s: pl.BlockSpec((1, cs, wb), lambda b, h, c: (s, row(b, h, c), h))
    return pl.pallas_call(
        kern,
        grid=(bsz, heads // hb, nch),
        in_specs=[
            proj_spec(0), proj_spec(1), proj_spec(2), proj_spec(3),
            pl.BlockSpec((1, cs, RWKV_LORA_PAD), lambda b, h, c: (0, row(b, h, c), 0)),
            pl.BlockSpec((1, cs, RWKV_LORA_PAD), lambda b, h, c: (1, row(b, h, c), 0)),
            pl.BlockSpec((1, RWKV_LORA_PAD, wb), lambda b, h, c: (0, 0, h)),
            pl.BlockSpec((1, RWKV_LORA_PAD, wb), lambda b, h, c: (1, 0, h)),
            pl.BlockSpec((8, wb), lambda b, h, c: (0, h)),
            pl.BlockSpec((1, hb, HEAD_DIM, HEAD_DIM), lambda b, h, c: (b, h, 0, 0)),
        ],
        out_specs=[
            pl.BlockSpec((cs, wb), lambda b, h, c: (row(b, h, c), h)),
            pl.BlockSpec((1, hb, HEAD_DIM, HEAD_DIM), lambda b, h, c: (b, h, 0, 0)),
        ],
        out_shape=[
            jax.ShapeDtypeStruct((m, e), BF16),
            jax.ShapeDtypeStruct((bsz, heads, HEAD_DIM, HEAD_DIM), F32),
        ],
        scratch_shapes=[pltpu.VMEM((ngrp, RWKV_HPB * HEAD_DIM, RWKV_HPB * HEAD_DIM), F32)],
        compiler_params=_cparams(("parallel", "parallel", "arbitrary")),
        name="rwkv_scan",
    )(rkvg, rkvg, rkvg, rkvg, lora1, lora1, lora_b, lora_b, par, s0)


def _rwkv_short_kernel(r_ref, k_ref, v_ref, g_ref, lw_ref, la_ref, bw_ref, ba_ref, par_ref, s0_ref,
                       ones_ref, tile_ref, y_ref, sout_ref, *, nvalid, rows_out):
    ct = 8
    hp = RWKV_SHORT_HEADS
    wl = hp * HEAD_DIM
    nr = hp * ct
    ones_bd = ones_ref[...]
    tile_t = tile_ref[...]

    def segsum(x):
        return jnp.concatenate(
            [_dot_split_l(x[:, 256 * j:256 * (j + 1)], ones_bd, terms=2) for j in range(wl // 256)], axis=1)

    par = par_ref[...]
    w0, a0, k_k, k_a, r_k, ln_g, ln_b = (par[i:i + 1] for i in range(7))
    r, k, v, g = r_ref[0], k_ref[0], v_ref[0], g_ref[0]
    w_raw = w0 + _dot(jnp.tanh(lw_ref[0]).astype(BF16), bw_ref[0])
    a = jax.nn.sigmoid(a0 + _dot(la_ref[0].astype(BF16), ba_ref[0]))

    ti = lax.broadcasted_iota(jnp.int32, (ct, 1), 0)
    valid = ti < nvalid
    lw = jnp.where(valid, -jnp.exp(-_softplus(-w_raw) - 0.5), 0.0)
    kk = k * k_k
    kk = jnp.where(valid, kk / jnp.maximum(jnp.sqrt(segsum(kk * kk)), 1e-12), 0.0)
    kp = jnp.where(valid, k * (1.0 + (a - 1.0) * k_a), 0.0)
    vm = jnp.where(valid, v, 0.0)

    cw = lw
    for s in (1, 2, 4):
        cw = cw + jnp.where(ti >= s, pltpu.roll(cw, s, 0), 0.0)
    cwl = cw[ct - 1:ct]
    wend = jnp.exp(cwl - cw)
    wi = jnp.exp(-cw)
    b = kk * a
    at = -kk * jnp.exp(cw - lw)
    rt = r * jnp.exp(cw)

    rr = lax.broadcasted_iota(jnp.int32, (nr, wl), 0)
    cc = lax.broadcasted_iota(jnp.int32, (nr, wl), 1)
    blk = _idiv(rr, ct) == _idiv(cc, HEAD_DIM)

    def bd(x):
        return jnp.where(blk, jnp.concatenate([x] * hp, axis=0), 0.0).astype(BF16)

    lhs = jnp.concatenate([bd(at), bd(rt)], axis=0)
    rhs = jnp.concatenate([bd(b * wi), bd(kp * wi)], axis=0)
    sc = _nt(lhs, rhs)
    ri = lax.broadcasted_iota(jnp.int32, (2 * nr, 2 * nr), 0)
    ci = lax.broadcasted_iota(jnp.int32, (2 * nr, 2 * nr), 1)
    same = _idiv(_imod(ri, nr), ct) == _idiv(_imod(ci, nr), ct)
    tt = _imod(ri, ct)
    jj = _imod(ci, ct)
    sc = jnp.where(same & (tt + (ri >= nr).astype(jnp.int32) > jj), sc, 0.0)
    mab, mak, nrb, nrk = sc[:nr, :nr], sc[:nr, nr:], sc[nr:, :nr], sc[nr:, nr:]

    r1 = lax.broadcasted_iota(jnp.int32, (nr, nr), 0)
    c1 = lax.broadcasted_iota(jnp.int32, (nr, nr), 1)
    t1 = _imod(r1, ct)
    j1 = _imod(c1, ct)
    tinv = (r1 == c1).astype(F32) + jnp.where(
        (_idiv(r1, 2) == _idiv(c1, 2)) & (_imod(t1, 2) == 1) & (_imod(j1, 2) == 0), mab, 0.0)
    msz = 2
    while msz < nvalid:
        off = ((_idiv(r1, 2 * msz) == _idiv(c1, 2 * msz)) & (_imod(t1, 2 * msz) >= msz)
               & (_imod(j1, 2 * msz) < msz))
        tb = tinv.astype(BF16)
        tinv = tinv + _dot(_dot(tb, jnp.where(off, mab, 0.0).astype(BF16)).astype(BF16), tb)
        msz *= 2

    s_old = s0_ref[0].reshape(wl, HEAD_DIM)
    lhs_rows = _dot(lhs, tile_t).astype(BF16)
    x = _nt(lhs_rows, s_old.astype(BF16))
    xa = jnp.where(blk, x[:nr], 0.0)
    xr = jnp.where(blk, x[nr:], 0.0)
    vbd = bd(vm)
    u = _dot(tinv.astype(BF16), (xa + _dot(mak.astype(BF16), vbd)).astype(BF16))
    uv = jnp.concatenate([u.astype(BF16), vbd], axis=0)
    yb = xr + _dot(jnp.concatenate([nrb, nrk], axis=1).astype(BF16), uv)
    bk_rows = _dot(jnp.concatenate([bd(b * wend), bd(kp * wend)], axis=0), tile_t).astype(BF16)
    ds = _tn(uv, bk_rows)

    r16 = lax.broadcasted_iota(jnp.int32, (hp, wl), 0)
    c16 = lax.broadcasted_iota(jnp.int32, (hp, wl), 1)
    wc = jnp.exp(cwl)
    wc_rows = _dot_split_l(jnp.where(r16 == _idiv(c16, HEAD_DIM), wc, 0.0), tile_t)
    rsel = lax.broadcasted_iota(jnp.int32, (hp, HEAD_DIM), 0)
    for h in range(hp):
        rs = slice(h * HEAD_DIM, (h + 1) * HEAD_DIM)
        wc_h = jnp.sum(jnp.where(rsel == h, wc_rows, 0.0), axis=0, keepdims=True)
        sout_ref[0, h] = s_old[rs] * wc_h + ds[rs]

    y = yb[0:ct]
    for h in range(1, hp):
        y = y + yb[h * ct:(h + 1) * ct]
    inv = 1.0 / HEAD_DIM
    yc = y - segsum(y) * inv
    y = yc * lax.rsqrt(segsum(yc * yc) * inv + RWKV_LN_EPS) * ln_g + ln_b
    y = (y + segsum(r * kp * r_k) * v) * _silu(g)
    y = jnp.where(valid, y, 0.0)
    y_ref[...] = _pad_rows(y, rows_out).astype(BF16)


def rwkv_short(rkvg, lora1, lora_b, par, s0, *, bsz, tp, hi):
    _, m, e = rkvg.shape
    heads = e // HEAD_DIM
    hp = RWKV_SHORT_HEADS
    wl = hp * HEAD_DIM
    rb = tp // 8
    w4 = RWKV_HPB * HEAD_DIM
    ones_bd = (jnp.arange(w4)[:, None] // HEAD_DIM == jnp.arange(w4)[None, :] // HEAD_DIM).astype(BF16)
    tile_t = (jnp.arange(wl)[:, None] % HEAD_DIM == jnp.arange(HEAD_DIM)[None, :]).astype(BF16)
    kern = functools.partial(_rwkv_short_kernel, nvalid=hi, rows_out=tp)
    proj_spec = lambda s: pl.BlockSpec((1, 8, wl), lambda b, h: (s, b * rb, h))
    return pl.pallas_call(
        kern,
        grid=(bsz, heads // hp),
        in_specs=[
            proj_spec(0), proj_spec(1), proj_spec(2), proj_spec(3),
            pl.BlockSpec((1, 8, RWKV_LORA_PAD), lambda b, h: (0, b * rb, 0)),
            pl.BlockSpec((1, 8, RWKV_LORA_PAD), lambda b, h: (1, b * rb, 0)),
            pl.BlockSpec((1, RWKV_LORA_PAD, wl), lambda b, h: (0, 0, h)),
            pl.BlockSpec((1, RWKV_LORA_PAD, wl), lambda b, h: (1, 0, h)),
            pl.BlockSpec((8, wl), lambda b, h: (0, h)),
            pl.BlockSpec((1, hp, HEAD_DIM, HEAD_DIM), lambda b, h: (b, h, 0, 0)),
            pl.BlockSpec((w4, w4), lambda b, h: (0, 0)),
            pl.BlockSpec((wl, HEAD_DIM), lambda b, h: (0, 0)),
        ],
        out_specs=[
            pl.BlockSpec((tp, wl), lambda b, h: (b, h)),
            pl.BlockSpec((1, hp, HEAD_DIM, HEAD_DIM), lambda b, h: (b, h, 0, 0)),
        ],
        out_shape=[
            jax.ShapeDtypeStruct((m, e), BF16),
            jax.ShapeDtypeStruct((bsz, heads, HEAD_DIM, HEAD_DIM), F32),
        ],
        compiler_params=_cparams(("parallel", "parallel")),
        name="rwkv_short",
    )(rkvg, rkvg, rkvg, rkvg, lora1, lora1, lora_b, lora_b, par, s0, ones_bd, tile_t)


def _rwkv_lanes_kernel(r_ref, k_ref, v_ref, g_ref, lw_ref, la_ref, bw_ref, ba_ref, par_ref, s0_ref,
                       y_ref, sout_ref, tk, tw, tb, tq, tr, tv, ty, *, nt, nb):
    hp = RWKV_LANE_HEADS
    wl = hp * HEAD_DIM
    ri = lax.broadcasted_iota(jnp.int32, (wl, wl), 0)
    ci = lax.broadcasted_iota(jnp.int32, (wl, wl), 1)
    ones_bd = (_idiv(ri, HEAD_DIM) == _idiv(ci, HEAD_DIM)).astype(BF16)

    def segsum(x):
        return _dot_split_l(x, ones_bd, terms=2)

    par = par_ref[...]
    w0, a0, k_k, k_a, r_k, ln_g, ln_b = (par[i:i + 1] for i in range(7))
    r, k, v, g = (x[0].astype(F32) for x in (r_ref, k_ref, v_ref, g_ref))
    w_raw = w0 + _dot(jnp.tanh(lw_ref[0]).astype(BF16), bw_ref[0])
    a = jax.nn.sigmoid(a0 + _dot(la_ref[0].astype(BF16), ba_ref[0]))
    decay = jnp.exp(-jnp.exp(-_softplus(-w_raw) - 0.5))
    kk = k * k_k
    kk = kk / jnp.maximum(jnp.sqrt(segsum(kk * kk)), 1e-12)
    kp = k * (1.0 + (a - 1.0) * k_a)
    bb = kk * a

    for t in range(nt):
        rows = slice(t * nb, (t + 1) * nb)
        tk[t] = (-kk[rows]).T
        tw[t] = decay[rows].T
        tb[t] = bb[rows].T
        tq[t] = kp[rows].T
        tr[t] = r[rows].T
        tv[t] = v[rows].T

    for hh in range(hp):
        ks = slice(hh * HEAD_DIM, (hh + 1) * HEAD_DIM)

        def body(vi, carry, hh=hh, ks=ks):
            sv = s0_ref[hh, vi]
            row = hh * HEAD_DIM + vi
            for t in range(nt):
                sa = jnp.sum(sv * tk[t, ks, :], axis=0, keepdims=True)
                sv = sv * tw[t, ks, :] + sa * tb[t, ks, :] + tv[t, pl.ds(row, 1), :] * tq[t, ks, :]
                ty[t, pl.ds(row, 1), :] = jnp.sum(sv * tr[t, ks, :], axis=0, keepdims=True)
            sout_ref[hh, vi] = sv
            return carry

        lax.fori_loop(0, HEAD_DIM, body, 0, unroll=4)

    inv = 1.0 / HEAD_DIM
    for t in range(nt):
        rows = slice(t * nb, (t + 1) * nb)
        y = ty[t].T
        yc = y - segsum(y) * inv
        y = yc * lax.rsqrt(segsum(yc * yc) * inv + RWKV_LN_EPS) * ln_g + ln_b
        y = (y + segsum(r[rows] * kp[rows] * r_k) * v[rows]) * _silu(g[rows])
        y_ref[rows, :] = y.astype(BF16)


def rwkv_lanes(rkvg, lora1, lora_b, par, s0t, *, nt, nb):
    _, m, e = rkvg.shape
    heads = e // HEAD_DIM
    hp = RWKV_LANE_HEADS
    wl = hp * HEAD_DIM
    kern = functools.partial(_rwkv_lanes_kernel, nt=nt, nb=nb)
    proj_spec = lambda s: pl.BlockSpec((1, m, wl), lambda h: (s, 0, h))
    state_spec = pl.BlockSpec((hp, HEAD_DIM, HEAD_DIM, nb), lambda h: (h, 0, 0, 0))
    tile = pltpu.VMEM((nt, wl, nb), F32)
    return pl.pallas_call(
        kern,
        grid=(heads // hp,),
        in_specs=[
            proj_spec(0), proj_spec(1), proj_spec(2), proj_spec(3),
            pl.BlockSpec((1, m, RWKV_LORA_PAD), lambda h: (0, 0, 0)),
            pl.BlockSpec((1, m, RWKV_LORA_PAD), lambda h: (1, 0, 0)),
            pl.BlockSpec((1, RWKV_LORA_PAD, wl), lambda h: (0, 0, h)),
            pl.BlockSpec((1, RWKV_LORA_PAD, wl), lambda h: (1, 0, h)),
            pl.BlockSpec((8, wl), lambda h: (0, h)),
            state_spec,
        ],
        out_specs=[pl.BlockSpec((m, wl), lambda h: (0, h)), state_spec],
        out_shape=[jax.ShapeDtypeStruct((m, e), BF16), jax.ShapeDtypeStruct(s0t.shape, F32)],
        scratch_shapes=[tile] * 7,
        compiler_params=_cparams(("parallel",)),
        name="rwkv_lanes",
    )(rkvg, rkvg, rkvg, rkvg, lora1, lora1, lora_b, lora_b, par, s0t)


def _prep_weights(norm_g, ssd_w_in, ssd_dt_bias, ssd_a_log, ssd_d, ssd_w_out,
                  rwkv_w_rkvg, rwkv_w_lora_a, rwkv_w_lora_b, rwkv_a_lora_a, rwkv_a_lora_b,
                  rwkv_w0, rwkv_a0, rwkv_k_k, rwkv_k_a, rwkv_r_k, rwkv_ln_g, rwkv_ln_b, rwkv_w_out,
                  ret_w_in, ret_w_out):
    d_inner = ssd_w_out.shape[1]
    n_main = d_inner + d_inner + 2 * SSD_GROUPS * SSD_STATE
    ns = ssd_w_in.shape[0]
    d_model = ssd_w_in.shape[1]

    def head_lanes(p):
        p = p.reshape(ns, SSD_GROUPS, 1, SSD_HPG)
        return jnp.pad(p, ((0, 0), (0, 0), (0, 0), (0, 128 - SSD_HPG)))

    w_dt = ssd_w_in[:, :, n_main:].reshape(ns, d_model, SSD_GROUPS, SSD_HPG)
    w_dt = jnp.pad(w_dt, ((0, 0), (0, 0), (0, 0), (0, 128 - SSD_HPG))).reshape(ns, d_model, SSD_GROUPS * 128)
    rank = rwkv_w_lora_a.shape[2]
    lora_a = jnp.stack([rwkv_w_lora_a, rwkv_a_lora_a], axis=1)
    lora_a = jnp.pad(lora_a, ((0, 0), (0, 0), (0, 0), (0, RWKV_LORA_PAD - rank)))
    lora_b = jnp.stack([rwkv_w_lora_b, rwkv_a_lora_b], axis=1)
    lora_b = jnp.pad(lora_b, ((0, 0), (0, 0), (0, RWKV_LORA_PAD - rank), (0, 0)))
    nr = rwkv_w0.shape[0]
    par = jnp.stack([rwkv_w0, rwkv_a0, rwkv_k_k, rwkv_k_a, rwkv_r_k.reshape(nr, -1), rwkv_ln_g, rwkv_ln_b,
                     jnp.zeros_like(rwkv_w0)], axis=1)
    return dict(
        ssd_w_main=ssd_w_in[:, :, :n_main].astype(BF16), ssd_w_dt=w_dt.astype(BF16),
        ssd_dtb=head_lanes(ssd_dt_bias), ssd_alog=head_lanes(ssd_a_log), ssd_dskip=head_lanes(ssd_d),
        ssd_w_out=ssd_w_out.astype(BF16),
        rwkv_w=rwkv_w_rkvg.astype(BF16), rwkv_lora_a=lora_a.astype(BF16), rwkv_lora_b=lora_b.astype(BF16),
        rwkv_par=par, rwkv_w_out=rwkv_w_out.astype(BF16),
        ret_w_in=ret_w_in.astype(BF16), ret_w_out=ret_w_out.astype(BF16),
    )


def _trunk(h, conv_st, ssd_st, shift_st, wkv_st, ret_st, pos, *, bsz, tp, lq, lo, hi, depth,
           norm_g, final_norm_g, wts, ssd_conv_w, ssd_conv_b, ssd_norm_g, rwkv_mu, ret_norm_g):
    d_model = h.shape[1]
    geo = dict(bsz=bsz, tp=tp, lo=lo, hi=hi)
    new_conv, new_shift, new_wkv, new_ret = [], [], [], []
    new_ssd = None

    half = RET_QK // 2
    inv_freq = 1.0 / (RET_THETA_BASE ** jnp.linspace(0.0, 1.0, half, dtype=F32))
    ang = pos.astype(F32)[:, None] * inv_freq
    cos, sin = jnp.cos(ang), jnp.sin(ang)
    log_gamma = jnp.log1p(-jnp.exp2(-5.0 - jnp.arange(RET_HEADS, dtype=F32)))
    lg = jnp.broadcast_to(log_gamma[:, None, None], (RET_HEADS, 8, 128))

    for layer in range(depth):
        kind, j = layer % 3, layer // 3
        u = rmsnorm(h, norm_g[layer], F32 if kind == 1 else BF16)
        if kind == 0:
            proj = matmul(u, wts["ssd_w_main"][j], out_dtype=BF16)
            dt_raw = matmul(u, wts["ssd_w_dt"][j])
            conv_init = jnp.pad(conv_st[j], ((0, 0), (8 - (SSD_CONV - 1), 0), (0, 0)))
            y, new_ssd = ssd_scan(proj, dt_raw, conv_init, ssd_st, j, new_ssd, ssd_conv_w[j], ssd_conv_b[j],
                                  wts["ssd_dtb"][j], wts["ssd_alog"][j], wts["ssd_dskip"][j], ssd_norm_g[j],
                                  lq=lq, **geo)
            nk = SSD_CONV - 1
            last = u.reshape(bsz, tp, d_model)[:, hi - nk:hi].reshape(bsz * nk, d_model)
            last = jnp.pad(last, ((0, -(bsz * nk) % 16), (0, 0)))
            xbc = matmul(last, wts["ssd_w_main"][j])[:bsz * nk, y.shape[1]:]
            new_conv.append(xbc.reshape(bsz, nk, -1))
            h = matmul(y, wts["ssd_w_out"][j], res=h)
        elif kind == 1 and lo == 0 and hi <= 8 and bsz % LANES == 0:
            tmajor = lambda x: jnp.swapaxes(x.reshape(bsz, tp, -1)[:, :hi], 0, 1)
            uc = tmajor(u)
            prev = jnp.concatenate([shift_st[j][None], uc[:-1]], axis=0).reshape(hi * bsz, d_model)
            uc = uc.reshape(hi * bsz, d_model)
            rkvg = mix_matmul(uc, prev, rwkv_mu[j][:4], wts["rwkv_w"][j], out_dtype=BF16)
            lora1 = mix_matmul(uc, prev, rwkv_mu[j][4:], wts["rwkv_lora_a"][j])
            y, s_t = rwkv_lanes(rkvg, lora1, wts["rwkv_lora_b"][j], wts["rwkv_par"][j],
                                jnp.transpose(wkv_st[j], (1, 2, 3, 0)), nt=hi, nb=bsz)
            new_shift.append(u.reshape(bsz, tp, d_model)[:, hi - 1])
            new_wkv.append(jnp.transpose(s_t, (3, 0, 1, 2)))
            hc = matmul(y, wts["rwkv_w_out"][j], res=tmajor(h).reshape(hi * bsz, d_model))
            hc = jnp.swapaxes(hc.reshape(hi, bsz, d_model), 0, 1)
            h = jnp.pad(hc, ((0, 0), (0, tp - hi), (0, 0))).reshape(bsz * tp, d_model)
        elif kind == 1:
            u3 = u.reshape(bsz, tp, d_model)
            prev = jnp.concatenate([shift_st[j][:, None, :], u3[:, :-1]], axis=1).reshape(bsz * tp, d_model)
            rkvg = mix_matmul(u, prev, rwkv_mu[j][:4], wts["rwkv_w"][j], out_dtype=BF16)
            lora1 = mix_matmul(u, prev, rwkv_mu[j][4:], wts["rwkv_lora_a"][j])
            if lo == 0 and hi <= 8 and tp <= 16:
                y, s_new = rwkv_short(rkvg.astype(F32), lora1, wts["rwkv_lora_b"][j], wts["rwkv_par"][j], wkv_st[j],
                                      bsz=bsz, tp=tp, hi=hi)
            else:
                y, s_new = rwkv_scan(rkvg, lora1, wts["rwkv_lora_b"][j], wts["rwkv_par"][j], wkv_st[j], **geo)
            new_shift.append(u3[:, hi - 1])
            new_wkv.append(s_new)
            h = matmul(y, wts["rwkv_w_out"][j], res=h)
        else:
            proj = matmul(u, wts["ret_w_in"][j], out_dtype=BF16)
            y, s_new = ret_scan(proj, cos, sin, lg, ret_st[j], ret_norm_g[j], lq=lq, **geo)
            new_ret.append(s_new)
            h = matmul(y, wts["ret_w_out"][j], res=h)
    y = rmsnorm(h, final_norm_g)
    return (y, jnp.stack(new_conv), new_ssd, jnp.stack(new_shift), jnp.stack(new_wkv), jnp.stack(new_ret))


def kernel(x_prompt, x_sample, state_ssd_conv, state_ssd, state_rwkv_shift, state_rwkv_wkv, state_ret, meta_tokens, norm_g, final_norm_g, ssd_w_in, ssd_conv_w, ssd_conv_b, ssd_dt_bias, ssd_a_log, ssd_d, ssd_norm_g, ssd_w_out, rwkv_mu, rwkv_w_rkvg, rwkv_w0, rwkv_w_lora_a, rwkv_w_lora_b, rwkv_a0, rwkv_a_lora_a, rwkv_a_lora_b, rwkv_k_k, rwkv_k_a, rwkv_r_k, rwkv_ln_g, rwkv_ln_b, rwkv_w_out, ret_w_in, ret_norm_g, ret_w_out):
    depth = norm_g.shape[0]
    d_model = x_prompt.shape[2]
    wts = _prep_weights(norm_g, ssd_w_in, ssd_dt_bias, ssd_a_log, ssd_d, ssd_w_out,
                        rwkv_w_rkvg, rwkv_w_lora_a, rwkv_w_lora_b, rwkv_a_lora_a, rwkv_a_lora_b,
                        rwkv_w0, rwkv_a0, rwkv_k_k, rwkv_k_a, rwkv_r_k, rwkv_ln_g, rwkv_ln_b, rwkv_w_out,
                        ret_w_in, ret_w_out)
    common = dict(depth=depth, norm_g=norm_g, final_norm_g=final_norm_g, wts=wts, ssd_conv_w=ssd_conv_w,
                  ssd_conv_b=ssd_conv_b, ssd_norm_g=ssd_norm_g, rwkv_mu=rwkv_mu, ret_norm_g=ret_norm_g)

    bp, seq, _ = x_prompt.shape
    lq_p = 128
    lo_p = lq_p - N_META
    tp_p = lo_p + N_META + seq
    h_p = jnp.concatenate([jnp.zeros((bp, lo_p, d_model), F32),
                           jnp.broadcast_to(meta_tokens[None], (bp, N_META, d_model)), x_prompt], axis=1)
    zeros_like_b = lambda s: jnp.zeros((s.shape[0], bp) + s.shape[2:], F32)
    pos_p = jnp.maximum(jnp.arange(tp_p) - lo_p, 0)
    outs_p = _trunk(h_p.reshape(bp * tp_p, d_model), zeros_like_b(state_ssd_conv), zeros_like_b(state_ssd),
                    zeros_like_b(state_rwkv_shift), zeros_like_b(state_rwkv_wkv), zeros_like_b(state_ret), pos_p,
                    bsz=bp, tp=tp_p, lq=lq_p, lo=lo_p, hi=tp_p, **common)
    y_prompt = outs_p[0].reshape(bp, tp_p, d_model)[:, lo_p + N_META:]

    bs, ds, _ = x_sample.shape
    tp_s = 16
    h_s = jnp.concatenate([x_sample, jnp.zeros((bs, tp_s - ds, d_model), F32)], axis=1)
    pos_s = PAST_LEN + jnp.arange(tp_s)
    outs_s = _trunk(h_s.reshape(bs * tp_s, d_model), state_ssd_conv, state_ssd, state_rwkv_shift, state_rwkv_wkv,
                    state_ret, pos_s, bsz=bs, tp=tp_s, lq=tp_s, lo=0, hi=ds, **common)
    y_sample = outs_s[0].reshape(bs, tp_s, d_model)[:, :ds]

    return (y_prompt, y_sample) + tuple(outs_p[1:]) + tuple(outs_s[1:])
```

```python
import functools
import math

import jax
import jax.numpy as jnp
from jax import lax
from jax.experimental import pallas as pl
from jax.experimental.pallas import tpu as pltpu

F32 = jnp.float32
BF16 = jnp.bfloat16

EPS = 1e-6
N_META = 16
HEAD_DIM = 64
SSD_STATE = 128
SSD_GROUPS = 8
SSD_HPG = 8
SSD_CONV = 4
RET_HEADS = 8
RET_QK = 256
RET_V = 512
RET_THETA_BASE = 10000.0
RWKV_LORA_PAD = 128
RWKV_CHUNK = 64
RWKV_HPB = 4
RWKV_GROUPS_PER_STEP = 8
RWKV_SHORT_HEADS = 16
RWKV_LANE_HEADS = 2
LANES = 128
MIN_CHUNK_ROWS = 16
PAST_LEN = 16384
RWKV_LN_EPS = 1e-5 * HEAD_DIM
NEG = -1e30
ROW_TILE = 512
MATMUL_VMEM_BUDGET = 40 * 1024 * 1024
VMEM_LIMIT = 56 * 1024 * 1024


def _cparams(sem):
    return pltpu.CompilerParams(dimension_semantics=sem, vmem_limit_bytes=VMEM_LIMIT)


def _nt(a, b):
    return lax.dot_general(a, b, (((1,), (1,)), ((), ())), preferred_element_type=F32)


def _tn(a, b):
    return lax.dot_general(a, b, (((0,), (0,)), ((), ())), preferred_element_type=F32)


def _dot(a, b):
    return jnp.dot(a, b, preferred_element_type=F32)


def _split(x, terms):
    parts = []
    r = x
    for i in range(terms):
        p = r.astype(BF16)
        parts.append(p)
        if i + 1 < terms:
            r = r - p.astype(F32)
    return parts


def _dot_split_l(x, m, terms=3):
    acc = None
    for p in _split(x, terms):
        d = _dot(p, m)
        acc = d if acc is None else acc + d
    return acc


def _dot_split_r(m, x, terms=3):
    acc = None
    for p in _split(x, terms):
        d = _dot(m, p)
        acc = d if acc is None else acc + d
    return acc


def _pad_rows(x, rows):
    if x.shape[0] == rows:
        return x
    return jnp.concatenate([x, jnp.zeros((rows - x.shape[0], x.shape[1]), x.dtype)], axis=0)


def _idiv(x, n):
    return jnp.right_shift(x, int(math.log2(n)))


def _imod(x, n):
    return jnp.bitwise_and(x, n - 1)


def _silu(x):
    h = 0.5 * x
    return h + h * jnp.tanh(h)


def _softplus(x):
    return jnp.maximum(x, 0.0) + jnp.log(1.0 + jnp.exp(-jnp.abs(x)))


def _rmsnorm_kernel(x_ref, g_ref, o_ref):
    x = x_ref[...]
    ms = jnp.mean(x * x, axis=-1, keepdims=True)
    o_ref[...] = (x * lax.rsqrt(ms + EPS) * g_ref[...]).astype(o_ref.dtype)


def _row_tile(m):
    tm = math.gcd(m, ROW_TILE)
    assert tm % 16 == 0, m
    return tm


def _matmul_tiles(m, k, n, *, a_bytes, n_a, cast, n_out):
    best = None
    for tn in (t for t in (1024, 512, 256, 128) if n % t == 0):
        for tm in (t for t in range(16, m + 1, 16) if m % t == 0):
            need = (2 * n_a * tm * k * a_bytes + (tm * k * 2 if cast else 0)
                    + 2 * k * tn * 2 + 2 * n_out * tm * tn * 4)
            if need <= MATMUL_VMEM_BUDGET and (best is None or (tm * tn, tm) > (best[0] * best[1], best[0])):
                best = (tm, tn)
    assert best is not None, (m, k, n)
    return best


def rmsnorm(x, g, out_dtype=F32):
    m, d = x.shape
    tm = _row_tile(m)
    return pl.pallas_call(
        _rmsnorm_kernel,
        grid=(m // tm,),
        in_specs=[pl.BlockSpec((tm, d), lambda i: (i, 0)), pl.BlockSpec((1, d), lambda i: (0, 0))],
        out_specs=pl.BlockSpec((tm, d), lambda i: (i, 0)),
        out_shape=jax.ShapeDtypeStruct((m, d), out_dtype),
        compiler_params=_cparams(("parallel",)),
        name="rmsnorm",
    )(x, g.reshape(1, d))


def _mm_kernel(a_ref, w_ref, *rest, has_res, cast):
    rest = list(rest)
    abf_ref = rest.pop() if cast else a_ref
    o_ref = rest.pop()

    if cast:
        @pl.when(pl.program_id(1) == 0)
        def _():
            abf_ref[...] = a_ref[...].astype(BF16)

    acc = _dot(abf_ref[...], w_ref[...])
    if has_res:
        acc = rest[0][...] + acc
    o_ref[...] = acc.astype(o_ref.dtype)


def matmul(a, w, res=None, out_dtype=F32):
    m, k = a.shape
    n = w.shape[1]
    cast = a.dtype != BF16
    tm, tn = _matmul_tiles(m, k, n, a_bytes=a.dtype.itemsize, n_a=1, cast=cast, n_out=2 if res is not None else 1)
    in_specs = [pl.BlockSpec((tm, k), lambda i, j: (i, 0)), pl.BlockSpec((k, tn), lambda i, j: (0, j))]
    args = [a, w]
    if res is not None:
        in_specs.append(pl.BlockSpec((tm, tn), lambda i, j: (i, j)))
        args.append(res)
    return pl.pallas_call(
        functools.partial(_mm_kernel, has_res=res is not None, cast=cast),
        grid=(m // tm, n // tn),
        in_specs=in_specs,
        out_specs=pl.BlockSpec((tm, tn), lambda i, j: (i, j)),
        out_shape=jax.ShapeDtypeStruct((m, n), out_dtype),
        scratch_shapes=[pltpu.VMEM((tm, k), BF16)] if cast else [],
        compiler_params=_cparams(("parallel", "arbitrary")),
        name="matmul_res" if res is not None else "matmul",
    )(*args)


def _mixmm_kernel(u_ref, p_ref, mu_ref, w_ref, o_ref, xm_ref):
    @pl.when(pl.program_id(2) == 0)
    def _():
        u = u_ref[...]
        xm_ref[...] = (u + (p_ref[...] - u) * mu_ref[0]).astype(BF16)

    o_ref[0] = _dot(xm_ref[...], w_ref[0]).astype(o_ref.dtype)


def mix_matmul(u, prev, mu, w, out_dtype=F32):
    m, k = u.shape
    s, _, n = w.shape
    tm, tn = _matmul_tiles(m, k, n, a_bytes=4, n_a=2, cast=True, n_out=1)
    return pl.pallas_call(
        _mixmm_kernel,
        grid=(m // tm, s, n // tn),
        in_specs=[
            pl.BlockSpec((tm, k), lambda i, si, j: (i, 0)),
            pl.BlockSpec((tm, k), lambda i, si, j: (i, 0)),
            pl.BlockSpec((1, 1, k), lambda i, si, j: (si, 0, 0)),
            pl.BlockSpec((1, k, tn), lambda i, si, j: (si, 0, j)),
        ],
        out_specs=pl.BlockSpec((1, tm, tn), lambda i, si, j: (si, i, j)),
        out_shape=jax.ShapeDtypeStruct((s, m, n), out_dtype),
        scratch_shapes=[pltpu.VMEM((tm, k), BF16)],
        compiler_params=_cparams(("parallel", "arbitrary", "arbitrary")),
        name="mix_matmul",
    )(u, prev, mu.reshape(s, 1, k), w)


def _conv_silu(cur, car_ref, cols, w, b, lq, lb):
    car_ref[8:8 + lq, cols] = cur
    acc = b + cur * w[SSD_CONV - 1:SSD_CONV]
    for s in range(1, SSD_CONV):
        acc = acc + car_ref[8 - s:8 - s + lq, cols] * w[SSD_CONV - 1 - s:SSD_CONV - s]
    car_ref[0:8, cols] = cur[lb - 8:lb]
    return _silu(acc)


def _ssd_kernel(z_ref, x_ref, b_ref, c_ref, dt_ref, ci_ref, s0_ref, cw_ref, cb_ref, dtb_ref, alog_ref,
                d_ref, ng_ref, *rest, lq, lo, hi, nchunks, out_layer):
    y_ref, sout_ref, car, st = rest[-4:]
    c = pl.program_id(1)
    gw = SSD_HPG * HEAD_DIM

    @pl.when(c == 0)
    def _():
        car[0:8, :] = ci_ref[0]
        st[...] = s0_ref[0, 0].reshape(SSD_GROUPS * gw, SSD_STATE)

    gens = [_ssd_group(g, z_ref, x_ref, b_ref, c_ref, dt_ref, cw_ref, cb_ref, dtb_ref, alog_ref, d_ref,
                       ng_ref, y_ref, car, st, lb=lq, lo=lo, hi=hi) for g in range(SSD_GROUPS)]
    for _ in zip(*gens):
        pass

    @pl.when(c == nchunks - 1)
    def _():
        for layer in range(sout_ref.shape[0]):
            if layer == out_layer:
                sout_ref[layer, 0] = st[...].reshape(SSD_GROUPS * SSD_HPG, HEAD_DIM, SSD_STATE)
            else:
                sout_ref[layer, 0] = jnp.zeros(sout_ref.shape[2:], F32)


def _ssd_group(g, z_ref, x_ref, b_ref, c_ref, dt_ref, cw_ref, cb_ref, dtb_ref, alog_ref, d_ref, ng_ref,
               y_ref, car, st, *, lb, lo, hi):
    c = pl.program_id(1)
    lq = max(lb, MIN_CHUNK_ROWS)
    gw = SSD_HPG * HEAD_DIM
    d_inner = SSD_GROUPS * gw
    xs = slice(g * gw, (g + 1) * gw)
    ns = slice(g * SSD_STATE, (g + 1) * SSD_STATE)
    bs = slice(d_inner + g * SSD_STATE, d_inner + (g + 1) * SSD_STATE)
    cs_ = slice(d_inner + (SSD_GROUPS + g) * SSD_STATE, d_inner + (SSD_GROUPS + g + 1) * SSD_STATE)

    f32 = lambda x: _pad_rows(x.astype(F32), lq)
    xc = _conv_silu(f32(x_ref[:, xs]), car, xs, cw_ref[:, xs], cb_ref[:, xs], lq, lb)
    bc = _conv_silu(f32(b_ref[:, ns]), car, bs, cw_ref[:, bs], cb_ref[:, bs], lq, lb)
    cc = _conv_silu(f32(c_ref[:, ns]), car, cs_, cw_ref[:, cs_], cb_ref[:, cs_], lq, lb)

    ti = lax.broadcasted_iota(jnp.int32, (lq, 1), 0)
    pos = c * lb + ti
    valid = (ti < lb) & (pos >= lo) & (pos < hi)
    lane = lax.broadcasted_iota(jnp.int32, (lq, 128), 1)
    dt = _softplus(f32(dt_ref[:, ns]) + dtb_ref[:, ns])
    dt = jnp.where(valid & (lane < SSD_HPG), dt, 0.0)
    la = dt * (-jnp.exp(alog_ref[:, ns]))
    yield

    ri = lax.broadcasted_iota(jnp.int32, (lq, lq), 0)
    ci = lax.broadcasted_iota(jnp.int32, (lq, lq), 1)
    tril = (ri >= ci).astype(BF16)
    triu = (ri <= ci).astype(BF16)
    acum = _dot_split_r(tril, la)
    acum_t = sum(_tn(part, triu) for part in _split(la, 3))
    yield
    a_end = acum[lq - 1:lq]
    dec_end = jnp.exp(a_end - acum)
    e_in = jnp.exp(acum)
    cd = jnp.exp(a_end)

    bcp = bc.astype(BF16)
    ccb = cc.astype(BF16)
    g_sc = _nt(ccb, bcp)
    st_old = st[xs, :]
    y_in = _nt(ccb, st_old.astype(BF16))
    yield

    causal = ci <= ri
    lane_q = lax.broadcasted_iota(jnp.int32, (lq, 128), 1) < HEAD_DIM
    lane_k = lane_q
    row_k = lax.broadcasted_iota(jnp.int32, (128, 1), 0) < HEAD_DIM
    dvec = d_ref[:, ns]

    ys = []
    for p in range(SSD_HPG // 2):
        h0, h1 = 2 * p, 2 * p + 1
        xp = xc[:, 128 * p:128 * (p + 1)]
        vp = xp * jnp.where(lane_q, dt[:, h0:h0 + 1], dt[:, h1:h1 + 1])
        vpp = vp
        yp = y_in[:, 128 * p:128 * (p + 1)] * jnp.where(lane_q, e_in[:, h0:h0 + 1], e_in[:, h1:h1 + 1])
        yp = yp + xp * jnp.where(lane_q, dvec[:, h0:h0 + 1], dvec[:, h1:h1 + 1])
        for hh, h in ((0, h0), (1, h1)):
            seg = acum[:, h:h + 1] - acum_t[h:h + 1, :]
            lm = jnp.exp(jnp.where(causal, seg, NEG))
            pm = (g_sc * lm).astype(BF16)
            vm = jnp.where(lane_k if hh == 0 else jnp.logical_not(lane_k), vpp, 0.0).astype(BF16)
            yp = yp + _dot(pm, vm)
        ys.append(yp)
        vend = vpp * jnp.where(lane_k, dec_end[:, h0:h0 + 1], dec_end[:, h1:h1 + 1])
        upd = _tn(vend.astype(BF16), bcp)
        cdp = jnp.where(row_k, cd[:, h0:h0 + 1], cd[:, h1:h1 + 1])
        st[g * gw + 128 * p:g * gw + 128 * (p + 1), :] = st_old[128 * p:128 * (p + 1), :] * cdp + upd
        yield

    y = jnp.concatenate(ys, axis=1)
    y = y * _silu(f32(z_ref[:, xs]))
    ms = jnp.mean(y * y, axis=-1, keepdims=True)
    y = y * lax.rsqrt(ms + EPS) * ng_ref[:, xs]
    y_ref[:, xs] = jnp.where(valid, y, 0.0)[:lb].astype(y_ref.dtype)
    yield


def ssd_scan(proj, dt_raw, conv_init, s0_all, layer, s_buf, conv_w, conv_b, dtb, alog, dskip, norm_g,
             *, bsz, tp, lq, lo, hi):
    m = proj.shape[0]
    heads = SSD_GROUPS * SSD_HPG
    nlayers = s0_all.shape[0]
    state_spec = pl.BlockSpec((1, 1, heads, HEAD_DIM, SSD_STATE), lambda b, c: (layer, b, 0, 0, 0))
    if s_buf is not None:
        out_state_spec, out_layer = state_spec, 0
        extra_specs, extra_args, aliases = [pl.BlockSpec(memory_space=pl.ANY)], [s_buf], {13: 1}
    else:
        out_state_spec = pl.BlockSpec((nlayers, 1, heads, HEAD_DIM, SSD_STATE), lambda b, c: (0, b, 0, 0, 0))
        out_layer = layer
        extra_specs, extra_args, aliases = [], [], {}
    nch = tp // lq
    d_inner = heads * HEAD_DIM
    gn = SSD_GROUPS * SSD_STATE
    conv_dim = d_inner + 2 * gn
    row = lambda b, c: b * nch + c
    const = lambda b, c: (0, 0)
    kern = functools.partial(_ssd_kernel, lq=lq, lo=lo, hi=hi, nchunks=nch, out_layer=out_layer)
    return pl.pallas_call(
        kern,
        grid=(bsz, nch),
        in_specs=[
            pl.BlockSpec((lq, d_inner), lambda b, c: (row(b, c), 0)),
            pl.BlockSpec((lq, d_inner), lambda b, c: (row(b, c), 1)),
            pl.BlockSpec((lq, gn), lambda b, c: (row(b, c), 2 * d_inner // gn)),
            pl.BlockSpec((lq, gn), lambda b, c: (row(b, c), 2 * d_inner // gn + 1)),
            pl.BlockSpec((lq, gn), lambda b, c: (row(b, c), 0)),
            pl.BlockSpec((1, 8, conv_dim), lambda b, c: (b, 0, 0)),
            state_spec,
            pl.BlockSpec((SSD_CONV, conv_dim), const),
            pl.BlockSpec((1, conv_dim), const),
            pl.BlockSpec((1, gn), const),
            pl.BlockSpec((1, gn), const),
            pl.BlockSpec((1, gn), const),
            pl.BlockSpec((1, d_inner), const),
        ] + extra_specs,
        out_specs=[
            pl.BlockSpec((lq, d_inner), lambda b, c: (row(b, c), 0)),
            out_state_spec,
        ],
        out_shape=[
            jax.ShapeDtypeStruct((m, d_inner), proj.dtype),
            jax.ShapeDtypeStruct(s0_all.shape, F32),
        ],
        scratch_shapes=[pltpu.VMEM((8 + max(lq, MIN_CHUNK_ROWS), conv_dim), F32),
                        pltpu.VMEM((d_inner, SSD_STATE), F32)],
        input_output_aliases=aliases,
        compiler_params=_cparams(("parallel", "arbitrary")),
        name="ssd_scan",
    )(proj, proj, proj, proj, dt_raw, conv_init, s0_all, conv_w, conv_b.reshape(1, -1),
      dtb.reshape(1, -1), alog.reshape(1, -1), dskip.reshape(1, -1), norm_g.reshape(1, -1), *extra_args)


def _ret_kernel(q_ref, k_ref, v_ref, g_ref, cos_ref, sin_ref, lg_ref, s0_ref, ng_ref,
                y_ref, sout_ref, *, lq, lo, hi):
    c = pl.program_id(1)

    @pl.when(c == 0)
    def _():
        sout_ref[...] = s0_ref[...]

    gens = [_ret_head(h, q_ref, k_ref, v_ref, g_ref, cos_ref, sin_ref, lg_ref, ng_ref, y_ref, sout_ref,
                      lb=lq, lo=lo, hi=hi) for h in range(RET_HEADS)]
    for _ in zip(*gens):
        pass


def _ret_head(h, q_ref, k_ref, v_ref, g_ref, cos_ref, sin_ref, lg_ref, ng_ref, y_ref, st_ref, *, lb, lo, hi):
    c = pl.program_id(1)
    lq = max(lb, MIN_CHUNK_ROWS)
    f32 = lambda x: _pad_rows(x.astype(F32), lq)
    qs = slice(h * RET_QK, (h + 1) * RET_QK)
    vs = slice(h * RET_V, (h + 1) * RET_V)
    lg = lg_ref[h][0:1, 0:1]
    nv = float(hi - lo)

    def count(p):
        return jnp.clip((p + 1 - lo).astype(F32), 0.0, nv)

    base = c * lb
    ti = lax.broadcasted_iota(jnp.int32, (lq, 1), 0)
    pos_i = base + ti
    valid = (ti < lb) & (pos_i >= lo) & (pos_i < hi)
    cnt_i = count(pos_i)
    cnt_j = count(base + lax.broadcasted_iota(jnp.int32, (1, lq), 1))
    cnt_jc = cnt_i
    cnt0 = count(base - 1 + jnp.zeros((1, 1), jnp.int32))
    cnt_end = count(base + lb - 1 + jnp.zeros((1, 1), jnp.int32))

    cos = f32(cos_ref[...])
    sin = f32(sin_ref[...])
    half = RET_QK // 2

    def rot(x):
        x1, x2 = x[:, :half], x[:, half:]
        return jnp.concatenate([x1 * cos - x2 * sin, x1 * sin + x2 * cos], axis=1)

    qr = rot(f32(q_ref[:, qs])).astype(BF16)
    kr = jnp.where(valid, rot(f32(k_ref[:, qs])) * (RET_QK ** -0.5), 0.0)
    v = jnp.where(valid, f32(v_ref[:, vs]), 0.0)
    krp = kr.astype(BF16)
    vp = v
    yield

    sc = _nt(qr, krp)
    s_old = st_ref[0, h]
    y_in = _nt(qr, s_old.astype(BF16))
    yield
    qi = lax.broadcasted_iota(jnp.int32, (lq, lq), 0)
    kj = lax.broadcasted_iota(jnp.int32, (lq, lq), 1)
    dm = jnp.exp(jnp.where(kj <= qi, lg * (cnt_i - cnt_j), NEG))
    y = _dot((sc * dm).astype(BF16), vp.astype(BF16))
    y = y + y_in * jnp.exp(lg * (cnt_i - cnt0))
    vend = vp * jnp.exp(lg * (cnt_end - cnt_jc))
    st_ref[0, h] = s_old * jnp.exp(lg * (cnt_end - cnt0)) + _tn(vend.astype(BF16), krp)
    yield

    ms = jnp.mean(y * y, axis=-1, keepdims=True)
    y = y * lax.rsqrt(ms + EPS) * ng_ref[:, vs] * _silu(f32(g_ref[:, vs]))
    y_ref[:, vs] = jnp.where(valid, y, 0.0)[:lb].astype(y_ref.dtype)
    yield


def ret_scan(proj, cos, sin, lg, s0, norm_g, *, bsz, tp, lq, lo, hi):
    m = proj.shape[0]
    nch = tp // lq
    d_inner = RET_HEADS * RET_V
    d_qk = RET_HEADS * RET_QK
    row = lambda b, c: b * nch + c
    state_spec = pl.BlockSpec((1, RET_HEADS, RET_V, RET_QK), lambda b, c: (b, 0, 0, 0))
    kern = functools.partial(_ret_kernel, lq=lq, lo=lo, hi=hi)
    return pl.pallas_call(
        kern,
        grid=(bsz, nch),
        in_specs=[
            pl.BlockSpec((lq, d_qk), lambda b, c: (row(b, c), 0)),
            pl.BlockSpec((lq, d_qk), lambda b, c: (row(b, c), 1)),
            pl.BlockSpec((lq, d_inner), lambda b, c: (row(b, c), 2 * d_qk // d_inner)),
            pl.BlockSpec((lq, d_inner), lambda b, c: (row(b, c), 2 * d_qk // d_inner + 1)),
            pl.BlockSpec((lq, RET_QK // 2), lambda b, c: (c, 0)),
            pl.BlockSpec((lq, RET_QK // 2), lambda b, c: (c, 0)),
            pl.BlockSpec((RET_HEADS, 8, 128), lambda b, c: (0, 0, 0)),
            state_spec,
            pl.BlockSpec((1, d_inner), lambda b, c: (0, 0)),
        ],
        out_specs=[
            pl.BlockSpec((lq, d_inner), lambda b, c: (row(b, c), 0)),
            state_spec,
        ],
        out_shape=[
            jax.ShapeDtypeStruct((m, d_inner), proj.dtype),
            jax.ShapeDtypeStruct((bsz, RET_HEADS, RET_V, RET_QK), F32),
        ],
        compiler_params=_cparams(("parallel", "arbitrary")),
        name="ret_scan",
    )(proj, proj, proj, proj, cos, sin, lg, s0, norm_g.reshape(1, -1))


def _rwkv_kernel(r_ref, k_ref, v_ref, g_ref, lw_ref, la_ref, bw_ref, ba_ref, par_ref, s0_ref,
                 y_ref, sout_ref, st, *, lo, hi, nchunks, ngrp):
    c = pl.program_id(2)
    w4 = RWKV_HPB * HEAD_DIM
    r2 = lax.broadcasted_iota(jnp.int32, (w4, w4), 0)
    c2 = lax.broadcasted_iota(jnp.int32, (w4, w4), 1)
    blk = _idiv(r2, HEAD_DIM) == _idiv(c2, HEAD_DIM)

    @pl.when(c == 0)
    def _():
        tile = (lax.broadcasted_iota(jnp.int32, (HEAD_DIM, w4), 0)
                == _imod(lax.broadcasted_iota(jnp.int32, (HEAD_DIM, w4), 1), HEAD_DIM)).astype(BF16)
        for gi in range(ngrp):
            s0 = s0_ref[0, gi * RWKV_HPB:(gi + 1) * RWKV_HPB].reshape(w4, HEAD_DIM)
            st[gi] = jnp.where(blk, _dot_split_l(s0, tile), 0.0)

    s_new = [None] * ngrp
    gens = [_rwkv_group(gi, r_ref, k_ref, v_ref, g_ref, lw_ref, la_ref, bw_ref, ba_ref, par_ref,
                        y_ref, st, s_new, lo=lo, hi=hi) for gi in range(ngrp)]
    for _ in zip(*gens):
        pass

    @pl.when(c == nchunks - 1)
    def _():
        tile_t = (_imod(lax.broadcasted_iota(jnp.int32, (w4, HEAD_DIM), 0), HEAD_DIM)
                  == lax.broadcasted_iota(jnp.int32, (w4, HEAD_DIM), 1)).astype(BF16)
        for gi in range(ngrp):
            sout_ref[0, gi * RWKV_HPB:(gi + 1) * RWKV_HPB] = _dot_split_l(s_new[gi], tile_t).reshape(
                RWKV_HPB, HEAD_DIM, HEAD_DIM)


def _rwkv_group(gi, r_ref, k_ref, v_ref, g_ref, lw_ref, la_ref, bw_ref, ba_ref, par_ref, y_ref, st, out,
                *, lo, hi):
    c = pl.program_id(2)
    cs = RWKV_CHUNK
    w4 = RWKV_HPB * HEAD_DIM
    sl = slice(gi * w4, (gi + 1) * w4)

    nr = RWKV_HPB * cs
    ones_bd = (_idiv(lax.broadcasted_iota(jnp.int32, (w4, w4), 0), HEAD_DIM)
               == _idiv(lax.broadcasted_iota(jnp.int32, (w4, w4), 1), HEAD_DIM)).astype(BF16)
    blk = (_idiv(lax.broadcasted_iota(jnp.int32, (nr, w4), 0), cs)
           == _idiv(lax.broadcasted_iota(jnp.int32, (nr, w4), 1), HEAD_DIM))
    r2 = lax.broadcasted_iota(jnp.int32, (nr, nr), 0)
    c2 = lax.broadcasted_iota(jnp.int32, (nr, nr), 1)
    same = _idiv(r2, cs) == _idiv(c2, cs)

    def segsum(x):
        return _dot_split_l(x, ones_bd, terms=2)

    par = par_ref[:, sl]
    w0, a0, k_k, k_a, r_k, ln_g, ln_b = (par[i:i + 1] for i in range(7))

    r = r_ref[0, :, sl].astype(F32)
    k = k_ref[0, :, sl].astype(F32)
    v = v_ref[0, :, sl].astype(F32)
    g = g_ref[0, :, sl].astype(F32)
    w_raw = w0 + _dot(jnp.tanh(lw_ref[0]).astype(BF16), bw_ref[0, :, sl])
    a = jax.nn.sigmoid(a0 + _dot(la_ref[0].astype(BF16), ba_ref[0, :, sl]))

    ti = lax.broadcasted_iota(jnp.int32, (cs, 1), 0)
    pos = c * cs + ti
    valid = (pos >= lo) & (pos < hi)

    lw = -jnp.exp(-_softplus(-w_raw) - 0.5)
    kk = k * k_k
    kk = kk / jnp.maximum(jnp.sqrt(segsum(kk * kk)), 1e-12)
    yield
    kp = k * (1.0 + (a - 1.0) * k_a)
    lw = jnp.where(valid, lw, 0.0)
    kk = jnp.where(valid, kk, 0.0)
    kp = jnp.where(valid, kp, 0.0)
    vm = jnp.where(valid, v, 0.0)

    tri = (lax.broadcasted_iota(jnp.int32, (cs, cs), 0) >= lax.broadcasted_iota(jnp.int32, (cs, cs), 1)).astype(BF16)
    cw = _dot_split_r(tri, lw)
    yield
    cwl = cw[cs - 1:cs]
    wt = jnp.exp(cw)
    wi = jnp.exp(-cw)
    wend = jnp.exp(cwl - cw)
    b = kk * a
    at = -kk * jnp.exp(cw - lw)
    rt = r * wt

    def bd(x):
        return jnp.where(blk, jnp.concatenate([x] * RWKV_HPB, axis=0), 0.0).astype(BF16)

    lhs = jnp.concatenate([bd(at), bd(rt)], axis=0)
    rhs = jnp.concatenate([bd(b * wi), bd(kp * wi)], axis=0)
    sc = _nt(lhs, rhs)
    yield
    tt = _imod(r2, cs)
    jj = _imod(c2, cs)
    strict = same & (tt > jj)
    incl = same & (tt >= jj)
    mab = jnp.where(strict, sc[:nr, :nr], 0.0)
    mak = jnp.where(strict, sc[:nr, nr:], 0.0)
    nrb = jnp.where(incl, sc[nr:, :nr], 0.0)
    nrk = jnp.where(incl, sc[nr:, nr:], 0.0)

    tinv = (r2 == c2).astype(F32) + jnp.where(
        (_idiv(r2, 2) == _idiv(c2, 2)) & (_imod(tt, 2) == 1) & (_imod(jj, 2) == 0), mab, 0.0)
    msz = 2
    while msz < cs:
        off = ((_idiv(r2, 2 * msz) == _idiv(c2, 2 * msz)) & (_imod(tt, 2 * msz) >= msz)
               & (_imod(jj, 2 * msz) < msz))
        tb = tinv.astype(BF16)
        tno = _dot(tb, jnp.where(off, mab, 0.0).astype(BF16)).astype(BF16)
        yield
        tinv = tinv + _dot(tno, tb)
        yield
        msz *= 2

    s_old = st[gi]
    x = _nt(lhs, s_old.astype(BF16))
    yield
    vbd = bd(vm)
    u = _dot(tinv.astype(BF16), (x[:nr] + _dot(mak.astype(BF16), vbd)).astype(BF16))
    yield
    ub = u.astype(BF16)
    yb = x[nr:] + _dot(nrb.astype(BF16), ub) + _dot(nrk.astype(BF16), vbd)
    yield
    s_new = s_old * wt[cs - 1:cs] + _tn(jnp.concatenate([ub, vbd], axis=0),
                                        jnp.concatenate([bd(b * wend), bd(kp * wend)], axis=0))
    st[gi] = s_new
    out[gi] = s_new
    yield

    y = sum(yb[i * cs:(i + 1) * cs] for i in range(RWKV_HPB))
    inv = 1.0 / HEAD_DIM
    yc = y - segsum(y) * inv
    yield
    y = yc * lax.rsqrt(segsum(yc * yc) * inv + RWKV_LN_EPS) * ln_g + ln_b
    y = (y + segsum(r * kp * r_k) * v) * _silu(g)
    y_ref[:, sl] = jnp.where(valid, y, 0.0).astype(BF16)
    yield


def rwkv_scan(rkvg, lora1, lora_b, par, s0, *, bsz, tp, lo, hi):
    _, m, e = rkvg.shape
    cs = RWKV_CHUNK
    nch = tp // cs
    heads = e // HEAD_DIM
    ngrp = RWKV_GROUPS_PER_STEP
    hb = RWKV_HPB * ngrp
    wb = hb * HEAD_DIM
    row = lambda b, h, c: b * nch + c
    kern = functools.partial(_rwkv_kernel, lo=lo, hi=hi, nchunks=nch, ngrp=ngrp)
    proj_spec = lambda s: pl.BlockSpec((1, cs, wb), lambda b, h, c: (s, row(b, h, c), h))
    return pl.pallas_call(
        kern,
        grid=(bsz, heads // hb, nch),
        in_specs=[
            proj_spec(0), proj_spec(1), proj_spec(2), proj_spec(3),
            pl.BlockSpec((1, cs, RWKV_LORA_PAD), lambda b, h, c: (0, row(b, h, c), 0)),
            pl.BlockSpec((1, cs, RWKV_LORA_PAD), lambda b, h, c: (1, row(b, h, c), 0)),
            pl.BlockSpec((1, RWKV_LORA_PAD, wb), lambda b, h, c: (0, 0, h)),
            pl.BlockSpec((1, RWKV_LORA_PAD, wb), lambda b, h, c: (1, 0, h)),
            pl.BlockSpec((8, wb), lambda b, h, c: (0, h)),
            pl.BlockSpec((1, hb, HEAD_DIM, HEAD_DIM), lambda b, h, c: (b, h, 0, 0)),
        ],
        out_specs=[
            pl.BlockSpec((cs, wb), lambda b, h, c: (row(b, h, c), h)),
            pl.BlockSpec((1, hb, HEAD_DIM, HEAD_DIM), lambda b, h, c: (b, h, 0, 0)),
        ],
        out_shape=[
            jax.ShapeDtypeStruct((m, e), BF16),
            jax.ShapeDtypeStruct((bsz, heads, HEAD_DIM, HEAD_DIM), F32),
        ],
        scratch_shapes=[pltpu.VMEM((ngrp, RWKV_HPB * HEAD_DIM, RWKV_HPB * HEAD_DIM), F32)],
        compiler_params=_cparams(("parallel", "parallel", "arbitrary")),
        name="rwkv_scan",
    )(rkvg, rkvg, rkvg, rkvg, lora1, lora1, lora_b, lora_b, par, s0)


def _rwkv_short_kernel(r_ref, k_ref, v_ref, g_ref, lw_ref, la_ref, bw_ref, ba_ref, par_ref, s0_ref,
                       ones_ref, tile_ref, y_ref, sout_ref, *, nvalid, rows_out):
    ct = 8
    hp = RWKV_SHORT_HEADS
    wl = hp * HEAD_DIM
    nr = hp * ct
    ones_bd = ones_ref[...]
    tile_t = tile_ref[...]

    def segsum(x):
        return jnp.concatenate(
            [_dot_split_l(x[:, 256 * j:256 * (j + 1)], ones_bd, terms=2) for j in range(wl // 256)], axis=1)

    par = par_ref[...]
    w0, a0, k_k, k_a, r_k, ln_g, ln_b = (par[i:i + 1] for i in range(7))
    r, k, v, g = r_ref[0], k_ref[0], v_ref[0], g_ref[0]
    w_raw = w0 + _dot(jnp.tanh(lw_ref[0]).astype(BF16), bw_ref[0])
    a = jax.nn.sigmoid(a0 + _dot(la_ref[0].astype(BF16), ba_ref[0]))

    ti = lax.broadcasted_iota(jnp.int32, (ct, 1), 0)
    valid = ti < nvalid
    lw = jnp.where(valid, -jnp.exp(-_softplus(-w_raw) - 0.5), 0.0)
    kk = k * k_k
    kk = jnp.where(valid, kk / jnp.maximum(jnp.sqrt(segsum(kk * kk)), 1e-12), 0.0)
    kp = jnp.where(valid, k * (1.0 + (a - 1.0) * k_a), 0.0)
    vm = jnp.where(valid, v, 0.0)

    cw = lw
    for s in (1, 2, 4):
        cw = cw + jnp.where(ti >= s, pltpu.roll(cw, s, 0), 0.0)
    cwl = cw[ct - 1:ct]
    wend = jnp.exp(cwl - cw)
    wi = jnp.exp(-cw)
    b = kk * a
    at = -kk * jnp.exp(cw - lw)
    rt = r * jnp.exp(cw)

    rr = lax.broadcasted_iota(jnp.int32, (nr, wl), 0)
    cc = lax.broadcasted_iota(jnp.int32, (nr, wl), 1)
    blk = _idiv(rr, ct) == _idiv(cc, HEAD_DIM)

    def bd(x):
        return jnp.where(blk, jnp.concatenate([x] * hp, axis=0), 0.0).astype(BF16)

    lhs = jnp.concatenate([bd(at), bd(rt)], axis=0)
    rhs = jnp.concatenate([bd(b * wi), bd(kp * wi)], axis=0)
    sc = _nt(lhs, rhs)
    ri = lax.broadcasted_iota(jnp.int32, (2 * nr, 2 * nr), 0)
    ci = lax.broadcasted_iota(jnp.int32, (2 * nr, 2 * nr), 1)
    same = _idiv(_imod(ri, nr), ct) == _idiv(_imod(ci, nr), ct)
    tt = _imod(ri, ct)
    jj = _imod(ci, ct)
    sc = jnp.where(same & (tt + (ri >= nr).astype(jnp.int32) > jj), sc, 0.0)
    mab, mak, nrb, nrk = sc[:nr, :nr], sc[:nr, nr:], sc[nr:, :nr], sc[nr:, nr:]

    r1 = lax.broadcasted_iota(jnp.int32, (nr, nr), 0)
    c1 = lax.broadcasted_iota(jnp.int32, (nr, nr), 1)
    t1 = _imod(r1, ct)
    j1 = _imod(c1, ct)
    tinv = (r1 == c1).astype(F32) + jnp.where(
        (_idiv(r1, 2) == _idiv(c1, 2)) & (_imod(t1, 2) == 1) & (_imod(j1, 2) == 0), mab, 0.0)
    msz = 2
    while msz < nvalid:
        off = ((_idiv(r1, 2 * msz) == _idiv(c1, 2 * msz)) & (_imod(t1, 2 * msz) >= msz)
               & (_imod(j1, 2 * msz) < msz))
        tb = tinv.astype(BF16)
        tinv = tinv + _dot(_dot(tb, jnp.where(off, mab, 0.0).astype(BF16)).astype(BF16), tb)
        msz *= 2

    s_old = s0_ref[0].reshape(wl, HEAD_DIM)
    lhs_rows = _dot(lhs, tile_t).astype(BF16)
    x = _nt(lhs_rows, s_old.astype(BF16))
    xa = jnp.where(blk, x[:nr], 0.0)
    xr = jnp.where(blk, x[nr:], 0.0)
    vbd = bd(vm)
    u = _dot(tinv.astype(BF16), (xa + _dot(mak.astype(BF16), vbd)).astype(BF16))
    uv = jnp.concatenate([u.astype(BF16), vbd], axis=0)
    yb = xr + _dot(jnp.concatenate([nrb, nrk], axis=1).astype(BF16), uv)
    bk_rows = _dot(jnp.concatenate([bd(b * wend), bd(kp * wend)], axis=0), tile_t).astype(BF16)
    ds = _tn(uv, bk_rows)

    r16 = lax.broadcasted_iota(jnp.int32, (hp, wl), 0)
    c16 = lax.broadcasted_iota(jnp.int32, (hp, wl), 1)
    wc = jnp.exp(cwl)
    wc_rows = _dot_split_l(jnp.where(r16 == _idiv(c16, HEAD_DIM), wc, 0.0), tile_t)
    rsel = lax.broadcasted_iota(jnp.int32, (hp, HEAD_DIM), 0)
    for h in range(hp):
        rs = slice(h * HEAD_DIM, (h + 1) * HEAD_DIM)
        wc_h = jnp.sum(jnp.where(rsel == h, wc_rows, 0.0), axis=0, keepdims=True)
        sout_ref[0, h] = s_old[rs] * wc_h + ds[rs]

    y = yb[0:ct]
    for h in range(1, hp):
        y = y + yb[h * ct:(h + 1) * ct]
    inv = 1.0 / HEAD_DIM
    yc = y - segsum(y) * inv
    y = yc * lax.rsqrt(segsum(yc * yc) * inv + RWKV_LN_EPS) * ln_g + ln_b
    y = (y + segsum(r * kp * r_k) * v) * _silu(g)
    y = jnp.where(valid, y, 0.0)
    y_ref[...] = _pad_rows(y, rows_out).astype(y_ref.dtype)


def rwkv_short(rkvg, lora1, lora_b, par, s0, *, bsz, tp, hi):
    _, m, e = rkvg.shape
    heads = e // HEAD_DIM
    hp = RWKV_SHORT_HEADS
    wl = hp * HEAD_DIM
    rb = tp // 8
    w4 = RWKV_HPB * HEAD_DIM
    ones_bd = (jnp.arange(w4)[:, None] // HEAD_DIM == jnp.arange(w4)[None, :] // HEAD_DIM).astype(BF16)
    tile_t = (jnp.arange(wl)[:, None] % HEAD_DIM == jnp.arange(HEAD_DIM)[None, :]).astype(BF16)
    kern = functools.partial(_rwkv_short_kernel, nvalid=hi, rows_out=tp)
    proj_spec = lambda s: pl.BlockSpec((1, 8, wl), lambda b, h: (s, b * rb, h))
    return pl.pallas_call(
        kern,
        grid=(bsz, heads // hp),
        in_specs=[
            proj_spec(0), proj_spec(1), proj_spec(2), proj_spec(3),
            pl.BlockSpec((1, 8, RWKV_LORA_PAD), lambda b, h: (0, b * rb, 0)),
            pl.BlockSpec((1, 8, RWKV_LORA_PAD), lambda b, h: (1, b * rb, 0)),
            pl.BlockSpec((1, RWKV_LORA_PAD, wl), lambda b, h: (0, 0, h)),
            pl.BlockSpec((1, RWKV_LORA_PAD, wl), lambda b, h: (1, 0, h)),
            pl.BlockSpec((8, wl), lambda b, h: (0, h)),
            pl.BlockSpec((1, hp, HEAD_DIM, HEAD_DIM), lambda b, h: (b, h, 0, 0)),
            pl.BlockSpec((w4, w4), lambda b, h: (0, 0)),
            pl.BlockSpec((wl, HEAD_DIM), lambda b, h: (0, 0)),
        ],
        out_specs=[
            pl.BlockSpec((tp, wl), lambda b, h: (b, h)),
            pl.BlockSpec((1, hp, HEAD_DIM, HEAD_DIM), lambda b, h: (b, h, 0, 0)),
        ],
        out_shape=[
            jax.ShapeDtypeStruct((m, e), rkvg.dtype),
            jax.ShapeDtypeStruct((bsz, heads, HEAD_DIM, HEAD_DIM), F32),
        ],
        compiler_params=_cparams(("parallel", "parallel")),
        name="rwkv_short",
    )(rkvg, rkvg, rkvg, rkvg, lora1, lora1, lora_b, lora_b, par, s0, ones_bd, tile_t)


def _rwkv_lanes_kernel(r_ref, k_ref, v_ref, g_ref, lw_ref, la_ref, bw_ref, ba_ref, par_ref, s0_ref,
                       y_ref, sout_ref, tk, tw, tb, tq, tr, tv, ty, *, nt, nb):
    hp = RWKV_LANE_HEADS
    wl = hp * HEAD_DIM
    ri = lax.broadcasted_iota(jnp.int32, (wl, wl), 0)
    ci = lax.broadcasted_iota(jnp.int32, (wl, wl), 1)
    ones_bd = (_idiv(ri, HEAD_DIM) == _idiv(ci, HEAD_DIM)).astype(BF16)

    def segsum(x):
        return _dot_split_l(x, ones_bd, terms=2)

    par = par_ref[...]
    w0, a0, k_k, k_a, r_k, ln_g, ln_b = (par[i:i + 1] for i in range(7))
    r, k, v, g = (x[0].astype(F32) for x in (r_ref, k_ref, v_ref, g_ref))
    w_raw = w0 + _dot(jnp.tanh(lw_ref[0]).astype(BF16), bw_ref[0])
    a = jax.nn.sigmoid(a0 + _dot(la_ref[0].astype(BF16), ba_ref[0]))
    decay = jnp.exp(-jnp.exp(-_softplus(-w_raw) - 0.5))
    kk = k * k_k
    kk = kk / jnp.maximum(jnp.sqrt(segsum(kk * kk)), 1e-12)
    kp = k * (1.0 + (a - 1.0) * k_a)
    bb = kk * a

    for t in range(nt):
        rows = slice(t * nb, (t + 1) * nb)
        tk[t] = (-kk[rows]).T
        tw[t] = decay[rows].T
        tb[t] = bb[rows].T
        tq[t] = kp[rows].T
        tr[t] = r[rows].T
        tv[t] = v[rows].T

    for hh in range(hp):
        ks = slice(hh * HEAD_DIM, (hh + 1) * HEAD_DIM)

        def body(vi, carry, hh=hh, ks=ks):
            sv = s0_ref[hh, vi]
            row = hh * HEAD_DIM + vi
            for t in range(nt):
                sa = jnp.sum(sv * tk[t, ks, :], axis=0, keepdims=True)
                sv = sv * tw[t, ks, :] + sa * tb[t, ks, :] + tv[t, pl.ds(row, 1), :] * tq[t, ks, :]
                ty[t, pl.ds(row, 1), :] = jnp.sum(sv * tr[t, ks, :], axis=0, keepdims=True)
            sout_ref[hh, vi] = sv
            return carry

        lax.fori_loop(0, HEAD_DIM, body, 0, unroll=4)

    inv = 1.0 / HEAD_DIM
    for t in range(nt):
        rows = slice(t * nb, (t + 1) * nb)
        y = ty[t].T
        yc = y - segsum(y) * inv
        y = yc * lax.rsqrt(segsum(yc * yc) * inv + RWKV_LN_EPS) * ln_g + ln_b
        y = (y + segsum(r[rows] * kp[rows] * r_k) * v[rows]) * _silu(g[rows])
        y_ref[rows, :] = y.astype(BF16)


def rwkv_lanes(rkvg, lora1, lora_b, par, s0t, *, nt, nb):
    _, m, e = rkvg.shape
    heads = e // HEAD_DIM
    hp = RWKV_LANE_HEADS
    wl = hp * HEAD_DIM
    kern = functools.partial(_rwkv_lanes_kernel, nt=nt, nb=nb)
    proj_spec = lambda s: pl.BlockSpec((1, m, wl), lambda h: (s, 0, h))
    state_spec = pl.BlockSpec((hp, HEAD_DIM, HEAD_DIM, nb), lambda h: (h, 0, 0, 0))
    tile = pltpu.VMEM((nt, wl, nb), F32)
    return pl.pallas_call(
        kern,
        grid=(heads // hp,),
        in_specs=[
            proj_spec(0), proj_spec(1), proj_spec(2), proj_spec(3),
            pl.BlockSpec((1, m, RWKV_LORA_PAD), lambda h: (0, 0, 0)),
            pl.BlockSpec((1, m, RWKV_LORA_PAD), lambda h: (1, 0, 0)),
            pl.BlockSpec((1, RWKV_LORA_PAD, wl), lambda h: (0, 0, h)),
            pl.BlockSpec((1, RWKV_LORA_PAD, wl), lambda h: (1, 0, h)),
            pl.BlockSpec((8, wl), lambda h: (0, h)),
            state_spec,
        ],
        out_specs=[pl.BlockSpec((m, wl), lambda h: (0, h)), state_spec],
        out_shape=[jax.ShapeDtypeStruct((m, e), BF16), jax.ShapeDtypeStruct(s0t.shape, F32)],
        scratch_shapes=[tile] * 7,
        compiler_params=_cparams(("parallel",)),
        name="rwkv_lanes",
    )(rkvg, rkvg, rkvg, rkvg, lora1, lora1, lora_b, lora_b, par, s0t)


def _prep_weights(norm_g, ssd_w_in, ssd_dt_bias, ssd_a_log, ssd_d, ssd_w_out,
                  rwkv_w_rkvg, rwkv_w_lora_a, rwkv_w_lora_b, rwkv_a_lora_a, rwkv_a_lora_b,
                  rwkv_w0, rwkv_a0, rwkv_k_k, rwkv_k_a, rwkv_r_k, rwkv_ln_g, rwkv_ln_b, rwkv_w_out,
                  ret_w_in, ret_w_out):
    d_inner = ssd_w_out.shape[1]
    n_main = d_inner + d_inner + 2 * SSD_GROUPS * SSD_STATE
    ns = ssd_w_in.shape[0]
    d_model = ssd_w_in.shape[1]

    def head_lanes(p):
        p = p.reshape(ns, SSD_GROUPS, 1, SSD_HPG)
        return jnp.pad(p, ((0, 0), (0, 0), (0, 0), (0, 128 - SSD_HPG)))

    w_dt = ssd_w_in[:, :, n_main:].reshape(ns, d_model, SSD_GROUPS, SSD_HPG)
    w_dt = jnp.pad(w_dt, ((0, 0), (0, 0), (0, 0), (0, 128 - SSD_HPG))).reshape(ns, d_model, SSD_GROUPS * 128)
    rank = rwkv_w_lora_a.shape[2]
    lora_a = jnp.stack([rwkv_w_lora_a, rwkv_a_lora_a], axis=1)
    lora_a = jnp.pad(lora_a, ((0, 0), (0, 0), (0, 0), (0, RWKV_LORA_PAD - rank)))
    lora_b = jnp.stack([rwkv_w_lora_b, rwkv_a_lora_b], axis=1)
    lora_b = jnp.pad(lora_b, ((0, 0), (0, 0), (0, RWKV_LORA_PAD - rank), (0, 0)))
    nr = rwkv_w0.shape[0]
    par = jnp.stack([rwkv_w0, rwkv_a0, rwkv_k_k, rwkv_k_a, rwkv_r_k.reshape(nr, -1), rwkv_ln_g, rwkv_ln_b,
                     jnp.zeros_like(rwkv_w0)], axis=1)
    return dict(
        ssd_w_main=ssd_w_in[:, :, :n_main].astype(BF16), ssd_w_dt=w_dt.astype(BF16),
        ssd_dtb=head_lanes(ssd_dt_bias), ssd_alog=head_lanes(ssd_a_log), ssd_dskip=head_lanes(ssd_d),
        ssd_w_out=ssd_w_out.astype(BF16),
        rwkv_w=rwkv_w_rkvg.astype(BF16), rwkv_lora_a=lora_a.astype(BF16), rwkv_lora_b=lora_b.astype(BF16),
        rwkv_par=par, rwkv_w_out=rwkv_w_out.astype(BF16),
        ret_w_in=ret_w_in.astype(BF16), ret_w_out=ret_w_out.astype(BF16),
    )


def _trunk(h, conv_st, ssd_st, shift_st, wkv_st, ret_st, pos, *, bsz, tp, lq, lo, hi, depth,
           norm_g, final_norm_g, wts, ssd_conv_w, ssd_conv_b, ssd_norm_g, rwkv_mu, ret_norm_g):
    d_model = h.shape[1]
    geo = dict(bsz=bsz, tp=tp, lo=lo, hi=hi)
    new_conv, new_shift, new_wkv, new_ret = [], [], [], []
    new_ssd = None
    act = BF16 if lq % 16 == 0 else F32

    half = RET_QK // 2
    inv_freq = 1.0 / (RET_THETA_BASE ** jnp.linspace(0.0, 1.0, half, dtype=F32))
    ang = pos.astype(F32)[:, None] * inv_freq
    cos, sin = jnp.cos(ang), jnp.sin(ang)
    log_gamma = jnp.log1p(-jnp.exp2(-5.0 - jnp.arange(RET_HEADS, dtype=F32)))
    lg = jnp.broadcast_to(log_gamma[:, None, None], (RET_HEADS, 8, 128))

    for layer in range(depth):
        kind, j = layer % 3, layer // 3
        u = rmsnorm(h, norm_g[layer], F32 if kind == 1 else BF16)
        if kind == 0:
            proj = matmul(u, wts["ssd_w_main"][j], out_dtype=act)
            dt_raw = matmul(u, wts["ssd_w_dt"][j])
            conv_init = jnp.pad(conv_st[j], ((0, 0), (8 - (SSD_CONV - 1), 0), (0, 0)))
            y, new_ssd = ssd_scan(proj, dt_raw, conv_init, ssd_st, j, new_ssd, ssd_conv_w[j], ssd_conv_b[j],
                                  wts["ssd_dtb"][j], wts["ssd_alog"][j], wts["ssd_dskip"][j], ssd_norm_g[j],
                                  lq=lq, **geo)
            nk = SSD_CONV - 1
            last = u.reshape(bsz, tp, d_model)[:, hi - nk:hi].reshape(bsz * nk, d_model)
            last = jnp.pad(last, ((0, -(bsz * nk) % 16), (0, 0)))
            xbc = matmul(last, wts["ssd_w_main"][j])[:bsz * nk, y.shape[1]:]
            new_conv.append(xbc.reshape(bsz, nk, -1))
            h = matmul(y, wts["ssd_w_out"][j], res=h)
        elif kind == 1 and lo == 0 and hi <= 8 and bsz % LANES == 0:
            tmajor = lambda x: jnp.swapaxes(x.reshape(bsz, tp, -1)[:, :hi], 0, 1)
            uc = tmajor(u)
            prev = jnp.concatenate([shift_st[j][None], uc[:-1]], axis=0).reshape(hi * bsz, d_model)
            uc = uc.reshape(hi * bsz, d_model)
            rkvg = mix_matmul(uc, prev, rwkv_mu[j][:4], wts["rwkv_w"][j], out_dtype=BF16)
            lora1 = mix_matmul(uc, prev, rwkv_mu[j][4:], wts["rwkv_lora_a"][j])
            y, s_t = rwkv_lanes(rkvg, lora1, wts["rwkv_lora_b"][j], wts["rwkv_par"][j],
                                jnp.transpose(wkv_st[j], (1, 2, 3, 0)), nt=hi, nb=bsz)
            new_shift.append(u.reshape(bsz, tp, d_model)[:, hi - 1])
            new_wkv.append(jnp.transpose(s_t, (3, 0, 1, 2)))
            hc = matmul(y, wts["rwkv_w_out"][j], res=tmajor(h).reshape(hi * bsz, d_model))
            hc = jnp.swapaxes(hc.reshape(hi, bsz, d_model), 0, 1)
            h = jnp.pad(hc, ((0, 0), (0, tp - hi), (0, 0))).reshape(bsz * tp, d_model)
        elif kind == 1:
            u3 = u.reshape(bsz, tp, d_model)
            prev = jnp.concatenate([shift_st[j][:, None, :], u3[:, :-1]], axis=1).reshape(bsz * tp, d_model)
            rkvg = mix_matmul(u, prev, rwkv_mu[j][:4], wts["rwkv_w"][j], out_dtype=BF16)
            lora1 = mix_matmul(u, prev, rwkv_mu[j][4:], wts["rwkv_lora_a"][j])
            if lo == 0 and hi <= 8 and tp <= 16:
                y, s_new = rwkv_short(rkvg.astype(F32), lora1, wts["rwkv_lora_b"][j], wts["rwkv_par"][j], wkv_st[j],
                                      bsz=bsz, tp=tp, hi=hi)
            else:
                y, s_new = rwkv_scan(rkvg, lora1, wts["rwkv_lora_b"][j], wts["rwkv_par"][j], wkv_st[j], **geo)
            new_shift.append(u3[:, hi - 1])
            new_wkv.append(s_new)
            h = matmul(y, wts["rwkv_w_out"][j], res=h)
        else:
            proj = matmul(u, wts["ret_w_in"][j], out_dtype=act)
            y, s_new = ret_scan(proj, cos, sin, lg, ret_st[j], ret_norm_g[j], lq=lq, **geo)
            new_ret.append(s_new)
            h = matmul(y, wts["ret_w_out"][j], res=h)
    y = rmsnorm(h, final_norm_g)
    return (y, jnp.stack(new_conv), new_ssd, jnp.stack(new_shift), jnp.stack(new_wkv), jnp.stack(new_ret))


def kernel(x_prompt, x_sample, state_ssd_conv, state_ssd, state_rwkv_shift, state_rwkv_wkv, state_ret, meta_tokens, norm_g, final_norm_g, ssd_w_in, ssd_conv_w, ssd_conv_b, ssd_dt_bias, ssd_a_log, ssd_d, ssd_norm_g, ssd_w_out, rwkv_mu, rwkv_w_rkvg, rwkv_w0, rwkv_w_lora_a, rwkv_w_lora_b, rwkv_a0, rwkv_a_lora_a, rwkv_a_lora_b, rwkv_k_k, rwkv_k_a, rwkv_r_k, rwkv_ln_g, rwkv_ln_b, rwkv_w_out, ret_w_in, ret_norm_g, ret_w_out):
    depth = norm_g.shape[0]
    d_model = x_prompt.shape[2]
    wts = _prep_weights(norm_g, ssd_w_in, ssd_dt_bias, ssd_a_log, ssd_d, ssd_w_out,
                        rwkv_w_rkvg, rwkv_w_lora_a, rwkv_w_lora_b, rwkv_a_lora_a, rwkv_a_lora_b,
                        rwkv_w0, rwkv_a0, rwkv_k_k, rwkv_k_a, rwkv_r_k, rwkv_ln_g, rwkv_ln_b, rwkv_w_out,
                        ret_w_in, ret_w_out)
    common = dict(depth=depth, norm_g=norm_g, final_norm_g=final_norm_g, wts=wts, ssd_conv_w=ssd_conv_w,
                  ssd_conv_b=ssd_conv_b, ssd_norm_g=ssd_norm_g, rwkv_mu=rwkv_mu, ret_norm_g=ret_norm_g)

    bp, seq, _ = x_prompt.shape
    lq_p = 128
    lo_p = lq_p - N_META
    tp_p = lo_p + N_META + seq
    h_p = jnp.concatenate([jnp.zeros((bp, lo_p, d_model), F32),
                           jnp.broadcast_to(meta_tokens[None], (bp, N_META, d_model)), x_prompt], axis=1)
    zeros_like_b = lambda s: jnp.zeros((s.shape[0], bp) + s.shape[2:], F32)
    pos_p = jnp.maximum(jnp.arange(tp_p) - lo_p, 0)
    outs_p = _trunk(h_p.reshape(bp * tp_p, d_model), zeros_like_b(state_ssd_conv), zeros_like_b(state_ssd),
                    zeros_like_b(state_rwkv_shift), zeros_like_b(state_rwkv_wkv), zeros_like_b(state_ret), pos_p,
                    bsz=bp, tp=tp_p, lq=lq_p, lo=lo_p, hi=tp_p, **common)
    y_prompt = outs_p[0].reshape(bp, tp_p, d_model)[:, lo_p + N_META:]

    bs, ds, _ = x_sample.shape
    tp_s = 8
    h_s = jnp.concatenate([x_sample, jnp.zeros((bs, tp_s - ds, d_model), F32)], axis=1)
    pos_s = PAST_LEN + jnp.arange(tp_s)
    outs_s = _trunk(h_s.reshape(bs * tp_s, d_model), state_ssd_conv, state_ssd, state_rwkv_shift, state_rwkv_wkv,
                    state_ret, pos_s, bsz=bs, tp=tp_s, lq=tp_s, lo=0, hi=ds, **common)
    y_sample = outs_s[0].reshape(bs, tp_s, d_model)[:, :ds]

    return (y_prompt, y_sample) + tuple(outs_p[1:]) + tuple(outs_s[1:])
```

```python
import functools
import math

import jax
import jax.numpy as jnp
from jax import lax
from jax.experimental import pallas as pl
from jax.experimental.pallas import tpu as pltpu

F32 = jnp.float32
BF16 = jnp.bfloat16

EPS = 1e-6
N_META = 16
HEAD_DIM = 64
SSD_STATE = 128
SSD_GROUPS = 8
SSD_HPG = 8
SSD_CONV = 4
RET_HEADS = 8
RET_QK = 256
RET_V = 512
RET_THETA_BASE = 10000.0
RWKV_LORA_PAD = 128
RWKV_CHUNK = 64
RWKV_HPB = 4
RWKV_GROUPS_PER_STEP = 8
RWKV_SHORT_HEADS = 16
RWKV_LANE_HEADS = 2
LANES = 128
MIN_CHUNK_ROWS = 16
PAST_LEN = 16384
RWKV_LN_EPS = 1e-5 * HEAD_DIM
NEG = -1e30
ROW_TILE = 512
MATMUL_VMEM_BUDGET = 40 * 1024 * 1024
VMEM_LIMIT = 56 * 1024 * 1024


def _cparams(sem):
    return pltpu.CompilerParams(dimension_semantics=sem, vmem_limit_bytes=VMEM_LIMIT)


def _nt(a, b):
    return lax.dot_general(a, b, (((1,), (1,)), ((), ())), preferred_element_type=F32)


def _tn(a, b):
    return lax.dot_general(a, b, (((0,), (0,)), ((), ())), preferred_element_type=F32)


def _dot(a, b):
    return jnp.dot(a, b, preferred_element_type=F32)


def _split(x, terms):
    parts = []
    r = x
    for i in range(terms):
        p = r.astype(BF16)
        parts.append(p)
        if i + 1 < terms:
            r = r - p.astype(F32)
    return parts


def _dot_split_l(x, m, terms=3):
    acc = None
    for p in _split(x, terms):
        d = _dot(p, m)
        acc = d if acc is None else acc + d
    return acc


def _dot_split_r(m, x, terms=3):
    acc = None
    for p in _split(x, terms):
        d = _dot(m, p)
        acc = d if acc is None else acc + d
    return acc


def _pad_rows(x, rows):
    if x.shape[0] == rows:
        return x
    return jnp.concatenate([x, jnp.zeros((rows - x.shape[0], x.shape[1]), x.dtype)], axis=0)


def _idiv(x, n):
    return jnp.right_shift(x, int(math.log2(n)))


def _imod(x, n):
    return jnp.bitwise_and(x, n - 1)


def _silu(x):
    h = 0.5 * x
    return h + h * jnp.tanh(h)


def _softplus(x):
    return jnp.maximum(x, 0.0) + jnp.log(1.0 + jnp.exp(-jnp.abs(x)))


def _rmsnorm_kernel(x_ref, g_ref, o_ref):
    x = x_ref[...]
    ms = jnp.mean(x * x, axis=-1, keepdims=True)
    o_ref[...] = (x * lax.rsqrt(ms + EPS) * g_ref[...]).astype(o_ref.dtype)


def _row_tile(m):
    tm = math.gcd(m, ROW_TILE)
    assert tm % 16 == 0, m
    return tm


def _matmul_tiles(m, k, n, *, a_bytes, n_a, cast, n_out):
    best = None
    for tn in (t for t in (1024, 512, 256, 128) if n % t == 0):
        for tm in (t for t in range(16, m + 1, 16) if m % t == 0):
            need = (2 * n_a * tm * k * a_bytes + (tm * k * 2 if cast else 0)
                    + 2 * k * tn * 2 + 2 * n_out * tm * tn * 4)
            if need <= MATMUL_VMEM_BUDGET and (best is None or (tm * tn, tm) > (best[0] * best[1], best[0])):
                best = (tm, tn)
    assert best is not None, (m, k, n)
    return best


def rmsnorm(x, g, out_dtype=F32):
    m, d = x.shape
    tm = _row_tile(m)
    return pl.pallas_call(
        _rmsnorm_kernel,
        grid=(m // tm,),
        in_specs=[pl.BlockSpec((tm, d), lambda i: (i, 0)), pl.BlockSpec((1, d), lambda i: (0, 0))],
        out_specs=pl.BlockSpec((tm, d), lambda i: (i, 0)),
        out_shape=jax.ShapeDtypeStruct((m, d), out_dtype),
        compiler_params=_cparams(("parallel",)),
        name="rmsnorm",
    )(x, g.reshape(1, d))


def _mm_kernel(a_ref, w_ref, *rest, has_res, cast):
    rest = list(rest)
    abf_ref = rest.pop() if cast else a_ref
    o_ref = rest.pop()

    if cast:
        @pl.when(pl.program_id(1) == 0)
        def _():
            abf_ref[...] = a_ref[...].astype(BF16)

    acc = _dot(abf_ref[...], w_ref[...])
    if has_res:
        acc = rest[0][...] + acc
    o_ref[...] = acc.astype(o_ref.dtype)


def matmul(a, w, res=None, out_dtype=F32):
    m, k = a.shape
    n = w.shape[1]
    cast = a.dtype != BF16
    tm, tn = _matmul_tiles(m, k, n, a_bytes=a.dtype.itemsize, n_a=1, cast=cast, n_out=2 if res is not None else 1)
    in_specs = [pl.BlockSpec((tm, k), lambda i, j: (i, 0)), pl.BlockSpec((k, tn), lambda i, j: (0, j))]
    args = [a, w]
    if res is not None:
        in_specs.append(pl.BlockSpec((tm, tn), lambda i, j: (i, j)))
        args.append(res)
    return pl.pallas_call(
        functools.partial(_mm_kernel, has_res=res is not None, cast=cast),
        grid=(m // tm, n // tn),
        in_specs=in_specs,
        out_specs=pl.BlockSpec((tm, tn), lambda i, j: (i, j)),
        out_shape=jax.ShapeDtypeStruct((m, n), out_dtype),
        scratch_shapes=[pltpu.VMEM((tm, k), BF16)] if cast else [],
        compiler_params=_cparams(("parallel", "arbitrary")),
        name="matmul_res" if res is not None else "matmul",
    )(*args)


def _mixmm_kernel(u_ref, p_ref, mu_ref, w_ref, o_ref, xm_ref):
    @pl.when(pl.program_id(2) == 0)
    def _():
        u = u_ref[...]
        xm_ref[...] = (u + (p_ref[...] - u) * mu_ref[0]).astype(BF16)

    o_ref[0] = _dot(xm_ref[...], w_ref[0]).astype(o_ref.dtype)


def mix_matmul(u, prev, mu, w, out_dtype=F32):
    m, k = u.shape
    s, _, n = w.shape
    tm, tn = _matmul_tiles(m, k, n, a_bytes=4, n_a=2, cast=True, n_out=1)
    return pl.pallas_call(
        _mixmm_kernel,
        grid=(m // tm, s, n // tn),
        in_specs=[
            pl.BlockSpec((tm, k), lambda i, si, j: (i, 0)),
            pl.BlockSpec((tm, k), lambda i, si, j: (i, 0)),
            pl.BlockSpec((1, 1, k), lambda i, si, j: (si, 0, 0)),
            pl.BlockSpec((1, k, tn), lambda i, si, j: (si, 0, j)),
        ],
        out_specs=pl.BlockSpec((1, tm, tn), lambda i, si, j: (si, i, j)),
        out_shape=jax.ShapeDtypeStruct((s, m, n), out_dtype),
        scratch_shapes=[pltpu.VMEM((tm, k), BF16)],
        compiler_params=_cparams(("parallel", "arbitrary", "arbitrary")),
        name="mix_matmul",
    )(u, prev, mu.reshape(s, 1, k), w)


def _conv_silu(cur, car_ref, cols, w, b, lq, lb):
    car_ref[8:8 + lq, cols] = cur
    acc = b + cur * w[SSD_CONV - 1:SSD_CONV]
    for s in range(1, SSD_CONV):
        acc = acc + car_ref[8 - s:8 - s + lq, cols] * w[SSD_CONV - 1 - s:SSD_CONV - s]
    car_ref[0:8, cols] = cur[lb - 8:lb]
    return _silu(acc)


def _ssd_kernel(z_ref, x_ref, b_ref, c_ref, dt_ref, ci_ref, s0_ref, cw_ref, cb_ref, dtb_ref, alog_ref,
                d_ref, ng_ref, *rest, lq, lo, hi, nchunks, out_layer):
    y_ref, sout_ref, car, st = rest[-4:]
    c = pl.program_id(1)
    gw = SSD_HPG * HEAD_DIM

    @pl.when(c == 0)
    def _():
        car[0:8, :] = ci_ref[0]
        st[...] = s0_ref[0, 0].reshape(SSD_GROUPS * gw, SSD_STATE)

    gens = [_ssd_group(g, z_ref, x_ref, b_ref, c_ref, dt_ref, cw_ref, cb_ref, dtb_ref, alog_ref, d_ref,
                       ng_ref, y_ref, car, st, lb=lq, lo=lo, hi=hi) for g in range(SSD_GROUPS)]
    for _ in zip(*gens):
        pass

    @pl.when(c == nchunks - 1)
    def _():
        for layer in range(sout_ref.shape[0]):
            if layer == out_layer:
                sout_ref[layer, 0] = st[...].reshape(SSD_GROUPS * SSD_HPG, HEAD_DIM, SSD_STATE)
            else:
                sout_ref[layer, 0] = jnp.zeros(sout_ref.shape[2:], F32)


def _ssd_group(g, z_ref, x_ref, b_ref, c_ref, dt_ref, cw_ref, cb_ref, dtb_ref, alog_ref, d_ref, ng_ref,
               y_ref, car, st, *, lb, lo, hi):
    c = pl.program_id(1)
    lq = max(lb, MIN_CHUNK_ROWS)
    gw = SSD_HPG * HEAD_DIM
    d_inner = SSD_GROUPS * gw
    xs = slice(g * gw, (g + 1) * gw)
    ns = slice(g * SSD_STATE, (g + 1) * SSD_STATE)
    bs = slice(d_inner + g * SSD_STATE, d_inner + (g + 1) * SSD_STATE)
    cs_ = slice(d_inner + (SSD_GROUPS + g) * SSD_STATE, d_inner + (SSD_GROUPS + g + 1) * SSD_STATE)

    f32 = lambda x: _pad_rows(x.astype(F32), lq)
    xc = _conv_silu(f32(x_ref[:, xs]), car, xs, cw_ref[:, xs], cb_ref[:, xs], lq, lb)
    bc = _conv_silu(f32(b_ref[:, ns]), car, bs, cw_ref[:, bs], cb_ref[:, bs], lq, lb)
    cc = _conv_silu(f32(c_ref[:, ns]), car, cs_, cw_ref[:, cs_], cb_ref[:, cs_], lq, lb)

    ti = lax.broadcasted_iota(jnp.int32, (lq, 1), 0)
    pos = c * lb + ti
    valid = (ti < lb) & (pos >= lo) & (pos < hi)
    lane = lax.broadcasted_iota(jnp.int32, (lq, 128), 1)
    dt = _softplus(f32(dt_ref[:, ns]) + dtb_ref[:, ns])
    dt = jnp.where(valid & (lane < SSD_HPG), dt, 0.0)
    la = dt * (-jnp.exp(alog_ref[:, ns]))
    yield

    ri = lax.broadcasted_iota(jnp.int32, (lq, lq), 0)
    ci = lax.broadcasted_iota(jnp.int32, (lq, lq), 1)
    tril = (ri >= ci).astype(BF16)
    triu = (ri <= ci).astype(BF16)
    acum = _dot_split_r(tril, la)
    acum_t = sum(_tn(part, triu) for part in _split(la, 3))
    yield
    a_end = acum[lq - 1:lq]
    dec_end = jnp.exp(a_end - acum)
    e_in = jnp.exp(acum)
    cd = jnp.exp(a_end)

    bcp = bc.astype(BF16)
    ccb = cc.astype(BF16)
    g_sc = _nt(ccb, bcp)
    st_old = st[xs, :]
    y_in = _nt(ccb, st_old.astype(BF16))
    yield

    causal = ci <= ri
    lane_q = lax.broadcasted_iota(jnp.int32, (lq, 128), 1) < HEAD_DIM
    lane_k = lane_q
    row_k = lax.broadcasted_iota(jnp.int32, (128, 1), 0) < HEAD_DIM
    dvec = d_ref[:, ns]

    er = lax.broadcasted_iota(jnp.int32, (2 * 128, gw), 0)
    ec = lax.broadcasted_iota(jnp.int32, (2 * 128, gw), 1)
    spread_m = (_imod(er, 128) == _idiv(ec, HEAD_DIM)).astype(BF16)
    spread = lambda f: _dot(jnp.concatenate(_split(f, 2), axis=1), spread_m)
    dt_x = spread(dt)
    e_x = spread(e_in)
    dec_x = spread(dec_end)
    yield

    ys = []
    for p in range(SSD_HPG // 2):
        h0, h1 = 2 * p, 2 * p + 1
        ps = slice(128 * p, 128 * (p + 1))
        xp = xc[:, ps]
        vp = xp * dt_x[:, ps]
        vpp = vp
        yp = y_in[:, ps] * e_x[:, ps]
        yp = yp + xp * jnp.where(lane_q, dvec[:, h0:h0 + 1], dvec[:, h1:h1 + 1])
        for hh, h in ((0, h0), (1, h1)):
            seg = acum[:, h:h + 1] - acum_t[h:h + 1, :]
            lm = jnp.exp(jnp.where(causal, seg, NEG))
            pm = (g_sc * lm).astype(BF16)
            vm = jnp.where(lane_k if hh == 0 else jnp.logical_not(lane_k), vpp, 0.0).astype(BF16)
            yp = yp + _dot(pm, vm)
        ys.append(yp)
        vend = vpp * dec_x[:, ps]
        upd = _tn(vend.astype(BF16), bcp)
        cdp = jnp.where(row_k, cd[:, h0:h0 + 1], cd[:, h1:h1 + 1])
        st[g * gw + 128 * p:g * gw + 128 * (p + 1), :] = st_old[128 * p:128 * (p + 1), :] * cdp + upd
        yield

    y = jnp.concatenate(ys, axis=1)
    y = y * _silu(f32(z_ref[:, xs]))
    ms = jnp.mean(y * y, axis=-1, keepdims=True)
    y = y * lax.rsqrt(ms + EPS) * ng_ref[:, xs]
    y_ref[:, xs] = jnp.where(valid, y, 0.0)[:lb].astype(y_ref.dtype)
    yield


def ssd_scan(proj, dt_raw, conv_init, s0_all, layer, s_buf, conv_w, conv_b, dtb, alog, dskip, norm_g,
             *, bsz, tp, lq, lo, hi):
    m = proj.shape[0]
    heads = SSD_GROUPS * SSD_HPG
    nlayers = s0_all.shape[0]
    state_spec = pl.BlockSpec((1, 1, heads, HEAD_DIM, SSD_STATE), lambda b, c: (layer, b, 0, 0, 0))
    if s_buf is not None:
        out_state_spec, out_layer = state_spec, 0
        extra_specs, extra_args, aliases = [pl.BlockSpec(memory_space=pl.ANY)], [s_buf], {13: 1}
    else:
        out_state_spec = pl.BlockSpec((nlayers, 1, heads, HEAD_DIM, SSD_STATE), lambda b, c: (0, b, 0, 0, 0))
        out_layer = layer
        extra_specs, extra_args, aliases = [], [], {}
    nch = tp // lq
    d_inner = heads * HEAD_DIM
    gn = SSD_GROUPS * SSD_STATE
    conv_dim = d_inner + 2 * gn
    row = lambda b, c: b * nch + c
    const = lambda b, c: (0, 0)
    kern = functools.partial(_ssd_kernel, lq=lq, lo=lo, hi=hi, nchunks=nch, out_layer=out_layer)
    return pl.pallas_call(
        kern,
        grid=(bsz, nch),
        in_specs=[
            pl.BlockSpec((lq, d_inner), lambda b, c: (row(b, c), 0)),
            pl.BlockSpec((lq, d_inner), lambda b, c: (row(b, c), 1)),
            pl.BlockSpec((lq, gn), lambda b, c: (row(b, c), 2 * d_inner // gn)),
            pl.BlockSpec((lq, gn), lambda b, c: (row(b, c), 2 * d_inner // gn + 1)),
            pl.BlockSpec((lq, gn), lambda b, c: (row(b, c), 0)),
            pl.BlockSpec((1, 8, conv_dim), lambda b, c: (b, 0, 0)),
            state_spec,
            pl.BlockSpec((SSD_CONV, conv_dim), const),
            pl.BlockSpec((1, conv_dim), const),
            pl.BlockSpec((1, gn), const),
            pl.BlockSpec((1, gn), const),
            pl.BlockSpec((1, gn), const),
            pl.BlockSpec((1, d_inner), const),
        ] + extra_specs,
        out_specs=[
            pl.BlockSpec((lq, d_inner), lambda b, c: (row(b, c), 0)),
            out_state_spec,
        ],
        out_shape=[
            jax.ShapeDtypeStruct((m, d_inner), proj.dtype),
            jax.ShapeDtypeStruct(s0_all.shape, F32),
        ],
        scratch_shapes=[pltpu.VMEM((8 + max(lq, MIN_CHUNK_ROWS), conv_dim), F32),
                        pltpu.VMEM((d_inner, SSD_STATE), F32)],
        input_output_aliases=aliases,
        compiler_params=_cparams(("parallel", "arbitrary")),
        name="ssd_scan",
    )(proj, proj, proj, proj, dt_raw, conv_init, s0_all, conv_w, conv_b.reshape(1, -1),
      dtb.reshape(1, -1), alog.reshape(1, -1), dskip.reshape(1, -1), norm_g.reshape(1, -1), *extra_args)


def _ret_kernel(q_ref, k_ref, v_ref, g_ref, cos_ref, sin_ref, lg_ref, s0_ref, ng_ref,
                y_ref, sout_ref, *, lq, lo, hi):
    c = pl.program_id(1)

    @pl.when(c == 0)
    def _():
        sout_ref[...] = s0_ref[...]

    gens = [_ret_head(h, q_ref, k_ref, v_ref, g_ref, cos_ref, sin_ref, lg_ref, ng_ref, y_ref, sout_ref,
                      lb=lq, lo=lo, hi=hi) for h in range(RET_HEADS)]
    for _ in zip(*gens):
        pass


def _ret_head(h, q_ref, k_ref, v_ref, g_ref, cos_ref, sin_ref, lg_ref, ng_ref, y_ref, st_ref, *, lb, lo, hi):
    c = pl.program_id(1)
    lq = max(lb, MIN_CHUNK_ROWS)
    f32 = lambda x: _pad_rows(x.astype(F32), lq)
    qs = slice(h * RET_QK, (h + 1) * RET_QK)
    vs = slice(h * RET_V, (h + 1) * RET_V)
    lg = lg_ref[h][0:1, 0:1]
    nv = float(hi - lo)

    def count(p):
        return jnp.clip((p + 1 - lo).astype(F32), 0.0, nv)

    base = c * lb
    ti = lax.broadcasted_iota(jnp.int32, (lq, 1), 0)
    pos_i = base + ti
    valid = (ti < lb) & (pos_i >= lo) & (pos_i < hi)
    cnt_i = count(pos_i)
    cnt_j = count(base + lax.broadcasted_iota(jnp.int32, (1, lq), 1))
    cnt_jc = cnt_i
    cnt0 = count(base - 1 + jnp.zeros((1, 1), jnp.int32))
    cnt_end = count(base + lb - 1 + jnp.zeros((1, 1), jnp.int32))

    cos = f32(cos_ref[...])
    sin = f32(sin_ref[...])
    half = RET_QK // 2

    def rot(x):
        x1, x2 = x[:, :half], x[:, half:]
        return jnp.concatenate([x1 * cos - x2 * sin, x1 * sin + x2 * cos], axis=1)

    qr = rot(f32(q_ref[:, qs])).astype(BF16)
    kr = jnp.where(valid, rot(f32(k_ref[:, qs])) * (RET_QK ** -0.5), 0.0)
    v = jnp.where(valid, f32(v_ref[:, vs]), 0.0)
    krp = kr.astype(BF16)
    vp = v
    yield

    sc = _nt(qr, krp)
    s_old = st_ref[0, h]
    y_in = _nt(qr, s_old.astype(BF16))
    yield
    qi = lax.broadcasted_iota(jnp.int32, (lq, lq), 0)
    kj = lax.broadcasted_iota(jnp.int32, (lq, lq), 1)
    dm = jnp.exp(jnp.where(kj <= qi, lg * (cnt_i - cnt_j), NEG))
    y = _dot((sc * dm).astype(BF16), vp.astype(BF16))
    y = y + y_in * jnp.exp(lg * (cnt_i - cnt0))
    vend = vp * jnp.exp(lg * (cnt_end - cnt_jc))
    st_ref[0, h] = s_old * jnp.exp(lg * (cnt_end - cnt0)) + _tn(vend.astype(BF16), krp)
    yield

    ms = jnp.mean(y * y, axis=-1, keepdims=True)
    y = y * lax.rsqrt(ms + EPS) * ng_ref[:, vs] * _silu(f32(g_ref[:, vs]))
    y_ref[:, vs] = jnp.where(valid, y, 0.0)[:lb].astype(y_ref.dtype)
    yield


def ret_scan(proj, cos, sin, lg, s0, norm_g, *, bsz, tp, lq, lo, hi):
    m = proj.shape[0]
    nch = tp // lq
    d_inner = RET_HEADS * RET_V
    d_qk = RET_HEADS * RET_QK
    row = lambda b, c: b * nch + c
    state_spec = pl.BlockSpec((1, RET_HEADS, RET_V, RET_QK), lambda b, c: (b, 0, 0, 0))
    kern = functools.partial(_ret_kernel, lq=lq, lo=lo, hi=hi)
    return pl.pallas_call(
        kern,
        grid=(bsz, nch),
        in_specs=[
            pl.BlockSpec((lq, d_qk), lambda b, c: (row(b, c), 0)),
            pl.BlockSpec((lq, d_qk), lambda b, c: (row(b, c), 1)),
            pl.BlockSpec((lq, d_inner), lambda b, c: (row(b, c), 2 * d_qk // d_inner)),
            pl.BlockSpec((lq, d_inner), lambda b, c: (row(b, c), 2 * d_qk // d_inner + 1)),
            pl.BlockSpec((lq, RET_QK // 2), lambda b, c: (c, 0)),
            pl.BlockSpec((lq, RET_QK // 2), lambda b, c: (c, 0)),
            pl.BlockSpec((RET_HEADS, 8, 128), lambda b, c: (0, 0, 0)),
            state_spec,
            pl.BlockSpec((1, d_inner), lambda b, c: (0, 0)),
        ],
        out_specs=[
            pl.BlockSpec((lq, d_inner), lambda b, c: (row(b, c), 0)),
            state_spec,
        ],
        out_shape=[
            jax.ShapeDtypeStruct((m, d_inner), proj.dtype),
            jax.ShapeDtypeStruct((bsz, RET_HEADS, RET_V, RET_QK), F32),
        ],
        compiler_params=_cparams(("parallel", "arbitrary")),
        name="ret_scan",
    )(proj, proj, proj, proj, cos, sin, lg, s0, norm_g.reshape(1, -1))


def _rwkv_kernel(r_ref, k_ref, v_ref, g_ref, lw_ref, la_ref, bw_ref, ba_ref, par_ref, s0_ref,
                 y_ref, sout_ref, st, *, lo, hi, nchunks, ngrp):
    c = pl.program_id(2)
    w4 = RWKV_HPB * HEAD_DIM
    r2 = lax.broadcasted_iota(jnp.int32, (w4, w4), 0)
    c2 = lax.broadcasted_iota(jnp.int32, (w4, w4), 1)
    blk = _idiv(r2, HEAD_DIM) == _idiv(c2, HEAD_DIM)

    @pl.when(c == 0)
    def _():
        tile = (lax.broadcasted_iota(jnp.int32, (HEAD_DIM, w4), 0)
                == _imod(lax.broadcasted_iota(jnp.int32, (HEAD_DIM, w4), 1), HEAD_DIM)).astype(BF16)
        for gi in range(ngrp):
            s0 = s0_ref[0, gi * RWKV_HPB:(gi + 1) * RWKV_HPB].reshape(w4, HEAD_DIM)
            st[gi] = jnp.where(blk, _dot_split_l(s0, tile), 0.0)

    s_new = [None] * ngrp
    gens = [_rwkv_group(gi, r_ref, k_ref, v_ref, g_ref, lw_ref, la_ref, bw_ref, ba_ref, par_ref,
                        y_ref, st, s_new, lo=lo, hi=hi) for gi in range(ngrp)]
    for _ in zip(*gens):
        pass

    @pl.when(c == nchunks - 1)
    def _():
        tile_t = (_imod(lax.broadcasted_iota(jnp.int32, (w4, HEAD_DIM), 0), HEAD_DIM)
                  == lax.broadcasted_iota(jnp.int32, (w4, HEAD_DIM), 1)).astype(BF16)
        for gi in range(ngrp):
            sout_ref[0, gi * RWKV_HPB:(gi + 1) * RWKV_HPB] = _dot_split_l(s_new[gi], tile_t).reshape(
                RWKV_HPB, HEAD_DIM, HEAD_DIM)


def _rwkv_group(gi, r_ref, k_ref, v_ref, g_ref, lw_ref, la_ref, bw_ref, ba_ref, par_ref, y_ref, st, out,
                *, lo, hi):
    c = pl.program_id(2)
    cs = RWKV_CHUNK
    w4 = RWKV_HPB * HEAD_DIM
    sl = slice(gi * w4, (gi + 1) * w4)

    nr = RWKV_HPB * cs
    ones_bd = (_idiv(lax.broadcasted_iota(jnp.int32, (w4, w4), 0), HEAD_DIM)
               == _idiv(lax.broadcasted_iota(jnp.int32, (w4, w4), 1), HEAD_DIM)).astype(BF16)
    blk = (_idiv(lax.broadcasted_iota(jnp.int32, (nr, w4), 0), cs)
           == _idiv(lax.broadcasted_iota(jnp.int32, (nr, w4), 1), HEAD_DIM))
    r2 = lax.broadcasted_iota(jnp.int32, (nr, nr), 0)
    c2 = lax.broadcasted_iota(jnp.int32, (nr, nr), 1)
    same = _idiv(r2, cs) == _idiv(c2, cs)

    def segsum(x):
        return _dot_split_l(x, ones_bd, terms=2)

    par = par_ref[:, sl]
    w0, a0, k_k, k_a, r_k, ln_g, ln_b = (par[i:i + 1] for i in range(7))

    r = r_ref[0, :, sl].astype(F32)
    k = k_ref[0, :, sl].astype(F32)
    v = v_ref[0, :, sl].astype(F32)
    g = g_ref[0, :, sl].astype(F32)
    w_raw = w0 + _dot(jnp.tanh(lw_ref[0]).astype(BF16), bw_ref[0, :, sl])
    a = jax.nn.sigmoid(a0 + _dot(la_ref[0].astype(BF16), ba_ref[0, :, sl]))

    ti = lax.broadcasted_iota(jnp.int32, (cs, 1), 0)
    pos = c * cs + ti
    valid = (pos >= lo) & (pos < hi)

    lw = -jnp.exp(-_softplus(-w_raw) - 0.5)
    kk = k * k_k
    kk = kk / jnp.maximum(jnp.sqrt(segsum(kk * kk)), 1e-12)
    yield
    kp = k * (1.0 + (a - 1.0) * k_a)
    lw = jnp.where(valid, lw, 0.0)
    kk = jnp.where(valid, kk, 0.0)
    kp = jnp.where(valid, kp, 0.0)
    vm = jnp.where(valid, v, 0.0)

    tri = (lax.broadcasted_iota(jnp.int32, (cs, cs), 0) >= lax.broadcasted_iota(jnp.int32, (cs, cs), 1)).astype(BF16)
    cw = _dot_split_r(tri, lw)
    yield
    cwl = cw[cs - 1:cs]
    wt = jnp.exp(cw)
    wi = jnp.exp(-cw)
    wend = jnp.exp(cwl - cw)
    b = kk * a
    at = -kk * jnp.exp(cw - lw)
    rt = r * wt

    def bd(x):
        return jnp.where(blk, jnp.concatenate([x] * RWKV_HPB, axis=0), 0.0).astype(BF16)

    lhs = jnp.concatenate([bd(at), bd(rt)], axis=0)
    rhs = jnp.concatenate([bd(b * wi), bd(kp * wi)], axis=0)
    sc = _nt(lhs, rhs)
    yield
    tt = _imod(r2, cs)
    jj = _imod(c2, cs)
    strict = same & (tt > jj)
    incl = same & (tt >= jj)
    mab = jnp.where(strict, sc[:nr, :nr], 0.0)
    mak = jnp.where(strict, sc[:nr, nr:], 0.0)
    nrb = jnp.where(incl, sc[nr:, :nr], 0.0)
    nrk = jnp.where(incl, sc[nr:, nr:], 0.0)

    tinv = (r2 == c2).astype(F32) + jnp.where(
        (_idiv(r2, 2) == _idiv(c2, 2)) & (_imod(tt, 2) == 1) & (_imod(jj, 2) == 0), mab, 0.0)
    msz = 2
    while msz < cs:
        off = ((_idiv(r2, 2 * msz) == _idiv(c2, 2 * msz)) & (_imod(tt, 2 * msz) >= msz)
               & (_imod(jj, 2 * msz) < msz))
        tb = tinv.astype(BF16)
        tno = _dot(tb, jnp.where(off, mab, 0.0).astype(BF16)).astype(BF16)
        yield
        tinv = tinv + _dot(tno, tb)
        yield
        msz *= 2

    s_old = st[gi]
    x = _nt(lhs, s_old.astype(BF16))
    yield
    vbd = bd(vm)
    u = _dot(tinv.astype(BF16), (x[:nr] + _dot(mak.astype(BF16), vbd)).astype(BF16))
    yield
    ub = u.astype(BF16)
    yb = x[nr:] + _dot(nrb.astype(BF16), ub) + _dot(nrk.astype(BF16), vbd)
    yield
    s_new = s_old * wt[cs - 1:cs] + _tn(jnp.concatenate([ub, vbd], axis=0),
                                        jnp.concatenate([bd(b * wend), bd(kp * wend)], axis=0))
    st[gi] = s_new
    out[gi] = s_new
    yield

    y = sum(yb[i * cs:(i + 1) * cs] for i in range(RWKV_HPB))
    inv = 1.0 / HEAD_DIM
    yc = y - segsum(y) * inv
    yield
    y = yc * lax.rsqrt(segsum(yc * yc) * inv + RWKV_LN_EPS) * ln_g + ln_b
    y = (y + segsum(r * kp * r_k) * v) * _silu(g)
    y_ref[:, sl] = jnp.where(valid, y, 0.0).astype(BF16)
    yield


def rwkv_scan(rkvg, lora1, lora_b, par, s0, *, bsz, tp, lo, hi):
    _, m, e = rkvg.shape
    cs = RWKV_CHUNK
    nch = tp // cs
    heads = e // HEAD_DIM
    ngrp = RWKV_GROUPS_PER_STEP
    hb = RWKV_HPB * ngrp
    wb = hb * HEAD_DIM
    row = lambda b, h, c: b * nch + c
    kern = functools.partial(_rwkv_kernel, lo=lo, hi=hi, nchunks=nch, ngrp=ngrp)
    proj_spec = lambda s: pl.BlockSpec((1, cs, wb), lambda b, h, c: (s, row(b, h, c), h))
    return pl.pallas_call(
        kern,
        grid=(bsz, heads // hb, nch),
        in_specs=[
            proj_spec(0), proj_spec(1), proj_spec(2), proj_spec(3),
            pl.BlockSpec((1, cs, RWKV_LORA_PAD), lambda b, h, c: (0, row(b, h, c), 0)),
            pl.BlockSpec((1, cs, RWKV_LORA_PAD), lambda b, h, c: (1, row(b, h, c), 0)),
            pl.BlockSpec((1, RWKV_LORA_PAD, wb), lambda b, h, c: (0, 0, h)),
            pl.BlockSpec((1, RWKV_LORA_PAD, wb), lambda b, h, c: (1, 0, h)),
            pl.BlockSpec((8, wb), lambda b, h, c: (0, h)),
            pl.BlockSpec((1, hb, HEAD_DIM, HEAD_DIM), lambda b, h, c: (b, h, 0, 0)),
        ],
        out_specs=[
            pl.BlockSpec((cs, wb), lambda b, h, c: (row(b, h, c), h)),
            pl.BlockSpec((1, hb, HEAD_DIM, HEAD_DIM), lambda b, h, c: (b, h, 0, 0)),
        ],
        out_shape=[
            jax.ShapeDtypeStruct((m, e), BF16),
            jax.ShapeDtypeStruct((bsz, heads, HEAD_DIM, HEAD_DIM), F32),
        ],
        scratch_shapes=[pltpu.VMEM((ngrp, RWKV_HPB * HEAD_DIM, RWKV_HPB * HEAD_DIM), F32)],
        compiler_params=_cparams(("parallel", "parallel", "arbitrary")),
        name="rwkv_scan",
    )(rkvg, rkvg, rkvg, rkvg, lora1, lora1, lora_b, lora_b, par, s0)


def _rwkv_short_kernel(r_ref, k_ref, v_ref, g_ref, lw_ref, la_ref, bw_ref, ba_ref, par_ref, s0_ref,
                       ones_ref, tile_ref, y_ref, sout_ref, *, nvalid, rows_out):
    ct = 8
    hp = RWKV_SHORT_HEADS
    wl = hp * HEAD_DIM
    nr = hp * ct
    ones_bd = ones_ref[...]
    tile_t = tile_ref[...]

    def segsum(x):
        return jnp.concatenate(
            [_dot_split_l(x[:, 256 * j:256 * (j + 1)], ones_bd, terms=2) for j in range(wl // 256)], axis=1)

    par = par_ref[...]
    w0, a0, k_k, k_a, r_k, ln_g, ln_b = (par[i:i + 1] for i in range(7))
    r, k, v, g = r_ref[0], k_ref[0], v_ref[0], g_ref[0]
    w_raw = w0 + _dot(jnp.tanh(lw_ref[0]).astype(BF16), bw_ref[0])
    a = jax.nn.sigmoid(a0 + _dot(la_ref[0].astype(BF16), ba_ref[0]))

    ti = lax.broadcasted_iota(jnp.int32, (ct, 1), 0)
    valid = ti < nvalid
    lw = jnp.where(valid, -jnp.exp(-_softplus(-w_raw) - 0.5), 0.0)
    kk = k * k_k
    kk = jnp.where(valid, kk / jnp.maximum(jnp.sqrt(segsum(kk * kk)), 1e-12), 0.0)
    kp = jnp.where(valid, k * (1.0 + (a - 1.0) * k_a), 0.0)
    vm = jnp.where(valid, v, 0.0)

    cw = lw
    for s in (1, 2, 4):
        cw = cw + jnp.where(ti >= s, pltpu.roll(cw, s, 0), 0.0)
    cwl = cw[ct - 1:ct]
    wend = jnp.exp(cwl - cw)
    wi = jnp.exp(-cw)
    b = kk * a
    at = -kk * jnp.exp(cw - lw)
    rt = r * jnp.exp(cw)

    rr = lax.broadcasted_iota(jnp.int32, (nr, wl), 0)
    cc = lax.broadcasted_iota(jnp.int32, (nr, wl), 1)
    blk = _idiv(rr, ct) == _idiv(cc, HEAD_DIM)

    def bd(x):
        return jnp.where(blk, jnp.concatenate([x] * hp, axis=0), 0.0).astype(BF16)

    lhs = jnp.concatenate([bd(at), bd(rt)], axis=0)
    rhs = jnp.concatenate([bd(b * wi), bd(kp * wi)], axis=0)
    sc = _nt(lhs, rhs)
    ri = lax.broadcasted_iota(jnp.int32, (2 * nr, 2 * nr), 0)
    ci = lax.broadcasted_iota(jnp.int32, (2 * nr, 2 * nr), 1)
    same = _idiv(_imod(ri, nr), ct) == _idiv(_imod(ci, nr), ct)
    tt = _imod(ri, ct)
    jj = _imod(ci, ct)
    sc = jnp.where(same & (tt + (ri >= nr).astype(jnp.int32) > jj), sc, 0.0)
    mab, mak, nrb, nrk = sc[:nr, :nr], sc[:nr, nr:], sc[nr:, :nr], sc[nr:, nr:]

    r1 = lax.broadcasted_iota(jnp.int32, (nr, nr), 0)
    c1 = lax.broadcasted_iota(jnp.int32, (nr, nr), 1)
    t1 = _imod(r1, ct)
    j1 = _imod(c1, ct)
    tinv = (r1 == c1).astype(F32) + jnp.where(
        (_idiv(r1, 2) == _idiv(c1, 2)) & (_imod(t1, 2) == 1) & (_imod(j1, 2) == 0), mab, 0.0)
    msz = 2
    while msz < nvalid:
        off = ((_idiv(r1, 2 * msz) == _idiv(c1, 2 * msz)) & (_imod(t1, 2 * msz) >= msz)
               & (_imod(j1, 2 * msz) < msz))
        tb = tinv.astype(BF16)
        tinv = tinv + _dot(_dot(tb, jnp.where(off, mab, 0.0).astype(BF16)).astype(BF16), tb)
        msz *= 2

    s_old = s0_ref[0].reshape(wl, HEAD_DIM)
    lhs_rows = _dot(lhs, tile_t).astype(BF16)
    x = _nt(lhs_rows, s_old.astype(BF16))
    xa = jnp.where(blk, x[:nr], 0.0)
    xr = jnp.where(blk, x[nr:], 0.0)
    vbd = bd(vm)
    u = _dot(tinv.astype(BF16), (xa + _dot(mak.astype(BF16), vbd)).astype(BF16))
    uv = jnp.concatenate([u.astype(BF16), vbd], axis=0)
    yb = xr + _dot(jnp.concatenate([nrb, nrk], axis=1).astype(BF16), uv)
    bk_rows = _dot(jnp.concatenate([bd(b * wend), bd(kp * wend)], axis=0), tile_t).astype(BF16)
    ds = _tn(uv, bk_rows)

    r16 = lax.broadcasted_iota(jnp.int32, (hp, wl), 0)
    c16 = lax.broadcasted_iota(jnp.int32, (hp, wl), 1)
    wc = jnp.exp(cwl)
    wc_rows = _dot_split_l(jnp.where(r16 == _idiv(c16, HEAD_DIM), wc, 0.0), tile_t)
    rsel = lax.broadcasted_iota(jnp.int32, (hp, HEAD_DIM), 0)
    for h in range(hp):
        rs = slice(h * HEAD_DIM, (h + 1) * HEAD_DIM)
        wc_h = jnp.sum(jnp.where(rsel == h, wc_rows, 0.0), axis=0, keepdims=True)
        sout_ref[0, h] = s_old[rs] * wc_h + ds[rs]

    y = yb[0:ct]
    for h in range(1, hp):
        y = y + yb[h * ct:(h + 1) * ct]
    inv = 1.0 / HEAD_DIM
    yc = y - segsum(y) * inv
    y = yc * lax.rsqrt(segsum(yc * yc) * inv + RWKV_LN_EPS) * ln_g + ln_b
    y = (y + segsum(r * kp * r_k) * v) * _silu(g)
    y = jnp.where(valid, y, 0.0)
    y_ref[...] = _pad_rows(y, rows_out).astype(y_ref.dtype)


def rwkv_short(rkvg, lora1, lora_b, par, s0, *, bsz, tp, hi):
    _, m, e = rkvg.shape
    heads = e // HEAD_DIM
    hp = RWKV_SHORT_HEADS
    wl = hp * HEAD_DIM
    rb = tp // 8
    w4 = RWKV_HPB * HEAD_DIM
    ones_bd = (jnp.arange(w4)[:, None] // HEAD_DIM == jnp.arange(w4)[None, :] // HEAD_DIM).astype(BF16)
    tile_t = (jnp.arange(wl)[:, None] % HEAD_DIM == jnp.arange(HEAD_DIM)[None, :]).astype(BF16)
    kern = functools.partial(_rwkv_short_kernel, nvalid=hi, rows_out=tp)
    proj_spec = lambda s: pl.BlockSpec((1, 8, wl), lambda b, h: (s, b * rb, h))
    return pl.pallas_call(
        kern,
        grid=(bsz, heads // hp),
        in_specs=[
            proj_spec(0), proj_spec(1), proj_spec(2), proj_spec(3),
            pl.BlockSpec((1, 8, RWKV_LORA_PAD), lambda b, h: (0, b * rb, 0)),
            pl.BlockSpec((1, 8, RWKV_LORA_PAD), lambda b, h: (1, b * rb, 0)),
            pl.BlockSpec((1, RWKV_LORA_PAD, wl), lambda b, h: (0, 0, h)),
            pl.BlockSpec((1, RWKV_LORA_PAD, wl), lambda b, h: (1, 0, h)),
            pl.BlockSpec((8, wl), lambda b, h: (0, h)),
            pl.BlockSpec((1, hp, HEAD_DIM, HEAD_DIM), lambda b, h: (b, h, 0, 0)),
            pl.BlockSpec((w4, w4), lambda b, h: (0, 0)),
            pl.BlockSpec((wl, HEAD_DIM), lambda b, h: (0, 0)),
        ],
        out_specs=[
            pl.BlockSpec((tp, wl), lambda b, h: (b, h)),
            pl.BlockSpec((1, hp, HEAD_DIM, HEAD_DIM), lambda b, h: (b, h, 0, 0)),
        ],
        out_shape=[
            jax.ShapeDtypeStruct((m, e), rkvg.dtype),
            jax.ShapeDtypeStruct((bsz, heads, HEAD_DIM, HEAD_DIM), F32),
        ],
        compiler_params=_cparams(("parallel", "parallel")),
        name="rwkv_short",
    )(rkvg, rkvg, rkvg, rkvg, lora1, lora1, lora_b, lora_b, par, s0, ones_bd, tile_t)


def _rwkv_lanes_kernel(r_ref, k_ref, v_ref, g_ref, lw_ref, la_ref, bw_ref, ba_ref, par_ref, s0_ref,
                       y_ref, sout_ref, tk, tw, tb, tq, tr, tv, ty, *, nt, nb):
    hp = RWKV_LANE_HEADS
    wl = hp * HEAD_DIM
    ri = lax.broadcasted_iota(jnp.int32, (wl, wl), 0)
    ci = lax.broadcasted_iota(jnp.int32, (wl, wl), 1)
    ones_bd = (_idiv(ri, HEAD_DIM) == _idiv(ci, HEAD_DIM)).astype(BF16)

    def segsum(x):
        return _dot_split_l(x, ones_bd, terms=2)

    par = par_ref[...]
    w0, a0, k_k, k_a, r_k, ln_g, ln_b = (par[i:i + 1] for i in range(7))
    r, k, v, g = (x[0].astype(F32) for x in (r_ref, k_ref, v_ref, g_ref))
    w_raw = w0 + _dot(jnp.tanh(lw_ref[0]).astype(BF16), bw_ref[0])
    a = jax.nn.sigmoid(a0 + _dot(la_ref[0].astype(BF16), ba_ref[0]))
    decay = jnp.exp(-jnp.exp(-_softplus(-w_raw) - 0.5))
    kk = k * k_k
    kk = kk / jnp.maximum(jnp.sqrt(segsum(kk * kk)), 1e-12)
    kp = k * (1.0 + (a - 1.0) * k_a)
    bb = kk * a

    for t in range(nt):
        rows = slice(t * nb, (t + 1) * nb)
        tk[t] = (-kk[rows]).T
        tw[t] = decay[rows].T
        tb[t] = bb[rows].T
        tq[t] = kp[rows].T
        tr[t] = r[rows].T
        tv[t] = v[rows].T

    for hh in range(hp):
        ks = slice(hh * HEAD_DIM, (hh + 1) * HEAD_DIM)

        def body(vi, carry, hh=hh, ks=ks):
            sv = s0_ref[hh, vi]
            row = hh * HEAD_DIM + vi
            for t in range(nt):
                sa = jnp.sum(sv * tk[t, ks, :], axis=0, keepdims=True)
                sv = sv * tw[t, ks, :] + sa * tb[t, ks, :] + tv[t, pl.ds(row, 1), :] * tq[t, ks, :]
                ty[t, pl.ds(row, 1), :] = jnp.sum(sv * tr[t, ks, :], axis=0, keepdims=True)
            sout_ref[hh, vi] = sv
            return carry

        lax.fori_loop(0, HEAD_DIM, body, 0, unroll=4)

    inv = 1.0 / HEAD_DIM
    for t in range(nt):
        rows = slice(t * nb, (t + 1) * nb)
        y = ty[t].T
        yc = y - segsum(y) * inv
        y = yc * lax.rsqrt(segsum(yc * yc) * inv + RWKV_LN_EPS) * ln_g + ln_b
        y = (y + segsum(r[rows] * kp[rows] * r_k) * v[rows]) * _silu(g[rows])
        y_ref[rows, :] = y.astype(BF16)


def rwkv_lanes(rkvg, lora1, lora_b, par, s0t, *, nt, nb):
    _, m, e = rkvg.shape
    heads = e // HEAD_DIM
    hp = RWKV_LANE_HEADS
    wl = hp * HEAD_DIM
    kern = functools.partial(_rwkv_lanes_kernel, nt=nt, nb=nb)
    proj_spec = lambda s: pl.BlockSpec((1, m, wl), lambda h: (s, 0, h))
    state_spec = pl.BlockSpec((hp, HEAD_DIM, HEAD_DIM, nb), lambda h: (h, 0, 0, 0))
    tile = pltpu.VMEM((nt, wl, nb), F32)
    return pl.pallas_call(
        kern,
        grid=(heads // hp,),
        in_specs=[
            proj_spec(0), proj_spec(1), proj_spec(2), proj_spec(3),
            pl.BlockSpec((1, m, RWKV_LORA_PAD), lambda h: (0, 0, 0)),
            pl.BlockSpec((1, m, RWKV_LORA_PAD), lambda h: (1, 0, 0)),
            pl.BlockSpec((1, RWKV_LORA_PAD, wl), lambda h: (0, 0, h)),
            pl.BlockSpec((1, RWKV_LORA_PAD, wl), lambda h: (1, 0, h)),
            pl.BlockSpec((8, wl), lambda h: (0, h)),
            state_spec,
        ],
        out_specs=[pl.BlockSpec((m, wl), lambda h: (0, h)), state_spec],
        out_shape=[jax.ShapeDtypeStruct((m, e), BF16), jax.ShapeDtypeStruct(s0t.shape, F32)],
        scratch_shapes=[tile] * 7,
        compiler_params=_cparams(("parallel",)),
        name="rwkv_lanes",
    )(rkvg, rkvg, rkvg, rkvg, lora1, lora1, lora_b, lora_b, par, s0t)


def _prep_weights(norm_g, ssd_w_in, ssd_dt_bias, ssd_a_log, ssd_d, ssd_w_out,
                  rwkv_w_rkvg, rwkv_w_lora_a, rwkv_w_lora_b, rwkv_a_lora_a, rwkv_a_lora_b,
                  rwkv_w0, rwkv_a0, rwkv_k_k, rwkv_k_a, rwkv_r_k, rwkv_ln_g, rwkv_ln_b, rwkv_w_out,
                  ret_w_in, ret_w_out):
    d_inner = ssd_w_out.shape[1]
    n_main = d_inner + d_inner + 2 * SSD_GROUPS * SSD_STATE
    ns = ssd_w_in.shape[0]
    d_model = ssd_w_in.shape[1]

    def head_lanes(p):
        p = p.reshape(ns, SSD_GROUPS, 1, SSD_HPG)
        return jnp.pad(p, ((0, 0), (0, 0), (0, 0), (0, 128 - SSD_HPG)))

    w_dt = ssd_w_in[:, :, n_main:].reshape(ns, d_model, SSD_GROUPS, SSD_HPG)
    w_dt = jnp.pad(w_dt, ((0, 0), (0, 0), (0, 0), (0, 128 - SSD_HPG))).reshape(ns, d_model, SSD_GROUPS * 128)
    rank = rwkv_w_lora_a.shape[2]
    lora_a = jnp.stack([rwkv_w_lora_a, rwkv_a_lora_a], axis=1)
    lora_a = jnp.pad(lora_a, ((0, 0), (0, 0), (0, 0), (0, RWKV_LORA_PAD - rank)))
    lora_b = jnp.stack([rwkv_w_lora_b, rwkv_a_lora_b], axis=1)
    lora_b = jnp.pad(lora_b, ((0, 0), (0, 0), (0, RWKV_LORA_PAD - rank), (0, 0)))
    nr = rwkv_w0.shape[0]
    par = jnp.stack([rwkv_w0, rwkv_a0, rwkv_k_k, rwkv_k_a, rwkv_r_k.reshape(nr, -1), rwkv_ln_g, rwkv_ln_b,
                     jnp.zeros_like(rwkv_w0)], axis=1)
    return dict(
        ssd_w_main=ssd_w_in[:, :, :n_main].astype(BF16), ssd_w_dt=w_dt.astype(BF16),
        ssd_dtb=head_lanes(ssd_dt_bias), ssd_alog=head_lanes(ssd_a_log), ssd_dskip=head_lanes(ssd_d),
        ssd_w_out=ssd_w_out.astype(BF16),
        rwkv_w=rwkv_w_rkvg.astype(BF16), rwkv_lora_a=lora_a.astype(BF16), rwkv_lora_b=lora_b.astype(BF16),
        rwkv_par=par, rwkv_w_out=rwkv_w_out.astype(BF16),
        ret_w_in=ret_w_in.astype(BF16), ret_w_out=ret_w_out.astype(BF16),
    )


def _trunk(h, conv_st, ssd_st, shift_st, wkv_st, ret_st, pos, *, bsz, tp, lq, lo, hi, depth,
           norm_g, final_norm_g, wts, ssd_conv_w, ssd_conv_b, ssd_norm_g, rwkv_mu, ret_norm_g):
    d_model = h.shape[1]
    geo = dict(bsz=bsz, tp=tp, lo=lo, hi=hi)
    new_conv, new_shift, new_wkv, new_ret = [], [], [], []
    new_ssd = None
    act = BF16 if lq % 16 == 0 else F32

    half = RET_QK // 2
    inv_freq = 1.0 / (RET_THETA_BASE ** jnp.linspace(0.0, 1.0, half, dtype=F32))
    ang = pos.astype(F32)[:, None] * inv_freq
    cos, sin = jnp.cos(ang), jnp.sin(ang)
    log_gamma = jnp.log1p(-jnp.exp2(-5.0 - jnp.arange(RET_HEADS, dtype=F32)))
    lg = jnp.broadcast_to(log_gamma[:, None, None], (RET_HEADS, 8, 128))

    for layer in range(depth):
        kind, j = layer % 3, layer // 3
        u = rmsnorm(h, norm_g[layer], F32 if kind == 1 else BF16)
        if kind == 0:
            proj = matmul(u, wts["ssd_w_main"][j], out_dtype=act)
            dt_raw = matmul(u, wts["ssd_w_dt"][j])
            conv_init = jnp.pad(conv_st[j], ((0, 0), (8 - (SSD_CONV - 1), 0), (0, 0)))
            y, new_ssd = ssd_scan(proj, dt_raw, conv_init, ssd_st, j, new_ssd, ssd_conv_w[j], ssd_conv_b[j],
                                  wts["ssd_dtb"][j], wts["ssd_alog"][j], wts["ssd_dskip"][j], ssd_norm_g[j],
                                  lq=lq, **geo)
            nk = SSD_CONV - 1
            last = u.reshape(bsz, tp, d_model)[:, hi - nk:hi].reshape(bsz * nk, d_model)
            last = jnp.pad(last, ((0, -(bsz * nk) % 16), (0, 0)))
            xbc = matmul(last, wts["ssd_w_main"][j])[:bsz * nk, y.shape[1]:]
            new_conv.append(xbc.reshape(bsz, nk, -1))
            h = matmul(y, wts["ssd_w_out"][j], res=h)
        elif kind == 1 and lo == 0 and hi <= 8 and bsz % LANES == 0:
            tmajor = lambda x: jnp.swapaxes(x.reshape(bsz, tp, -1)[:, :hi], 0, 1)
            uc = tmajor(u)
            prev = jnp.concatenate([shift_st[j][None], uc[:-1]], axis=0).reshape(hi * bsz, d_model)
            uc = uc.reshape(hi * bsz, d_model)
            rkvg = mix_matmul(uc, prev, rwkv_mu[j][:4], wts["rwkv_w"][j], out_dtype=BF16)
            lora1 = mix_matmul(uc, prev, rwkv_mu[j][4:], wts["rwkv_lora_a"][j])
            y, s_t = rwkv_lanes(rkvg, lora1, wts["rwkv_lora_b"][j], wts["rwkv_par"][j],
                                jnp.transpose(wkv_st[j], (1, 2, 3, 0)), nt=hi, nb=bsz)
            new_shift.append(u.reshape(bsz, tp, d_model)[:, hi - 1])
            new_wkv.append(jnp.transpose(s_t, (3, 0, 1, 2)))
            hc = matmul(y, wts["rwkv_w_out"][j], res=tmajor(h).reshape(hi * bsz, d_model))
            hc = jnp.swapaxes(hc.reshape(hi, bsz, d_model), 0, 1)
            h = jnp.pad(hc, ((0, 0), (0, tp - hi), (0, 0))).reshape(bsz * tp, d_model)
        elif kind == 1:
            u3 = u.reshape(bsz, tp, d_model)
            prev = jnp.concatenate([shift_st[j][:, None, :], u3[:, :-1]], axis=1).reshape(bsz * tp, d_model)
            rkvg = mix_matmul(u, prev, rwkv_mu[j][:4], wts["rwkv_w"][j], out_dtype=BF16)
            lora1 = mix_matmul(u, prev, rwkv_mu[j][4:], wts["rwkv_lora_a"][j])
            if lo == 0 and hi <= 8 and tp <= 16:
                y, s_new = rwkv_short(rkvg.astype(F32), lora1, wts["rwkv_lora_b"][j], wts["rwkv_par"][j], wkv_st[j],
                                      bsz=bsz, tp=tp, hi=hi)
            else:
                y, s_new = rwkv_scan(rkvg, lora1, wts["rwkv_lora_b"][j], wts["rwkv_par"][j], wkv_st[j], **geo)
            new_shift.append(u3[:, hi - 1])
            new_wkv.append(s_new)
            h = matmul(y, wts["rwkv_w_out"][j], res=h)
        else:
            proj = matmul(u, wts["ret_w_in"][j], out_dtype=act)
            y, s_new = ret_scan(proj, cos, sin, lg, ret_st[j], ret_norm_g[j], lq=lq, **geo)
            new_ret.append(s_new)
            h = matmul(y, wts["ret_w_out"][j], res=h)
    y = rmsnorm(h, final_norm_g)
    return (y, jnp.stack(new_conv), new_ssd, jnp.stack(new_shift), jnp.stack(new_wkv), jnp.stack(new_ret))


def kernel(x_prompt, x_sample, state_ssd_conv, state_ssd, state_rwkv_shift, state_rwkv_wkv, state_ret, meta_tokens, norm_g, final_norm_g, ssd_w_in, ssd_conv_w, ssd_conv_b, ssd_dt_bias, ssd_a_log, ssd_d, ssd_norm_g, ssd_w_out, rwkv_mu, rwkv_w_rkvg, rwkv_w0, rwkv_w_lora_a, rwkv_w_lora_b, rwkv_a0, rwkv_a_lora_a, rwkv_a_lora_b, rwkv_k_k, rwkv_k_a, rwkv_r_k, rwkv_ln_g, rwkv_ln_b, rwkv_w_out, ret_w_in, ret_norm_g, ret_w_out):
    depth = norm_g.shape[0]
    d_model = x_prompt.shape[2]
    wts = _prep_weights(norm_g, ssd_w_in, ssd_dt_bias, ssd_a_log, ssd_d, ssd_w_out,
                        rwkv_w_rkvg, rwkv_w_lora_a, rwkv_w_lora_b, rwkv_a_lora_a, rwkv_a_lora_b,
                        rwkv_w0, rwkv_a0, rwkv_k_k, rwkv_k_a, rwkv_r_k, rwkv_ln_g, rwkv_ln_b, rwkv_w_out,
                        ret_w_in, ret_w_out)
    common = dict(depth=depth, norm_g=norm_g, final_norm_g=final_norm_g, wts=wts, ssd_conv_w=ssd_conv_w,
                  ssd_conv_b=ssd_conv_b, ssd_norm_g=ssd_norm_g, rwkv_mu=rwkv_mu, ret_norm_g=ret_norm_g)

    bp, seq, _ = x_prompt.shape
    lq_p = 128
    lo_p = lq_p - N_META
    tp_p = lo_p + N_META + seq
    h_p = jnp.concatenate([jnp.zeros((bp, lo_p, d_model), F32),
                           jnp.broadcast_to(meta_tokens[None], (bp, N_META, d_model)), x_prompt], axis=1)
    zeros_like_b = lambda s: jnp.zeros((s.shape[0], bp) + s.shape[2:], F32)
    pos_p = jnp.maximum(jnp.arange(tp_p) - lo_p, 0)
    outs_p = _trunk(h_p.reshape(bp * tp_p, d_model), zeros_like_b(state_ssd_conv), zeros_like_b(state_ssd),
                    zeros_like_b(state_rwkv_shift), zeros_like_b(state_rwkv_wkv), zeros_like_b(state_ret), pos_p,
                    bsz=bp, tp=tp_p, lq=lq_p, lo=lo_p, hi=tp_p, **common)
    y_prompt = outs_p[0].reshape(bp, tp_p, d_model)[:, lo_p + N_META:]

    bs, ds, _ = x_sample.shape
    tp_s = 8
    h_s = jnp.concatenate([x_sample, jnp.zeros((bs, tp_s - ds, d_model), F32)], axis=1)
    pos_s = PAST_LEN + jnp.arange(tp_s)
    outs_s = _trunk(h_s.reshape(bs * tp_s, d_model), state_ssd_conv, state_ssd, state_rwkv_shift, state_rwkv_wkv,
                    state_ret, pos_s, bsz=bs, tp=tp_s, lq=tp_s, lo=0, hi=ds, **common)
    y_sample = outs_s[0].reshape(bs, tp_s, d_model)[:, :ds]

    return (y_prompt, y_sample) + tuple(outs_p[1:]) + tuple(outs_s[1:])
```

```python
import functools
import math

import jax
import jax.numpy as jnp
from jax import lax
from jax.experimental import pallas as pl
from jax.experimental.pallas import tpu as pltpu

F32 = jnp.float32
BF16 = jnp.bfloat16

EPS = 1e-6
N_META = 16
HEAD_DIM = 64
SSD_STATE = 128
SSD_GROUPS = 8
SSD_HPG = 8
SSD_CONV = 4
RET_HEADS = 8
RET_QK = 256
RET_V = 512
RET_THETA_BASE = 10000.0
RWKV_LORA_PAD = 128
RWKV_CHUNK = 64
RWKV_HPB = 4
RWKV_GROUPS_PER_STEP = 8
RWKV_SHORT_HEADS = 16
RWKV_LANE_HEADS = 2
LANES = 128
MIN_CHUNK_ROWS = 16
PAST_LEN = 16384
RWKV_LN_EPS = 1e-5 * HEAD_DIM
NEG = -1e30
ROW_TILE = 512
V7X_VMEM_BYTES = 64 * 1024 * 1024
VMEM_LIMIT = V7X_VMEM_BYTES * 7 // 8
MATMUL_VMEM_BUDGET = V7X_VMEM_BYTES * 5 // 8


def _cparams(sem):
    return pltpu.CompilerParams(dimension_semantics=sem, vmem_limit_bytes=VMEM_LIMIT)


def _nt(a, b):
    return lax.dot_general(a, b, (((1,), (1,)), ((), ())), preferred_element_type=F32)


def _tn(a, b):
    return lax.dot_general(a, b, (((0,), (0,)), ((), ())), preferred_element_type=F32)


def _dot(a, b):
    return jnp.dot(a, b, preferred_element_type=F32)


def _split(x, terms):
    parts = []
    r = x
    for i in range(terms):
        p = r.astype(BF16)
        parts.append(p)
        if i + 1 < terms:
            r = r - p.astype(F32)
    return parts


def _dot_split_l(x, m, terms=3):
    acc = None
    for p in _split(x, terms):
        d = _dot(p, m)
        acc = d if acc is None else acc + d
    return acc


def _dot_split_r(m, x, terms=3):
    acc = None
    for p in _split(x, terms):
        d = _dot(m, p)
        acc = d if acc is None else acc + d
    return acc


def _pad_rows(x, rows):
    if x.shape[0] == rows:
        return x
    return jnp.concatenate([x, jnp.zeros((rows - x.shape[0], x.shape[1]), x.dtype)], axis=0)


def _idiv(x, n):
    return jnp.right_shift(x, int(math.log2(n)))


def _imod(x, n):
    return jnp.bitwise_and(x, n - 1)


def _silu(x):
    h = 0.5 * x
    return h + h * jnp.tanh(h)


def _softplus(x):
    return jnp.maximum(x, 0.0) + jnp.log(1.0 + jnp.exp(-jnp.abs(x)))


def _rmsnorm_kernel(x_ref, g_ref, o_ref):
    x = x_ref[...]
    ms = jnp.mean(x * x, axis=-1, keepdims=True)
    o_ref[...] = (x * lax.rsqrt(ms + EPS) * g_ref[...]).astype(o_ref.dtype)


def _row_tile(m):
    tm = math.gcd(m, ROW_TILE)
    assert tm % 16 == 0, m
    return tm


def _matmul_tiles(m, k, n, *, a_bytes, n_a, cast, n_out):
    best = None
    for tn in (t for t in (1024, 512, 256, 128) if n % t == 0):
        for tm in (t for t in range(16, m + 1, 16) if m % t == 0):
            need = (2 * n_a * tm * k * a_bytes + (tm * k * 2 if cast else 0)
                    + 2 * k * tn * 2 + 2 * n_out * tm * tn * 4)
            if need <= MATMUL_VMEM_BUDGET and (best is None or (tm * tn, tm) > (best[0] * best[1], best[0])):
                best = (tm, tn)
    assert best is not None, (m, k, n)
    return best


def rmsnorm(x, g, out_dtype=F32):
    m, d = x.shape
    tm = _row_tile(m)
    return pl.pallas_call(
        _rmsnorm_kernel,
        grid=(m // tm,),
        in_specs=[pl.BlockSpec((tm, d), lambda i: (i, 0)), pl.BlockSpec((1, d), lambda i: (0, 0))],
        out_specs=pl.BlockSpec((tm, d), lambda i: (i, 0)),
        out_shape=jax.ShapeDtypeStruct((m, d), out_dtype),
        compiler_params=_cparams(("parallel",)),
        name="rmsnorm",
    )(x, g.reshape(1, d))


def _mm_kernel(a_ref, w_ref, *rest, has_res, cast, norm):
    rest = list(rest)
    abf_ref = rest.pop() if cast else a_ref
    o_ref = rest.pop()

    if cast:
        @pl.when(pl.program_id(1) == 0)
        def _():
            x = a_ref[...]
            if norm:
                x = x * lax.rsqrt(jnp.mean(x * x, axis=-1, keepdims=True) + EPS) * rest[0][...]
            abf_ref[...] = x.astype(BF16)

    acc = _dot(abf_ref[...], w_ref[...])
    if has_res:
        acc = rest[-1][...] + acc
    o_ref[...] = acc.astype(o_ref.dtype)


def matmul(a, w, res=None, out_dtype=F32, norm_g=None):
    m, k = a.shape
    n = w.shape[1]
    cast = a.dtype != BF16
    assert cast or norm_g is None
    tm, tn = _matmul_tiles(m, k, n, a_bytes=a.dtype.itemsize, n_a=1, cast=cast, n_out=2 if res is not None else 1)
    in_specs = [pl.BlockSpec((tm, k), lambda i, j: (i, 0)), pl.BlockSpec((k, tn), lambda i, j: (0, j))]
    args = [a, w]
    if norm_g is not None:
        in_specs.append(pl.BlockSpec((1, k), lambda i, j: (0, 0)))
        args.append(norm_g.reshape(1, k))
    if res is not None:
        in_specs.append(pl.BlockSpec((tm, tn), lambda i, j: (i, j)))
        args.append(res)
    return pl.pallas_call(
        functools.partial(_mm_kernel, has_res=res is not None, cast=cast, norm=norm_g is not None),
        grid=(m // tm, n // tn),
        in_specs=in_specs,
        out_specs=pl.BlockSpec((tm, tn), lambda i, j: (i, j)),
        out_shape=jax.ShapeDtypeStruct((m, n), out_dtype),
        scratch_shapes=[pltpu.VMEM((tm, k), BF16)] if cast else [],
        compiler_params=_cparams(("parallel", "arbitrary")),
        name="matmul_res" if res is not None else "matmul",
    )(*args)


def _mixmm_kernel(u_ref, p_ref, mu_ref, w_ref, o_ref, xm_ref):
    @pl.when(pl.program_id(2) == 0)
    def _():
        u = u_ref[...]
        xm_ref[...] = (u + (p_ref[...] - u) * mu_ref[0]).astype(BF16)

    o_ref[0] = _dot(xm_ref[...], w_ref[0]).astype(o_ref.dtype)


def mix_matmul(u, prev, mu, w, out_dtype=F32):
    m, k = u.shape
    s, _, n = w.shape
    tm, tn = _matmul_tiles(m, k, n, a_bytes=4, n_a=2, cast=True, n_out=1)
    return pl.pallas_call(
        _mixmm_kernel,
        grid=(m // tm, s, n // tn),
        in_specs=[
            pl.BlockSpec((tm, k), lambda i, si, j: (i, 0)),
            pl.BlockSpec((tm, k), lambda i, si, j: (i, 0)),
            pl.BlockSpec((1, 1, k), lambda i, si, j: (si, 0, 0)),
            pl.BlockSpec((1, k, tn), lambda i, si, j: (si, 0, j)),
        ],
        out_specs=pl.BlockSpec((1, tm, tn), lambda i, si, j: (si, i, j)),
        out_shape=jax.ShapeDtypeStruct((s, m, n), out_dtype),
        scratch_shapes=[pltpu.VMEM((tm, k), BF16)],
        compiler_params=_cparams(("parallel", "arbitrary", "arbitrary")),
        name="mix_matmul",
    )(u, prev, mu.reshape(s, 1, k), w)


def _conv_silu(cur, car_ref, cols, w, b, lq, lb):
    car_ref[8:8 + lq, cols] = cur
    acc = b + cur * w[SSD_CONV - 1:SSD_CONV]
    for s in range(1, SSD_CONV):
        acc = acc + car_ref[8 - s:8 - s + lq, cols] * w[SSD_CONV - 1 - s:SSD_CONV - s]
    car_ref[0:8, cols] = cur[lb - 8:lb]
    return _silu(acc)


def _ssd_kernel(z_ref, x_ref, b_ref, c_ref, dt_ref, ci_ref, s0_ref, cw_ref, cb_ref, dtb_ref, alog_ref,
                d_ref, ng_ref, *rest, lq, lo, hi, nchunks, out_layer):
    y_ref, sout_ref, car, st = rest[-4:]
    c = pl.program_id(1)
    gw = SSD_HPG * HEAD_DIM

    @pl.when(c == 0)
    def _():
        car[0:8, :] = ci_ref[0]
        st[...] = s0_ref[0, 0].reshape(SSD_GROUPS * gw, SSD_STATE)

    gens = [_ssd_group(g, z_ref, x_ref, b_ref, c_ref, dt_ref, cw_ref, cb_ref, dtb_ref, alog_ref, d_ref,
                       ng_ref, y_ref, car, st, lb=lq, lo=lo, hi=hi) for g in range(SSD_GROUPS)]
    for _ in zip(*gens):
        pass

    @pl.when(c == nchunks - 1)
    def _():
        for layer in range(sout_ref.shape[0]):
            if layer == out_layer:
                sout_ref[layer, 0] = st[...].reshape(SSD_GROUPS * SSD_HPG, HEAD_DIM, SSD_STATE)
            else:
                sout_ref[layer, 0] = jnp.zeros(sout_ref.shape[2:], F32)


def _ssd_group(g, z_ref, x_ref, b_ref, c_ref, dt_ref, cw_ref, cb_ref, dtb_ref, alog_ref, d_ref, ng_ref,
               y_ref, car, st, *, lb, lo, hi):
    c = pl.program_id(1)
    lq = max(lb, MIN_CHUNK_ROWS)
    gw = SSD_HPG * HEAD_DIM
    d_inner = SSD_GROUPS * gw
    xs = slice(g * gw, (g + 1) * gw)
    ns = slice(g * SSD_STATE, (g + 1) * SSD_STATE)
    bs = slice(d_inner + g * SSD_STATE, d_inner + (g + 1) * SSD_STATE)
    cs_ = slice(d_inner + (SSD_GROUPS + g) * SSD_STATE, d_inner + (SSD_GROUPS + g + 1) * SSD_STATE)

    f32 = lambda x: _pad_rows(x.astype(F32), lq)
    xc = _conv_silu(f32(x_ref[:, xs]), car, xs, cw_ref[:, xs], cb_ref[:, xs], lq, lb)
    bc = _conv_silu(f32(b_ref[:, ns]), car, bs, cw_ref[:, bs], cb_ref[:, bs], lq, lb)
    cc = _conv_silu(f32(c_ref[:, ns]), car, cs_, cw_ref[:, cs_], cb_ref[:, cs_], lq, lb)

    ti = lax.broadcasted_iota(jnp.int32, (lq, 1), 0)
    pos = c * lb + ti
    valid = (ti < lb) & (pos >= lo) & (pos < hi)
    lane = lax.broadcasted_iota(jnp.int32, (lq, 128), 1)
    dt = _softplus(f32(dt_ref[:, ns]) + dtb_ref[:, ns])
    dt = jnp.where(valid & (lane < SSD_HPG), dt, 0.0)
    la = dt * (-jnp.exp(alog_ref[:, ns]))
    yield

    ri = lax.broadcasted_iota(jnp.int32, (lq, lq), 0)
    ci = lax.broadcasted_iota(jnp.int32, (lq, lq), 1)
    tril = (ri >= ci).astype(BF16)
    triu = (ri <= ci).astype(BF16)
    acum = _dot_split_r(tril, la)
    acum_t = sum(_tn(part, triu) for part in _split(la, 3))
    yield
    a_end = acum[lq - 1:lq]
    dec_end = jnp.exp(a_end - acum)
    e_in = jnp.exp(acum)
    cd = jnp.exp(a_end)

    bcp = bc.astype(BF16)
    ccb = cc.astype(BF16)
    g_sc = _nt(ccb, bcp)
    st_old = st[xs, :]
    y_in = _nt(ccb, st_old.astype(BF16))
    yield

    causal = ci <= ri
    lane_q = lax.broadcasted_iota(jnp.int32, (lq, 128), 1) < HEAD_DIM
    lane_k = lane_q
    row_k = lax.broadcasted_iota(jnp.int32, (128, 1), 0) < HEAD_DIM
    dvec = d_ref[:, ns]

    er = lax.broadcasted_iota(jnp.int32, (2 * LANES, gw), 0)
    ec = lax.broadcasted_iota(jnp.int32, (2 * LANES, gw), 1)
    spread_m = (_imod(er, LANES) == _idiv(ec, HEAD_DIM)).astype(BF16)
    spread = lambda f: _dot(jnp.concatenate(_split(f, 2), axis=1), spread_m)
    dt_x = spread(dt)
    e_x = spread(e_in)
    dec_x = spread(dec_end)
    yield

    ys = []
    for p in range(SSD_HPG // 2):
        h0, h1 = 2 * p, 2 * p + 1
        ps = slice(128 * p, 128 * (p + 1))
        xp = xc[:, ps]
        vp = xp * dt_x[:, ps]
        vpp = vp
        yp = y_in[:, ps] * e_x[:, ps]
        yp = yp + xp * jnp.where(lane_q, dvec[:, h0:h0 + 1], dvec[:, h1:h1 + 1])
        for hh, h in ((0, h0), (1, h1)):
            seg = acum[:, h:h + 1] - acum_t[h:h + 1, :]
            lm = jnp.exp(jnp.where(causal, seg, NEG))
            pm = (g_sc * lm).astype(BF16)
            vm = jnp.where(lane_k if hh == 0 else jnp.logical_not(lane_k), vpp, 0.0).astype(BF16)
            yp = yp + _dot(pm, vm)
        ys.append(yp)
        vend = vpp * dec_x[:, ps]
        upd = _tn(vend.astype(BF16), bcp)
        cdp = jnp.where(row_k, cd[:, h0:h0 + 1], cd[:, h1:h1 + 1])
        st[g * gw + 128 * p:g * gw + 128 * (p + 1), :] = st_old[128 * p:128 * (p + 1), :] * cdp + upd
        yield

    y = jnp.concatenate(ys, axis=1)
    y = y * _silu(f32(z_ref[:, xs]))
    ms = jnp.mean(y * y, axis=-1, keepdims=True)
    y = y * lax.rsqrt(ms + EPS) * ng_ref[:, xs]
    y_ref[:, xs] = jnp.where(valid, y, 0.0)[:lb].astype(y_ref.dtype)
    yield


def ssd_scan(proj, dt_raw, conv_init, s0_all, layer, s_buf, conv_w, conv_b, dtb, alog, dskip, norm_g,
             *, bsz, tp, lq, lo, hi):
    m = proj.shape[0]
    heads = SSD_GROUPS * SSD_HPG
    nlayers = s0_all.shape[0]
    state_spec = pl.BlockSpec((1, 1, heads, HEAD_DIM, SSD_STATE), lambda b, c: (layer, b, 0, 0, 0))
    if s_buf is not None:
        out_state_spec, out_layer = state_spec, 0
        extra_specs, extra_args, aliases = [pl.BlockSpec(memory_space=pl.ANY)], [s_buf], {13: 1}
    else:
        out_state_spec = pl.BlockSpec((nlayers, 1, heads, HEAD_DIM, SSD_STATE), lambda b, c: (0, b, 0, 0, 0))
        out_layer = layer
        extra_specs, extra_args, aliases = [], [], {}
    nch = tp // lq
    d_inner = heads * HEAD_DIM
    gn = SSD_GROUPS * SSD_STATE
    conv_dim = d_inner + 2 * gn
    row = lambda b, c: b * nch + c
    const = lambda b, c: (0, 0)
    kern = functools.partial(_ssd_kernel, lq=lq, lo=lo, hi=hi, nchunks=nch, out_layer=out_layer)
    return pl.pallas_call(
        kern,
        grid=(bsz, nch),
        in_specs=[
            pl.BlockSpec((lq, d_inner), lambda b, c: (row(b, c), 0)),
            pl.BlockSpec((lq, d_inner), lambda b, c: (row(b, c), 1)),
            pl.BlockSpec((lq, gn), lambda b, c: (row(b, c), 2 * d_inner // gn)),
            pl.BlockSpec((lq, gn), lambda b, c: (row(b, c), 2 * d_inner // gn + 1)),
            pl.BlockSpec((lq, gn), lambda b, c: (row(b, c), 0)),
            pl.BlockSpec((1, 8, conv_dim), lambda b, c: (b, 0, 0)),
            state_spec,
            pl.BlockSpec((SSD_CONV, conv_dim), const),
            pl.BlockSpec((1, conv_dim), const),
            pl.BlockSpec((1, gn), const),
            pl.BlockSpec((1, gn), const),
            pl.BlockSpec((1, gn), const),
            pl.BlockSpec((1, d_inner), const),
        ] + extra_specs,
        out_specs=[
            pl.BlockSpec((lq, d_inner), lambda b, c: (row(b, c), 0)),
            out_state_spec,
        ],
        out_shape=[
            jax.ShapeDtypeStruct((m, d_inner), proj.dtype),
            jax.ShapeDtypeStruct(s0_all.shape, F32),
        ],
        scratch_shapes=[pltpu.VMEM((8 + max(lq, MIN_CHUNK_ROWS), conv_dim), F32),
                        pltpu.VMEM((d_inner, SSD_STATE), F32)],
        input_output_aliases=aliases,
        compiler_params=_cparams(("parallel", "arbitrary")),
        name="ssd_scan",
    )(proj, proj, proj, proj, dt_raw, conv_init, s0_all, conv_w, conv_b.reshape(1, -1),
      dtb.reshape(1, -1), alog.reshape(1, -1), dskip.reshape(1, -1), norm_g.reshape(1, -1), *extra_args)


def _ret_kernel(q_ref, k_ref, v_ref, g_ref, cos_ref, sin_ref, lg_ref, s0_ref, ng_ref,
                y_ref, sout_ref, *, lq, lo, hi):
    c = pl.program_id(1)

    @pl.when(c == 0)
    def _():
        sout_ref[...] = s0_ref[...]

    gens = [_ret_head(h, q_ref, k_ref, v_ref, g_ref, cos_ref, sin_ref, lg_ref, ng_ref, y_ref, sout_ref,
                      lb=lq, lo=lo, hi=hi) for h in range(RET_HEADS)]
    for _ in zip(*gens):
        pass


def _ret_head(h, q_ref, k_ref, v_ref, g_ref, cos_ref, sin_ref, lg_ref, ng_ref, y_ref, st_ref, *, lb, lo, hi):
    c = pl.program_id(1)
    lq = max(lb, MIN_CHUNK_ROWS)
    f32 = lambda x: _pad_rows(x.astype(F32), lq)
    qs = slice(h * RET_QK, (h + 1) * RET_QK)
    vs = slice(h * RET_V, (h + 1) * RET_V)
    lg = lg_ref[h][0:1, 0:1]
    nv = float(hi - lo)

    def count(p):
        return jnp.clip((p + 1 - lo).astype(F32), 0.0, nv)

    base = c * lb
    ti = lax.broadcasted_iota(jnp.int32, (lq, 1), 0)
    pos_i = base + ti
    valid = (ti < lb) & (pos_i >= lo) & (pos_i < hi)
    cnt_i = count(pos_i)
    cnt_j = count(base + lax.broadcasted_iota(jnp.int32, (1, lq), 1))
    cnt_jc = cnt_i
    cnt0 = count(base - 1 + jnp.zeros((1, 1), jnp.int32))
    cnt_end = count(base + lb - 1 + jnp.zeros((1, 1), jnp.int32))

    cos = f32(cos_ref[...])
    sin = f32(sin_ref[...])
    half = RET_QK // 2

    def rot(x):
        x1, x2 = x[:, :half], x[:, half:]
        return jnp.concatenate([x1 * cos - x2 * sin, x1 * sin + x2 * cos], axis=1)

    qr = rot(f32(q_ref[:, qs])).astype(BF16)
    kr = jnp.where(valid, rot(f32(k_ref[:, qs])) * (RET_QK ** -0.5), 0.0)
    v = jnp.where(valid, f32(v_ref[:, vs]), 0.0)
    krp = kr.astype(BF16)
    vp = v
    yield

    sc = _nt(qr, krp)
    s_old = st_ref[0, h]
    y_in = _nt(qr, s_old.astype(BF16))
    yield
    qi = lax.broadcasted_iota(jnp.int32, (lq, lq), 0)
    kj = lax.broadcasted_iota(jnp.int32, (lq, lq), 1)
    dm = jnp.exp(jnp.where(kj <= qi, lg * (cnt_i - cnt_j), NEG))
    y = _dot((sc * dm).astype(BF16), vp.astype(BF16))
    y = y + y_in * jnp.exp(lg * (cnt_i - cnt0))
    vend = vp * jnp.exp(lg * (cnt_end - cnt_jc))
    st_ref[0, h] = s_old * jnp.exp(lg * (cnt_end - cnt0)) + _tn(vend.astype(BF16), krp)
    yield

    ms = jnp.mean(y * y, axis=-1, keepdims=True)
    y = y * lax.rsqrt(ms + EPS) * ng_ref[:, vs] * _silu(f32(g_ref[:, vs]))
    y_ref[:, vs] = jnp.where(valid, y, 0.0)[:lb].astype(y_ref.dtype)
    yield


def ret_scan(proj, cos, sin, lg, s0, norm_g, *, bsz, tp, lq, lo, hi):
    m = proj.shape[0]
    nch = tp // lq
    d_inner = RET_HEADS * RET_V
    d_qk = RET_HEADS * RET_QK
    row = lambda b, c: b * nch + c
    state_spec = pl.BlockSpec((1, RET_HEADS, RET_V, RET_QK), lambda b, c: (b, 0, 0, 0))
    kern = functools.partial(_ret_kernel, lq=lq, lo=lo, hi=hi)
    return pl.pallas_call(
        kern,
        grid=(bsz, nch),
        in_specs=[
            pl.BlockSpec((lq, d_qk), lambda b, c: (row(b, c), 0)),
            pl.BlockSpec((lq, d_qk), lambda b, c: (row(b, c), 1)),
            pl.BlockSpec((lq, d_inner), lambda b, c: (row(b, c), 2 * d_qk // d_inner)),
            pl.BlockSpec((lq, d_inner), lambda b, c: (row(b, c), 2 * d_qk // d_inner + 1)),
            pl.BlockSpec((lq, RET_QK // 2), lambda b, c: (c, 0)),
            pl.BlockSpec((lq, RET_QK // 2), lambda b, c: (c, 0)),
            pl.BlockSpec((RET_HEADS, 8, 128), lambda b, c: (0, 0, 0)),
            state_spec,
            pl.BlockSpec((1, d_inner), lambda b, c: (0, 0)),
        ],
        out_specs=[
            pl.BlockSpec((lq, d_inner), lambda b, c: (row(b, c), 0)),
            state_spec,
        ],
        out_shape=[
            jax.ShapeDtypeStruct((m, d_inner), proj.dtype),
            jax.ShapeDtypeStruct((bsz, RET_HEADS, RET_V, RET_QK), F32),
        ],
        compiler_params=_cparams(("parallel", "arbitrary")),
        name="ret_scan",
    )(proj, proj, proj, proj, cos, sin, lg, s0, norm_g.reshape(1, -1))


def _rwkv_kernel(r_ref, k_ref, v_ref, g_ref, lw_ref, la_ref, bw_ref, ba_ref, par_ref, s0_ref,
                 y_ref, sout_ref, st, *, lo, hi, nchunks, ngrp):
    c = pl.program_id(2)
    w4 = RWKV_HPB * HEAD_DIM
    r2 = lax.broadcasted_iota(jnp.int32, (w4, w4), 0)
    c2 = lax.broadcasted_iota(jnp.int32, (w4, w4), 1)
    blk = _idiv(r2, HEAD_DIM) == _idiv(c2, HEAD_DIM)

    @pl.when(c == 0)
    def _():
        tile = (lax.broadcasted_iota(jnp.int32, (HEAD_DIM, w4), 0)
                == _imod(lax.broadcasted_iota(jnp.int32, (HEAD_DIM, w4), 1), HEAD_DIM)).astype(BF16)
        for gi in range(ngrp):
            s0 = s0_ref[0, gi * RWKV_HPB:(gi + 1) * RWKV_HPB].reshape(w4, HEAD_DIM)
            st[gi] = jnp.where(blk, _dot_split_l(s0, tile), 0.0)

    s_new = [None] * ngrp
    gens = [_rwkv_group(gi, r_ref, k_ref, v_ref, g_ref, lw_ref, la_ref, bw_ref, ba_ref, par_ref,
                        y_ref, st, s_new, lo=lo, hi=hi) for gi in range(ngrp)]
    for _ in zip(*gens):
        pass

    @pl.when(c == nchunks - 1)
    def _():
        tile_t = (_imod(lax.broadcasted_iota(jnp.int32, (w4, HEAD_DIM), 0), HEAD_DIM)
                  == lax.broadcasted_iota(jnp.int32, (w4, HEAD_DIM), 1)).astype(BF16)
        for gi in range(ngrp):
            sout_ref[0, gi * RWKV_HPB:(gi + 1) * RWKV_HPB] = _dot_split_l(s_new[gi], tile_t).reshape(
                RWKV_HPB, HEAD_DIM, HEAD_DIM)


def _rwkv_group(gi, r_ref, k_ref, v_ref, g_ref, lw_ref, la_ref, bw_ref, ba_ref, par_ref, y_ref, st, out,
                *, lo, hi):
    c = pl.program_id(2)
    cs = RWKV_CHUNK
    w4 = RWKV_HPB * HEAD_DIM
    sl = slice(gi * w4, (gi + 1) * w4)

    nr = RWKV_HPB * cs
    ones_bd = (_idiv(lax.broadcasted_iota(jnp.int32, (w4, w4), 0), HEAD_DIM)
               == _idiv(lax.broadcasted_iota(jnp.int32, (w4, w4), 1), HEAD_DIM)).astype(BF16)
    blk = (_idiv(lax.broadcasted_iota(jnp.int32, (nr, w4), 0), cs)
           == _idiv(lax.broadcasted_iota(jnp.int32, (nr, w4), 1), HEAD_DIM))
    r2 = lax.broadcasted_iota(jnp.int32, (nr, nr), 0)
    c2 = lax.broadcasted_iota(jnp.int32, (nr, nr), 1)
    same = _idiv(r2, cs) == _idiv(c2, cs)

    def segsum(x):
        return _dot_split_l(x, ones_bd, terms=2)

    par = par_ref[:, sl]
    w0, a0, k_k, k_a, r_k, ln_g, ln_b = (par[i:i + 1] for i in range(7))

    r = r_ref[0, :, sl].astype(F32)
    k = k_ref[0, :, sl].astype(F32)
    v = v_ref[0, :, sl].astype(F32)
    g = g_ref[0, :, sl].astype(F32)
    w_raw = w0 + _dot(jnp.tanh(lw_ref[0]).astype(BF16), bw_ref[0, :, sl])
    a = jax.nn.sigmoid(a0 + _dot(la_ref[0].astype(BF16), ba_ref[0, :, sl]))

    ti = lax.broadcasted_iota(jnp.int32, (cs, 1), 0)
    pos = c * cs + ti
    valid = (pos >= lo) & (pos < hi)

    lw = -jnp.exp(-_softplus(-w_raw) - 0.5)
    kk = k * k_k
    kk = kk / jnp.maximum(jnp.sqrt(segsum(kk * kk)), 1e-12)
    yield
    kp = k * (1.0 + (a - 1.0) * k_a)
    lw = jnp.where(valid, lw, 0.0)
    kk = jnp.where(valid, kk, 0.0)
    kp = jnp.where(valid, kp, 0.0)
    vm = jnp.where(valid, v, 0.0)

    tri = (lax.broadcasted_iota(jnp.int32, (cs, cs), 0) >= lax.broadcasted_iota(jnp.int32, (cs, cs), 1)).astype(BF16)
    cw = _dot_split_r(tri, lw)
    yield
    cwl = cw[cs - 1:cs]
    wt = jnp.exp(cw)
    wi = jnp.exp(-cw)
    wend = jnp.exp(cwl - cw)
    b = kk * a
    at = -kk * jnp.exp(cw - lw)
    rt = r * wt

    def bd(x):
        return jnp.where(blk, jnp.concatenate([x] * RWKV_HPB, axis=0), 0.0).astype(BF16)

    lhs = jnp.concatenate([bd(at), bd(rt)], axis=0)
    rhs = jnp.concatenate([bd(b * wi), bd(kp * wi)], axis=0)
    sc = _nt(lhs, rhs)
    yield
    tt = _imod(r2, cs)
    jj = _imod(c2, cs)
    strict = same & (tt > jj)
    incl = same & (tt >= jj)
    mab = jnp.where(strict, sc[:nr, :nr], 0.0)
    mak = jnp.where(strict, sc[:nr, nr:], 0.0)
    nrb = jnp.where(incl, sc[nr:, :nr], 0.0)
    nrk = jnp.where(incl, sc[nr:, nr:], 0.0)

    tinv = (r2 == c2).astype(F32) + jnp.where(
        (_idiv(r2, 2) == _idiv(c2, 2)) & (_imod(tt, 2) == 1) & (_imod(jj, 2) == 0), mab, 0.0)
    msz = 2
    while msz < cs:
        off = ((_idiv(r2, 2 * msz) == _idiv(c2, 2 * msz)) & (_imod(tt, 2 * msz) >= msz)
               & (_imod(jj, 2 * msz) < msz))
        tb = tinv.astype(BF16)
        tno = _dot(tb, jnp.where(off, mab, 0.0).astype(BF16)).astype(BF16)
        yield
        tinv = tinv + _dot(tno, tb)
        yield
        msz *= 2

    s_old = st[gi]
    x = _nt(lhs, s_old.astype(BF16))
    yield
    vbd = bd(vm)
    u = _dot(tinv.astype(BF16), (x[:nr] + _dot(mak.astype(BF16), vbd)).astype(BF16))
    yield
    ub = u.astype(BF16)
    yb = x[nr:] + _dot(nrb.astype(BF16), ub) + _dot(nrk.astype(BF16), vbd)
    yield
    s_new = s_old * wt[cs - 1:cs] + _tn(jnp.concatenate([ub, vbd], axis=0),
                                        jnp.concatenate([bd(b * wend), bd(kp * wend)], axis=0))
    st[gi] = s_new
    out[gi] = s_new
    yield

    y = sum(yb[i * cs:(i + 1) * cs] for i in range(RWKV_HPB))
    inv = 1.0 / HEAD_DIM
    yc = y - segsum(y) * inv
    yield
    y = yc * lax.rsqrt(segsum(yc * yc) * inv + RWKV_LN_EPS) * ln_g + ln_b
    y = (y + segsum(r * kp * r_k) * v) * _silu(g)
    y_ref[:, sl] = jnp.where(valid, y, 0.0).astype(BF16)
    yield


def rwkv_scan(rkvg, lora1, lora_b, par, s0, *, bsz, tp, lo, hi):
    _, m, e = rkvg.shape
    cs = RWKV_CHUNK
    nch = tp // cs
    heads = e // HEAD_DIM
    ngrp = RWKV_GROUPS_PER_STEP
    hb = RWKV_HPB * ngrp
    wb = hb * HEAD_DIM
    row = lambda b, h, c: b * nch + c
    kern = functools.partial(_rwkv_kernel, lo=lo, hi=hi, nchunks=nch, ngrp=ngrp)
    proj_spec = lambda s: pl.BlockSpec((1, cs, wb), lambda b, h, c: (s, row(b, h, c), h))
    return pl.pallas_call(
        kern,
        grid=(bsz, heads // hb, nch),
        in_specs=[
            proj_spec(0), proj_spec(1), proj_spec(2), proj_spec(3),
            pl.BlockSpec((1, cs, RWKV_LORA_PAD), lambda b, h, c: (0, row(b, h, c), 0)),
            pl.BlockSpec((1, cs, RWKV_LORA_PAD), lambda b, h, c: (1, row(b, h, c), 0)),
            pl.BlockSpec((1, RWKV_LORA_PAD, wb), lambda b, h, c: (0, 0, h)),
            pl.BlockSpec((1, RWKV_LORA_PAD, wb), lambda b, h, c: (1, 0, h)),
            pl.BlockSpec((8, wb), lambda b, h, c: (0, h)),
            pl.BlockSpec((1, hb, HEAD_DIM, HEAD_DIM), lambda b, h, c: (b, h, 0, 0)),
        ],
        out_specs=[
            pl.BlockSpec((cs, wb), lambda b, h, c: (row(b, h, c), h)),
            pl.BlockSpec((1, hb, HEAD_DIM, HEAD_DIM), lambda b, h, c: (b, h, 0, 0)),
        ],
        out_shape=[
            jax.ShapeDtypeStruct((m, e), BF16),
            jax.ShapeDtypeStruct((bsz, heads, HEAD_DIM, HEAD_DIM), F32),
        ],
        scratch_shapes=[pltpu.VMEM((ngrp, RWKV_HPB * HEAD_DIM, RWKV_HPB * HEAD_DIM), F32)],
        compiler_params=_cparams(("parallel", "parallel", "arbitrary")),
        name="rwkv_scan",
    )(rkvg, rkvg, rkvg, rkvg, lora1, lora1, lora_b, lora_b, par, s0)


def _rwkv_short_kernel(r_ref, k_ref, v_ref, g_ref, lw_ref, la_ref, bw_ref, ba_ref, par_ref, s0_ref,
                       ones_ref, tile_ref, y_ref, sout_ref, *, nvalid, rows_out):
    ct = 8
    hp = RWKV_SHORT_HEADS
    wl = hp * HEAD_DIM
    nr = hp * ct
    ones_bd = ones_ref[...]
    tile_t = tile_ref[...]

    def segsum(x):
        return jnp.concatenate(
            [_dot_split_l(x[:, 256 * j:256 * (j + 1)], ones_bd, terms=2) for j in range(wl // 256)], axis=1)

    par = par_ref[...]
    w0, a0, k_k, k_a, r_k, ln_g, ln_b = (par[i:i + 1] for i in range(7))
    r, k, v, g = r_ref[0], k_ref[0], v_ref[0], g_ref[0]
    w_raw = w0 + _dot(jnp.tanh(lw_ref[0]).astype(BF16), bw_ref[0])
    a = jax.nn.sigmoid(a0 + _dot(la_ref[0].astype(BF16), ba_ref[0]))

    ti = lax.broadcasted_iota(jnp.int32, (ct, 1), 0)
    valid = ti < nvalid
    lw = jnp.where(valid, -jnp.exp(-_softplus(-w_raw) - 0.5), 0.0)
    kk = k * k_k
    kk = jnp.where(valid, kk / jnp.maximum(jnp.sqrt(segsum(kk * kk)), 1e-12), 0.0)
    kp = jnp.where(valid, k * (1.0 + (a - 1.0) * k_a), 0.0)
    vm = jnp.where(valid, v, 0.0)

    cw = lw
    for s in (1, 2, 4):
        cw = cw + jnp.where(ti >= s, pltpu.roll(cw, s, 0), 0.0)
    cwl = cw[ct - 1:ct]
    wend = jnp.exp(cwl - cw)
    wi = jnp.exp(-cw)
    b = kk * a
    at = -kk * jnp.exp(cw - lw)
    rt = r * jnp.exp(cw)

    rr = lax.broadcasted_iota(jnp.int32, (nr, wl), 0)
    cc = lax.broadcasted_iota(jnp.int32, (nr, wl), 1)
    blk = _idiv(rr, ct) == _idiv(cc, HEAD_DIM)

    def bd(x):
        return jnp.where(blk, jnp.concatenate([x] * hp, axis=0), 0.0).astype(BF16)

    lhs = jnp.concatenate([bd(at), bd(rt)], axis=0)
    rhs = jnp.concatenate([bd(b * wi), bd(kp * wi)], axis=0)
    sc = _nt(lhs, rhs)
    ri = lax.broadcasted_iota(jnp.int32, (2 * nr, 2 * nr), 0)
    ci = lax.broadcasted_iota(jnp.int32, (2 * nr, 2 * nr), 1)
    same = _idiv(_imod(ri, nr), ct) == _idiv(_imod(ci, nr), ct)
    tt = _imod(ri, ct)
    jj = _imod(ci, ct)
    sc = jnp.where(same & (tt + (ri >= nr).astype(jnp.int32) > jj), sc, 0.0)
    mab, mak, nrb, nrk = sc[:nr, :nr], sc[:nr, nr:], sc[nr:, :nr], sc[nr:, nr:]

    r1 = lax.broadcasted_iota(jnp.int32, (nr, nr), 0)
    c1 = lax.broadcasted_iota(jnp.int32, (nr, nr), 1)
    t1 = _imod(r1, ct)
    j1 = _imod(c1, ct)
    tinv = (r1 == c1).astype(F32) + jnp.where(
        (_idiv(r1, 2) == _idiv(c1, 2)) & (_imod(t1, 2) == 1) & (_imod(j1, 2) == 0), mab, 0.0)
    msz = 2
    while msz < nvalid:
        off = ((_idiv(r1, 2 * msz) == _idiv(c1, 2 * msz)) & (_imod(t1, 2 * msz) >= msz)
               & (_imod(j1, 2 * msz) < msz))
        tb = tinv.astype(BF16)
        tinv = tinv + _dot(_dot(tb, jnp.where(off, mab, 0.0).astype(BF16)).astype(BF16), tb)
        msz *= 2

    s_old = s0_ref[0].reshape(wl, HEAD_DIM)
    lhs_rows = _dot(lhs, tile_t).astype(BF16)
    x = _nt(lhs_rows, s_old.astype(BF16))
    xa = jnp.where(blk, x[:nr], 0.0)
    xr = jnp.where(blk, x[nr:], 0.0)
    vbd = bd(vm)
    u = _dot(tinv.astype(BF16), (xa + _dot(mak.astype(BF16), vbd)).astype(BF16))
    uv = jnp.concatenate([u.astype(BF16), vbd], axis=0)
    yb = xr + _dot(jnp.concatenate([nrb, nrk], axis=1).astype(BF16), uv)
    bk_rows = _dot(jnp.concatenate([bd(b * wend), bd(kp * wend)], axis=0), tile_t).astype(BF16)
    ds = _tn(uv, bk_rows)

    r16 = lax.broadcasted_iota(jnp.int32, (hp, wl), 0)
    c16 = lax.broadcasted_iota(jnp.int32, (hp, wl), 1)
    wc = jnp.exp(cwl)
    wc_rows = _dot_split_l(jnp.where(r16 == _idiv(c16, HEAD_DIM), wc, 0.0), tile_t)
    rsel = lax.broadcasted_iota(jnp.int32, (hp, HEAD_DIM), 0)
    for h in range(hp):
        rs = slice(h * HEAD_DIM, (h + 1) * HEAD_DIM)
        wc_h = jnp.sum(jnp.where(rsel == h, wc_rows, 0.0), axis=0, keepdims=True)
        sout_ref[0, h] = s_old[rs] * wc_h + ds[rs]

    y = yb[0:ct]
    for h in range(1, hp):
        y = y + yb[h * ct:(h + 1) * ct]
    inv = 1.0 / HEAD_DIM
    yc = y - segsum(y) * inv
    y = yc * lax.rsqrt(segsum(yc * yc) * inv + RWKV_LN_EPS) * ln_g + ln_b
    y = (y + segsum(r * kp * r_k) * v) * _silu(g)
    y = jnp.where(valid, y, 0.0)
    y_ref[...] = _pad_rows(y, rows_out).astype(y_ref.dtype)


def rwkv_short(rkvg, lora1, lora_b, par, s0, *, bsz, tp, hi):
    _, m, e = rkvg.shape
    heads = e // HEAD_DIM
    hp = RWKV_SHORT_HEADS
    wl = hp * HEAD_DIM
    rb = tp // 8
    w4 = RWKV_HPB * HEAD_DIM
    ones_bd = (jnp.arange(w4)[:, None] // HEAD_DIM == jnp.arange(w4)[None, :] // HEAD_DIM).astype(BF16)
    tile_t = (jnp.arange(wl)[:, None] % HEAD_DIM == jnp.arange(HEAD_DIM)[None, :]).astype(BF16)
    kern = functools.partial(_rwkv_short_kernel, nvalid=hi, rows_out=tp)
    proj_spec = lambda s: pl.BlockSpec((1, 8, wl), lambda b, h: (s, b * rb, h))
    return pl.pallas_call(
        kern,
        grid=(bsz, heads // hp),
        in_specs=[
            proj_spec(0), proj_spec(1), proj_spec(2), proj_spec(3),
            pl.BlockSpec((1, 8, RWKV_LORA_PAD), lambda b, h: (0, b * rb, 0)),
            pl.BlockSpec((1, 8, RWKV_LORA_PAD), lambda b, h: (1, b * rb, 0)),
            pl.BlockSpec((1, RWKV_LORA_PAD, wl), lambda b, h: (0, 0, h)),
            pl.BlockSpec((1, RWKV_LORA_PAD, wl), lambda b, h: (1, 0, h)),
            pl.BlockSpec((8, wl), lambda b, h: (0, h)),
            pl.BlockSpec((1, hp, HEAD_DIM, HEAD_DIM), lambda b, h: (b, h, 0, 0)),
            pl.BlockSpec((w4, w4), lambda b, h: (0, 0)),
            pl.BlockSpec((wl, HEAD_DIM), lambda b, h: (0, 0)),
        ],
        out_specs=[
            pl.BlockSpec((tp, wl), lambda b, h: (b, h)),
            pl.BlockSpec((1, hp, HEAD_DIM, HEAD_DIM), lambda b, h: (b, h, 0, 0)),
        ],
        out_shape=[
            jax.ShapeDtypeStruct((m, e), rkvg.dtype),
            jax.ShapeDtypeStruct((bsz, heads, HEAD_DIM, HEAD_DIM), F32),
        ],
        compiler_params=_cparams(("parallel", "parallel")),
        name="rwkv_short",
    )(rkvg, rkvg, rkvg, rkvg, lora1, lora1, lora_b, lora_b, par, s0, ones_bd, tile_t)


def _rwkv_lanes_kernel(r_ref, k_ref, v_ref, g_ref, lw_ref, la_ref, bw_ref, ba_ref, par_ref, s0_ref,
                       y_ref, sout_ref, tk, tw, tb, tq, tr, tv, ty, *, nt, nb):
    hp = RWKV_LANE_HEADS
    wl = hp * HEAD_DIM
    ri = lax.broadcasted_iota(jnp.int32, (wl, wl), 0)
    ci = lax.broadcasted_iota(jnp.int32, (wl, wl), 1)
    ones_bd = (_idiv(ri, HEAD_DIM) == _idiv(ci, HEAD_DIM)).astype(BF16)

    def segsum(x):
        return _dot_split_l(x, ones_bd, terms=2)

    par = par_ref[...]
    w0, a0, k_k, k_a, r_k, ln_g, ln_b = (par[i:i + 1] for i in range(7))
    r, k, v, g = (x[0].astype(F32) for x in (r_ref, k_ref, v_ref, g_ref))
    w_raw = w0 + _dot(jnp.tanh(lw_ref[0]).astype(BF16), bw_ref[0])
    a = jax.nn.sigmoid(a0 + _dot(la_ref[0].astype(BF16), ba_ref[0]))
    decay = jnp.exp(-jnp.exp(-_softplus(-w_raw) - 0.5))
    kk = k * k_k
    kk = kk / jnp.maximum(jnp.sqrt(segsum(kk * kk)), 1e-12)
    kp = k * (1.0 + (a - 1.0) * k_a)
    bb = kk * a

    for t in range(nt):
        rows = slice(t * nb, (t + 1) * nb)
        tk[t] = (-kk[rows]).T
        tw[t] = decay[rows].T
        tb[t] = bb[rows].T
        tq[t] = kp[rows].T
        tr[t] = r[rows].T
        tv[t] = v[rows].T

    for hh in range(hp):
        ks = slice(hh * HEAD_DIM, (hh + 1) * HEAD_DIM)

        def body(vi, carry, hh=hh, ks=ks):
            sv = s0_ref[hh, vi]
            row = hh * HEAD_DIM + vi
            for t in range(nt):
                sa = jnp.sum(sv * tk[t, ks, :], axis=0, keepdims=True)
                sv = sv * tw[t, ks, :] + sa * tb[t, ks, :] + tv[t, pl.ds(row, 1), :] * tq[t, ks, :]
                ty[t, pl.ds(row, 1), :] = jnp.sum(sv * tr[t, ks, :], axis=0, keepdims=True)
            sout_ref[hh, vi] = sv
            return carry

        lax.fori_loop(0, HEAD_DIM, body, 0, unroll=4)

    inv = 1.0 / HEAD_DIM
    for t in range(nt):
        rows = slice(t * nb, (t + 1) * nb)
        y = ty[t].T
        yc = y - segsum(y) * inv
        y = yc * lax.rsqrt(segsum(yc * yc) * inv + RWKV_LN_EPS) * ln_g + ln_b
        y = (y + segsum(r[rows] * kp[rows] * r_k) * v[rows]) * _silu(g[rows])
        y_ref[rows, :] = y.astype(BF16)


def rwkv_lanes(rkvg, lora1, lora_b, par, s0t, *, nt, nb):
    _, m, e = rkvg.shape
    heads = e // HEAD_DIM
    hp = RWKV_LANE_HEADS
    wl = hp * HEAD_DIM
    kern = functools.partial(_rwkv_lanes_kernel, nt=nt, nb=nb)
    proj_spec = lambda s: pl.BlockSpec((1, m, wl), lambda h: (s, 0, h))
    state_spec = pl.BlockSpec((hp, HEAD_DIM, HEAD_DIM, nb), lambda h: (h, 0, 0, 0))
    tile = pltpu.VMEM((nt, wl, nb), F32)
    return pl.pallas_call(
        kern,
        grid=(heads // hp,),
        in_specs=[
            proj_spec(0), proj_spec(1), proj_spec(2), proj_spec(3),
            pl.BlockSpec((1, m, RWKV_LORA_PAD), lambda h: (0, 0, 0)),
            pl.BlockSpec((1, m, RWKV_LORA_PAD), lambda h: (1, 0, 0)),
            pl.BlockSpec((1, RWKV_LORA_PAD, wl), lambda h: (0, 0, h)),
            pl.BlockSpec((1, RWKV_LORA_PAD, wl), lambda h: (1, 0, h)),
            pl.BlockSpec((8, wl), lambda h: (0, h)),
            state_spec,
        ],
        out_specs=[pl.BlockSpec((m, wl), lambda h: (0, h)), state_spec],
        out_shape=[jax.ShapeDtypeStruct((m, e), BF16), jax.ShapeDtypeStruct(s0t.shape, F32)],
        scratch_shapes=[tile] * 7,
        compiler_params=_cparams(("parallel",)),
        name="rwkv_lanes",
    )(rkvg, rkvg, rkvg, rkvg, lora1, lora1, lora_b, lora_b, par, s0t)


def _prep_weights(norm_g, ssd_w_in, ssd_dt_bias, ssd_a_log, ssd_d, ssd_w_out,
                  rwkv_w_rkvg, rwkv_w_lora_a, rwkv_w_lora_b, rwkv_a_lora_a, rwkv_a_lora_b,
                  rwkv_w0, rwkv_a0, rwkv_k_k, rwkv_k_a, rwkv_r_k, rwkv_ln_g, rwkv_ln_b, rwkv_w_out,
                  ret_w_in, ret_w_out):
    d_inner = ssd_w_out.shape[1]
    n_main = d_inner + d_inner + 2 * SSD_GROUPS * SSD_STATE
    ns = ssd_w_in.shape[0]
    d_model = ssd_w_in.shape[1]

    def head_lanes(p):
        p = p.reshape(ns, SSD_GROUPS, 1, SSD_HPG)
        return jnp.pad(p, ((0, 0), (0, 0), (0, 0), (0, 128 - SSD_HPG)))

    w_dt = ssd_w_in[:, :, n_main:].reshape(ns, d_model, SSD_GROUPS, SSD_HPG)
    w_dt = jnp.pad(w_dt, ((0, 0), (0, 0), (0, 0), (0, 128 - SSD_HPG))).reshape(ns, d_model, SSD_GROUPS * 128)
    rank = rwkv_w_lora_a.shape[2]
    lora_a = jnp.stack([rwkv_w_lora_a, rwkv_a_lora_a], axis=1)
    lora_a = jnp.pad(lora_a, ((0, 0), (0, 0), (0, 0), (0, RWKV_LORA_PAD - rank)))
    lora_b = jnp.stack([rwkv_w_lora_b, rwkv_a_lora_b], axis=1)
    lora_b = jnp.pad(lora_b, ((0, 0), (0, 0), (0, RWKV_LORA_PAD - rank), (0, 0)))
    nr = rwkv_w0.shape[0]
    par = jnp.stack([rwkv_w0, rwkv_a0, rwkv_k_k, rwkv_k_a, rwkv_r_k.reshape(nr, -1), rwkv_ln_g, rwkv_ln_b,
                     jnp.zeros_like(rwkv_w0)], axis=1)
    return dict(
        ssd_w_main=ssd_w_in[:, :, :n_main].astype(BF16), ssd_w_dt=w_dt.astype(BF16),
        ssd_dtb=head_lanes(ssd_dt_bias), ssd_alog=head_lanes(ssd_a_log), ssd_dskip=head_lanes(ssd_d),
        ssd_w_out=ssd_w_out.astype(BF16),
        rwkv_w=rwkv_w_rkvg.astype(BF16), rwkv_lora_a=lora_a.astype(BF16), rwkv_lora_b=lora_b.astype(BF16),
        rwkv_par=par, rwkv_w_out=rwkv_w_out.astype(BF16),
        ret_w_in=ret_w_in.astype(BF16), ret_w_out=ret_w_out.astype(BF16),
    )


def _trunk(h, conv_st, ssd_st, shift_st, wkv_st, ret_st, pos, *, bsz, tp, lq, lo, hi, depth,
           norm_g, final_norm_g, wts, ssd_conv_w, ssd_conv_b, ssd_norm_g, rwkv_mu, ret_norm_g):
    d_model = h.shape[1]
    geo = dict(bsz=bsz, tp=tp, lo=lo, hi=hi)
    new_conv, new_shift, new_wkv, new_ret = [], [], [], []
    new_ssd = None
    act = BF16 if lq % 16 == 0 else F32

    half = RET_QK // 2
    inv_freq = 1.0 / (RET_THETA_BASE ** jnp.linspace(0.0, 1.0, half, dtype=F32))
    ang = pos.astype(F32)[:, None] * inv_freq
    cos, sin = jnp.cos(ang), jnp.sin(ang)
    log_gamma = jnp.log1p(-jnp.exp2(-5.0 - jnp.arange(RET_HEADS, dtype=F32)))
    lg = jnp.broadcast_to(log_gamma[:, None, None], (RET_HEADS, 8, 128))

    for layer in range(depth):
        kind, j = layer % 3, layer // 3
        ng = norm_g[layer]
        if kind == 0:
            proj = matmul(h, wts["ssd_w_main"][j], out_dtype=act, norm_g=ng)
            dt_raw = matmul(h, wts["ssd_w_dt"][j], norm_g=ng)
            conv_init = jnp.pad(conv_st[j], ((0, 0), (8 - (SSD_CONV - 1), 0), (0, 0)))
            y, new_ssd = ssd_scan(proj, dt_raw, conv_init, ssd_st, j, new_ssd, ssd_conv_w[j], ssd_conv_b[j],
                                  wts["ssd_dtb"][j], wts["ssd_alog"][j], wts["ssd_dskip"][j], ssd_norm_g[j],
                                  lq=lq, **geo)
            nk = SSD_CONV - 1
            last = h.reshape(bsz, tp, d_model)[:, hi - nk:hi].reshape(bsz * nk, d_model)
            last = jnp.pad(last, ((0, -(bsz * nk) % 16), (0, 0)))
            xbc = matmul(last, wts["ssd_w_main"][j], norm_g=ng)[:bsz * nk, y.shape[1]:]
            new_conv.append(xbc.reshape(bsz, nk, -1))
            h = matmul(y, wts["ssd_w_out"][j], res=h)
        elif kind == 1 and lo == 0 and hi <= 8 and bsz % LANES == 0:
            u = rmsnorm(h, ng)
            tmajor = lambda x: jnp.swapaxes(x.reshape(bsz, tp, -1)[:, :hi], 0, 1)
            uc = tmajor(u)
            prev = jnp.concatenate([shift_st[j][None], uc[:-1]], axis=0).reshape(hi * bsz, d_model)
            uc = uc.reshape(hi * bsz, d_model)
            rkvg = mix_matmul(uc, prev, rwkv_mu[j][:4], wts["rwkv_w"][j], out_dtype=BF16)
            lora1 = mix_matmul(uc, prev, rwkv_mu[j][4:], wts["rwkv_lora_a"][j])
            y, s_t = rwkv_lanes(rkvg, lora1, wts["rwkv_lora_b"][j], wts["rwkv_par"][j],
                                jnp.transpose(wkv_st[j], (1, 2, 3, 0)), nt=hi, nb=bsz)
            new_shift.append(u.reshape(bsz, tp, d_model)[:, hi - 1])
            new_wkv.append(jnp.transpose(s_t, (3, 0, 1, 2)))
            hc = matmul(y, wts["rwkv_w_out"][j], res=tmajor(h).reshape(hi * bsz, d_model))
            hc = jnp.swapaxes(hc.reshape(hi, bsz, d_model), 0, 1)
            h = jnp.pad(hc, ((0, 0), (0, tp - hi), (0, 0))).reshape(bsz * tp, d_model)
        elif kind == 1:
            u = rmsnorm(h, ng)
            u3 = u.reshape(bsz, tp, d_model)
            prev = jnp.concatenate([shift_st[j][:, None, :], u3[:, :-1]], axis=1).reshape(bsz * tp, d_model)
            rkvg = mix_matmul(u, prev, rwkv_mu[j][:4], wts["rwkv_w"][j], out_dtype=BF16)
            lora1 = mix_matmul(u, prev, rwkv_mu[j][4:], wts["rwkv_lora_a"][j])
            if lo == 0 and hi <= 8 and tp <= 16:
                y, s_new = rwkv_short(rkvg.astype(F32), lora1, wts["rwkv_lora_b"][j], wts["rwkv_par"][j], wkv_st[j],
                                      bsz=bsz, tp=tp, hi=hi)
            else:
                y, s_new = rwkv_scan(rkvg, lora1, wts["rwkv_lora_b"][j], wts["rwkv_par"][j], wkv_st[j], **geo)
            new_shift.append(u3[:, hi - 1])
            new_wkv.append(s_new)
            h = matmul(y, wts["rwkv_w_out"][j], res=h)
        else:
            proj = matmul(h, wts["ret_w_in"][j], out_dtype=act, norm_g=ng)
            y, s_new = ret_scan(proj, cos, sin, lg, ret_st[j], ret_norm_g[j], lq=lq, **geo)
            new_ret.append(s_new)
            h = matmul(y, wts["ret_w_out"][j], res=h)
    y = rmsnorm(h, final_norm_g)
    return (y, jnp.stack(new_conv), new_ssd, jnp.stack(new_shift), jnp.stack(new_wkv), jnp.stack(new_ret))


def kernel(x_prompt, x_sample, state_ssd_conv, state_ssd, state_rwkv_shift, state_rwkv_wkv, state_ret, meta_tokens, norm_g, final_norm_g, ssd_w_in, ssd_conv_w, ssd_conv_b, ssd_dt_bias, ssd_a_log, ssd_d, ssd_norm_g, ssd_w_out, rwkv_mu, rwkv_w_rkvg, rwkv_w0, rwkv_w_lora_a, rwkv_w_lora_b, rwkv_a0, rwkv_a_lora_a, rwkv_a_lora_b, rwkv_k_k, rwkv_k_a, rwkv_r_k, rwkv_ln_g, rwkv_ln_b, rwkv_w_out, ret_w_in, ret_norm_g, ret_w_out):
    depth = norm_g.shape[0]
    d_model = x_prompt.shape[2]
    wts = _prep_weights(norm_g, ssd_w_in, ssd_dt_bias, ssd_a_log, ssd_d, ssd_w_out,
                        rwkv_w_rkvg, rwkv_w_lora_a, rwkv_w_lora_b, rwkv_a_lora_a, rwkv_a_lora_b,
                        rwkv_w0, rwkv_a0, rwkv_k_k, rwkv_k_a, rwkv_r_k, rwkv_ln_g, rwkv_ln_b, rwkv_w_out,
                        ret_w_in, ret_w_out)
    common = dict(depth=depth, norm_g=norm_g, final_norm_g=final_norm_g, wts=wts, ssd_conv_w=ssd_conv_w,
                  ssd_conv_b=ssd_conv_b, ssd_norm_g=ssd_norm_g, rwkv_mu=rwkv_mu, ret_norm_g=ret_norm_g)

    bp, seq, _ = x_prompt.shape
    lq_p = 128
    lo_p = lq_p - N_META
    tp_p = lo_p + N_META + seq
    h_p = jnp.concatenate([jnp.zeros((bp, lo_p, d_model), F32),
                           jnp.broadcast_to(meta_tokens[None], (bp, N_META, d_model)), x_prompt], axis=1)
    zeros_like_b = lambda s: jnp.zeros((s.shape[0], bp) + s.shape[2:], F32)
    pos_p = jnp.maximum(jnp.arange(tp_p) - lo_p, 0)
    outs_p = _trunk(h_p.reshape(bp * tp_p, d_model), zeros_like_b(state_ssd_conv), zeros_like_b(state_ssd),
                    zeros_like_b(state_rwkv_shift), zeros_like_b(state_rwkv_wkv), zeros_like_b(state_ret), pos_p,
                    bsz=bp, tp=tp_p, lq=lq_p, lo=lo_p, hi=tp_p, **common)
    y_prompt = outs_p[0].reshape(bp, tp_p, d_model)[:, lo_p + N_META:]

    bs, ds, _ = x_sample.shape
    tp_s = 8
    h_s = jnp.concatenate([x_sample, jnp.zeros((bs, tp_s - ds, d_model), F32)], axis=1)
    pos_s = PAST_LEN + jnp.arange(tp_s)
    outs_s = _trunk(h_s.reshape(bs * tp_s, d_model), state_ssd_conv, state_ssd, state_rwkv_shift, state_rwkv_wkv,
                    state_ret, pos_s, bsz=bs, tp=tp_s, lq=tp_s, lo=0, hi=ds, **common)
    y_sample = outs_s[0].reshape(bs, tp_s, d_model)[:, :ds]

    return (y_prompt, y_sample) + tuple(outs_p[1:]) + tuple(outs_s[1:])
```

```python
import functools
import math

import jax
import jax.numpy as jnp
from jax import lax
from jax.experimental import pallas as pl
from jax.experimental.pallas import tpu as pltpu

F32 = jnp.float32
BF16 = jnp.bfloat16

EPS = 1e-6
N_META = 16
HEAD_DIM = 64
SSD_STATE = 128
SSD_GROUPS = 8
SSD_HPG = 8
SSD_CONV = 4
RET_HEADS = 8
RET_QK = 256
RET_V = 512
RET_THETA_BASE = 10000.0
RWKV_LORA_PAD = 128
RWKV_CHUNK = 64
RWKV_HPB = 4
RWKV_GROUPS_PER_STEP = 8
RWKV_SHORT_HEADS = 16
RWKV_LANE_HEADS = 2
LANES = 128
MIN_CHUNK_ROWS = 16
PAST_LEN = 16384
RWKV_LN_EPS = 1e-5 * HEAD_DIM
NEG = -1e30
ROW_TILE = 512
V7X_VMEM_BYTES = 64 * 1024 * 1024
VMEM_LIMIT = V7X_VMEM_BYTES * 7 // 8
MATMUL_VMEM_BUDGET = V7X_VMEM_BYTES * 5 // 8


def _cparams(sem):
    return pltpu.CompilerParams(dimension_semantics=sem, vmem_limit_bytes=VMEM_LIMIT)


def _nt(a, b):
    return lax.dot_general(a, b, (((1,), (1,)), ((), ())), preferred_element_type=F32)


def _tn(a, b):
    return lax.dot_general(a, b, (((0,), (0,)), ((), ())), preferred_element_type=F32)


def _dot(a, b):
    return jnp.dot(a, b, preferred_element_type=F32)


def _split(x, terms):
    parts = []
    r = x
    for i in range(terms):
        p = r.astype(BF16)
        parts.append(p)
        if i + 1 < terms:
            r = r - p.astype(F32)
    return parts


def _dot_split_l(x, m, terms=3):
    acc = None
    for p in _split(x, terms):
        d = _dot(p, m)
        acc = d if acc is None else acc + d
    return acc


def _dot_split_r(m, x, terms=3):
    acc = None
    for p in _split(x, terms):
        d = _dot(m, p)
        acc = d if acc is None else acc + d
    return acc


def _pad_rows(x, rows):
    if x.shape[0] == rows:
        return x
    return jnp.concatenate([x, jnp.zeros((rows - x.shape[0], x.shape[1]), x.dtype)], axis=0)


def _idiv(x, n):
    return jnp.right_shift(x, int(math.log2(n)))


def _imod(x, n):
    return jnp.bitwise_and(x, n - 1)


def _silu(x):
    h = 0.5 * x
    return h + h * jnp.tanh(h)


def _softplus(x):
    return jnp.maximum(x, 0.0) + jnp.log(1.0 + jnp.exp(-jnp.abs(x)))


def _rmsnorm_kernel(x_ref, g_ref, o_ref):
    x = x_ref[...]
    ms = jnp.mean(x * x, axis=-1, keepdims=True)
    o_ref[...] = (x * lax.rsqrt(ms + EPS) * g_ref[...]).astype(o_ref.dtype)


def _row_tile(m):
    tm = math.gcd(m, ROW_TILE)
    assert tm % 16 == 0, m
    return tm


def _matmul_tiles(m, k, n, *, a_bytes, n_a, cast, n_out):
    best = None
    for tn in (t for t in (1024, 512, 256, 128) if n % t == 0):
        for tm in (t for t in range(16, m + 1, 16) if m % t == 0):
            need = (2 * n_a * tm * k * a_bytes + (tm * k * 2 if cast else 0)
                    + 2 * k * tn * 2 + 2 * n_out * tm * tn * 4)
            if need <= MATMUL_VMEM_BUDGET and (best is None or (tm * tn, tm) > (best[0] * best[1], best[0])):
                best = (tm, tn)
    assert best is not None, (m, k, n)
    return best


def rmsnorm(x, g, out_dtype=F32):
    m, d = x.shape
    tm = _row_tile(m)
    return pl.pallas_call(
        _rmsnorm_kernel,
        grid=(m // tm,),
        in_specs=[pl.BlockSpec((tm, d), lambda i: (i, 0)), pl.BlockSpec((1, d), lambda i: (0, 0))],
        out_specs=pl.BlockSpec((tm, d), lambda i: (i, 0)),
        out_shape=jax.ShapeDtypeStruct((m, d), out_dtype),
        compiler_params=_cparams(("parallel",)),
        name="rmsnorm",
    )(x, g.reshape(1, d))


def _mm_kernel(a_ref, w_ref, *rest, has_res, cast, norm):
    rest = list(rest)
    abf_ref = rest.pop() if cast else a_ref
    o_ref = rest.pop()

    if cast:
        @pl.when(pl.program_id(1) == 0)
        def _():
            x = a_ref[...]
            if norm:
                x = x * lax.rsqrt(jnp.mean(x * x, axis=-1, keepdims=True) + EPS) * rest[0][...]
            abf_ref[...] = x.astype(BF16)

    acc = _dot(abf_ref[...], w_ref[...])
    if has_res:
        acc = rest[-1][...] + acc
    o_ref[...] = acc.astype(o_ref.dtype)


def matmul(a, w, res=None, out_dtype=F32, norm_g=None):
    m, k = a.shape
    n = w.shape[1]
    cast = a.dtype != BF16 or norm_g is not None
    tm, tn = _matmul_tiles(m, k, n, a_bytes=a.dtype.itemsize, n_a=1, cast=cast, n_out=2 if res is not None else 1)
    in_specs = [pl.BlockSpec((tm, k), lambda i, j: (i, 0)), pl.BlockSpec((k, tn), lambda i, j: (0, j))]
    args = [a, w]
    if norm_g is not None:
        in_specs.append(pl.BlockSpec((1, k), lambda i, j: (0, 0)))
        args.append(norm_g.reshape(1, k))
    if res is not None:
        in_specs.append(pl.BlockSpec((tm, tn), lambda i, j: (i, j)))
        args.append(res)
    return pl.pallas_call(
        functools.partial(_mm_kernel, has_res=res is not None, cast=cast, norm=norm_g is not None),
        grid=(m // tm, n // tn),
        in_specs=in_specs,
        out_specs=pl.BlockSpec((tm, tn), lambda i, j: (i, j)),
        out_shape=jax.ShapeDtypeStruct((m, n), out_dtype),
        scratch_shapes=[pltpu.VMEM((tm, k), BF16)] if cast else [],
        compiler_params=_cparams(("parallel", "arbitrary")),
        name="matmul_res" if res is not None else "matmul",
    )(*args)


def _mixmm_kernel(u_ref, p_ref, mu_ref, w_ref, o_ref, xm_ref):
    @pl.when(pl.program_id(2) == 0)
    def _():
        u = u_ref[...]
        xm_ref[...] = (u + (p_ref[...] - u) * mu_ref[0]).astype(BF16)

    o_ref[0] = _dot(xm_ref[...], w_ref[0]).astype(o_ref.dtype)


def mix_matmul(u, prev, mu, w, out_dtype=F32):
    m, k = u.shape
    s, _, n = w.shape
    tm, tn = _matmul_tiles(m, k, n, a_bytes=4, n_a=2, cast=True, n_out=1)
    return pl.pallas_call(
        _mixmm_kernel,
        grid=(m // tm, s, n // tn),
        in_specs=[
            pl.BlockSpec((tm, k), lambda i, si, j: (i, 0)),
            pl.BlockSpec((tm, k), lambda i, si, j: (i, 0)),
            pl.BlockSpec((1, 1, k), lambda i, si, j: (si, 0, 0)),
            pl.BlockSpec((1, k, tn), lambda i, si, j: (si, 0, j)),
        ],
        out_specs=pl.BlockSpec((1, tm, tn), lambda i, si, j: (si, i, j)),
        out_shape=jax.ShapeDtypeStruct((s, m, n), out_dtype),
        scratch_shapes=[pltpu.VMEM((tm, k), BF16)],
        compiler_params=_cparams(("parallel", "arbitrary", "arbitrary")),
        name="mix_matmul",
    )(u, prev, mu.reshape(s, 1, k), w)


def _conv_silu(cur, car_ref, cols, w, b, lq, lb):
    car_ref[8:8 + lq, cols] = cur
    acc = b + cur * w[SSD_CONV - 1:SSD_CONV]
    for s in range(1, SSD_CONV):
        acc = acc + car_ref[8 - s:8 - s + lq, cols] * w[SSD_CONV - 1 - s:SSD_CONV - s]
    car_ref[0:8, cols] = cur[lb - 8:lb]
    return _silu(acc)


def _ssd_kernel(z_ref, x_ref, b_ref, c_ref, dt_ref, ci_ref, s0_ref, cw_ref, cb_ref, dtb_ref, alog_ref,
                d_ref, ng_ref, *rest, lq, lo, hi, nchunks, out_layer):
    y_ref, sout_ref, car, st = rest[-4:]
    c = pl.program_id(1)
    gw = SSD_HPG * HEAD_DIM

    @pl.when(c == 0)
    def _():
        car[0:8, :] = ci_ref[0]
        st[...] = s0_ref[0, 0].reshape(SSD_GROUPS * gw, SSD_STATE)

    gens = [_ssd_group(g, z_ref, x_ref, b_ref, c_ref, dt_ref, cw_ref, cb_ref, dtb_ref, alog_ref, d_ref,
                       ng_ref, y_ref, car, st, lb=lq, lo=lo, hi=hi) for g in range(SSD_GROUPS)]
    for _ in zip(*gens):
        pass

    @pl.when(c == nchunks - 1)
    def _():
        for layer in range(sout_ref.shape[0]):
            if layer == out_layer:
                sout_ref[layer, 0] = st[...].reshape(SSD_GROUPS * SSD_HPG, HEAD_DIM, SSD_STATE)
            else:
                sout_ref[layer, 0] = jnp.zeros(sout_ref.shape[2:], F32)


def _ssd_group(g, z_ref, x_ref, b_ref, c_ref, dt_ref, cw_ref, cb_ref, dtb_ref, alog_ref, d_ref, ng_ref,
               y_ref, car, st, *, lb, lo, hi):
    c = pl.program_id(1)
    lq = max(lb, MIN_CHUNK_ROWS)
    gw = SSD_HPG * HEAD_DIM
    d_inner = SSD_GROUPS * gw
    xs = slice(g * gw, (g + 1) * gw)
    ns = slice(g * SSD_STATE, (g + 1) * SSD_STATE)
    bs = slice(d_inner + g * SSD_STATE, d_inner + (g + 1) * SSD_STATE)
    cs_ = slice(d_inner + (SSD_GROUPS + g) * SSD_STATE, d_inner + (SSD_GROUPS + g + 1) * SSD_STATE)

    f32 = lambda x: _pad_rows(x.astype(F32), lq)
    xc = _conv_silu(f32(x_ref[:, xs]), car, xs, cw_ref[:, xs], cb_ref[:, xs], lq, lb)
    bc = _conv_silu(f32(b_ref[:, ns]), car, bs, cw_ref[:, bs], cb_ref[:, bs], lq, lb)
    cc = _conv_silu(f32(c_ref[:, ns]), car, cs_, cw_ref[:, cs_], cb_ref[:, cs_], lq, lb)

    ti = lax.broadcasted_iota(jnp.int32, (lq, 1), 0)
    pos = c * lb + ti
    valid = (ti < lb) & (pos >= lo) & (pos < hi)
    lane = lax.broadcasted_iota(jnp.int32, (lq, 128), 1)
    dt = _softplus(f32(dt_ref[:, ns]) + dtb_ref[:, ns])
    dt = jnp.where(valid & (lane < SSD_HPG), dt, 0.0)
    la = dt * (-jnp.exp(alog_ref[:, ns]))
    yield

    ri = lax.broadcasted_iota(jnp.int32, (lq, lq), 0)
    ci = lax.broadcasted_iota(jnp.int32, (lq, lq), 1)
    tril = (ri >= ci).astype(BF16)
    triu = (ri <= ci).astype(BF16)
    acum = _dot_split_r(tril, la)
    acum_t = sum(_tn(part, triu) for part in _split(la, 3))
    yield
    a_end = acum[lq - 1:lq]
    dec_end = jnp.exp(a_end - acum)
    e_in = jnp.exp(acum)
    cd = jnp.exp(a_end)

    bcp = bc.astype(BF16)
    ccb = cc.astype(BF16)
    g_sc = _nt(ccb, bcp)
    st_old = st[xs, :]
    y_in = _nt(ccb, st_old.astype(BF16))
    yield

    causal = ci <= ri
    lane_q = lax.broadcasted_iota(jnp.int32, (lq, 128), 1) < HEAD_DIM
    lane_k = lane_q
    row_k = lax.broadcasted_iota(jnp.int32, (128, 1), 0) < HEAD_DIM
    dvec = d_ref[:, ns]

    er = lax.broadcasted_iota(jnp.int32, (2 * LANES, gw), 0)
    ec = lax.broadcasted_iota(jnp.int32, (2 * LANES, gw), 1)
    spread_m = (_imod(er, LANES) == _idiv(ec, HEAD_DIM)).astype(BF16)
    spread = lambda f: _dot(jnp.concatenate(_split(f, 2), axis=1), spread_m)
    dt_x = spread(dt)
    e_x = spread(e_in)
    dec_x = spread(dec_end)
    yield

    ys = []
    for p in range(SSD_HPG // 2):
        h0, h1 = 2 * p, 2 * p + 1
        ps = slice(128 * p, 128 * (p + 1))
        xp = xc[:, ps]
        vp = xp * dt_x[:, ps]
        vpp = vp
        yp = y_in[:, ps] * e_x[:, ps]
        yp = yp + xp * jnp.where(lane_q, dvec[:, h0:h0 + 1], dvec[:, h1:h1 + 1])
        for hh, h in ((0, h0), (1, h1)):
            seg = acum[:, h:h + 1] - acum_t[h:h + 1, :]
            lm = jnp.exp(jnp.where(causal, seg, NEG))
            pm = (g_sc * lm).astype(BF16)
            vm = jnp.where(lane_k if hh == 0 else jnp.logical_not(lane_k), vpp, 0.0).astype(BF16)
            yp = yp + _dot(pm, vm)
        ys.append(yp)
        vend = vpp * dec_x[:, ps]
        upd = _tn(vend.astype(BF16), bcp)
        cdp = jnp.where(row_k, cd[:, h0:h0 + 1], cd[:, h1:h1 + 1])
        st[g * gw + 128 * p:g * gw + 128 * (p + 1), :] = st_old[128 * p:128 * (p + 1), :] * cdp + upd
        yield

    y = jnp.concatenate(ys, axis=1)
    y = y * _silu(f32(z_ref[:, xs]))
    ms = jnp.mean(y * y, axis=-1, keepdims=True)
    y = y * lax.rsqrt(ms + EPS) * ng_ref[:, xs]
    y_ref[:, xs] = jnp.where(valid, y, 0.0)[:lb].astype(y_ref.dtype)
    yield


def ssd_scan(proj, dt_raw, conv_init, s0_all, layer, s_buf, conv_w, conv_b, dtb, alog, dskip, norm_g,
             *, bsz, tp, lq, lo, hi):
    m = proj.shape[0]
    heads = SSD_GROUPS * SSD_HPG
    nlayers = s0_all.shape[0]
    state_spec = pl.BlockSpec((1, 1, heads, HEAD_DIM, SSD_STATE), lambda b, c: (layer, b, 0, 0, 0))
    if s_buf is not None:
        out_state_spec, out_layer = state_spec, 0
        extra_specs, extra_args, aliases = [pl.BlockSpec(memory_space=pl.ANY)], [s_buf], {13: 1}
    else:
        out_state_spec = pl.BlockSpec((nlayers, 1, heads, HEAD_DIM, SSD_STATE), lambda b, c: (0, b, 0, 0, 0))
        out_layer = layer
        extra_specs, extra_args, aliases = [], [], {}
    nch = tp // lq
    d_inner = heads * HEAD_DIM
    gn = SSD_GROUPS * SSD_STATE
    conv_dim = d_inner + 2 * gn
    row = lambda b, c: b * nch + c
    const = lambda b, c: (0, 0)
    kern = functools.partial(_ssd_kernel, lq=lq, lo=lo, hi=hi, nchunks=nch, out_layer=out_layer)
    return pl.pallas_call(
        kern,
        grid=(bsz, nch),
        in_specs=[
            pl.BlockSpec((lq, d_inner), lambda b, c: (row(b, c), 0)),
            pl.BlockSpec((lq, d_inner), lambda b, c: (row(b, c), 1)),
            pl.BlockSpec((lq, gn), lambda b, c: (row(b, c), 2 * d_inner // gn)),
            pl.BlockSpec((lq, gn), lambda b, c: (row(b, c), 2 * d_inner // gn + 1)),
            pl.BlockSpec((lq, gn), lambda b, c: (row(b, c), 0)),
            pl.BlockSpec((1, 8, conv_dim), lambda b, c: (b, 0, 0)),
            state_spec,
            pl.BlockSpec((SSD_CONV, conv_dim), const),
            pl.BlockSpec((1, conv_dim), const),
            pl.BlockSpec((1, gn), const),
            pl.BlockSpec((1, gn), const),
            pl.BlockSpec((1, gn), const),
            pl.BlockSpec((1, d_inner), const),
        ] + extra_specs,
        out_specs=[
            pl.BlockSpec((lq, d_inner), lambda b, c: (row(b, c), 0)),
            out_state_spec,
        ],
        out_shape=[
            jax.ShapeDtypeStruct((m, d_inner), proj.dtype),
            jax.ShapeDtypeStruct(s0_all.shape, F32),
        ],
        scratch_shapes=[pltpu.VMEM((8 + max(lq, MIN_CHUNK_ROWS), conv_dim), F32),
                        pltpu.VMEM((d_inner, SSD_STATE), F32)],
        input_output_aliases=aliases,
        compiler_params=_cparams(("parallel", "arbitrary")),
        name="ssd_scan",
    )(proj, proj, proj, proj, dt_raw, conv_init, s0_all, conv_w, conv_b.reshape(1, -1),
      dtb.reshape(1, -1), alog.reshape(1, -1), dskip.reshape(1, -1), norm_g.reshape(1, -1), *extra_args)


def _ret_kernel(q_ref, k_ref, v_ref, g_ref, cos_ref, sin_ref, lg_ref, s0_ref, ng_ref,
                y_ref, sout_ref, *, lq, lo, hi):
    c = pl.program_id(1)

    @pl.when(c == 0)
    def _():
        sout_ref[...] = s0_ref[...]

    gens = [_ret_head(h, q_ref, k_ref, v_ref, g_ref, cos_ref, sin_ref, lg_ref, ng_ref, y_ref, sout_ref,
                      lb=lq, lo=lo, hi=hi) for h in range(RET_HEADS)]
    for _ in zip(*gens):
        pass


def _ret_head(h, q_ref, k_ref, v_ref, g_ref, cos_ref, sin_ref, lg_ref, ng_ref, y_ref, st_ref, *, lb, lo, hi):
    c = pl.program_id(1)
    lq = max(lb, MIN_CHUNK_ROWS)
    f32 = lambda x: _pad_rows(x.astype(F32), lq)
    qs = slice(h * RET_QK, (h + 1) * RET_QK)
    vs = slice(h * RET_V, (h + 1) * RET_V)
    lg = lg_ref[h][0:1, 0:1]
    nv = float(hi - lo)

    def count(p):
        return jnp.clip((p + 1 - lo).astype(F32), 0.0, nv)

    base = c * lb
    ti = lax.broadcasted_iota(jnp.int32, (lq, 1), 0)
    pos_i = base + ti
    valid = (ti < lb) & (pos_i >= lo) & (pos_i < hi)
    cnt_i = count(pos_i)
    cnt_j = count(base + lax.broadcasted_iota(jnp.int32, (1, lq), 1))
    cnt_jc = cnt_i
    cnt0 = count(base - 1 + jnp.zeros((1, 1), jnp.int32))
    cnt_end = count(base + lb - 1 + jnp.zeros((1, 1), jnp.int32))

    cos = f32(cos_ref[...])
    sin = f32(sin_ref[...])
    half = RET_QK // 2

    def rot(x):
        x1, x2 = x[:, :half], x[:, half:]
        return jnp.concatenate([x1 * cos - x2 * sin, x1 * sin + x2 * cos], axis=1)

    qr = rot(f32(q_ref[:, qs])).astype(BF16)
    kr = jnp.where(valid, rot(f32(k_ref[:, qs])) * (RET_QK ** -0.5), 0.0)
    v = jnp.where(valid, f32(v_ref[:, vs]), 0.0)
    krp = kr.astype(BF16)
    vp = v
    yield

    sc = _nt(qr, krp)
    s_old = st_ref[0, h]
    y_in = _nt(qr, s_old.astype(BF16))
    yield
    qi = lax.broadcasted_iota(jnp.int32, (lq, lq), 0)
    kj = lax.broadcasted_iota(jnp.int32, (lq, lq), 1)
    dm = jnp.exp(jnp.where(kj <= qi, lg * (cnt_i - cnt_j), NEG))
    y = _dot((sc * dm).astype(BF16), vp.astype(BF16))
    y = y + y_in * jnp.exp(lg * (cnt_i - cnt0))
    vend = vp * jnp.exp(lg * (cnt_end - cnt_jc))
    st_ref[0, h] = s_old * jnp.exp(lg * (cnt_end - cnt0)) + _tn(vend.astype(BF16), krp)
    yield

    ms = jnp.mean(y * y, axis=-1, keepdims=True)
    y = y * lax.rsqrt(ms + EPS) * ng_ref[:, vs] * _silu(f32(g_ref[:, vs]))
    y_ref[:, vs] = jnp.where(valid, y, 0.0)[:lb].astype(y_ref.dtype)
    yield


def ret_scan(proj, cos, sin, lg, s0, norm_g, *, bsz, tp, lq, lo, hi):
    m = proj.shape[0]
    nch = tp // lq
    d_inner = RET_HEADS * RET_V
    d_qk = RET_HEADS * RET_QK
    row = lambda b, c: b * nch + c
    state_spec = pl.BlockSpec((1, RET_HEADS, RET_V, RET_QK), lambda b, c: (b, 0, 0, 0))
    kern = functools.partial(_ret_kernel, lq=lq, lo=lo, hi=hi)
    return pl.pallas_call(
        kern,
        grid=(bsz, nch),
        in_specs=[
            pl.BlockSpec((lq, d_qk), lambda b, c: (row(b, c), 0)),
            pl.BlockSpec((lq, d_qk), lambda b, c: (row(b, c), 1)),
            pl.BlockSpec((lq, d_inner), lambda b, c: (row(b, c), 2 * d_qk // d_inner)),
            pl.BlockSpec((lq, d_inner), lambda b, c: (row(b, c), 2 * d_qk // d_inner + 1)),
            pl.BlockSpec((lq, RET_QK // 2), lambda b, c: (c, 0)),
            pl.BlockSpec((lq, RET_QK // 2), lambda b, c: (c, 0)),
            pl.BlockSpec((RET_HEADS, 8, 128), lambda b, c: (0, 0, 0)),
            state_spec,
            pl.BlockSpec((1, d_inner), lambda b, c: (0, 0)),
        ],
        out_specs=[
            pl.BlockSpec((lq, d_inner), lambda b, c: (row(b, c), 0)),
            state_spec,
        ],
        out_shape=[
            jax.ShapeDtypeStruct((m, d_inner), proj.dtype),
            jax.ShapeDtypeStruct((bsz, RET_HEADS, RET_V, RET_QK), F32),
        ],
        compiler_params=_cparams(("parallel", "arbitrary")),
        name="ret_scan",
    )(proj, proj, proj, proj, cos, sin, lg, s0, norm_g.reshape(1, -1))


def _rwkv_kernel(r_ref, k_ref, v_ref, g_ref, lw_ref, la_ref, bw_ref, ba_ref, par_ref, s0_ref,
                 y_ref, sout_ref, st, *, lo, hi, nchunks, ngrp):
    c = pl.program_id(2)
    w4 = RWKV_HPB * HEAD_DIM
    r2 = lax.broadcasted_iota(jnp.int32, (w4, w4), 0)
    c2 = lax.broadcasted_iota(jnp.int32, (w4, w4), 1)
    blk = _idiv(r2, HEAD_DIM) == _idiv(c2, HEAD_DIM)

    @pl.when(c == 0)
    def _():
        tile = (lax.broadcasted_iota(jnp.int32, (HEAD_DIM, w4), 0)
                == _imod(lax.broadcasted_iota(jnp.int32, (HEAD_DIM, w4), 1), HEAD_DIM)).astype(BF16)
        for gi in range(ngrp):
            s0 = s0_ref[0, gi * RWKV_HPB:(gi + 1) * RWKV_HPB].reshape(w4, HEAD_DIM)
            st[gi] = jnp.where(blk, _dot_split_l(s0, tile), 0.0)

    has_rows = ((c + 1) * RWKV_CHUNK > lo) & (c * RWKV_CHUNK < hi)

    @pl.when(has_rows)
    def _():
        gens = [_rwkv_group(gi, r_ref, k_ref, v_ref, g_ref, lw_ref, la_ref, bw_ref, ba_ref, par_ref,
                            y_ref, st, lo=lo, hi=hi) for gi in range(ngrp)]
        for _ in zip(*gens):
            pass

    @pl.when(jnp.logical_not(has_rows))
    def _():
        y_ref[...] = jnp.zeros(y_ref.shape, y_ref.dtype)

    @pl.when(c == nchunks - 1)
    def _():
        tile_t = (_imod(lax.broadcasted_iota(jnp.int32, (w4, HEAD_DIM), 0), HEAD_DIM)
                  == lax.broadcasted_iota(jnp.int32, (w4, HEAD_DIM), 1)).astype(BF16)
        for gi in range(ngrp):
            sout_ref[0, gi * RWKV_HPB:(gi + 1) * RWKV_HPB] = _dot_split_l(st[gi], tile_t).reshape(
                RWKV_HPB, HEAD_DIM, HEAD_DIM)


def _rwkv_group(gi, r_ref, k_ref, v_ref, g_ref, lw_ref, la_ref, bw_ref, ba_ref, par_ref, y_ref, st,
                *, lo, hi):
    c = pl.program_id(2)
    cs = RWKV_CHUNK
    w4 = RWKV_HPB * HEAD_DIM
    sl = slice(gi * w4, (gi + 1) * w4)

    nr = RWKV_HPB * cs
    ones_bd = (_idiv(lax.broadcasted_iota(jnp.int32, (w4, w4), 0), HEAD_DIM)
               == _idiv(lax.broadcasted_iota(jnp.int32, (w4, w4), 1), HEAD_DIM)).astype(BF16)
    blk = (_idiv(lax.broadcasted_iota(jnp.int32, (nr, w4), 0), cs)
           == _idiv(lax.broadcasted_iota(jnp.int32, (nr, w4), 1), HEAD_DIM))
    r2 = lax.broadcasted_iota(jnp.int32, (nr, nr), 0)
    c2 = lax.broadcasted_iota(jnp.int32, (nr, nr), 1)
    same = _idiv(r2, cs) == _idiv(c2, cs)

    def segsum(x):
        return _dot_split_l(x, ones_bd, terms=2)

    par = par_ref[:, sl]
    w0, a0, k_k, k_a, r_k, ln_g, ln_b = (par[i:i + 1] for i in range(7))

    r = r_ref[0, :, sl].astype(F32)
    k = k_ref[0, :, sl].astype(F32)
    v = v_ref[0, :, sl].astype(F32)
    g = g_ref[0, :, sl].astype(F32)
    w_raw = w0 + _dot(jnp.tanh(lw_ref[0]).astype(BF16), bw_ref[0, :, sl])
    a = jax.nn.sigmoid(a0 + _dot(la_ref[0].astype(BF16), ba_ref[0, :, sl]))

    ti = lax.broadcasted_iota(jnp.int32, (cs, 1), 0)
    pos = c * cs + ti
    valid = (pos >= lo) & (pos < hi)

    lw = -jnp.exp(-_softplus(-w_raw) - 0.5)
    kk = k * k_k
    kk = kk / jnp.maximum(jnp.sqrt(segsum(kk * kk)), 1e-12)
    yield
    kp = k * (1.0 + (a - 1.0) * k_a)
    lw = jnp.where(valid, lw, 0.0)
    kk = jnp.where(valid, kk, 0.0)
    kp = jnp.where(valid, kp, 0.0)
    vm = jnp.where(valid, v, 0.0)

    tri = (lax.broadcasted_iota(jnp.int32, (cs, cs), 0) >= lax.broadcasted_iota(jnp.int32, (cs, cs), 1)).astype(BF16)
    cw = _dot_split_r(tri, lw)
    yield
    cwl = cw[cs - 1:cs]
    wt = jnp.exp(cw)
    wi = jnp.exp(-cw)
    wend = jnp.exp(cwl - cw)
    b = kk * a
    at = -kk * jnp.exp(cw - lw)
    rt = r * wt

    def bd(x):
        return jnp.where(blk, jnp.concatenate([x] * RWKV_HPB, axis=0), 0.0).astype(BF16)

    lhs = jnp.concatenate([bd(at), bd(rt)], axis=0)
    rhs = jnp.concatenate([bd(b * wi), bd(kp * wi)], axis=0)
    sc = _nt(lhs, rhs)
    yield
    tt = _imod(r2, cs)
    jj = _imod(c2, cs)
    strict = same & (tt > jj)
    incl = same & (tt >= jj)
    mab = jnp.where(strict, sc[:nr, :nr], 0.0)
    mak = jnp.where(strict, sc[:nr, nr:], 0.0)
    nrb = jnp.where(incl, sc[nr:, :nr], 0.0)
    nrk = jnp.where(incl, sc[nr:, nr:], 0.0)

    tinv = (r2 == c2).astype(F32) + jnp.where(
        (_idiv(r2, 2) == _idiv(c2, 2)) & (_imod(tt, 2) == 1) & (_imod(jj, 2) == 0), mab, 0.0)
    msz = 2
    while msz < cs:
        off = ((_idiv(r2, 2 * msz) == _idiv(c2, 2 * msz)) & (_imod(tt, 2 * msz) >= msz)
               & (_imod(jj, 2 * msz) < msz))
        tb = tinv.astype(BF16)
        tno = _dot(tb, jnp.where(off, mab, 0.0).astype(BF16)).astype(BF16)
        yield
        tinv = tinv + _dot(tno, tb)
        yield
        msz *= 2

    s_old = st[gi]
    x = _nt(lhs, s_old.astype(BF16))
    yield
    vbd = bd(vm)
    u = _dot(tinv.astype(BF16), (x[:nr] + _dot(mak.astype(BF16), vbd)).astype(BF16))
    yield
    ub = u.astype(BF16)
    yb = x[nr:] + _dot(nrb.astype(BF16), ub) + _dot(nrk.astype(BF16), vbd)
    yield
    s_new = s_old * wt[cs - 1:cs] + _tn(jnp.concatenate([ub, vbd], axis=0),
                                        jnp.concatenate([bd(b * wend), bd(kp * wend)], axis=0))
    st[gi] = s_new
    yield

    y = sum(yb[i * cs:(i + 1) * cs] for i in range(RWKV_HPB))
    inv = 1.0 / HEAD_DIM
    yc = y - segsum(y) * inv
    yield
    y = yc * lax.rsqrt(segsum(yc * yc) * inv + RWKV_LN_EPS) * ln_g + ln_b
    y = (y + segsum(r * kp * r_k) * v) * _silu(g)
    y_ref[:, sl] = jnp.where(valid, y, 0.0).astype(BF16)
    yield


def rwkv_scan(rkvg, lora1, lora_b, par, s0, *, bsz, tp, lo, hi):
    _, m, e = rkvg.shape
    cs = RWKV_CHUNK
    nch = tp // cs
    heads = e // HEAD_DIM
    ngrp = RWKV_GROUPS_PER_STEP
    hb = RWKV_HPB * ngrp
    wb = hb * HEAD_DIM
    row = lambda b, h, c: b * nch + c
    kern = functools.partial(_rwkv_kernel, lo=lo, hi=hi, nchunks=nch, ngrp=ngrp)
    proj_spec = lambda s: pl.BlockSpec((1, cs, wb), lambda b, h, c: (s, row(b, h, c), h))
    return pl.pallas_call(
        kern,
        grid=(bsz, heads // hb, nch),
        in_specs=[
            proj_spec(0), proj_spec(1), proj_spec(2), proj_spec(3),
            pl.BlockSpec((1, cs, RWKV_LORA_PAD), lambda b, h, c: (0, row(b, h, c), 0)),
            pl.BlockSpec((1, cs, RWKV_LORA_PAD), lambda b, h, c: (1, row(b, h, c), 0)),
            pl.BlockSpec((1, RWKV_LORA_PAD, wb), lambda b, h, c: (0, 0, h)),
            pl.BlockSpec((1, RWKV_LORA_PAD, wb), lambda b, h, c: (1, 0, h)),
            pl.BlockSpec((8, wb), lambda b, h, c: (0, h)),
            pl.BlockSpec((1, hb, HEAD_DIM, HEAD_DIM), lambda b, h, c: (b, h, 0, 0)),
        ],
        out_specs=[
            pl.BlockSpec((cs, wb), lambda b, h, c: (row(b, h, c), h)),
            pl.BlockSpec((1, hb, HEAD_DIM, HEAD_DIM), lambda b, h, c: (b, h, 0, 0)),
        ],
        out_shape=[
            jax.ShapeDtypeStruct((m, e), BF16),
            jax.ShapeDtypeStruct((bsz, heads, HEAD_DIM, HEAD_DIM), F32),
        ],
        scratch_shapes=[pltpu.VMEM((ngrp, RWKV_HPB * HEAD_DIM, RWKV_HPB * HEAD_DIM), F32)],
        compiler_params=_cparams(("parallel", "parallel", "arbitrary")),
        name="rwkv_scan",
    )(rkvg, rkvg, rkvg, rkvg, lora1, lora1, lora_b, lora_b, par, s0)


def _rwkv_short_kernel(r_ref, k_ref, v_ref, g_ref, lw_ref, la_ref, bw_ref, ba_ref, par_ref, s0_ref,
                       ones_ref, tile_ref, y_ref, sout_ref, *, nvalid, rows_out):
    ct = 8
    hp = RWKV_SHORT_HEADS
    wl = hp * HEAD_DIM
    nr = hp * ct
    ones_bd = ones_ref[...]
    tile_t = tile_ref[...]

    def segsum(x):
        return jnp.concatenate(
            [_dot_split_l(x[:, 256 * j:256 * (j + 1)], ones_bd, terms=2) for j in range(wl // 256)], axis=1)

    par = par_ref[...]
    w0, a0, k_k, k_a, r_k, ln_g, ln_b = (par[i:i + 1] for i in range(7))
    r, k, v, g = r_ref[0], k_ref[0], v_ref[0], g_ref[0]
    w_raw = w0 + _dot(jnp.tanh(lw_ref[0]).astype(BF16), bw_ref[0])
    a = jax.nn.sigmoid(a0 + _dot(la_ref[0].astype(BF16), ba_ref[0]))

    ti = lax.broadcasted_iota(jnp.int32, (ct, 1), 0)
    valid = ti < nvalid
    lw = jnp.where(valid, -jnp.exp(-_softplus(-w_raw) - 0.5), 0.0)
    kk = k * k_k
    kk = jnp.where(valid, kk / jnp.maximum(jnp.sqrt(segsum(kk * kk)), 1e-12), 0.0)
    kp = jnp.where(valid, k * (1.0 + (a - 1.0) * k_a), 0.0)
    vm = jnp.where(valid, v, 0.0)

    cw = lw
    for s in (1, 2, 4):
        cw = cw + jnp.where(ti >= s, pltpu.roll(cw, s, 0), 0.0)
    cwl = cw[ct - 1:ct]
    wend = jnp.exp(cwl - cw)
    wi = jnp.exp(-cw)
    b = kk * a
    at = -kk * jnp.exp(cw - lw)
    rt = r * jnp.exp(cw)

    rr = lax.broadcasted_iota(jnp.int32, (nr, wl), 0)
    cc = lax.broadcasted_iota(jnp.int32, (nr, wl), 1)
    blk = _idiv(rr, ct) == _idiv(cc, HEAD_DIM)

    def bd(x):
        return jnp.where(blk, jnp.concatenate([x] * hp, axis=0), 0.0).astype(BF16)

    lhs = jnp.concatenate([bd(at), bd(rt)], axis=0)
    rhs = jnp.concatenate([bd(b * wi), bd(kp * wi)], axis=0)
    sc = _nt(lhs, rhs)
    ri = lax.broadcasted_iota(jnp.int32, (2 * nr, 2 * nr), 0)
    ci = lax.broadcasted_iota(jnp.int32, (2 * nr, 2 * nr), 1)
    same = _idiv(_imod(ri, nr), ct) == _idiv(_imod(ci, nr), ct)
    tt = _imod(ri, ct)
    jj = _imod(ci, ct)
    sc = jnp.where(same & (tt + (ri >= nr).astype(jnp.int32) > jj), sc, 0.0)
    mab, mak, nrb, nrk = sc[:nr, :nr], sc[:nr, nr:], sc[nr:, :nr], sc[nr:, nr:]

    r1 = lax.broadcasted_iota(jnp.int32, (nr, nr), 0)
    c1 = lax.broadcasted_iota(jnp.int32, (nr, nr), 1)
    t1 = _imod(r1, ct)
    j1 = _imod(c1, ct)
    tinv = (r1 == c1).astype(F32) + jnp.where(
        (_idiv(r1, 2) == _idiv(c1, 2)) & (_imod(t1, 2) == 1) & (_imod(j1, 2) == 0), mab, 0.0)
    msz = 2
    while msz < nvalid:
        off = ((_idiv(r1, 2 * msz) == _idiv(c1, 2 * msz)) & (_imod(t1, 2 * msz) >= msz)
               & (_imod(j1, 2 * msz) < msz))
        tb = tinv.astype(BF16)
        tinv = tinv + _dot(_dot(tb, jnp.where(off, mab, 0.0).astype(BF16)).astype(BF16), tb)
        msz *= 2

    s_old = s0_ref[0].reshape(wl, HEAD_DIM)
    lhs_rows = _dot(lhs, tile_t).astype(BF16)
    x = _nt(lhs_rows, s_old.astype(BF16))
    xa = jnp.where(blk, x[:nr], 0.0)
    xr = jnp.where(blk, x[nr:], 0.0)
    vbd = bd(vm)
    u = _dot(tinv.astype(BF16), (xa + _dot(mak.astype(BF16), vbd)).astype(BF16))
    uv = jnp.concatenate([u.astype(BF16), vbd], axis=0)
    yb = xr + _dot(jnp.concatenate([nrb, nrk], axis=1).astype(BF16), uv)
    bk_rows = _dot(jnp.concatenate([bd(b * wend), bd(kp * wend)], axis=0), tile_t).astype(BF16)
    ds = _tn(uv, bk_rows)

    r16 = lax.broadcasted_iota(jnp.int32, (hp, wl), 0)
    c16 = lax.broadcasted_iota(jnp.int32, (hp, wl), 1)
    wc = jnp.exp(cwl)
    wc_rows = _dot_split_l(jnp.where(r16 == _idiv(c16, HEAD_DIM), wc, 0.0), tile_t)
    rsel = lax.broadcasted_iota(jnp.int32, (hp, HEAD_DIM), 0)
    for h in range(hp):
        rs = slice(h * HEAD_DIM, (h + 1) * HEAD_DIM)
        wc_h = jnp.sum(jnp.where(rsel == h, wc_rows, 0.0), axis=0, keepdims=True)
        sout_ref[0, h] = s_old[rs] * wc_h + ds[rs]

    y = yb[0:ct]
    for h in range(1, hp):
        y = y + yb[h * ct:(h + 1) * ct]
    inv = 1.0 / HEAD_DIM
    yc = y - segsum(y) * inv
    y = yc * lax.rsqrt(segsum(yc * yc) * inv + RWKV_LN_EPS) * ln_g + ln_b
    y = (y + segsum(r * kp * r_k) * v) * _silu(g)
    y = jnp.where(valid, y, 0.0)
    y_ref[...] = _pad_rows(y, rows_out).astype(y_ref.dtype)


def rwkv_short(rkvg, lora1, lora_b, par, s0, *, bsz, tp, hi):
    _, m, e = rkvg.shape
    heads = e // HEAD_DIM
    hp = RWKV_SHORT_HEADS
    wl = hp * HEAD_DIM
    rb = tp // 8
    w4 = RWKV_HPB * HEAD_DIM
    ones_bd = (jnp.arange(w4)[:, None] // HEAD_DIM == jnp.arange(w4)[None, :] // HEAD_DIM).astype(BF16)
    tile_t = (jnp.arange(wl)[:, None] % HEAD_DIM == jnp.arange(HEAD_DIM)[None, :]).astype(BF16)
    kern = functools.partial(_rwkv_short_kernel, nvalid=hi, rows_out=tp)
    proj_spec = lambda s: pl.BlockSpec((1, 8, wl), lambda b, h: (s, b * rb, h))
    return pl.pallas_call(
        kern,
        grid=(bsz, heads // hp),
        in_specs=[
            proj_spec(0), proj_spec(1), proj_spec(2), proj_spec(3),
            pl.BlockSpec((1, 8, RWKV_LORA_PAD), lambda b, h: (0, b * rb, 0)),
            pl.BlockSpec((1, 8, RWKV_LORA_PAD), lambda b, h: (1, b * rb, 0)),
            pl.BlockSpec((1, RWKV_LORA_PAD, wl), lambda b, h: (0, 0, h)),
            pl.BlockSpec((1, RWKV_LORA_PAD, wl), lambda b, h: (1, 0, h)),
            pl.BlockSpec((8, wl), lambda b, h: (0, h)),
            pl.BlockSpec((1, hp, HEAD_DIM, HEAD_DIM), lambda b, h: (b, h, 0, 0)),
            pl.BlockSpec((w4, w4), lambda b, h: (0, 0)),
            pl.BlockSpec((wl, HEAD_DIM), lambda b, h: (0, 0)),
        ],
        out_specs=[
            pl.BlockSpec((tp, wl), lambda b, h: (b, h)),
            pl.BlockSpec((1, hp, HEAD_DIM, HEAD_DIM), lambda b, h: (b, h, 0, 0)),
        ],
        out_shape=[
            jax.ShapeDtypeStruct((m, e), rkvg.dtype),
            jax.ShapeDtypeStruct((bsz, heads, HEAD_DIM, HEAD_DIM), F32),
        ],
        compiler_params=_cparams(("parallel", "parallel")),
        name="rwkv_short",
    )(rkvg, rkvg, rkvg, rkvg, lora1, lora1, lora_b, lora_b, par, s0, ones_bd, tile_t)


def _rwkv_lanes_kernel(r_ref, k_ref, v_ref, g_ref, lw_ref, la_ref, bw_ref, ba_ref, par_ref, s0_ref,
                       y_ref, sout_ref, tk, tw, tb, tq, tr, tv, ty, *, nt, nb):
    hp = RWKV_LANE_HEADS
    wl = hp * HEAD_DIM
    ri = lax.broadcasted_iota(jnp.int32, (wl, wl), 0)
    ci = lax.broadcasted_iota(jnp.int32, (wl, wl), 1)
    ones_bd = (_idiv(ri, HEAD_DIM) == _idiv(ci, HEAD_DIM)).astype(BF16)

    def segsum(x):
        return _dot_split_l(x, ones_bd, terms=2)

    par = par_ref[...]
    w0, a0, k_k, k_a, r_k, ln_g, ln_b = (par[i:i + 1] for i in range(7))
    r, k, v, g = (x[0].astype(F32) for x in (r_ref, k_ref, v_ref, g_ref))
    w_raw = w0 + _dot(jnp.tanh(lw_ref[0]).astype(BF16), bw_ref[0])
    a = jax.nn.sigmoid(a0 + _dot(la_ref[0].astype(BF16), ba_ref[0]))
    decay = jnp.exp(-jnp.exp(-_softplus(-w_raw) - 0.5))
    kk = k * k_k
    kk = kk / jnp.maximum(jnp.sqrt(segsum(kk * kk)), 1e-12)
    kp = k * (1.0 + (a - 1.0) * k_a)
    bb = kk * a

    for t in range(nt):
        rows = slice(t * nb, (t + 1) * nb)
        tk[t] = (-kk[rows]).T
        tw[t] = decay[rows].T
        tb[t] = bb[rows].T
        tq[t] = kp[rows].T
        tr[t] = r[rows].T
        tv[t] = v[rows].T

    for hh in range(hp):
        ks = slice(hh * HEAD_DIM, (hh + 1) * HEAD_DIM)

        def body(vi, carry, hh=hh, ks=ks):
            sv = s0_ref[hh, vi]
            row = hh * HEAD_DIM + vi
            for t in range(nt):
                sa = jnp.sum(sv * tk[t, ks, :], axis=0, keepdims=True)
                sv = sv * tw[t, ks, :] + sa * tb[t, ks, :] + tv[t, pl.ds(row, 1), :] * tq[t, ks, :]
                ty[t, pl.ds(row, 1), :] = jnp.sum(sv * tr[t, ks, :], axis=0, keepdims=True)
            sout_ref[hh, vi] = sv
            return carry

        lax.fori_loop(0, HEAD_DIM, body, 0, unroll=4)

    inv = 1.0 / HEAD_DIM
    for t in range(nt):
        rows = slice(t * nb, (t + 1) * nb)
        y = ty[t].T
        yc = y - segsum(y) * inv
        y = yc * lax.rsqrt(segsum(yc * yc) * inv + RWKV_LN_EPS) * ln_g + ln_b
        y = (y + segsum(r[rows] * kp[rows] * r_k) * v[rows]) * _silu(g[rows])
        y_ref[rows, :] = y.astype(BF16)


def rwkv_lanes(rkvg, lora1, lora_b, par, s0t, *, nt, nb):
    _, m, e = rkvg.shape
    heads = e // HEAD_DIM
    hp = RWKV_LANE_HEADS
    wl = hp * HEAD_DIM
    kern = functools.partial(_rwkv_lanes_kernel, nt=nt, nb=nb)
    proj_spec = lambda s: pl.BlockSpec((1, m, wl), lambda h: (s, 0, h))
    state_spec = pl.BlockSpec((hp, HEAD_DIM, HEAD_DIM, nb), lambda h: (h, 0, 0, 0))
    tile = pltpu.VMEM((nt, wl, nb), F32)
    return pl.pallas_call(
        kern,
        grid=(heads // hp,),
        in_specs=[
            proj_spec(0), proj_spec(1), proj_spec(2), proj_spec(3),
            pl.BlockSpec((1, m, RWKV_LORA_PAD), lambda h: (0, 0, 0)),
            pl.BlockSpec((1, m, RWKV_LORA_PAD), lambda h: (1, 0, 0)),
            pl.BlockSpec((1, RWKV_LORA_PAD, wl), lambda h: (0, 0, h)),
            pl.BlockSpec((1, RWKV_LORA_PAD, wl), lambda h: (1, 0, h)),
            pl.BlockSpec((8, wl), lambda h: (0, h)),
            state_spec,
        ],
        out_specs=[pl.BlockSpec((m, wl), lambda h: (0, h)), state_spec],
        out_shape=[jax.ShapeDtypeStruct((m, e), BF16), jax.ShapeDtypeStruct(s0t.shape, F32)],
        scratch_shapes=[tile] * 7,
        compiler_params=_cparams(("parallel",)),
        name="rwkv_lanes",
    )(rkvg, rkvg, rkvg, rkvg, lora1, lora1, lora_b, lora_b, par, s0t)


def _prep_weights(norm_g, ssd_w_in, ssd_dt_bias, ssd_a_log, ssd_d, ssd_w_out,
                  rwkv_w_rkvg, rwkv_w_lora_a, rwkv_w_lora_b, rwkv_a_lora_a, rwkv_a_lora_b,
                  rwkv_w0, rwkv_a0, rwkv_k_k, rwkv_k_a, rwkv_r_k, rwkv_ln_g, rwkv_ln_b, rwkv_w_out,
                  ret_w_in, ret_w_out):
    d_inner = ssd_w_out.shape[1]
    n_main = d_inner + d_inner + 2 * SSD_GROUPS * SSD_STATE
    ns = ssd_w_in.shape[0]
    d_model = ssd_w_in.shape[1]

    def head_lanes(p):
        p = p.reshape(ns, SSD_GROUPS, 1, SSD_HPG)
        return jnp.pad(p, ((0, 0), (0, 0), (0, 0), (0, 128 - SSD_HPG)))

    w_dt = ssd_w_in[:, :, n_main:].reshape(ns, d_model, SSD_GROUPS, SSD_HPG)
    w_dt = jnp.pad(w_dt, ((0, 0), (0, 0), (0, 0), (0, 128 - SSD_HPG))).reshape(ns, d_model, SSD_GROUPS * 128)
    rank = rwkv_w_lora_a.shape[2]
    lora_a = jnp.stack([rwkv_w_lora_a, rwkv_a_lora_a], axis=1)
    lora_a = jnp.pad(lora_a, ((0, 0), (0, 0), (0, 0), (0, RWKV_LORA_PAD - rank)))
    lora_b = jnp.stack([rwkv_w_lora_b, rwkv_a_lora_b], axis=1)
    lora_b = jnp.pad(lora_b, ((0, 0), (0, 0), (0, RWKV_LORA_PAD - rank), (0, 0)))
    nr = rwkv_w0.shape[0]
    par = jnp.stack([rwkv_w0, rwkv_a0, rwkv_k_k, rwkv_k_a, rwkv_r_k.reshape(nr, -1), rwkv_ln_g, rwkv_ln_b,
                     jnp.zeros_like(rwkv_w0)], axis=1)
    return dict(
        ssd_w_main=ssd_w_in[:, :, :n_main].astype(BF16), ssd_w_dt=w_dt.astype(BF16),
        ssd_dtb=head_lanes(ssd_dt_bias), ssd_alog=head_lanes(ssd_a_log), ssd_dskip=head_lanes(ssd_d),
        ssd_w_out=ssd_w_out.astype(BF16),
        rwkv_w=rwkv_w_rkvg.astype(BF16), rwkv_lora_a=lora_a.astype(BF16), rwkv_lora_b=lora_b.astype(BF16),
        rwkv_par=par, rwkv_w_out=rwkv_w_out.astype(BF16),
        ret_w_in=ret_w_in.astype(BF16), ret_w_out=ret_w_out.astype(BF16),
    )


def _trunk(h, conv_st, ssd_st, shift_st, wkv_st, ret_st, pos, *, bsz, tp, lq, lo, hi, depth,
           norm_g, final_norm_g, wts, ssd_conv_w, ssd_conv_b, ssd_norm_g, rwkv_mu, ret_norm_g):
    d_model = h.shape[1]
    geo = dict(bsz=bsz, tp=tp, lo=lo, hi=hi)
    new_conv, new_shift, new_wkv, new_ret = [], [], [], []
    new_ssd = None
    act = BF16 if lq % 16 == 0 else F32

    half = RET_QK // 2
    inv_freq = 1.0 / (RET_THETA_BASE ** jnp.linspace(0.0, 1.0, half, dtype=F32))
    ang = pos.astype(F32)[:, None] * inv_freq
    cos, sin = jnp.cos(ang), jnp.sin(ang)
    log_gamma = jnp.log1p(-jnp.exp2(-5.0 - jnp.arange(RET_HEADS, dtype=F32)))
    lg = jnp.broadcast_to(log_gamma[:, None, None], (RET_HEADS, 8, 128))

    for layer in range(depth):
        kind, j = layer % 3, layer // 3
        ng = norm_g[layer]
        if kind == 0:
            proj = matmul(h, wts["ssd_w_main"][j], out_dtype=act, norm_g=ng)
            dt_raw = matmul(h, wts["ssd_w_dt"][j], norm_g=ng)
            conv_init = jnp.pad(conv_st[j], ((0, 0), (8 - (SSD_CONV - 1), 0), (0, 0)))
            y, new_ssd = ssd_scan(proj, dt_raw, conv_init, ssd_st, j, new_ssd, ssd_conv_w[j], ssd_conv_b[j],
                                  wts["ssd_dtb"][j], wts["ssd_alog"][j], wts["ssd_dskip"][j], ssd_norm_g[j],
                                  lq=lq, **geo)
            nk = SSD_CONV - 1
            last = h.reshape(bsz, tp, d_model)[:, hi - nk:hi].reshape(bsz * nk, d_model)
            last = jnp.pad(last, ((0, -(bsz * nk) % 16), (0, 0)))
            xbc = matmul(last, wts["ssd_w_main"][j], norm_g=ng)[:bsz * nk, y.shape[1]:]
            new_conv.append(xbc.reshape(bsz, nk, -1))
            h = matmul(y, wts["ssd_w_out"][j], res=h)
        elif kind == 1 and lo == 0 and hi <= 8 and bsz % LANES == 0:
            u = rmsnorm(h, ng)
            tmajor = lambda x: jnp.swapaxes(x.reshape(bsz, tp, -1)[:, :hi], 0, 1)
            uc = tmajor(u)
            prev = jnp.concatenate([shift_st[j][None], uc[:-1]], axis=0).reshape(hi * bsz, d_model)
            uc = uc.reshape(hi * bsz, d_model)
            rkvg = mix_matmul(uc, prev, rwkv_mu[j][:4], wts["rwkv_w"][j], out_dtype=BF16)
            lora1 = mix_matmul(uc, prev, rwkv_mu[j][4:], wts["rwkv_lora_a"][j])
            y, s_t = rwkv_lanes(rkvg, lora1, wts["rwkv_lora_b"][j], wts["rwkv_par"][j],
                                jnp.transpose(wkv_st[j], (1, 2, 3, 0)), nt=hi, nb=bsz)
            new_shift.append(u.reshape(bsz, tp, d_model)[:, hi - 1])
            new_wkv.append(jnp.transpose(s_t, (3, 0, 1, 2)))
            hc = matmul(y, wts["rwkv_w_out"][j], res=tmajor(h).reshape(hi * bsz, d_model))
            hc = jnp.swapaxes(hc.reshape(hi, bsz, d_model), 0, 1)
            h = jnp.pad(hc, ((0, 0), (0, tp - hi), (0, 0))).reshape(bsz * tp, d_model)
        elif kind == 1:
            u = rmsnorm(h, ng)
            u3 = u.reshape(bsz, tp, d_model)
            prev = jnp.concatenate([shift_st[j][:, None, :], u3[:, :-1]], axis=1).reshape(bsz * tp, d_model)
            rkvg = mix_matmul(u, prev, rwkv_mu[j][:4], wts["rwkv_w"][j], out_dtype=BF16)
            lora1 = mix_matmul(u, prev, rwkv_mu[j][4:], wts["rwkv_lora_a"][j])
            if lo == 0 and hi <= 8 and tp <= 16:
                y, s_new = rwkv_short(rkvg.astype(F32), lora1, wts["rwkv_lora_b"][j], wts["rwkv_par"][j], wkv_st[j],
                                      bsz=bsz, tp=tp, hi=hi)
            else:
                y, s_new = rwkv_scan(rkvg, lora1, wts["rwkv_lora_b"][j], wts["rwkv_par"][j], wkv_st[j], **geo)
            new_shift.append(u3[:, hi - 1])
            new_wkv.append(s_new)
            h = matmul(y, wts["rwkv_w_out"][j], res=h)
        else:
            proj = matmul(h, wts["ret_w_in"][j], out_dtype=act, norm_g=ng)
            y, s_new = ret_scan(proj, cos, sin, lg, ret_st[j], ret_norm_g[j], lq=lq, **geo)
            new_ret.append(s_new)
            h = matmul(y, wts["ret_w_out"][j], res=h)
    y = rmsnorm(h, final_norm_g)
    return (y, jnp.stack(new_conv), new_ssd, jnp.stack(new_shift), jnp.stack(new_wkv), jnp.stack(new_ret))


def kernel(x_prompt, x_sample, state_ssd_conv, state_ssd, state_rwkv_shift, state_rwkv_wkv, state_ret, meta_tokens, norm_g, final_norm_g, ssd_w_in, ssd_conv_w, ssd_conv_b, ssd_dt_bias, ssd_a_log, ssd_d, ssd_norm_g, ssd_w_out, rwkv_mu, rwkv_w_rkvg, rwkv_w0, rwkv_w_lora_a, rwkv_w_lora_b, rwkv_a0, rwkv_a_lora_a, rwkv_a_lora_b, rwkv_k_k, rwkv_k_a, rwkv_r_k, rwkv_ln_g, rwkv_ln_b, rwkv_w_out, ret_w_in, ret_norm_g, ret_w_out):
    depth = norm_g.shape[0]
    d_model = x_prompt.shape[2]
    wts = _prep_weights(norm_g, ssd_w_in, ssd_dt_bias, ssd_a_log, ssd_d, ssd_w_out,
                        rwkv_w_rkvg, rwkv_w_lora_a, rwkv_w_lora_b, rwkv_a_lora_a, rwkv_a_lora_b,
                        rwkv_w0, rwkv_a0, rwkv_k_k, rwkv_k_a, rwkv_r_k, rwkv_ln_g, rwkv_ln_b, rwkv_w_out,
                        ret_w_in, ret_w_out)
    common = dict(depth=depth, norm_g=norm_g, final_norm_g=final_norm_g, wts=wts, ssd_conv_w=ssd_conv_w,
                  ssd_conv_b=ssd_conv_b, ssd_norm_g=ssd_norm_g, rwkv_mu=rwkv_mu, ret_norm_g=ret_norm_g)

    bp, seq, _ = x_prompt.shape
    lq_p = 128
    lo_p = lq_p - N_META
    tp_p = lo_p + N_META + seq
    h_p = jnp.concatenate([jnp.zeros((bp, lo_p, d_model), F32),
                           jnp.broadcast_to(meta_tokens[None], (bp, N_META, d_model)), x_prompt], axis=1)
    zeros_like_b = lambda s: jnp.zeros((s.shape[0], bp) + s.shape[2:], F32)
    pos_p = jnp.maximum(jnp.arange(tp_p) - lo_p, 0)
    outs_p = _trunk(h_p.reshape(bp * tp_p, d_model), zeros_like_b(state_ssd_conv), zeros_like_b(state_ssd),
                    zeros_like_b(state_rwkv_shift), zeros_like_b(state_rwkv_wkv), zeros_like_b(state_ret), pos_p,
                    bsz=bp, tp=tp_p, lq=lq_p, lo=lo_p, hi=tp_p, **common)
    y_prompt = outs_p[0].reshape(bp, tp_p, d_model)[:, lo_p + N_META:]

    bs, ds, _ = x_sample.shape
    tp_s = 8
    h_s = jnp.concatenate([x_sample, jnp.zeros((bs, tp_s - ds, d_model), F32)], axis=1)
    pos_s = PAST_LEN + jnp.arange(tp_s)
    outs_s = _trunk(h_s.reshape(bs * tp_s, d_model), state_ssd_conv, state_ssd, state_rwkv_shift, state_rwkv_wkv,
                    state_ret, pos_s, bsz=bs, tp=tp_s, lq=tp_s, lo=0, hi=ds, **common)
    y_sample = outs_s[0].reshape(bs, tp_s, d_model)[:, :ds]

    return (y_prompt, y_sample) + tuple(outs_p[1:]) + tuple(outs_s[1:])
```

```python
import functools
import math

import jax
import jax.numpy as jnp
from jax import lax
from jax.experimental import pallas as pl
from jax.experimental.pallas import tpu as pltpu

F32 = jnp.float32
BF16 = jnp.bfloat16

EPS = 1e-6
N_META = 16
HEAD_DIM = 64
SSD_STATE = 128
SSD_GROUPS = 8
SSD_HPG = 8
SSD_CONV = 4
RET_HEADS = 8
RET_QK = 256
RET_V = 512
RET_THETA_BASE = 10000.0
RWKV_LORA_PAD = 128
RWKV_CHUNK = 64
RWKV_HPB = 4
RWKV_GROUPS_PER_STEP = 8
RWKV_SHORT_HEADS = 16
RWKV_LANE_HEADS = 2
LANES = 128
MIN_CHUNK_ROWS = 16
PAST_LEN = 16384
RWKV_LN_EPS = 1e-5 * HEAD_DIM
NEG = -1e30
ROW_TILE = 512
V7X_VMEM_BYTES = 64 * 1024 * 1024
VMEM_LIMIT = V7X_VMEM_BYTES * 7 // 8
MATMUL_VMEM_BUDGET = V7X_VMEM_BYTES * 5 // 8


def _cparams(sem):
    return pltpu.CompilerParams(dimension_semantics=sem, vmem_limit_bytes=VMEM_LIMIT)


def _nt(a, b):
    return lax.dot_general(a, b, (((1,), (1,)), ((), ())), preferred_element_type=F32)


def _tn(a, b):
    return lax.dot_general(a, b, (((0,), (0,)), ((), ())), preferred_element_type=F32)


def _dot(a, b):
    return jnp.dot(a, b, preferred_element_type=F32)


def _split(x, terms):
    parts = []
    r = x
    for i in range(terms):
        p = r.astype(BF16)
        parts.append(p)
        if i + 1 < terms:
            r = r - p.astype(F32)
    return parts


def _dot_split_l(x, m, terms=3):
    acc = None
    for p in _split(x, terms):
        d = _dot(p, m)
        acc = d if acc is None else acc + d
    return acc


def _dot_split_r(m, x, terms=3):
    acc = None
    for p in _split(x, terms):
        d = _dot(m, p)
        acc = d if acc is None else acc + d
    return acc


def _pad_rows(x, rows):
    if x.shape[0] == rows:
        return x
    return jnp.concatenate([x, jnp.zeros((rows - x.shape[0], x.shape[1]), x.dtype)], axis=0)


def _idiv(x, n):
    return jnp.right_shift(x, int(math.log2(n)))


def _imod(x, n):
    return jnp.bitwise_and(x, n - 1)


def _silu(x):
    h = 0.5 * x
    return h + h * jnp.tanh(h)


def _softplus(x):
    return jnp.maximum(x, 0.0) + jnp.log(1.0 + jnp.exp(-jnp.abs(x)))


def _rmsnorm_kernel(x_ref, g_ref, o_ref):
    x = x_ref[...]
    ms = jnp.mean(x * x, axis=-1, keepdims=True)
    o_ref[...] = (x * lax.rsqrt(ms + EPS) * g_ref[...]).astype(o_ref.dtype)


def _row_tile(m):
    tm = math.gcd(m, ROW_TILE)
    assert tm % 16 == 0, m
    return tm


def _matmul_tiles(m, k, n, *, a_bytes, n_a, cast, n_out):
    best = None
    for tn in (t for t in (1024, 512, 256, 128) if n % t == 0):
        for tm in (t for t in range(16, m + 1, 16) if m % t == 0):
            need = (2 * n_a * tm * k * a_bytes + (tm * k * 2 if cast else 0)
                    + 2 * k * tn * 2 + 2 * n_out * tm * tn * 4)
            if need <= MATMUL_VMEM_BUDGET and (best is None or (tm * tn, tm) > (best[0] * best[1], best[0])):
                best = (tm, tn)
    assert best is not None, (m, k, n)
    return best


def rmsnorm(x, g, out_dtype=F32):
    m, d = x.shape
    tm = _row_tile(m)
    return pl.pallas_call(
        _rmsnorm_kernel,
        grid=(m // tm,),
        in_specs=[pl.BlockSpec((tm, d), lambda i: (i, 0)), pl.BlockSpec((1, d), lambda i: (0, 0))],
        out_specs=pl.BlockSpec((tm, d), lambda i: (i, 0)),
        out_shape=jax.ShapeDtypeStruct((m, d), out_dtype),
        compiler_params=_cparams(("parallel",)),
        name="rmsnorm",
    )(x, g.reshape(1, d))


def _mm_kernel(a_ref, w_ref, *rest, has_res, cast, norm):
    rest = list(rest)
    abf_ref = rest.pop() if cast else a_ref
    o_ref = rest.pop()

    if cast:
        @pl.when(pl.program_id(1) == 0)
        def _():
            x = a_ref[...]
            if norm:
                x = x * lax.rsqrt(jnp.mean(x * x, axis=-1, keepdims=True) + EPS) * rest[0][...]
            abf_ref[...] = x.astype(BF16)

    acc = _dot(abf_ref[...], w_ref[0])
    if has_res:
        acc = rest[-1][...] + acc
    o_ref[...] = acc.astype(o_ref.dtype)


def matmul(a, w, layer, res=None, out_dtype=F32, norm_g=None, n_cols=None):
    m, k = a.shape
    n = w.shape[2] if n_cols is None else n_cols
    cast = a.dtype != BF16 or norm_g is not None
    tm, tn = _matmul_tiles(m, k, n, a_bytes=a.dtype.itemsize, n_a=1, cast=cast, n_out=2 if res is not None else 1)
    in_specs = [pl.BlockSpec((tm, k), lambda i, j: (i, 0)), pl.BlockSpec((1, k, tn), lambda i, j: (layer, 0, j))]
    args = [a, w]
    if norm_g is not None:
        in_specs.append(pl.BlockSpec((1, k), lambda i, j: (0, 0)))
        args.append(norm_g.reshape(1, k))
    if res is not None:
        in_specs.append(pl.BlockSpec((tm, tn), lambda i, j: (i, j)))
        args.append(res)
    return pl.pallas_call(
        functools.partial(_mm_kernel, has_res=res is not None, cast=cast, norm=norm_g is not None),
        grid=(m // tm, n // tn),
        in_specs=in_specs,
        out_specs=pl.BlockSpec((tm, tn), lambda i, j: (i, j)),
        out_shape=jax.ShapeDtypeStruct((m, n), out_dtype),
        scratch_shapes=[pltpu.VMEM((tm, k), BF16)] if cast else [],
        compiler_params=_cparams(("parallel", "arbitrary")),
        name="matmul_res" if res is not None else "matmul",
    )(*args)


def _mixmm_kernel(u_ref, p_ref, mu_ref, w_ref, o_ref, xm_ref):
    @pl.when(pl.program_id(2) == 0)
    def _():
        u = u_ref[...]
        xm_ref[...] = (u + (p_ref[...] - u) * mu_ref[0]).astype(BF16)

    o_ref[0] = _dot(xm_ref[...], w_ref[0]).astype(o_ref.dtype)


def mix_matmul(u, prev, mu, w, out_dtype=F32):
    m, k = u.shape
    s, _, n = w.shape
    tm, tn = _matmul_tiles(m, k, n, a_bytes=4, n_a=2, cast=True, n_out=1)
    return pl.pallas_call(
        _mixmm_kernel,
        grid=(m // tm, s, n // tn),
        in_specs=[
            pl.BlockSpec((tm, k), lambda i, si, j: (i, 0)),
            pl.BlockSpec((tm, k), lambda i, si, j: (i, 0)),
            pl.BlockSpec((1, 1, k), lambda i, si, j: (si, 0, 0)),
            pl.BlockSpec((1, k, tn), lambda i, si, j: (si, 0, j)),
        ],
        out_specs=pl.BlockSpec((1, tm, tn), lambda i, si, j: (si, i, j)),
        out_shape=jax.ShapeDtypeStruct((s, m, n), out_dtype),
        scratch_shapes=[pltpu.VMEM((tm, k), BF16)],
        compiler_params=_cparams(("parallel", "arbitrary", "arbitrary")),
        name="mix_matmul",
    )(u, prev, mu.reshape(s, 1, k), w)


def _conv_silu(cur, car_ref, cols, w, b, lq, lb):
    car_ref[8:8 + lq, cols] = cur
    acc = b + cur * w[SSD_CONV - 1:SSD_CONV]
    for s in range(1, SSD_CONV):
        acc = acc + car_ref[8 - s:8 - s + lq, cols] * w[SSD_CONV - 1 - s:SSD_CONV - s]
    car_ref[0:8, cols] = cur[lb - 8:lb]
    return _silu(acc)


def _ssd_kernel(z_ref, x_ref, b_ref, c_ref, dt_ref, ci_ref, s0_ref, cw_ref, cb_ref, dtb_ref, alog_ref,
                d_ref, ng_ref, *rest, lq, lo, hi, nchunks, out_layer):
    y_ref, sout_ref, car, st = rest[-4:]
    c = pl.program_id(1)
    gw = SSD_HPG * HEAD_DIM

    @pl.when(c == 0)
    def _():
        car[0:8, :] = ci_ref[0]
        st[...] = s0_ref[0, 0].reshape(SSD_GROUPS * gw, SSD_STATE)

    gens = [_ssd_group(g, z_ref, x_ref, b_ref, c_ref, dt_ref, cw_ref, cb_ref, dtb_ref, alog_ref, d_ref,
                       ng_ref, y_ref, car, st, lb=lq, lo=lo, hi=hi) for g in range(SSD_GROUPS)]
    for _ in zip(*gens):
        pass

    @pl.when(c == nchunks - 1)
    def _():
        for layer in range(sout_ref.shape[0]):
            if layer == out_layer:
                sout_ref[layer, 0] = st[...].reshape(SSD_GROUPS * SSD_HPG, HEAD_DIM, SSD_STATE)
            else:
                sout_ref[layer, 0] = jnp.zeros(sout_ref.shape[2:], F32)


def _ssd_group(g, z_ref, x_ref, b_ref, c_ref, dt_ref, cw_ref, cb_ref, dtb_ref, alog_ref, d_ref, ng_ref,
               y_ref, car, st, *, lb, lo, hi):
    c = pl.program_id(1)
    lq = max(lb, MIN_CHUNK_ROWS)
    gw = SSD_HPG * HEAD_DIM
    d_inner = SSD_GROUPS * gw
    xs = slice(g * gw, (g + 1) * gw)
    ns = slice(g * SSD_STATE, (g + 1) * SSD_STATE)
    bs = slice(d_inner + g * SSD_STATE, d_inner + (g + 1) * SSD_STATE)
    cs_ = slice(d_inner + (SSD_GROUPS + g) * SSD_STATE, d_inner + (SSD_GROUPS + g + 1) * SSD_STATE)

    f32 = lambda x: _pad_rows(x.astype(F32), lq)
    xc = _conv_silu(f32(x_ref[:, xs]), car, xs, cw_ref[:, xs], cb_ref[:, xs], lq, lb)
    bc = _conv_silu(f32(b_ref[:, ns]), car, bs, cw_ref[:, bs], cb_ref[:, bs], lq, lb)
    cc = _conv_silu(f32(c_ref[:, ns]), car, cs_, cw_ref[:, cs_], cb_ref[:, cs_], lq, lb)

    ti = lax.broadcasted_iota(jnp.int32, (lq, 1), 0)
    pos = c * lb + ti
    valid = (ti < lb) & (pos >= lo) & (pos < hi)
    lane = lax.broadcasted_iota(jnp.int32, (lq, 128), 1)
    dt = _softplus(f32(dt_ref[:, ns]) + dtb_ref[:, ns])
    dt = jnp.where(valid & (lane < SSD_HPG), dt, 0.0)
    la = dt * (-jnp.exp(alog_ref[:, ns]))
    yield

    ri = lax.broadcasted_iota(jnp.int32, (lq, lq), 0)
    ci = lax.broadcasted_iota(jnp.int32, (lq, lq), 1)
    tril = (ri >= ci).astype(BF16)
    triu = (ri <= ci).astype(BF16)
    acum = _dot_split_r(tril, la)
    acum_t = sum(_tn(part, triu) for part in _split(la, 3))
    yield
    a_end = acum[lq - 1:lq]
    dec_end = jnp.exp(a_end - acum)
    e_in = jnp.exp(acum)
    cd = jnp.exp(a_end)

    bcp = bc.astype(BF16)
    ccb = cc.astype(BF16)
    g_sc = _nt(ccb, bcp)
    st_old = st[xs, :]
    y_in = _nt(ccb, st_old.astype(BF16))
    yield

    causal = ci <= ri
    lane_q = lax.broadcasted_iota(jnp.int32, (lq, 128), 1) < HEAD_DIM
    lane_k = lane_q
    row_k = lax.broadcasted_iota(jnp.int32, (128, 1), 0) < HEAD_DIM
    dvec = d_ref[:, ns]

    er = lax.broadcasted_iota(jnp.int32, (2 * LANES, gw), 0)
    ec = lax.broadcasted_iota(jnp.int32, (2 * LANES, gw), 1)
    spread_m = (_imod(er, LANES) == _idiv(ec, HEAD_DIM)).astype(BF16)
    spread = lambda f: _dot(jnp.concatenate(_split(f, 2), axis=1), spread_m)
    dt_x = spread(dt)
    e_x = spread(e_in)
    dec_x = spread(dec_end)
    yield

    ys = []
    for p in range(SSD_HPG // 2):
        h0, h1 = 2 * p, 2 * p + 1
        ps = slice(128 * p, 128 * (p + 1))
        xp = xc[:, ps]
        vp = xp * dt_x[:, ps]
        vpp = vp
        yp = y_in[:, ps] * e_x[:, ps]
        yp = yp + xp * jnp.where(lane_q, dvec[:, h0:h0 + 1], dvec[:, h1:h1 + 1])
        for hh, h in ((0, h0), (1, h1)):
            seg = acum[:, h:h + 1] - acum_t[h:h + 1, :]
            lm = jnp.exp(jnp.where(causal, seg, NEG))
            pm = (g_sc * lm).astype(BF16)
            vm = jnp.where(lane_k if hh == 0 else jnp.logical_not(lane_k), vpp, 0.0).astype(BF16)
            yp = yp + _dot(pm, vm)
        ys.append(yp)
        vend = vpp * dec_x[:, ps]
        upd = _tn(vend.astype(BF16), bcp)
        cdp = jnp.where(row_k, cd[:, h0:h0 + 1], cd[:, h1:h1 + 1])
        st[g * gw + 128 * p:g * gw + 128 * (p + 1), :] = st_old[128 * p:128 * (p + 1), :] * cdp + upd
        yield

    y = jnp.concatenate(ys, axis=1)
    y = y * _silu(f32(z_ref[:, xs]))
    ms = jnp.mean(y * y, axis=-1, keepdims=True)
    y = y * lax.rsqrt(ms + EPS) * ng_ref[:, xs]
    y_ref[:, xs] = jnp.where(valid, y, 0.0)[:lb].astype(y_ref.dtype)
    yield


def ssd_scan(proj, dt_raw, conv_init, s0_all, layer, s_buf, conv_w, conv_b, dtb, alog, dskip, norm_g,
             *, bsz, tp, lq, lo, hi):
    m = proj.shape[0]
    heads = SSD_GROUPS * SSD_HPG
    nlayers = s0_all.shape[0]
    state_spec = pl.BlockSpec((1, 1, heads, HEAD_DIM, SSD_STATE), lambda b, c: (layer, b, 0, 0, 0))
    if s_buf is not None:
        out_state_spec, out_layer = state_spec, 0
        extra_specs, extra_args, aliases = [pl.BlockSpec(memory_space=pl.ANY)], [s_buf], {13: 1}
    else:
        out_state_spec = pl.BlockSpec((nlayers, 1, heads, HEAD_DIM, SSD_STATE), lambda b, c: (0, b, 0, 0, 0))
        out_layer = layer
        extra_specs, extra_args, aliases = [], [], {}
    nch = tp // lq
    d_inner = heads * HEAD_DIM
    gn = SSD_GROUPS * SSD_STATE
    conv_dim = d_inner + 2 * gn
    row = lambda b, c: b * nch + c
    const = lambda b, c: (0, 0)
    kern = functools.partial(_ssd_kernel, lq=lq, lo=lo, hi=hi, nchunks=nch, out_layer=out_layer)
    return pl.pallas_call(
        kern,
        grid=(bsz, nch),
        in_specs=[
            pl.BlockSpec((lq, d_inner), lambda b, c: (row(b, c), 0)),
            pl.BlockSpec((lq, d_inner), lambda b, c: (row(b, c), 1)),
            pl.BlockSpec((lq, gn), lambda b, c: (row(b, c), 2 * d_inner // gn)),
            pl.BlockSpec((lq, gn), lambda b, c: (row(b, c), 2 * d_inner // gn + 1)),
            pl.BlockSpec((lq, gn), lambda b, c: (row(b, c), 0)),
            pl.BlockSpec((1, 8, conv_dim), lambda b, c: (b, 0, 0)),
            state_spec,
            pl.BlockSpec((SSD_CONV, conv_dim), const),
            pl.BlockSpec((1, conv_dim), const),
            pl.BlockSpec((1, gn), const),
            pl.BlockSpec((1, gn), const),
            pl.BlockSpec((1, gn), const),
            pl.BlockSpec((1, d_inner), const),
        ] + extra_specs,
        out_specs=[
            pl.BlockSpec((lq, d_inner), lambda b, c: (row(b, c), 0)),
            out_state_spec,
        ],
        out_shape=[
            jax.ShapeDtypeStruct((m, d_inner), proj.dtype),
            jax.ShapeDtypeStruct(s0_all.shape, F32),
        ],
        scratch_shapes=[pltpu.VMEM((8 + max(lq, MIN_CHUNK_ROWS), conv_dim), F32),
                        pltpu.VMEM((d_inner, SSD_STATE), F32)],
        input_output_aliases=aliases,
        compiler_params=_cparams(("parallel", "arbitrary")),
        name="ssd_scan",
    )(proj, proj, proj, proj, dt_raw, conv_init, s0_all, conv_w, conv_b.reshape(1, -1),
      dtb.reshape(1, -1), alog.reshape(1, -1), dskip.reshape(1, -1), norm_g.reshape(1, -1), *extra_args)


def _ret_kernel(q_ref, k_ref, v_ref, g_ref, cos_ref, sin_ref, lg_ref, s0_ref, ng_ref,
                y_ref, sout_ref, *, lq, lo, hi):
    c = pl.program_id(1)

    @pl.when(c == 0)
    def _():
        sout_ref[...] = s0_ref[...]

    gens = [_ret_head(h, q_ref, k_ref, v_ref, g_ref, cos_ref, sin_ref, lg_ref, ng_ref, y_ref, sout_ref,
                      lb=lq, lo=lo, hi=hi) for h in range(RET_HEADS)]
    for _ in zip(*gens):
        pass


def _ret_head(h, q_ref, k_ref, v_ref, g_ref, cos_ref, sin_ref, lg_ref, ng_ref, y_ref, st_ref, *, lb, lo, hi):
    c = pl.program_id(1)
    lq = max(lb, MIN_CHUNK_ROWS)
    f32 = lambda x: _pad_rows(x.astype(F32), lq)
    qs = slice(h * RET_QK, (h + 1) * RET_QK)
    vs = slice(h * RET_V, (h + 1) * RET_V)
    lg = lg_ref[h][0:1, 0:1]
    nv = float(hi - lo)

    def count(p):
        return jnp.clip((p + 1 - lo).astype(F32), 0.0, nv)

    base = c * lb
    ti = lax.broadcasted_iota(jnp.int32, (lq, 1), 0)
    pos_i = base + ti
    valid = (ti < lb) & (pos_i >= lo) & (pos_i < hi)
    cnt_i = count(pos_i)
    cnt_j = count(base + lax.broadcasted_iota(jnp.int32, (1, lq), 1))
    cnt_jc = cnt_i
    cnt0 = count(base - 1 + jnp.zeros((1, 1), jnp.int32))
    cnt_end = count(base + lb - 1 + jnp.zeros((1, 1), jnp.int32))

    cos = f32(cos_ref[...])
    sin = f32(sin_ref[...])
    half = RET_QK // 2

    def rot(x):
        x1, x2 = x[:, :half], x[:, half:]
        return jnp.concatenate([x1 * cos - x2 * sin, x1 * sin + x2 * cos], axis=1)

    qr = rot(f32(q_ref[:, qs])).astype(BF16)
    kr = jnp.where(valid, rot(f32(k_ref[:, qs])) * (RET_QK ** -0.5), 0.0)
    v = jnp.where(valid, f32(v_ref[:, vs]), 0.0)
    krp = kr.astype(BF16)
    vp = v
    yield

    sc = _nt(qr, krp)
    s_old = st_ref[0, h]
    y_in = _nt(qr, s_old.astype(BF16))
    yield
    qi = lax.broadcasted_iota(jnp.int32, (lq, lq), 0)
    kj = lax.broadcasted_iota(jnp.int32, (lq, lq), 1)
    dm = jnp.exp(jnp.where(kj <= qi, lg * (cnt_i - cnt_j), NEG))
    y = _dot((sc * dm).astype(BF16), vp.astype(BF16))
    y = y + y_in * jnp.exp(lg * (cnt_i - cnt0))
    vend = vp * jnp.exp(lg * (cnt_end - cnt_jc))
    st_ref[0, h] = s_old * jnp.exp(lg * (cnt_end - cnt0)) + _tn(vend.astype(BF16), krp)
    yield

    ms = jnp.mean(y * y, axis=-1, keepdims=True)
    y = y * lax.rsqrt(ms + EPS) * ng_ref[:, vs] * _silu(f32(g_ref[:, vs]))
    y_ref[:, vs] = jnp.where(valid, y, 0.0)[:lb].astype(y_ref.dtype)
    yield


def ret_scan(proj, cos, sin, lg, s0, norm_g, *, bsz, tp, lq, lo, hi):
    m = proj.shape[0]
    nch = tp // lq
    d_inner = RET_HEADS * RET_V
    d_qk = RET_HEADS * RET_QK
    row = lambda b, c: b * nch + c
    state_spec = pl.BlockSpec((1, RET_HEADS, RET_V, RET_QK), lambda b, c: (b, 0, 0, 0))
    kern = functools.partial(_ret_kernel, lq=lq, lo=lo, hi=hi)
    return pl.pallas_call(
        kern,
        grid=(bsz, nch),
        in_specs=[
            pl.BlockSpec((lq, d_qk), lambda b, c: (row(b, c), 0)),
            pl.BlockSpec((lq, d_qk), lambda b, c: (row(b, c), 1)),
            pl.BlockSpec((lq, d_inner), lambda b, c: (row(b, c), 2 * d_qk // d_inner)),
            pl.BlockSpec((lq, d_inner), lambda b, c: (row(b, c), 2 * d_qk // d_inner + 1)),
            pl.BlockSpec((lq, RET_QK // 2), lambda b, c: (c, 0)),
            pl.BlockSpec((lq, RET_QK // 2), lambda b, c: (c, 0)),
            pl.BlockSpec((RET_HEADS, 8, 128), lambda b, c: (0, 0, 0)),
            state_spec,
            pl.BlockSpec((1, d_inner), lambda b, c: (0, 0)),
        ],
        out_specs=[
            pl.BlockSpec((lq, d_inner), lambda b, c: (row(b, c), 0)),
            state_spec,
        ],
        out_shape=[
            jax.ShapeDtypeStruct((m, d_inner), proj.dtype),
            jax.ShapeDtypeStruct((bsz, RET_HEADS, RET_V, RET_QK), F32),
        ],
        compiler_params=_cparams(("parallel", "arbitrary")),
        name="ret_scan",
    )(proj, proj, proj, proj, cos, sin, lg, s0, norm_g.reshape(1, -1))


def _rwkv_kernel(r_ref, k_ref, v_ref, g_ref, lw_ref, la_ref, bw_ref, ba_ref, par_ref, s0_ref,
                 y_ref, sout_ref, st, *, lo, hi, nchunks, ngrp):
    c = pl.program_id(2)
    w4 = RWKV_HPB * HEAD_DIM
    r2 = lax.broadcasted_iota(jnp.int32, (w4, w4), 0)
    c2 = lax.broadcasted_iota(jnp.int32, (w4, w4), 1)
    blk = _idiv(r2, HEAD_DIM) == _idiv(c2, HEAD_DIM)

    @pl.when(c == 0)
    def _():
        tile = (lax.broadcasted_iota(jnp.int32, (HEAD_DIM, w4), 0)
                == _imod(lax.broadcasted_iota(jnp.int32, (HEAD_DIM, w4), 1), HEAD_DIM)).astype(BF16)
        for gi in range(ngrp):
            s0 = s0_ref[0, gi * RWKV_HPB:(gi + 1) * RWKV_HPB].reshape(w4, HEAD_DIM)
            st[gi] = jnp.where(blk, _dot_split_l(s0, tile), 0.0)

    has_rows = ((c + 1) * RWKV_CHUNK > lo) & (c * RWKV_CHUNK < hi)

    @pl.when(has_rows)
    def _():
        gens = [_rwkv_group(gi, r_ref, k_ref, v_ref, g_ref, lw_ref, la_ref, bw_ref, ba_ref, par_ref,
                            y_ref, st, lo=lo, hi=hi) for gi in range(ngrp)]
        for _ in zip(*gens):
            pass

    @pl.when(jnp.logical_not(has_rows))
    def _():
        y_ref[...] = jnp.zeros(y_ref.shape, y_ref.dtype)

    @pl.when(c == nchunks - 1)
    def _():
        tile_t = (_imod(lax.broadcasted_iota(jnp.int32, (w4, HEAD_DIM), 0), HEAD_DIM)
                  == lax.broadcasted_iota(jnp.int32, (w4, HEAD_DIM), 1)).astype(BF16)
        for gi in range(ngrp):
            sout_ref[0, gi * RWKV_HPB:(gi + 1) * RWKV_HPB] = _dot_split_l(st[gi], tile_t).reshape(
                RWKV_HPB, HEAD_DIM, HEAD_DIM)


def _rwkv_group(gi, r_ref, k_ref, v_ref, g_ref, lw_ref, la_ref, bw_ref, ba_ref, par_ref, y_ref, st,
                *, lo, hi):
    c = pl.program_id(2)
    cs = RWKV_CHUNK
    w4 = RWKV_HPB * HEAD_DIM
    sl = slice(gi * w4, (gi + 1) * w4)

    nr = RWKV_HPB * cs
    ones_bd = (_idiv(lax.broadcasted_iota(jnp.int32, (w4, w4), 0), HEAD_DIM)
               == _idiv(lax.broadcasted_iota(jnp.int32, (w4, w4), 1), HEAD_DIM)).astype(BF16)
    blk = (_idiv(lax.broadcasted_iota(jnp.int32, (nr, w4), 0), cs)
           == _idiv(lax.broadcasted_iota(jnp.int32, (nr, w4), 1), HEAD_DIM))
    r2 = lax.broadcasted_iota(jnp.int32, (nr, nr), 0)
    c2 = lax.broadcasted_iota(jnp.int32, (nr, nr), 1)
    same = _idiv(r2, cs) == _idiv(c2, cs)

    def segsum(x):
        return _dot_split_l(x, ones_bd, terms=2)

    par = par_ref[:, sl]
    w0, a0, k_k, k_a, r_k, ln_g, ln_b = (par[i:i + 1] for i in range(7))

    r = r_ref[0, :, sl].astype(F32)
    k = k_ref[0, :, sl].astype(F32)
    v = v_ref[0, :, sl].astype(F32)
    g = g_ref[0, :, sl].astype(F32)
    w_raw = w0 + _dot(jnp.tanh(lw_ref[0]).astype(BF16), bw_ref[0, :, sl])
    a = jax.nn.sigmoid(a0 + _dot(la_ref[0].astype(BF16), ba_ref[0, :, sl]))

    ti = lax.broadcasted_iota(jnp.int32, (cs, 1), 0)
    pos = c * cs + ti
    valid = (pos >= lo) & (pos < hi)

    lw = -jnp.exp(-_softplus(-w_raw) - 0.5)
    kk = k * k_k
    kk = kk / jnp.maximum(jnp.sqrt(segsum(kk * kk)), 1e-12)
    yield
    kp = k * (1.0 + (a - 1.0) * k_a)
    lw = jnp.where(valid, lw, 0.0)
    kk = jnp.where(valid, kk, 0.0)
    kp = jnp.where(valid, kp, 0.0)
    vm = jnp.where(valid, v, 0.0)

    tri = (lax.broadcasted_iota(jnp.int32, (cs, cs), 0) >= lax.broadcasted_iota(jnp.int32, (cs, cs), 1)).astype(BF16)
    cw = _dot_split_r(tri, lw)
    yield
    cwl = cw[cs - 1:cs]
    wt = jnp.exp(cw)
    wi = jnp.exp(-cw)
    wend = jnp.exp(cwl - cw)
    b = kk * a
    at = -kk * jnp.exp(cw - lw)
    rt = r * wt

    def bd(x):
        return jnp.where(blk, jnp.concatenate([x] * RWKV_HPB, axis=0), 0.0).astype(BF16)

    lhs = jnp.concatenate([bd(at), bd(rt)], axis=0)
    rhs = jnp.concatenate([bd(b * wi), bd(kp * wi)], axis=0)
    sc = _nt(lhs, rhs)
    yield
    tt = _imod(r2, cs)
    jj = _imod(c2, cs)
    strict = same & (tt > jj)
    incl = same & (tt >= jj)
    mab = jnp.where(strict, sc[:nr, :nr], 0.0)
    mak = jnp.where(strict, sc[:nr, nr:], 0.0)
    nrb = jnp.where(incl, sc[nr:, :nr], 0.0)
    nrk = jnp.where(incl, sc[nr:, nr:], 0.0)

    tinv = (r2 == c2).astype(F32) + jnp.where(
        (_idiv(r2, 2) == _idiv(c2, 2)) & (_imod(tt, 2) == 1) & (_imod(jj, 2) == 0), mab, 0.0)
    msz = 2
    while msz < cs:
        off = ((_idiv(r2, 2 * msz) == _idiv(c2, 2 * msz)) & (_imod(tt, 2 * msz) >= msz)
               & (_imod(jj, 2 * msz) < msz))
        tb = tinv.astype(BF16)
        tno = _dot(tb, jnp.where(off, mab, 0.0).astype(BF16)).astype(BF16)
        yield
        tinv = tinv + _dot(tno, tb)
        yield
        msz *= 2

    s_old = st[gi]
    x = _nt(lhs, s_old.astype(BF16))
    yield
    vbd = bd(vm)
    u = _dot(tinv.astype(BF16), (x[:nr] + _dot(mak.astype(BF16), vbd)).astype(BF16))
    yield
    ub = u.astype(BF16)
    yb = x[nr:] + _dot(nrb.astype(BF16), ub) + _dot(nrk.astype(BF16), vbd)
    yield
    s_new = s_old * wt[cs - 1:cs] + _tn(jnp.concatenate([ub, vbd], axis=0),
                                        jnp.concatenate([bd(b * wend), bd(kp * wend)], axis=0))
    st[gi] = s_new
    yield

    y = sum(yb[i * cs:(i + 1) * cs] for i in range(RWKV_HPB))
    inv = 1.0 / HEAD_DIM
    yc = y - segsum(y) * inv
    yield
    y = yc * lax.rsqrt(segsum(yc * yc) * inv + RWKV_LN_EPS) * ln_g + ln_b
    y = (y + segsum(r * kp * r_k) * v) * _silu(g)
    y_ref[:, sl] = jnp.where(valid, y, 0.0).astype(BF16)
    yield


def rwkv_scan(rkvg, lora1, lora_b, par, s0, *, bsz, tp, lo, hi):
    _, m, e = rkvg.shape
    cs = RWKV_CHUNK
    nch = tp // cs
    heads = e // HEAD_DIM
    ngrp = RWKV_GROUPS_PER_STEP
    hb = RWKV_HPB * ngrp
    wb = hb * HEAD_DIM
    row = lambda b, h, c: b * nch + c
    kern = functools.partial(_rwkv_kernel, lo=lo, hi=hi, nchunks=nch, ngrp=ngrp)
    proj_spec = lambda s: pl.BlockSpec((1, cs, wb), lambda b, h, c: (s, row(b, h, c), h))
    return pl.pallas_call(
        kern,
        grid=(bsz, heads // hb, nch),
        in_specs=[
            proj_spec(0), proj_spec(1), proj_spec(2), proj_spec(3),
            pl.BlockSpec((1, cs, RWKV_LORA_PAD), lambda b, h, c: (0, row(b, h, c), 0)),
            pl.BlockSpec((1, cs, RWKV_LORA_PAD), lambda b, h, c: (1, row(b, h, c), 0)),
            pl.BlockSpec((1, RWKV_LORA_PAD, wb), lambda b, h, c: (0, 0, h)),
            pl.BlockSpec((1, RWKV_LORA_PAD, wb), lambda b, h, c: (1, 0, h)),
            pl.BlockSpec((8, wb), lambda b, h, c: (0, h)),
            pl.BlockSpec((1, hb, HEAD_DIM, HEAD_DIM), lambda b, h, c: (b, h, 0, 0)),
        ],
        out_specs=[
            pl.BlockSpec((cs, wb), lambda b, h, c: (row(b, h, c), h)),
            pl.BlockSpec((1, hb, HEAD_DIM, HEAD_DIM), lambda b, h, c: (b, h, 0, 0)),
        ],
        out_shape=[
            jax.ShapeDtypeStruct((m, e), BF16),
            jax.ShapeDtypeStruct((bsz, heads, HEAD_DIM, HEAD_DIM), F32),
        ],
        scratch_shapes=[pltpu.VMEM((ngrp, RWKV_HPB * HEAD_DIM, RWKV_HPB * HEAD_DIM), F32)],
        compiler_params=_cparams(("parallel", "parallel", "arbitrary")),
        name="rwkv_scan",
    )(rkvg, rkvg, rkvg, rkvg, lora1, lora1, lora_b, lora_b, par, s0)


def _rwkv_short_kernel(r_ref, k_ref, v_ref, g_ref, lw_ref, la_ref, bw_ref, ba_ref, par_ref, s0_ref,
                       ones_ref, tile_ref, y_ref, sout_ref, *, nvalid, rows_out):
    ct = 8
    hp = RWKV_SHORT_HEADS
    wl = hp * HEAD_DIM
    nr = hp * ct
    ones_bd = ones_ref[...]
    tile_t = tile_ref[...]

    def segsum(x):
        return jnp.concatenate(
            [_dot_split_l(x[:, 256 * j:256 * (j + 1)], ones_bd, terms=2) for j in range(wl // 256)], axis=1)

    par = par_ref[...]
    w0, a0, k_k, k_a, r_k, ln_g, ln_b = (par[i:i + 1] for i in range(7))
    r, k, v, g = r_ref[0], k_ref[0], v_ref[0], g_ref[0]
    w_raw = w0 + _dot(jnp.tanh(lw_ref[0]).astype(BF16), bw_ref[0])
    a = jax.nn.sigmoid(a0 + _dot(la_ref[0].astype(BF16), ba_ref[0]))

    ti = lax.broadcasted_iota(jnp.int32, (ct, 1), 0)
    valid = ti < nvalid
    lw = jnp.where(valid, -jnp.exp(-_softplus(-w_raw) - 0.5), 0.0)
    kk = k * k_k
    kk = jnp.where(valid, kk / jnp.maximum(jnp.sqrt(segsum(kk * kk)), 1e-12), 0.0)
    kp = jnp.where(valid, k * (1.0 + (a - 1.0) * k_a), 0.0)
    vm = jnp.where(valid, v, 0.0)

    cw = lw
    for s in (1, 2, 4):
        cw = cw + jnp.where(ti >= s, pltpu.roll(cw, s, 0), 0.0)
    cwl = cw[ct - 1:ct]
    wend = jnp.exp(cwl - cw)
    wi = jnp.exp(-cw)
    b = kk * a
    at = -kk * jnp.exp(cw - lw)
    rt = r * jnp.exp(cw)

    rr = lax.broadcasted_iota(jnp.int32, (nr, wl), 0)
    cc = lax.broadcasted_iota(jnp.int32, (nr, wl), 1)
    blk = _idiv(rr, ct) == _idiv(cc, HEAD_DIM)

    def bd(x):
        return jnp.where(blk, jnp.concatenate([x] * hp, axis=0), 0.0).astype(BF16)

    lhs = jnp.concatenate([bd(at), bd(rt)], axis=0)
    rhs = jnp.concatenate([bd(b * wi), bd(kp * wi)], axis=0)
    sc = _nt(lhs, rhs)
    ri = lax.broadcasted_iota(jnp.int32, (2 * nr, 2 * nr), 0)
    ci = lax.broadcasted_iota(jnp.int32, (2 * nr, 2 * nr), 1)
    same = _idiv(_imod(ri, nr), ct) == _idiv(_imod(ci, nr), ct)
    tt = _imod(ri, ct)
    jj = _imod(ci, ct)
    sc = jnp.where(same & (tt + (ri >= nr).astype(jnp.int32) > jj), sc, 0.0)
    mab, mak, nrb, nrk = sc[:nr, :nr], sc[:nr, nr:], sc[nr:, :nr], sc[nr:, nr:]

    r1 = lax.broadcasted_iota(jnp.int32, (nr, nr), 0)
    c1 = lax.broadcasted_iota(jnp.int32, (nr, nr), 1)
    t1 = _imod(r1, ct)
    j1 = _imod(c1, ct)
    tinv = (r1 == c1).astype(F32) + jnp.where(
        (_idiv(r1, 2) == _idiv(c1, 2)) & (_imod(t1, 2) == 1) & (_imod(j1, 2) == 0), mab, 0.0)
    msz = 2
    while msz < nvalid:
        off = ((_idiv(r1, 2 * msz) == _idiv(c1, 2 * msz)) & (_imod(t1, 2 * msz) >= msz)
               & (_imod(j1, 2 * msz) < msz))
        tb = tinv.astype(BF16)
        tinv = tinv + _dot(_dot(tb, jnp.where(off, mab, 0.0).astype(BF16)).astype(BF16), tb)
        msz *= 2

    s_old = s0_ref[0].reshape(wl, HEAD_DIM)
    lhs_rows = _dot(lhs, tile_t).astype(BF16)
    x = _nt(lhs_rows, s_old.astype(BF16))
    xa = jnp.where(blk, x[:nr], 0.0)
    xr = jnp.where(blk, x[nr:], 0.0)
    vbd = bd(vm)
    u = _dot(tinv.astype(BF16), (xa + _dot(mak.astype(BF16), vbd)).astype(BF16))
    uv = jnp.concatenate([u.astype(BF16), vbd], axis=0)
    yb = xr + _dot(jnp.concatenate([nrb, nrk], axis=1).astype(BF16), uv)
    bk_rows = _dot(jnp.concatenate([bd(b * wend), bd(kp * wend)], axis=0), tile_t).astype(BF16)
    ds = _tn(uv, bk_rows)

    r16 = lax.broadcasted_iota(jnp.int32, (hp, wl), 0)
    c16 = lax.broadcasted_iota(jnp.int32, (hp, wl), 1)
    wc = jnp.exp(cwl)
    wc_rows = _dot_split_l(jnp.where(r16 == _idiv(c16, HEAD_DIM), wc, 0.0), tile_t)
    rsel = lax.broadcasted_iota(jnp.int32, (hp, HEAD_DIM), 0)
    for h in range(hp):
        rs = slice(h * HEAD_DIM, (h + 1) * HEAD_DIM)
        wc_h = jnp.sum(jnp.where(rsel == h, wc_rows, 0.0), axis=0, keepdims=True)
        sout_ref[0, h] = s_old[rs] * wc_h + ds[rs]

    y = yb[0:ct]
    for h in range(1, hp):
        y = y + yb[h * ct:(h + 1) * ct]
    inv = 1.0 / HEAD_DIM
    yc = y - segsum(y) * inv
    y = yc * lax.rsqrt(segsum(yc * yc) * inv + RWKV_LN_EPS) * ln_g + ln_b
    y = (y + segsum(r * kp * r_k) * v) * _silu(g)
    y = jnp.where(valid, y, 0.0)
    y_ref[...] = _pad_rows(y, rows_out).astype(y_ref.dtype)


def rwkv_short(rkvg, lora1, lora_b, par, s0, *, bsz, tp, hi):
    _, m, e = rkvg.shape
    heads = e // HEAD_DIM
    hp = RWKV_SHORT_HEADS
    wl = hp * HEAD_DIM
    rb = tp // 8
    w4 = RWKV_HPB * HEAD_DIM
    ones_bd = (jnp.arange(w4)[:, None] // HEAD_DIM == jnp.arange(w4)[None, :] // HEAD_DIM).astype(BF16)
    tile_t = (jnp.arange(wl)[:, None] % HEAD_DIM == jnp.arange(HEAD_DIM)[None, :]).astype(BF16)
    kern = functools.partial(_rwkv_short_kernel, nvalid=hi, rows_out=tp)
    proj_spec = lambda s: pl.BlockSpec((1, 8, wl), lambda b, h: (s, b * rb, h))
    return pl.pallas_call(
        kern,
        grid=(bsz, heads // hp),
        in_specs=[
            proj_spec(0), proj_spec(1), proj_spec(2), proj_spec(3),
            pl.BlockSpec((1, 8, RWKV_LORA_PAD), lambda b, h: (0, b * rb, 0)),
            pl.BlockSpec((1, 8, RWKV_LORA_PAD), lambda b, h: (1, b * rb, 0)),
            pl.BlockSpec((1, RWKV_LORA_PAD, wl), lambda b, h: (0, 0, h)),
            pl.BlockSpec((1, RWKV_LORA_PAD, wl), lambda b, h: (1, 0, h)),
            pl.BlockSpec((8, wl), lambda b, h: (0, h)),
            pl.BlockSpec((1, hp, HEAD_DIM, HEAD_DIM), lambda b, h: (b, h, 0, 0)),
            pl.BlockSpec((w4, w4), lambda b, h: (0, 0)),
            pl.BlockSpec((wl, HEAD_DIM), lambda b, h: (0, 0)),
        ],
        out_specs=[
            pl.BlockSpec((tp, wl), lambda b, h: (b, h)),
            pl.BlockSpec((1, hp, HEAD_DIM, HEAD_DIM), lambda b, h: (b, h, 0, 0)),
        ],
        out_shape=[
            jax.ShapeDtypeStruct((m, e), rkvg.dtype),
            jax.ShapeDtypeStruct((bsz, heads, HEAD_DIM, HEAD_DIM), F32),
        ],
        compiler_params=_cparams(("parallel", "parallel")),
        name="rwkv_short",
    )(rkvg, rkvg, rkvg, rkvg, lora1, lora1, lora_b, lora_b, par, s0, ones_bd, tile_t)


def _rwkv_lanes_kernel(r_ref, k_ref, v_ref, g_ref, lw_ref, la_ref, bw_ref, ba_ref, par_ref, s0_ref,
                       y_ref, sout_ref, tk, tw, tb, tq, tr, tv, ty, *, nt, nb):
    hp = RWKV_LANE_HEADS
    wl = hp * HEAD_DIM
    ri = lax.broadcasted_iota(jnp.int32, (wl, wl), 0)
    ci = lax.broadcasted_iota(jnp.int32, (wl, wl), 1)
    ones_bd = (_idiv(ri, HEAD_DIM) == _idiv(ci, HEAD_DIM)).astype(BF16)

    def segsum(x):
        return _dot_split_l(x, ones_bd, terms=2)

    par = par_ref[...]
    w0, a0, k_k, k_a, r_k, ln_g, ln_b = (par[i:i + 1] for i in range(7))
    r, k, v, g = (x[0].astype(F32) for x in (r_ref, k_ref, v_ref, g_ref))
    w_raw = w0 + _dot(jnp.tanh(lw_ref[0]).astype(BF16), bw_ref[0])
    a = jax.nn.sigmoid(a0 + _dot(la_ref[0].astype(BF16), ba_ref[0]))
    decay = jnp.exp(-jnp.exp(-_softplus(-w_raw) - 0.5))
    kk = k * k_k
    kk = kk / jnp.maximum(jnp.sqrt(segsum(kk * kk)), 1e-12)
    kp = k * (1.0 + (a - 1.0) * k_a)
    bb = kk * a

    for t in range(nt):
        rows = slice(t * nb, (t + 1) * nb)
        tk[t] = (-kk[rows]).T
        tw[t] = decay[rows].T
        tb[t] = bb[rows].T
        tq[t] = kp[rows].T
        tr[t] = r[rows].T
        tv[t] = v[rows].T

    for hh in range(hp):
        ks = slice(hh * HEAD_DIM, (hh + 1) * HEAD_DIM)

        def body(vi, carry, hh=hh, ks=ks):
            sv = s0_ref[hh, vi]
            row = hh * HEAD_DIM + vi
            for t in range(nt):
                sa = jnp.sum(sv * tk[t, ks, :], axis=0, keepdims=True)
                sv = sv * tw[t, ks, :] + sa * tb[t, ks, :] + tv[t, pl.ds(row, 1), :] * tq[t, ks, :]
                ty[t, pl.ds(row, 1), :] = jnp.sum(sv * tr[t, ks, :], axis=0, keepdims=True)
            sout_ref[hh, vi] = sv
            return carry

        lax.fori_loop(0, HEAD_DIM, body, 0, unroll=4)

    inv = 1.0 / HEAD_DIM
    for t in range(nt):
        rows = slice(t * nb, (t + 1) * nb)
        y = ty[t].T
        yc = y - segsum(y) * inv
        y = yc * lax.rsqrt(segsum(yc * yc) * inv + RWKV_LN_EPS) * ln_g + ln_b
        y = (y + segsum(r[rows] * kp[rows] * r_k) * v[rows]) * _silu(g[rows])
        y_ref[rows, :] = y.astype(BF16)


def rwkv_lanes(rkvg, lora1, lora_b, par, s0t, *, nt, nb):
    _, m, e = rkvg.shape
    heads = e // HEAD_DIM
    hp = RWKV_LANE_HEADS
    wl = hp * HEAD_DIM
    kern = functools.partial(_rwkv_lanes_kernel, nt=nt, nb=nb)
    proj_spec = lambda s: pl.BlockSpec((1, m, wl), lambda h: (s, 0, h))
    state_spec = pl.BlockSpec((hp, HEAD_DIM, HEAD_DIM, nb), lambda h: (h, 0, 0, 0))
    tile = pltpu.VMEM((nt, wl, nb), F32)
    return pl.pallas_call(
        kern,
        grid=(heads // hp,),
        in_specs=[
            proj_spec(0), proj_spec(1), proj_spec(2), proj_spec(3),
            pl.BlockSpec((1, m, RWKV_LORA_PAD), lambda h: (0, 0, 0)),
            pl.BlockSpec((1, m, RWKV_LORA_PAD), lambda h: (1, 0, 0)),
            pl.BlockSpec((1, RWKV_LORA_PAD, wl), lambda h: (0, 0, h)),
            pl.BlockSpec((1, RWKV_LORA_PAD, wl), lambda h: (1, 0, h)),
            pl.BlockSpec((8, wl), lambda h: (0, h)),
            state_spec,
        ],
        out_specs=[pl.BlockSpec((m, wl), lambda h: (0, h)), state_spec],
        out_shape=[jax.ShapeDtypeStruct((m, e), BF16), jax.ShapeDtypeStruct(s0t.shape, F32)],
        scratch_shapes=[tile] * 7,
        compiler_params=_cparams(("parallel",)),
        name="rwkv_lanes",
    )(rkvg, rkvg, rkvg, rkvg, lora1, lora1, lora_b, lora_b, par, s0t)


def _prep_weights(norm_g, ssd_w_in, ssd_dt_bias, ssd_a_log, ssd_d, ssd_w_out,
                  rwkv_w_rkvg, rwkv_w_lora_a, rwkv_w_lora_b, rwkv_a_lora_a, rwkv_a_lora_b,
                  rwkv_w0, rwkv_a0, rwkv_k_k, rwkv_k_a, rwkv_r_k, rwkv_ln_g, rwkv_ln_b, rwkv_w_out,
                  ret_w_in, ret_w_out):
    d_inner = ssd_w_out.shape[1]
    n_main = d_inner + d_inner + 2 * SSD_GROUPS * SSD_STATE
    ns = ssd_w_in.shape[0]
    d_model = ssd_w_in.shape[1]

    def head_lanes(p):
        p = p.reshape(ns, SSD_GROUPS, 1, SSD_HPG)
        return jnp.pad(p, ((0, 0), (0, 0), (0, 0), (0, 128 - SSD_HPG)))

    w_dt = ssd_w_in[:, :, n_main:].reshape(ns, d_model, SSD_GROUPS, SSD_HPG)
    w_dt = jnp.pad(w_dt, ((0, 0), (0, 0), (0, 0), (0, 128 - SSD_HPG))).reshape(ns, d_model, SSD_GROUPS * 128)
    rank = rwkv_w_lora_a.shape[2]
    lora_a = jnp.stack([rwkv_w_lora_a, rwkv_a_lora_a], axis=1)
    lora_a = jnp.pad(lora_a, ((0, 0), (0, 0), (0, 0), (0, RWKV_LORA_PAD - rank)))
    lora_b = jnp.stack([rwkv_w_lora_b, rwkv_a_lora_b], axis=1)
    lora_b = jnp.pad(lora_b, ((0, 0), (0, 0), (0, RWKV_LORA_PAD - rank), (0, 0)))
    nr = rwkv_w0.shape[0]
    par = jnp.stack([rwkv_w0, rwkv_a0, rwkv_k_k, rwkv_k_a, rwkv_r_k.reshape(nr, -1), rwkv_ln_g, rwkv_ln_b,
                     jnp.zeros_like(rwkv_w0)], axis=1)
    return dict(
        ssd_w_in=ssd_w_in.astype(BF16), ssd_n_main=n_main, ssd_w_dt=w_dt.astype(BF16),
        ssd_dtb=head_lanes(ssd_dt_bias), ssd_alog=head_lanes(ssd_a_log), ssd_dskip=head_lanes(ssd_d),
        ssd_w_out=ssd_w_out.astype(BF16),
        rwkv_w=rwkv_w_rkvg.astype(BF16), rwkv_lora_a=lora_a.astype(BF16), rwkv_lora_b=lora_b.astype(BF16),
        rwkv_par=par, rwkv_w_out=rwkv_w_out.astype(BF16),
        ret_w_in=ret_w_in.astype(BF16), ret_w_out=ret_w_out.astype(BF16),
    )


def _trunk(h, conv_st, ssd_st, shift_st, wkv_st, ret_st, pos, *, bsz, tp, lq, lo, hi, depth,
           norm_g, final_norm_g, wts, ssd_conv_w, ssd_conv_b, ssd_norm_g, rwkv_mu, ret_norm_g):
    d_model = h.shape[1]
    geo = dict(bsz=bsz, tp=tp, lo=lo, hi=hi)
    new_conv, new_shift, new_wkv, new_ret = [], [], [], []
    new_ssd = None
    act = BF16 if lq % 16 == 0 else F32

    half = RET_QK // 2
    inv_freq = 1.0 / (RET_THETA_BASE ** jnp.linspace(0.0, 1.0, half, dtype=F32))
    ang = pos.astype(F32)[:, None] * inv_freq
    cos, sin = jnp.cos(ang), jnp.sin(ang)
    log_gamma = jnp.log1p(-jnp.exp2(-5.0 - jnp.arange(RET_HEADS, dtype=F32)))
    lg = jnp.broadcast_to(log_gamma[:, None, None], (RET_HEADS, 8, 128))

    for layer in range(depth):
        kind, j = layer % 3, layer // 3
        ng = norm_g[layer]
        if kind == 0:
            n_main = wts["ssd_n_main"]
            proj = matmul(h, wts["ssd_w_in"], j, out_dtype=act, norm_g=ng, n_cols=n_main)
            dt_raw = matmul(h, wts["ssd_w_dt"], j, norm_g=ng)
            conv_init = jnp.pad(conv_st[j], ((0, 0), (8 - (SSD_CONV - 1), 0), (0, 0)))
            y, new_ssd = ssd_scan(proj, dt_raw, conv_init, ssd_st, j, new_ssd, ssd_conv_w[j], ssd_conv_b[j],
                                  wts["ssd_dtb"][j], wts["ssd_alog"][j], wts["ssd_dskip"][j], ssd_norm_g[j],
                                  lq=lq, **geo)
            nk = SSD_CONV - 1
            last = h.reshape(bsz, tp, d_model)[:, hi - nk:hi].reshape(bsz * nk, d_model)
            last = jnp.pad(last, ((0, -(bsz * nk) % 16), (0, 0)))
            xbc = matmul(last, wts["ssd_w_in"], j, norm_g=ng, n_cols=n_main)[:bsz * nk, y.shape[1]:]
            new_conv.append(xbc.reshape(bsz, nk, -1))
            h = matmul(y, wts["ssd_w_out"], j, res=h)
        elif kind == 1 and lo == 0 and hi <= 8 and bsz % LANES == 0:
            u = rmsnorm(h, ng)
            tmajor = lambda x: jnp.swapaxes(x.reshape(bsz, tp, -1)[:, :hi], 0, 1)
            uc = tmajor(u)
            prev = jnp.concatenate([shift_st[j][None], uc[:-1]], axis=0).reshape(hi * bsz, d_model)
            uc = uc.reshape(hi * bsz, d_model)
            rkvg = mix_matmul(uc, prev, rwkv_mu[j][:4], wts["rwkv_w"][j], out_dtype=BF16)
            lora1 = mix_matmul(uc, prev, rwkv_mu[j][4:], wts["rwkv_lora_a"][j])
            y, s_t = rwkv_lanes(rkvg, lora1, wts["rwkv_lora_b"][j], wts["rwkv_par"][j],
                                jnp.transpose(wkv_st[j], (1, 2, 3, 0)), nt=hi, nb=bsz)
            new_shift.append(u.reshape(bsz, tp, d_model)[:, hi - 1])
            new_wkv.append(jnp.transpose(s_t, (3, 0, 1, 2)))
            hc = matmul(y, wts["rwkv_w_out"], j, res=tmajor(h).reshape(hi * bsz, d_model))
            hc = jnp.swapaxes(hc.reshape(hi, bsz, d_model), 0, 1)
            h = jnp.pad(hc, ((0, 0), (0, tp - hi), (0, 0))).reshape(bsz * tp, d_model)
        elif kind == 1:
            u = rmsnorm(h, ng)
            u3 = u.reshape(bsz, tp, d_model)
            prev = jnp.concatenate([shift_st[j][:, None, :], u3[:, :-1]], axis=1).reshape(bsz * tp, d_model)
            rkvg = mix_matmul(u, prev, rwkv_mu[j][:4], wts["rwkv_w"][j], out_dtype=BF16)
            lora1 = mix_matmul(u, prev, rwkv_mu[j][4:], wts["rwkv_lora_a"][j])
            if lo == 0 and hi <= 8 and tp <= 16:
                y, s_new = rwkv_short(rkvg.astype(F32), lora1, wts["rwkv_lora_b"][j], wts["rwkv_par"][j], wkv_st[j],
                                      bsz=bsz, tp=tp, hi=hi)
            else:
                y, s_new = rwkv_scan(rkvg, lora1, wts["rwkv_lora_b"][j], wts["rwkv_par"][j], wkv_st[j], **geo)
            new_shift.append(u3[:, hi - 1])
            new_wkv.append(s_new)
            h = matmul(y, wts["rwkv_w_out"], j, res=h)
        else:
            proj = matmul(h, wts["ret_w_in"], j, out_dtype=act, norm_g=ng)
            y, s_new = ret_scan(proj, cos, sin, lg, ret_st[j], ret_norm_g[j], lq=lq, **geo)
            new_ret.append(s_new)
            h = matmul(y, wts["ret_w_out"], j, res=h)
    y = rmsnorm(h, final_norm_g)
    return (y, jnp.stack(new_conv), new_ssd, jnp.stack(new_shift), jnp.stack(new_wkv), jnp.stack(new_ret))


def kernel(x_prompt, x_sample, state_ssd_conv, state_ssd, state_rwkv_shift, state_rwkv_wkv, state_ret, meta_tokens, norm_g, final_norm_g, ssd_w_in, ssd_conv_w, ssd_conv_b, ssd_dt_bias, ssd_a_log, ssd_d, ssd_norm_g, ssd_w_out, rwkv_mu, rwkv_w_rkvg, rwkv_w0, rwkv_w_lora_a, rwkv_w_lora_b, rwkv_a0, rwkv_a_lora_a, rwkv_a_lora_b, rwkv_k_k, rwkv_k_a, rwkv_r_k, rwkv_ln_g, rwkv_ln_b, rwkv_w_out, ret_w_in, ret_norm_g, ret_w_out):
    depth = norm_g.shape[0]
    d_model = x_prompt.shape[2]
    wts = _prep_weights(norm_g, ssd_w_in, ssd_dt_bias, ssd_a_log, ssd_d, ssd_w_out,
                        rwkv_w_rkvg, rwkv_w_lora_a, rwkv_w_lora_b, rwkv_a_lora_a, rwkv_a_lora_b,
                        rwkv_w0, rwkv_a0, rwkv_k_k, rwkv_k_a, rwkv_r_k, rwkv_ln_g, rwkv_ln_b, rwkv_w_out,
                        ret_w_in, ret_w_out)
    common = dict(depth=depth, norm_g=norm_g, final_norm_g=final_norm_g, wts=wts, ssd_conv_w=ssd_conv_w,
                  ssd_conv_b=ssd_conv_b, ssd_norm_g=ssd_norm_g, rwkv_mu=rwkv_mu, ret_norm_g=ret_norm_g)

    bp, seq, _ = x_prompt.shape
    lq_p = 128
    lo_p = lq_p - N_META
    tp_p = lo_p + N_META + seq
    h_p = jnp.concatenate([jnp.zeros((bp, lo_p, d_model), F32),
                           jnp.broadcast_to(meta_tokens[None], (bp, N_META, d_model)), x_prompt], axis=1)
    zeros_like_b = lambda s: jnp.zeros((s.shape[0], bp) + s.shape[2:], F32)
    pos_p = jnp.maximum(jnp.arange(tp_p) - lo_p, 0)
    outs_p = _trunk(h_p.reshape(bp * tp_p, d_model), zeros_like_b(state_ssd_conv), zeros_like_b(state_ssd),
                    zeros_like_b(state_rwkv_shift), zeros_like_b(state_rwkv_wkv), zeros_like_b(state_ret), pos_p,
                    bsz=bp, tp=tp_p, lq=lq_p, lo=lo_p, hi=tp_p, **common)
    y_prompt = outs_p[0].reshape(bp, tp_p, d_model)[:, lo_p + N_META:]

    bs, ds, _ = x_sample.shape
    tp_s = 8
    h_s = jnp.concatenate([x_sample, jnp.zeros((bs, tp_s - ds, d_model), F32)], axis=1)
    pos_s = PAST_LEN + jnp.arange(tp_s)
    outs_s = _trunk(h_s.reshape(bs * tp_s, d_model), state_ssd_conv, state_ssd, state_rwkv_shift, state_rwkv_wkv,
                    state_ret, pos_s, bsz=bs, tp=tp_s, lq=tp_s, lo=0, hi=ds, **common)
    y_sample = outs_s[0].reshape(bs, tp_s, d_model)[:, :ds]

    return (y_prompt, y_sample) + tuple(outs_p[1:]) + tuple(outs_s[1:])
```

```python
import functools
import math

import jax
import jax.numpy as jnp
from jax import lax
from jax.experimental import pallas as pl
from jax.experimental.pallas import tpu as pltpu

F32 = jnp.float32
BF16 = jnp.bfloat16

EPS = 1e-6
N_META = 16
HEAD_DIM = 64
SSD_STATE = 128
SSD_GROUPS = 8
SSD_HPG = 8
SSD_CONV = 4
RET_HEADS = 8
RET_QK = 256
RET_V = 512
RET_THETA_BASE = 10000.0
RWKV_LORA_PAD = 128
RWKV_CHUNK = 64
RWKV_HPB = 4
RWKV_GROUPS_PER_STEP = 8
RWKV_SHORT_HEADS = 16
RWKV_LANE_HEADS = 2
LANES = 128
MIN_CHUNK_ROWS = 16
PAST_LEN = 16384
RWKV_LN_EPS = 1e-5 * HEAD_DIM
NEG = -1e30
ROW_TILE = 512
V7X_VMEM_BYTES = 64 * 1024 * 1024
VMEM_LIMIT = V7X_VMEM_BYTES * 7 // 8
MATMUL_VMEM_BUDGET = V7X_VMEM_BYTES * 5 // 8


def _cparams(sem):
    return pltpu.CompilerParams(dimension_semantics=sem, vmem_limit_bytes=VMEM_LIMIT)


def _nt(a, b):
    return lax.dot_general(a, b, (((1,), (1,)), ((), ())), preferred_element_type=F32)


def _tn(a, b):
    return lax.dot_general(a, b, (((0,), (0,)), ((), ())), preferred_element_type=F32)


def _dot(a, b):
    return jnp.dot(a, b, preferred_element_type=F32)


def _split(x, terms):
    parts = []
    r = x
    for i in range(terms):
        p = r.astype(BF16)
        parts.append(p)
        if i + 1 < terms:
            r = r - p.astype(F32)
    return parts


def _dot_split_l(x, m, terms=3):
    acc = None
    for p in _split(x, terms):
        d = _dot(p, m)
        acc = d if acc is None else acc + d
    return acc


def _dot_split_r(m, x, terms=3):
    acc = None
    for p in _split(x, terms):
        d = _dot(m, p)
        acc = d if acc is None else acc + d
    return acc


def _pad_rows(x, rows):
    if x.shape[0] == rows:
        return x
    return jnp.concatenate([x, jnp.zeros((rows - x.shape[0], x.shape[1]), x.dtype)], axis=0)


def _idiv(x, n):
    return jnp.right_shift(x, int(math.log2(n)))


def _imod(x, n):
    return jnp.bitwise_and(x, n - 1)


def _silu(x):
    h = 0.5 * x
    return h + h * jnp.tanh(h)


def _softplus(x):
    return jnp.maximum(x, 0.0) + jnp.log(1.0 + jnp.exp(-jnp.abs(x)))


def _rmsnorm_kernel(x_ref, g_ref, o_ref):
    x = x_ref[...]
    ms = jnp.mean(x * x, axis=-1, keepdims=True)
    o_ref[...] = (x * lax.rsqrt(ms + EPS) * g_ref[...]).astype(o_ref.dtype)


def _row_tile(m):
    tm = math.gcd(m, ROW_TILE)
    assert tm % 16 == 0, m
    return tm


def _matmul_tiles(m, k, n, *, a_bytes, n_a, cast, n_out):
    best = None
    for tn in (t for t in (1024, 512, 256, 128) if n % t == 0):
        for tm in (t for t in range(16, m + 1, 16) if m % t == 0):
            need = (2 * n_a * tm * k * a_bytes + (tm * k * 2 if cast else 0)
                    + 2 * k * tn * 2 + 2 * n_out * tm * tn * 4)
            if need <= MATMUL_VMEM_BUDGET and (best is None or (tm * tn, tm) > (best[0] * best[1], best[0])):
                best = (tm, tn)
    assert best is not None, (m, k, n)
    return best


def rmsnorm(x, g, out_dtype=F32):
    m, d = x.shape
    tm = _row_tile(m)
    return pl.pallas_call(
        _rmsnorm_kernel,
        grid=(m // tm,),
        in_specs=[pl.BlockSpec((tm, d), lambda i: (i, 0)), pl.BlockSpec((1, d), lambda i: (0, 0))],
        out_specs=pl.BlockSpec((tm, d), lambda i: (i, 0)),
        out_shape=jax.ShapeDtypeStruct((m, d), out_dtype),
        compiler_params=_cparams(("parallel",)),
        name="rmsnorm",
    )(x, g.reshape(1, d))


def _mm_kernel(a_ref, w_ref, *rest, has_res, cast, norm):
    rest = list(rest)
    abf_ref = rest.pop() if cast else a_ref
    o_ref = rest.pop()

    if cast:
        @pl.when(pl.program_id(1) == 0)
        def _():
            x = a_ref[...]
            if norm:
                x = x * lax.rsqrt(jnp.mean(x * x, axis=-1, keepdims=True) + EPS) * rest[0][...]
            abf_ref[...] = x.astype(BF16)

    acc = _dot(abf_ref[...], w_ref[0])
    if has_res:
        acc = rest[-1][...] + acc
    o_ref[...] = acc.astype(o_ref.dtype)


def matmul(a, w, layer, res=None, out_dtype=F32, norm_g=None, n_cols=None):
    m, k = a.shape
    n = w.shape[2] if n_cols is None else n_cols
    cast = a.dtype != BF16 or norm_g is not None
    tm, tn = _matmul_tiles(m, k, n, a_bytes=a.dtype.itemsize, n_a=1, cast=cast, n_out=2 if res is not None else 1)
    in_specs = [pl.BlockSpec((tm, k), lambda i, j: (i, 0)), pl.BlockSpec((1, k, tn), lambda i, j: (layer, 0, j))]
    args = [a, w]
    if norm_g is not None:
        in_specs.append(pl.BlockSpec((1, k), lambda i, j: (0, 0)))
        args.append(norm_g.reshape(1, k))
    if res is not None:
        in_specs.append(pl.BlockSpec((tm, tn), lambda i, j: (i, j)))
        args.append(res)
    return pl.pallas_call(
        functools.partial(_mm_kernel, has_res=res is not None, cast=cast, norm=norm_g is not None),
        grid=(m // tm, n // tn),
        in_specs=in_specs,
        out_specs=pl.BlockSpec((tm, tn), lambda i, j: (i, j)),
        out_shape=jax.ShapeDtypeStruct((m, n), out_dtype),
        scratch_shapes=[pltpu.VMEM((tm, k), BF16)] if cast else [],
        compiler_params=_cparams(("parallel", "arbitrary")),
        name="matmul_res" if res is not None else "matmul",
    )(*args)


def _mixmm_kernel(u_ref, p_ref, mu_ref, w_ref, o_ref, xm_ref, *, shift_rows):
    @pl.when(pl.program_id(2) == 0)
    def _():
        u = u_ref[...]
        if shift_rows:
            first = lax.broadcasted_iota(jnp.int32, u.shape, 0) == 0
            prev = jnp.where(first, p_ref[7:8, :], pltpu.roll(u, 1, 0))
        else:
            prev = p_ref[...]
        xm_ref[...] = (u + (prev - u) * mu_ref[0]).astype(BF16)

    o_ref[0] = _dot(xm_ref[...], w_ref[0]).astype(o_ref.dtype)


def mix_matmul(u, prev, mu, w, out_dtype=F32):
    m, k = u.shape
    s, _, n = w.shape
    shift_rows = prev is None
    tm, tn = _matmul_tiles(m, k, n, a_bytes=4, n_a=1 if shift_rows else 2, cast=True, n_out=1)
    if shift_rows:
        prev_spec = pl.BlockSpec((8, k), lambda i, si, j: (jnp.maximum(i * (tm // 8) - 1, 0), 0))
    else:
        prev_spec = pl.BlockSpec((tm, k), lambda i, si, j: (i, 0))
    return pl.pallas_call(
        functools.partial(_mixmm_kernel, shift_rows=shift_rows),
        grid=(m // tm, s, n // tn),
        in_specs=[
            pl.BlockSpec((tm, k), lambda i, si, j: (i, 0)),
            prev_spec,
            pl.BlockSpec((1, 1, k), lambda i, si, j: (si, 0, 0)),
            pl.BlockSpec((1, k, tn), lambda i, si, j: (si, 0, j)),
        ],
        out_specs=pl.BlockSpec((1, tm, tn), lambda i, si, j: (si, i, j)),
        out_shape=jax.ShapeDtypeStruct((s, m, n), out_dtype),
        scratch_shapes=[pltpu.VMEM((tm, k), BF16)],
        compiler_params=_cparams(("parallel", "arbitrary", "arbitrary")),
        name="mix_matmul",
    )(u, u if shift_rows else prev, mu.reshape(s, 1, k), w)


def _conv_silu(cur, car_ref, cols, w, b, lq, lb):
    car_ref[8:8 + lq, cols] = cur
    acc = b + cur * w[SSD_CONV - 1:SSD_CONV]
    for s in range(1, SSD_CONV):
        acc = acc + car_ref[8 - s:8 - s + lq, cols] * w[SSD_CONV - 1 - s:SSD_CONV - s]
    car_ref[0:8, cols] = cur[lb - 8:lb]
    return _silu(acc)


def _ssd_kernel(z_ref, x_ref, b_ref, c_ref, dt_ref, ci_ref, s0_ref, cw_ref, cb_ref, dtb_ref, alog_ref,
                d_ref, ng_ref, *rest, lq, lo, hi, nchunks, out_layer):
    y_ref, sout_ref, car, st = rest[-4:]
    c = pl.program_id(1)
    gw = SSD_HPG * HEAD_DIM

    @pl.when(c == 0)
    def _():
        car[0:8, :] = ci_ref[0]
        st[...] = s0_ref[0, 0].reshape(SSD_GROUPS * gw, SSD_STATE)

    gens = [_ssd_group(g, z_ref, x_ref, b_ref, c_ref, dt_ref, cw_ref, cb_ref, dtb_ref, alog_ref, d_ref,
                       ng_ref, y_ref, car, st, lb=lq, lo=lo, hi=hi) for g in range(SSD_GROUPS)]
    for _ in zip(*gens):
        pass

    @pl.when(c == nchunks - 1)
    def _():
        for layer in range(sout_ref.shape[0]):
            if layer == out_layer:
                sout_ref[layer, 0] = st[...].reshape(SSD_GROUPS * SSD_HPG, HEAD_DIM, SSD_STATE)
            else:
                sout_ref[layer, 0] = jnp.zeros(sout_ref.shape[2:], F32)


def _ssd_group(g, z_ref, x_ref, b_ref, c_ref, dt_ref, cw_ref, cb_ref, dtb_ref, alog_ref, d_ref, ng_ref,
               y_ref, car, st, *, lb, lo, hi):
    c = pl.program_id(1)
    lq = max(lb, MIN_CHUNK_ROWS)
    gw = SSD_HPG * HEAD_DIM
    d_inner = SSD_GROUPS * gw
    xs = slice(g * gw, (g + 1) * gw)
    ns = slice(g * SSD_STATE, (g + 1) * SSD_STATE)
    bs = slice(d_inner + g * SSD_STATE, d_inner + (g + 1) * SSD_STATE)
    cs_ = slice(d_inner + (SSD_GROUPS + g) * SSD_STATE, d_inner + (SSD_GROUPS + g + 1) * SSD_STATE)

    f32 = lambda x: _pad_rows(x.astype(F32), lq)
    xc = _conv_silu(f32(x_ref[:, xs]), car, xs, cw_ref[:, xs], cb_ref[:, xs], lq, lb)
    bc = _conv_silu(f32(b_ref[:, ns]), car, bs, cw_ref[:, bs], cb_ref[:, bs], lq, lb)
    cc = _conv_silu(f32(c_ref[:, ns]), car, cs_, cw_ref[:, cs_], cb_ref[:, cs_], lq, lb)

    ti = lax.broadcasted_iota(jnp.int32, (lq, 1), 0)
    pos = c * lb + ti
    valid = (ti < lb) & (pos >= lo) & (pos < hi)
    lane = lax.broadcasted_iota(jnp.int32, (lq, 128), 1)
    dt = _softplus(f32(dt_ref[:, ns]) + dtb_ref[:, ns])
    dt = jnp.where(valid & (lane < SSD_HPG), dt, 0.0)
    la = dt * (-jnp.exp(alog_ref[:, ns]))
    yield

    ri = lax.broadcasted_iota(jnp.int32, (lq, lq), 0)
    ci = lax.broadcasted_iota(jnp.int32, (lq, lq), 1)
    tril = (ri >= ci).astype(BF16)
    triu = (ri <= ci).astype(BF16)
    acum = _dot_split_r(tril, la)
    acum_t = sum(_tn(part, triu) for part in _split(la, 3))
    yield
    a_end = acum[lq - 1:lq]
    dec_end = jnp.exp(a_end - acum)
    e_in = jnp.exp(acum)
    cd = jnp.exp(a_end)

    bcp = bc.astype(BF16)
    ccb = cc.astype(BF16)
    g_sc = _nt(ccb, bcp)
    st_old = st[xs, :]
    y_in = _nt(ccb, st_old.astype(BF16))
    yield

    causal = ci <= ri
    lane_q = lax.broadcasted_iota(jnp.int32, (lq, 128), 1) < HEAD_DIM
    lane_k = lane_q
    row_k = lax.broadcasted_iota(jnp.int32, (128, 1), 0) < HEAD_DIM
    dvec = d_ref[:, ns]

    er = lax.broadcasted_iota(jnp.int32, (2 * LANES, gw), 0)
    ec = lax.broadcasted_iota(jnp.int32, (2 * LANES, gw), 1)
    spread_m = (_imod(er, LANES) == _idiv(ec, HEAD_DIM)).astype(BF16)
    spread = lambda f: _dot(jnp.concatenate(_split(f, 2), axis=1), spread_m)
    dt_x = spread(dt)
    e_x = spread(e_in)
    dec_x = spread(dec_end)
    yield

    ys = []
    for p in range(SSD_HPG // 2):
        h0, h1 = 2 * p, 2 * p + 1
        ps = slice(128 * p, 128 * (p + 1))
        xp = xc[:, ps]
        vp = xp * dt_x[:, ps]
        vpp = vp
        yp = y_in[:, ps] * e_x[:, ps]
        yp = yp + xp * jnp.where(lane_q, dvec[:, h0:h0 + 1], dvec[:, h1:h1 + 1])
        for hh, h in ((0, h0), (1, h1)):
            seg = acum[:, h:h + 1] - acum_t[h:h + 1, :]
            lm = jnp.exp(jnp.where(causal, seg, NEG))
            pm = (g_sc * lm).astype(BF16)
            vm = jnp.where(lane_k if hh == 0 else jnp.logical_not(lane_k), vpp, 0.0).astype(BF16)
            yp = yp + _dot(pm, vm)
        ys.append(yp)
        vend = vpp * dec_x[:, ps]
        upd = _tn(vend.astype(BF16), bcp)
        cdp = jnp.where(row_k, cd[:, h0:h0 + 1], cd[:, h1:h1 + 1])
        st[g * gw + 128 * p:g * gw + 128 * (p + 1), :] = st_old[128 * p:128 * (p + 1), :] * cdp + upd
        yield

    y = jnp.concatenate(ys, axis=1)
    y = y * _silu(f32(z_ref[:, xs]))
    ms = jnp.mean(y * y, axis=-1, keepdims=True)
    y = y * lax.rsqrt(ms + EPS) * ng_ref[:, xs]
    y_ref[:, xs] = jnp.where(valid, y, 0.0)[:lb].astype(y_ref.dtype)
    yield


def ssd_scan(proj, dt_raw, conv_init, s0_all, layer, s_buf, conv_w, conv_b, dtb, alog, dskip, norm_g,
             *, bsz, tp, lq, lo, hi):
    m = proj.shape[0]
    heads = SSD_GROUPS * SSD_HPG
    nlayers = s0_all.shape[0]
    state_spec = pl.BlockSpec((1, 1, heads, HEAD_DIM, SSD_STATE), lambda b, c: (layer, b, 0, 0, 0))
    if s_buf is not None:
        out_state_spec, out_layer = state_spec, 0
        extra_specs, extra_args, aliases = [pl.BlockSpec(memory_space=pl.ANY)], [s_buf], {13: 1}
    else:
        out_state_spec = pl.BlockSpec((nlayers, 1, heads, HEAD_DIM, SSD_STATE), lambda b, c: (0, b, 0, 0, 0))
        out_layer = layer
        extra_specs, extra_args, aliases = [], [], {}
    nch = tp // lq
    d_inner = heads * HEAD_DIM
    gn = SSD_GROUPS * SSD_STATE
    conv_dim = d_inner + 2 * gn
    row = lambda b, c: b * nch + c
    const = lambda b, c: (0, 0)
    kern = functools.partial(_ssd_kernel, lq=lq, lo=lo, hi=hi, nchunks=nch, out_layer=out_layer)
    return pl.pallas_call(
        kern,
        grid=(bsz, nch),
        in_specs=[
            pl.BlockSpec((lq, d_inner), lambda b, c: (row(b, c), 0)),
            pl.BlockSpec((lq, d_inner), lambda b, c: (row(b, c), 1)),
            pl.BlockSpec((lq, gn), lambda b, c: (row(b, c), 2 * d_inner // gn)),
            pl.BlockSpec((lq, gn), lambda b, c: (row(b, c), 2 * d_inner // gn + 1)),
            pl.BlockSpec((lq, gn), lambda b, c: (row(b, c), 0)),
            pl.BlockSpec((1, 8, conv_dim), lambda b, c: (b, 0, 0)),
            state_spec,
            pl.BlockSpec((SSD_CONV, conv_dim), const),
            pl.BlockSpec((1, conv_dim), const),
            pl.BlockSpec((1, gn), const),
            pl.BlockSpec((1, gn), const),
            pl.BlockSpec((1, gn), const),
            pl.BlockSpec((1, d_inner), const),
        ] + extra_specs,
        out_specs=[
            pl.BlockSpec((lq, d_inner), lambda b, c: (row(b, c), 0)),
            out_state_spec,
        ],
        out_shape=[
            jax.ShapeDtypeStruct((m, d_inner), proj.dtype),
            jax.ShapeDtypeStruct(s0_all.shape, F32),
        ],
        scratch_shapes=[pltpu.VMEM((8 + max(lq, MIN_CHUNK_ROWS), conv_dim), F32),
                        pltpu.VMEM((d_inner, SSD_STATE), F32)],
        input_output_aliases=aliases,
        compiler_params=_cparams(("parallel", "arbitrary")),
        name="ssd_scan",
    )(proj, proj, proj, proj, dt_raw, conv_init, s0_all, conv_w, conv_b.reshape(1, -1),
      dtb.reshape(1, -1), alog.reshape(1, -1), dskip.reshape(1, -1), norm_g.reshape(1, -1), *extra_args)


def _ret_kernel(q_ref, k_ref, v_ref, g_ref, cos_ref, sin_ref, lg_ref, s0_ref, ng_ref,
                y_ref, sout_ref, *, lq, lo, hi):
    c = pl.program_id(1)

    @pl.when(c == 0)
    def _():
        sout_ref[...] = s0_ref[...]

    gens = [_ret_head(h, q_ref, k_ref, v_ref, g_ref, cos_ref, sin_ref, lg_ref, ng_ref, y_ref, sout_ref,
                      lb=lq, lo=lo, hi=hi) for h in range(RET_HEADS)]
    for _ in zip(*gens):
        pass


def _ret_head(h, q_ref, k_ref, v_ref, g_ref, cos_ref, sin_ref, lg_ref, ng_ref, y_ref, st_ref, *, lb, lo, hi):
    c = pl.program_id(1)
    lq = max(lb, MIN_CHUNK_ROWS)
    f32 = lambda x: _pad_rows(x.astype(F32), lq)
    qs = slice(h * RET_QK, (h + 1) * RET_QK)
    vs = slice(h * RET_V, (h + 1) * RET_V)
    lg = lg_ref[h][0:1, 0:1]
    nv = float(hi - lo)

    def count(p):
        return jnp.clip((p + 1 - lo).astype(F32), 0.0, nv)

    base = c * lb
    ti = lax.broadcasted_iota(jnp.int32, (lq, 1), 0)
    pos_i = base + ti
    valid = (ti < lb) & (pos_i >= lo) & (pos_i < hi)
    cnt_i = count(pos_i)
    cnt_j = count(base + lax.broadcasted_iota(jnp.int32, (1, lq), 1))
    cnt_jc = cnt_i
    cnt0 = count(base - 1 + jnp.zeros((1, 1), jnp.int32))
    cnt_end = count(base + lb - 1 + jnp.zeros((1, 1), jnp.int32))

    cos = f32(cos_ref[...])
    sin = f32(sin_ref[...])
    half = RET_QK // 2

    def rot(x):
        x1, x2 = x[:, :half], x[:, half:]
        return jnp.concatenate([x1 * cos - x2 * sin, x1 * sin + x2 * cos], axis=1)

    qr = rot(f32(q_ref[:, qs])).astype(BF16)
    kr = jnp.where(valid, rot(f32(k_ref[:, qs])) * (RET_QK ** -0.5), 0.0)
    v = jnp.where(valid, f32(v_ref[:, vs]), 0.0)
    krp = kr.astype(BF16)
    vp = v
    yield

    sc = _nt(qr, krp)
    s_old = st_ref[0, h]
    y_in = _nt(qr, s_old.astype(BF16))
    yield
    qi = lax.broadcasted_iota(jnp.int32, (lq, lq), 0)
    kj = lax.broadcasted_iota(jnp.int32, (lq, lq), 1)
    dm = jnp.exp(jnp.where(kj <= qi, lg * (cnt_i - cnt_j), NEG))
    y = _dot((sc * dm).astype(BF16), vp.astype(BF16))
    y = y + y_in * jnp.exp(lg * (cnt_i - cnt0))
    vend = vp * jnp.exp(lg * (cnt_end - cnt_jc))
    st_ref[0, h] = s_old * jnp.exp(lg * (cnt_end - cnt0)) + _tn(vend.astype(BF16), krp)
    yield

    ms = jnp.mean(y * y, axis=-1, keepdims=True)
    y = y * lax.rsqrt(ms + EPS) * ng_ref[:, vs] * _silu(f32(g_ref[:, vs]))
    y_ref[:, vs] = jnp.where(valid, y, 0.0)[:lb].astype(y_ref.dtype)
    yield


def ret_scan(proj, cos, sin, lg, s0, norm_g, *, bsz, tp, lq, lo, hi):
    m = proj.shape[0]
    nch = tp // lq
    d_inner = RET_HEADS * RET_V
    d_qk = RET_HEADS * RET_QK
    row = lambda b, c: b * nch + c
    state_spec = pl.BlockSpec((1, RET_HEADS, RET_V, RET_QK), lambda b, c: (b, 0, 0, 0))
    kern = functools.partial(_ret_kernel, lq=lq, lo=lo, hi=hi)
    return pl.pallas_call(
        kern,
        grid=(bsz, nch),
        in_specs=[
            pl.BlockSpec((lq, d_qk), lambda b, c: (row(b, c), 0)),
            pl.BlockSpec((lq, d_qk), lambda b, c: (row(b, c), 1)),
            pl.BlockSpec((lq, d_inner), lambda b, c: (row(b, c), 2 * d_qk // d_inner)),
            pl.BlockSpec((lq, d_inner), lambda b, c: (row(b, c), 2 * d_qk // d_inner + 1)),
            pl.BlockSpec((lq, RET_QK // 2), lambda b, c: (c, 0)),
            pl.BlockSpec((lq, RET_QK // 2), lambda b, c: (c, 0)),
            pl.BlockSpec((RET_HEADS, 8, 128), lambda b, c: (0, 0, 0)),
            state_spec,
            pl.BlockSpec((1, d_inner), lambda b, c: (0, 0)),
        ],
        out_specs=[
            pl.BlockSpec((lq, d_inner), lambda b, c: (row(b, c), 0)),
            state_spec,
        ],
        out_shape=[
            jax.ShapeDtypeStruct((m, d_inner), proj.dtype),
            jax.ShapeDtypeStruct((bsz, RET_HEADS, RET_V, RET_QK), F32),
        ],
        compiler_params=_cparams(("parallel", "arbitrary")),
        name="ret_scan",
    )(proj, proj, proj, proj, cos, sin, lg, s0, norm_g.reshape(1, -1))


def _rwkv_kernel(r_ref, k_ref, v_ref, g_ref, lw_ref, la_ref, bw_ref, ba_ref, par_ref, s0_ref,
                 y_ref, sout_ref, st, *, lo, hi, nchunks, ngrp):
    c = pl.program_id(2)
    w4 = RWKV_HPB * HEAD_DIM
    r2 = lax.broadcasted_iota(jnp.int32, (w4, w4), 0)
    c2 = lax.broadcasted_iota(jnp.int32, (w4, w4), 1)
    blk = _idiv(r2, HEAD_DIM) == _idiv(c2, HEAD_DIM)

    @pl.when(c == 0)
    def _():
        tile = (lax.broadcasted_iota(jnp.int32, (HEAD_DIM, w4), 0)
                == _imod(lax.broadcasted_iota(jnp.int32, (HEAD_DIM, w4), 1), HEAD_DIM)).astype(BF16)
        for gi in range(ngrp):
            s0 = s0_ref[0, gi * RWKV_HPB:(gi + 1) * RWKV_HPB].reshape(w4, HEAD_DIM)
            st[gi] = jnp.where(blk, _dot_split_l(s0, tile), 0.0)

    has_rows = ((c + 1) * RWKV_CHUNK > lo) & (c * RWKV_CHUNK < hi)

    @pl.when(has_rows)
    def _():
        gens = [_rwkv_group(gi, r_ref, k_ref, v_ref, g_ref, lw_ref, la_ref, bw_ref, ba_ref, par_ref,
                            y_ref, st, lo=lo, hi=hi) for gi in range(ngrp)]
        for _ in zip(*gens):
            pass

    @pl.when(jnp.logical_not(has_rows))
    def _():
        y_ref[...] = jnp.zeros(y_ref.shape, y_ref.dtype)

    @pl.when(c == nchunks - 1)
    def _():
        tile_t = (_imod(lax.broadcasted_iota(jnp.int32, (w4, HEAD_DIM), 0), HEAD_DIM)
                  == lax.broadcasted_iota(jnp.int32, (w4, HEAD_DIM), 1)).astype(BF16)
        for gi in range(ngrp):
            sout_ref[0, gi * RWKV_HPB:(gi + 1) * RWKV_HPB] = _dot_split_l(st[gi], tile_t).reshape(
                RWKV_HPB, HEAD_DIM, HEAD_DIM)


def _rwkv_group(gi, r_ref, k_ref, v_ref, g_ref, lw_ref, la_ref, bw_ref, ba_ref, par_ref, y_ref, st,
                *, lo, hi):
    c = pl.program_id(2)
    cs = RWKV_CHUNK
    w4 = RWKV_HPB * HEAD_DIM
    sl = slice(gi * w4, (gi + 1) * w4)

    nr = RWKV_HPB * cs
    ones_bd = (_idiv(lax.broadcasted_iota(jnp.int32, (w4, w4), 0), HEAD_DIM)
               == _idiv(lax.broadcasted_iota(jnp.int32, (w4, w4), 1), HEAD_DIM)).astype(BF16)
    blk = (_idiv(lax.broadcasted_iota(jnp.int32, (nr, w4), 0), cs)
           == _idiv(lax.broadcasted_iota(jnp.int32, (nr, w4), 1), HEAD_DIM))
    r2 = lax.broadcasted_iota(jnp.int32, (nr, nr), 0)
    c2 = lax.broadcasted_iota(jnp.int32, (nr, nr), 1)
    same = _idiv(r2, cs) == _idiv(c2, cs)

    def segsum(x):
        return _dot_split_l(x, ones_bd, terms=2)

    par = par_ref[:, sl]
    w0, a0, k_k, k_a, r_k, ln_g, ln_b = (par[i:i + 1] for i in range(7))

    r = r_ref[0, :, sl].astype(F32)
    k = k_ref[0, :, sl].astype(F32)
    v = v_ref[0, :, sl].astype(F32)
    g = g_ref[0, :, sl].astype(F32)
    w_raw = w0 + _dot(jnp.tanh(lw_ref[0]).astype(BF16), bw_ref[0, :, sl])
    a = jax.nn.sigmoid(a0 + _dot(la_ref[0].astype(BF16), ba_ref[0, :, sl]))

    ti = lax.broadcasted_iota(jnp.int32, (cs, 1), 0)
    pos = c * cs + ti
    valid = (pos >= lo) & (pos < hi)

    lw = -jnp.exp(-_softplus(-w_raw) - 0.5)
    kk = k * k_k
    kk = kk / jnp.maximum(jnp.sqrt(segsum(kk * kk)), 1e-12)
    yield
    kp = k * (1.0 + (a - 1.0) * k_a)
    lw = jnp.where(valid, lw, 0.0)
    kk = jnp.where(valid, kk, 0.0)
    kp = jnp.where(valid, kp, 0.0)
    vm = jnp.where(valid, v, 0.0)

    tri = (lax.broadcasted_iota(jnp.int32, (cs, cs), 0) >= lax.broadcasted_iota(jnp.int32, (cs, cs), 1)).astype(BF16)
    cw = _dot_split_r(tri, lw)
    yield
    cwl = cw[cs - 1:cs]
    wt = jnp.exp(cw)
    wi = jnp.exp(-cw)
    wend = jnp.exp(cwl - cw)
    b = kk * a
    at = -kk * jnp.exp(cw - lw)
    rt = r * wt

    def bd(x):
        return jnp.where(blk, jnp.concatenate([x] * RWKV_HPB, axis=0), 0.0).astype(BF16)

    lhs = jnp.concatenate([bd(at), bd(rt)], axis=0)
    rhs = jnp.concatenate([bd(b * wi), bd(kp * wi)], axis=0)
    sc = _nt(lhs, rhs)
    yield
    tt = _imod(r2, cs)
    jj = _imod(c2, cs)
    strict = same & (tt > jj)
    incl = same & (tt >= jj)
    mab = jnp.where(strict, sc[:nr, :nr], 0.0)
    mak = jnp.where(strict, sc[:nr, nr:], 0.0)
    nrb = jnp.where(incl, sc[nr:, :nr], 0.0)
    nrk = jnp.where(incl, sc[nr:, nr:], 0.0)

    tinv = (r2 == c2).astype(F32) + jnp.where(
        (_idiv(r2, 2) == _idiv(c2, 2)) & (_imod(tt, 2) == 1) & (_imod(jj, 2) == 0), mab, 0.0)
    msz = 2
    while msz < cs:
        off = ((_idiv(r2, 2 * msz) == _idiv(c2, 2 * msz)) & (_imod(tt, 2 * msz) >= msz)
               & (_imod(jj, 2 * msz) < msz))
        tb = tinv.astype(BF16)
        tno = _dot(tb, jnp.where(off, mab, 0.0).astype(BF16)).astype(BF16)
        yield
        tinv = tinv + _dot(tno, tb)
        yield
        msz *= 2

    s_old = st[gi]
    x = _nt(lhs, s_old.astype(BF16))
    yield
    vbd = bd(vm)
    u = _dot(tinv.astype(BF16), (x[:nr] + _dot(mak.astype(BF16), vbd)).astype(BF16))
    yield
    ub = u.astype(BF16)
    yb = x[nr:] + _dot(nrb.astype(BF16), ub) + _dot(nrk.astype(BF16), vbd)
    yield
    s_new = s_old * wt[cs - 1:cs] + _tn(jnp.concatenate([ub, vbd], axis=0),
                                        jnp.concatenate([bd(b * wend), bd(kp * wend)], axis=0))
    st[gi] = s_new
    yield

    y = sum(yb[i * cs:(i + 1) * cs] for i in range(RWKV_HPB))
    inv = 1.0 / HEAD_DIM
    yc = y - segsum(y) * inv
    yield
    y = yc * lax.rsqrt(segsum(yc * yc) * inv + RWKV_LN_EPS) * ln_g + ln_b
    y = (y + segsum(r * kp * r_k) * v) * _silu(g)
    y_ref[:, sl] = jnp.where(valid, y, 0.0).astype(BF16)
    yield


def rwkv_scan(rkvg, lora1, lora_b, par, s0, *, bsz, tp, lo, hi):
    _, m, e = rkvg.shape
    cs = RWKV_CHUNK
    nch = tp // cs
    heads = e // HEAD_DIM
    ngrp = RWKV_GROUPS_PER_STEP
    hb = RWKV_HPB * ngrp
    wb = hb * HEAD_DIM
    row = lambda b, h, c: b * nch + c
    kern = functools.partial(_rwkv_kernel, lo=lo, hi=hi, nchunks=nch, ngrp=ngrp)
    proj_spec = lambda s: pl.BlockSpec((1, cs, wb), lambda b, h, c: (s, row(b, h, c), h))
    return pl.pallas_call(
        kern,
        grid=(bsz, heads // hb, nch),
        in_specs=[
            proj_spec(0), proj_spec(1), proj_spec(2), proj_spec(3),
            pl.BlockSpec((1, cs, RWKV_LORA_PAD), lambda b, h, c: (0, row(b, h, c), 0)),
            pl.BlockSpec((1, cs, RWKV_LORA_PAD), lambda b, h, c: (1, row(b, h, c), 0)),
            pl.BlockSpec((1, RWKV_LORA_PAD, wb), lambda b, h, c: (0, 0, h)),
            pl.BlockSpec((1, RWKV_LORA_PAD, wb), lambda b, h, c: (1, 0, h)),
            pl.BlockSpec((8, wb), lambda b, h, c: (0, h)),
            pl.BlockSpec((1, hb, HEAD_DIM, HEAD_DIM), lambda b, h, c: (b, h, 0, 0)),
        ],
        out_specs=[
            pl.BlockSpec((cs, wb), lambda b, h, c: (row(b, h, c), h)),
            pl.BlockSpec((1, hb, HEAD_DIM, HEAD_DIM), lambda b, h, c: (b, h, 0, 0)),
        ],
        out_shape=[
            jax.ShapeDtypeStruct((m, e), BF16),
            jax.ShapeDtypeStruct((bsz, heads, HEAD_DIM, HEAD_DIM), F32),
        ],
        scratch_shapes=[pltpu.VMEM((ngrp, RWKV_HPB * HEAD_DIM, RWKV_HPB * HEAD_DIM), F32)],
        compiler_params=_cparams(("parallel", "parallel", "arbitrary")),
        name="rwkv_scan",
    )(rkvg, rkvg, rkvg, rkvg, lora1, lora1, lora_b, lora_b, par, s0)


def _rwkv_short_kernel(r_ref, k_ref, v_ref, g_ref, lw_ref, la_ref, bw_ref, ba_ref, par_ref, s0_ref,
                       ones_ref, tile_ref, y_ref, sout_ref, *, nvalid, rows_out):
    ct = 8
    hp = RWKV_SHORT_HEADS
    wl = hp * HEAD_DIM
    nr = hp * ct
    ones_bd = ones_ref[...]
    tile_t = tile_ref[...]

    def segsum(x):
        return jnp.concatenate(
            [_dot_split_l(x[:, 256 * j:256 * (j + 1)], ones_bd, terms=2) for j in range(wl // 256)], axis=1)

    par = par_ref[...]
    w0, a0, k_k, k_a, r_k, ln_g, ln_b = (par[i:i + 1] for i in range(7))
    r, k, v, g = r_ref[0], k_ref[0], v_ref[0], g_ref[0]
    w_raw = w0 + _dot(jnp.tanh(lw_ref[0]).astype(BF16), bw_ref[0])
    a = jax.nn.sigmoid(a0 + _dot(la_ref[0].astype(BF16), ba_ref[0]))

    ti = lax.broadcasted_iota(jnp.int32, (ct, 1), 0)
    valid = ti < nvalid
    lw = jnp.where(valid, -jnp.exp(-_softplus(-w_raw) - 0.5), 0.0)
    kk = k * k_k
    kk = jnp.where(valid, kk / jnp.maximum(jnp.sqrt(segsum(kk * kk)), 1e-12), 0.0)
    kp = jnp.where(valid, k * (1.0 + (a - 1.0) * k_a), 0.0)
    vm = jnp.where(valid, v, 0.0)

    cw = lw
    for s in (1, 2, 4):
        cw = cw + jnp.where(ti >= s, pltpu.roll(cw, s, 0), 0.0)
    cwl = cw[ct - 1:ct]
    wend = jnp.exp(cwl - cw)
    wi = jnp.exp(-cw)
    b = kk * a
    at = -kk * jnp.exp(cw - lw)
    rt = r * jnp.exp(cw)

    rr = lax.broadcasted_iota(jnp.int32, (nr, wl), 0)
    cc = lax.broadcasted_iota(jnp.int32, (nr, wl), 1)
    blk = _idiv(rr, ct) == _idiv(cc, HEAD_DIM)

    def bd(x):
        return jnp.where(blk, jnp.concatenate([x] * hp, axis=0), 0.0).astype(BF16)

    lhs = jnp.concatenate([bd(at), bd(rt)], axis=0)
    rhs = jnp.concatenate([bd(b * wi), bd(kp * wi)], axis=0)
    sc = _nt(lhs, rhs)
    ri = lax.broadcasted_iota(jnp.int32, (2 * nr, 2 * nr), 0)
    ci = lax.broadcasted_iota(jnp.int32, (2 * nr, 2 * nr), 1)
    same = _idiv(_imod(ri, nr), ct) == _idiv(_imod(ci, nr), ct)
    tt = _imod(ri, ct)
    jj = _imod(ci, ct)
    sc = jnp.where(same & (tt + (ri >= nr).astype(jnp.int32) > jj), sc, 0.0)
    mab, mak, nrb, nrk = sc[:nr, :nr], sc[:nr, nr:], sc[nr:, :nr], sc[nr:, nr:]

    r1 = lax.broadcasted_iota(jnp.int32, (nr, nr), 0)
    c1 = lax.broadcasted_iota(jnp.int32, (nr, nr), 1)
    t1 = _imod(r1, ct)
    j1 = _imod(c1, ct)
    tinv = (r1 == c1).astype(F32) + jnp.where(
        (_idiv(r1, 2) == _idiv(c1, 2)) & (_imod(t1, 2) == 1) & (_imod(j1, 2) == 0), mab, 0.0)
    msz = 2
    while msz < nvalid:
        off = ((_idiv(r1, 2 * msz) == _idiv(c1, 2 * msz)) & (_imod(t1, 2 * msz) >= msz)
               & (_imod(j1, 2 * msz) < msz))
        tb = tinv.astype(BF16)
        tinv = tinv + _dot(_dot(tb, jnp.where(off, mab, 0.0).astype(BF16)).astype(BF16), tb)
        msz *= 2

    s_old = s0_ref[0].reshape(wl, HEAD_DIM)
    lhs_rows = _dot(lhs, tile_t).astype(BF16)
    x = _nt(lhs_rows, s_old.astype(BF16))
    xa = jnp.where(blk, x[:nr], 0.0)
    xr = jnp.where(blk, x[nr:], 0.0)
    vbd = bd(vm)
    u = _dot(tinv.astype(BF16), (xa + _dot(mak.astype(BF16), vbd)).astype(BF16))
    uv = jnp.concatenate([u.astype(BF16), vbd], axis=0)
    yb = xr + _dot(jnp.concatenate([nrb, nrk], axis=1).astype(BF16), uv)
    bk_rows = _dot(jnp.concatenate([bd(b * wend), bd(kp * wend)], axis=0), tile_t).astype(BF16)
    ds = _tn(uv, bk_rows)

    r16 = lax.broadcasted_iota(jnp.int32, (hp, wl), 0)
    c16 = lax.broadcasted_iota(jnp.int32, (hp, wl), 1)
    wc = jnp.exp(cwl)
    wc_rows = _dot_split_l(jnp.where(r16 == _idiv(c16, HEAD_DIM), wc, 0.0), tile_t)
    rsel = lax.broadcasted_iota(jnp.int32, (hp, HEAD_DIM), 0)
    for h in range(hp):
        rs = slice(h * HEAD_DIM, (h + 1) * HEAD_DIM)
        wc_h = jnp.sum(jnp.where(rsel == h, wc_rows, 0.0), axis=0, keepdims=True)
        sout_ref[0, h] = s_old[rs] * wc_h + ds[rs]

    y = yb[0:ct]
    for h in range(1, hp):
        y = y + yb[h * ct:(h + 1) * ct]
    inv = 1.0 / HEAD_DIM
    yc = y - segsum(y) * inv
    y = yc * lax.rsqrt(segsum(yc * yc) * inv + RWKV_LN_EPS) * ln_g + ln_b
    y = (y + segsum(r * kp * r_k) * v) * _silu(g)
    y = jnp.where(valid, y, 0.0)
    y_ref[...] = _pad_rows(y, rows_out).astype(y_ref.dtype)


def rwkv_short(rkvg, lora1, lora_b, par, s0, *, bsz, tp, hi):
    _, m, e = rkvg.shape
    heads = e // HEAD_DIM
    hp = RWKV_SHORT_HEADS
    wl = hp * HEAD_DIM
    rb = tp // 8
    w4 = RWKV_HPB * HEAD_DIM
    ones_bd = (jnp.arange(w4)[:, None] // HEAD_DIM == jnp.arange(w4)[None, :] // HEAD_DIM).astype(BF16)
    tile_t = (jnp.arange(wl)[:, None] % HEAD_DIM == jnp.arange(HEAD_DIM)[None, :]).astype(BF16)
    kern = functools.partial(_rwkv_short_kernel, nvalid=hi, rows_out=tp)
    proj_spec = lambda s: pl.BlockSpec((1, 8, wl), lambda b, h: (s, b * rb, h))
    return pl.pallas_call(
        kern,
        grid=(bsz, heads // hp),
        in_specs=[
            proj_spec(0), proj_spec(1), proj_spec(2), proj_spec(3),
            pl.BlockSpec((1, 8, RWKV_LORA_PAD), lambda b, h: (0, b * rb, 0)),
            pl.BlockSpec((1, 8, RWKV_LORA_PAD), lambda b, h: (1, b * rb, 0)),
            pl.BlockSpec((1, RWKV_LORA_PAD, wl), lambda b, h: (0, 0, h)),
            pl.BlockSpec((1, RWKV_LORA_PAD, wl), lambda b, h: (1, 0, h)),
            pl.BlockSpec((8, wl), lambda b, h: (0, h)),
            pl.BlockSpec((1, hp, HEAD_DIM, HEAD_DIM), lambda b, h: (b, h, 0, 0)),
            pl.BlockSpec((w4, w4), lambda b, h: (0, 0)),
            pl.BlockSpec((wl, HEAD_DIM), lambda b, h: (0, 0)),
        ],
        out_specs=[
            pl.BlockSpec((tp, wl), lambda b, h: (b, h)),
            pl.BlockSpec((1, hp, HEAD_DIM, HEAD_DIM), lambda b, h: (b, h, 0, 0)),
        ],
        out_shape=[
            jax.ShapeDtypeStruct((m, e), rkvg.dtype),
            jax.ShapeDtypeStruct((bsz, heads, HEAD_DIM, HEAD_DIM), F32),
        ],
        compiler_params=_cparams(("parallel", "parallel")),
        name="rwkv_short",
    )(rkvg, rkvg, rkvg, rkvg, lora1, lora1, lora_b, lora_b, par, s0, ones_bd, tile_t)


def _rwkv_lanes_kernel(r_ref, k_ref, v_ref, g_ref, lw_ref, la_ref, bw_ref, ba_ref, par_ref, s0_ref,
                       y_ref, sout_ref, tk, tw, tb, tq, tr, tv, ty, *, nt, nb):
    hp = RWKV_LANE_HEADS
    wl = hp * HEAD_DIM
    ri = lax.broadcasted_iota(jnp.int32, (wl, wl), 0)
    ci = lax.broadcasted_iota(jnp.int32, (wl, wl), 1)
    ones_bd = (_idiv(ri, HEAD_DIM) == _idiv(ci, HEAD_DIM)).astype(BF16)

    def segsum(x):
        return _dot_split_l(x, ones_bd, terms=2)

    par = par_ref[...]
    w0, a0, k_k, k_a, r_k, ln_g, ln_b = (par[i:i + 1] for i in range(7))
    r, k, v, g = (x[0].astype(F32) for x in (r_ref, k_ref, v_ref, g_ref))
    w_raw = w0 + _dot(jnp.tanh(lw_ref[0]).astype(BF16), bw_ref[0])
    a = jax.nn.sigmoid(a0 + _dot(la_ref[0].astype(BF16), ba_ref[0]))
    decay = jnp.exp(-jnp.exp(-_softplus(-w_raw) - 0.5))
    kk = k * k_k
    kk = kk / jnp.maximum(jnp.sqrt(segsum(kk * kk)), 1e-12)
    kp = k * (1.0 + (a - 1.0) * k_a)
    bb = kk * a

    for t in range(nt):
        rows = slice(t * nb, (t + 1) * nb)
        tk[t] = (-kk[rows]).T
        tw[t] = decay[rows].T
        tb[t] = bb[rows].T
        tq[t] = kp[rows].T
        tr[t] = r[rows].T
        tv[t] = v[rows].T

    for hh in range(hp):
        ks = slice(hh * HEAD_DIM, (hh + 1) * HEAD_DIM)

        def body(vi, carry, hh=hh, ks=ks):
            sv = s0_ref[hh, vi]
            row = hh * HEAD_DIM + vi
            for t in range(nt):
                sa = jnp.sum(sv * tk[t, ks, :], axis=0, keepdims=True)
                sv = sv * tw[t, ks, :] + sa * tb[t, ks, :] + tv[t, pl.ds(row, 1), :] * tq[t, ks, :]
                ty[t, pl.ds(row, 1), :] = jnp.sum(sv * tr[t, ks, :], axis=0, keepdims=True)
            sout_ref[hh, vi] = sv
            return carry

        lax.fori_loop(0, HEAD_DIM, body, 0, unroll=4)

    inv = 1.0 / HEAD_DIM
    for t in range(nt):
        rows = slice(t * nb, (t + 1) * nb)
        y = ty[t].T
        yc = y - segsum(y) * inv
        y = yc * lax.rsqrt(segsum(yc * yc) * inv + RWKV_LN_EPS) * ln_g + ln_b
        y = (y + segsum(r[rows] * kp[rows] * r_k) * v[rows]) * _silu(g[rows])
        y_ref[rows, :] = y.astype(BF16)


def rwkv_lanes(rkvg, lora1, lora_b, par, s0t, *, nt, nb):
    _, m, e = rkvg.shape
    heads = e // HEAD_DIM
    hp = RWKV_LANE_HEADS
    wl = hp * HEAD_DIM
    kern = functools.partial(_rwkv_lanes_kernel, nt=nt, nb=nb)
    proj_spec = lambda s: pl.BlockSpec((1, m, wl), lambda h: (s, 0, h))
    state_spec = pl.BlockSpec((hp, HEAD_DIM, HEAD_DIM, nb), lambda h: (h, 0, 0, 0))
    tile = pltpu.VMEM((nt, wl, nb), F32)
    return pl.pallas_call(
        kern,
        grid=(heads // hp,),
        in_specs=[
            proj_spec(0), proj_spec(1), proj_spec(2), proj_spec(3),
            pl.BlockSpec((1, m, RWKV_LORA_PAD), lambda h: (0, 0, 0)),
            pl.BlockSpec((1, m, RWKV_LORA_PAD), lambda h: (1, 0, 0)),
            pl.BlockSpec((1, RWKV_LORA_PAD, wl), lambda h: (0, 0, h)),
            pl.BlockSpec((1, RWKV_LORA_PAD, wl), lambda h: (1, 0, h)),
            pl.BlockSpec((8, wl), lambda h: (0, h)),
            state_spec,
        ],
        out_specs=[pl.BlockSpec((m, wl), lambda h: (0, h)), state_spec],
        out_shape=[jax.ShapeDtypeStruct((m, e), BF16), jax.ShapeDtypeStruct(s0t.shape, F32)],
        scratch_shapes=[tile] * 7,
        compiler_params=_cparams(("parallel",)),
        name="rwkv_lanes",
    )(rkvg, rkvg, rkvg, rkvg, lora1, lora1, lora_b, lora_b, par, s0t)


def _prep_weights(norm_g, ssd_w_in, ssd_dt_bias, ssd_a_log, ssd_d, ssd_w_out,
                  rwkv_w_rkvg, rwkv_w_lora_a, rwkv_w_lora_b, rwkv_a_lora_a, rwkv_a_lora_b,
                  rwkv_w0, rwkv_a0, rwkv_k_k, rwkv_k_a, rwkv_r_k, rwkv_ln_g, rwkv_ln_b, rwkv_w_out,
                  ret_w_in, ret_w_out):
    d_inner = ssd_w_out.shape[1]
    n_main = d_inner + d_inner + 2 * SSD_GROUPS * SSD_STATE
    ns = ssd_w_in.shape[0]
    d_model = ssd_w_in.shape[1]

    def head_lanes(p):
        p = p.reshape(ns, SSD_GROUPS, 1, SSD_HPG)
        return jnp.pad(p, ((0, 0), (0, 0), (0, 0), (0, 128 - SSD_HPG)))

    w_dt = ssd_w_in[:, :, n_main:].reshape(ns, d_model, SSD_GROUPS, SSD_HPG)
    w_dt = jnp.pad(w_dt, ((0, 0), (0, 0), (0, 0), (0, 128 - SSD_HPG))).reshape(ns, d_model, SSD_GROUPS * 128)
    rank = rwkv_w_lora_a.shape[2]
    lora_a = jnp.stack([rwkv_w_lora_a, rwkv_a_lora_a], axis=1)
    lora_a = jnp.pad(lora_a, ((0, 0), (0, 0), (0, 0), (0, RWKV_LORA_PAD - rank)))
    lora_b = jnp.stack([rwkv_w_lora_b, rwkv_a_lora_b], axis=1)
    lora_b = jnp.pad(lora_b, ((0, 0), (0, 0), (0, RWKV_LORA_PAD - rank), (0, 0)))
    nr = rwkv_w0.shape[0]
    par = jnp.stack([rwkv_w0, rwkv_a0, rwkv_k_k, rwkv_k_a, rwkv_r_k.reshape(nr, -1), rwkv_ln_g, rwkv_ln_b,
                     jnp.zeros_like(rwkv_w0)], axis=1)
    return dict(
        ssd_w_in=ssd_w_in.astype(BF16), ssd_n_main=n_main, ssd_w_dt=w_dt.astype(BF16),
        ssd_dtb=head_lanes(ssd_dt_bias), ssd_alog=head_lanes(ssd_a_log), ssd_dskip=head_lanes(ssd_d),
        ssd_w_out=ssd_w_out.astype(BF16),
        rwkv_w=rwkv_w_rkvg.astype(BF16), rwkv_lora_a=lora_a.astype(BF16), rwkv_lora_b=lora_b.astype(BF16),
        rwkv_par=par, rwkv_w_out=rwkv_w_out.astype(BF16),
        ret_w_in=ret_w_in.astype(BF16), ret_w_out=ret_w_out.astype(BF16),
    )


def _trunk(h, conv_st, ssd_st, shift_st, wkv_st, ret_st, pos, *, bsz, tp, lq, lo, hi, depth,
           norm_g, final_norm_g, wts, ssd_conv_w, ssd_conv_b, ssd_norm_g, rwkv_mu, ret_norm_g):
    d_model = h.shape[1]
    geo = dict(bsz=bsz, tp=tp, lo=lo, hi=hi)
    new_conv, new_shift, new_wkv, new_ret = [], [], [], []
    new_ssd = None
    act = BF16 if lq % 16 == 0 else F32

    half = RET_QK // 2
    inv_freq = 1.0 / (RET_THETA_BASE ** jnp.linspace(0.0, 1.0, half, dtype=F32))
    ang = pos.astype(F32)[:, None] * inv_freq
    cos, sin = jnp.cos(ang), jnp.sin(ang)
    log_gamma = jnp.log1p(-jnp.exp2(-5.0 - jnp.arange(RET_HEADS, dtype=F32)))
    lg = jnp.broadcast_to(log_gamma[:, None, None], (RET_HEADS, 8, 128))

    for layer in range(depth):
        kind, j = layer % 3, layer // 3
        ng = norm_g[layer]
        if kind == 0:
            n_main = wts["ssd_n_main"]
            proj = matmul(h, wts["ssd_w_in"], j, out_dtype=act, norm_g=ng, n_cols=n_main)
            dt_raw = matmul(h, wts["ssd_w_dt"], j, norm_g=ng)
            conv_init = jnp.pad(conv_st[j], ((0, 0), (8 - (SSD_CONV - 1), 0), (0, 0)))
            y, new_ssd = ssd_scan(proj, dt_raw, conv_init, ssd_st, j, new_ssd, ssd_conv_w[j], ssd_conv_b[j],
                                  wts["ssd_dtb"][j], wts["ssd_alog"][j], wts["ssd_dskip"][j], ssd_norm_g[j],
                                  lq=lq, **geo)
            nk = SSD_CONV - 1
            last = h.reshape(bsz, tp, d_model)[:, hi - nk:hi].reshape(bsz * nk, d_model)
            last = jnp.pad(last, ((0, -(bsz * nk) % 16), (0, 0)))
            xbc = matmul(last, wts["ssd_w_in"], j, norm_g=ng, n_cols=n_main)[:bsz * nk, y.shape[1]:]
            new_conv.append(xbc.reshape(bsz, nk, -1))
            h = matmul(y, wts["ssd_w_out"], j, res=h)
        elif kind == 1 and lo == 0 and hi <= 8 and bsz % LANES == 0:
            u = rmsnorm(h, ng)
            tmajor = lambda x: jnp.swapaxes(x.reshape(bsz, tp, -1)[:, :hi], 0, 1)
            uc = tmajor(u)
            prev = jnp.concatenate([shift_st[j][None], uc[:-1]], axis=0).reshape(hi * bsz, d_model)
            uc = uc.reshape(hi * bsz, d_model)
            rkvg = mix_matmul(uc, prev, rwkv_mu[j][:4], wts["rwkv_w"][j], out_dtype=BF16)
            lora1 = mix_matmul(uc, prev, rwkv_mu[j][4:], wts["rwkv_lora_a"][j])
            y, s_t = rwkv_lanes(rkvg, lora1, wts["rwkv_lora_b"][j], wts["rwkv_par"][j],
                                jnp.transpose(wkv_st[j], (1, 2, 3, 0)), nt=hi, nb=bsz)
            new_shift.append(u.reshape(bsz, tp, d_model)[:, hi - 1])
            new_wkv.append(jnp.transpose(s_t, (3, 0, 1, 2)))
            hc = matmul(y, wts["rwkv_w_out"], j, res=tmajor(h).reshape(hi * bsz, d_model))
            hc = jnp.swapaxes(hc.reshape(hi, bsz, d_model), 0, 1)
            h = jnp.pad(hc, ((0, 0), (0, tp - hi), (0, 0))).reshape(bsz * tp, d_model)
        elif kind == 1:
            u = rmsnorm(h, ng)
            u3 = u.reshape(bsz, tp, d_model)
            if lo > 0:
                prev = None
            else:
                prev = jnp.concatenate([shift_st[j][:, None, :], u3[:, :-1]], axis=1).reshape(bsz * tp, d_model)
            rkvg = mix_matmul(u, prev, rwkv_mu[j][:4], wts["rwkv_w"][j], out_dtype=BF16)
            lora1 = mix_matmul(u, prev, rwkv_mu[j][4:], wts["rwkv_lora_a"][j])
            if lo == 0 and hi <= 8 and tp <= 16:
                y, s_new = rwkv_short(rkvg.astype(F32), lora1, wts["rwkv_lora_b"][j], wts["rwkv_par"][j], wkv_st[j],
                                      bsz=bsz, tp=tp, hi=hi)
            else:
                y, s_new = rwkv_scan(rkvg, lora1, wts["rwkv_lora_b"][j], wts["rwkv_par"][j], wkv_st[j], **geo)
            new_shift.append(u3[:, hi - 1])
            new_wkv.append(s_new)
            h = matmul(y, wts["rwkv_w_out"], j, res=h)
        else:
            proj = matmul(h, wts["ret_w_in"], j, out_dtype=act, norm_g=ng)
            y, s_new = ret_scan(proj, cos, sin, lg, ret_st[j], ret_norm_g[j], lq=lq, **geo)
            new_ret.append(s_new)
            h = matmul(y, wts["ret_w_out"], j, res=h)
    y = rmsnorm(h, final_norm_g)
    return (y, jnp.stack(new_conv), new_ssd, jnp.stack(new_shift), jnp.stack(new_wkv), jnp.stack(new_ret))


def kernel(x_prompt, x_sample, state_ssd_conv, state_ssd, state_rwkv_shift, state_rwkv_wkv, state_ret, meta_tokens, norm_g, final_norm_g, ssd_w_in, ssd_conv_w, ssd_conv_b, ssd_dt_bias, ssd_a_log, ssd_d, ssd_norm_g, ssd_w_out, rwkv_mu, rwkv_w_rkvg, rwkv_w0, rwkv_w_lora_a, rwkv_w_lora_b, rwkv_a0, rwkv_a_lora_a, rwkv_a_lora_b, rwkv_k_k, rwkv_k_a, rwkv_r_k, rwkv_ln_g, rwkv_ln_b, rwkv_w_out, ret_w_in, ret_norm_g, ret_w_out):
    depth = norm_g.shape[0]
    d_model = x_prompt.shape[2]
    wts = _prep_weights(norm_g, ssd_w_in, ssd_dt_bias, ssd_a_log, ssd_d, ssd_w_out,
                        rwkv_w_rkvg, rwkv_w_lora_a, rwkv_w_lora_b, rwkv_a_lora_a, rwkv_a_lora_b,
                        rwkv_w0, rwkv_a0, rwkv_k_k, rwkv_k_a, rwkv_r_k, rwkv_ln_g, rwkv_ln_b, rwkv_w_out,
                        ret_w_in, ret_w_out)
    common = dict(depth=depth, norm_g=norm_g, final_norm_g=final_norm_g, wts=wts, ssd_conv_w=ssd_conv_w,
                  ssd_conv_b=ssd_conv_b, ssd_norm_g=ssd_norm_g, rwkv_mu=rwkv_mu, ret_norm_g=ret_norm_g)

    bp, seq, _ = x_prompt.shape
    lq_p = 128
    lo_p = lq_p - N_META
    tp_p = lo_p + N_META + seq
    h_p = jnp.concatenate([jnp.zeros((bp, lo_p, d_model), F32),
                           jnp.broadcast_to(meta_tokens[None], (bp, N_META, d_model)), x_prompt], axis=1)
    zeros_like_b = lambda s: jnp.zeros((s.shape[0], bp) + s.shape[2:], F32)
    pos_p = jnp.maximum(jnp.arange(tp_p) - lo_p, 0)
    outs_p = _trunk(h_p.reshape(bp * tp_p, d_model), zeros_like_b(state_ssd_conv), zeros_like_b(state_ssd),
                    zeros_like_b(state_rwkv_shift), zeros_like_b(state_rwkv_wkv), zeros_like_b(state_ret), pos_p,
                    bsz=bp, tp=tp_p, lq=lq_p, lo=lo_p, hi=tp_p, **common)
    y_prompt = outs_p[0].reshape(bp, tp_p, d_model)[:, lo_p + N_META:]

    bs, ds, _ = x_sample.shape
    tp_s = 8
    h_s = jnp.concatenate([x_sample, jnp.zeros((bs, tp_s - ds, d_model), F32)], axis=1)
    pos_s = PAST_LEN + jnp.arange(tp_s)
    outs_s = _trunk(h_s.reshape(bs * tp_s, d_model), state_ssd_conv, state_ssd, state_rwkv_shift, state_rwkv_wkv,
                    state_ret, pos_s, bsz=bs, tp=tp_s, lq=tp_s, lo=0, hi=ds, **common)
    y_sample = outs_s[0].reshape(bs, tp_s, d_model)[:, :ds]

    return (y_prompt, y_sample) + tuple(outs_p[1:]) + tuple(outs_s[1:])
```

```python
import functools
import math

import jax
import jax.numpy as jnp
from jax import lax
from jax.experimental import pallas as pl
from jax.experimental.pallas import tpu as pltpu

F32 = jnp.float32
BF16 = jnp.bfloat16

EPS = 1e-6
N_META = 16
HEAD_DIM = 64
SSD_STATE = 128
SSD_GROUPS = 8
SSD_HPG = 8
SSD_CONV = 4
RET_HEADS = 8
RET_QK = 256
RET_V = 512
RET_THETA_BASE = 10000.0
RWKV_LORA_PAD = 128
RWKV_CHUNK = 64
RWKV_HPB = 4
RWKV_GROUPS_PER_STEP = 16
RWKV_SHORT_HEADS = 16
RWKV_LANE_HEADS = 2
LANES = 128
MIN_CHUNK_ROWS = 16
PAST_LEN = 16384
RWKV_LN_EPS = 1e-5 * HEAD_DIM
NEG = -1e30
ROW_TILE = 512
V7X_VMEM_BYTES = 64 * 1024 * 1024
VMEM_LIMIT = V7X_VMEM_BYTES * 7 // 8
MATMUL_VMEM_BUDGET = V7X_VMEM_BYTES * 5 // 8


def _cparams(sem):
    return pltpu.CompilerParams(dimension_semantics=sem, vmem_limit_bytes=VMEM_LIMIT)


def _nt(a, b):
    return lax.dot_general(a, b, (((1,), (1,)), ((), ())), preferred_element_type=F32)


def _tn(a, b):
    return lax.dot_general(a, b, (((0,), (0,)), ((), ())), preferred_element_type=F32)


def _dot(a, b):
    return jnp.dot(a, b, preferred_element_type=F32)


def _split(x, terms):
    parts = []
    r = x
    for i in range(terms):
        p = r.astype(BF16)
        parts.append(p)
        if i + 1 < terms:
            r = r - p.astype(F32)
    return parts


def _dot_split_l(x, m, terms=3):
    acc = None
    for p in _split(x, terms):
        d = _dot(p, m)
        acc = d if acc is None else acc + d
    return acc


def _dot_split_r(m, x, terms=3):
    acc = None
    for p in _split(x, terms):
        d = _dot(m, p)
        acc = d if acc is None else acc + d
    return acc


def _pad_rows(x, rows):
    if x.shape[0] == rows:
        return x
    return jnp.concatenate([x, jnp.zeros((rows - x.shape[0], x.shape[1]), x.dtype)], axis=0)


def _idiv(x, n):
    return jnp.right_shift(x, int(math.log2(n)))


def _imod(x, n):
    return jnp.bitwise_and(x, n - 1)


def _silu(x):
    h = 0.5 * x
    return h + h * jnp.tanh(h)


def _softplus(x):
    return jnp.maximum(x, 0.0) + jnp.log(1.0 + jnp.exp(-jnp.abs(x)))


def _rmsnorm_kernel(x_ref, g_ref, o_ref):
    x = x_ref[...]
    ms = jnp.mean(x * x, axis=-1, keepdims=True)
    o_ref[...] = (x * lax.rsqrt(ms + EPS) * g_ref[...]).astype(o_ref.dtype)


def _row_tile(m):
    tm = math.gcd(m, ROW_TILE)
    assert tm % 16 == 0, m
    return tm


def _matmul_tiles(m, k, n, *, a_bytes, n_a, cast, n_out):
    best = None
    for tn in (t for t in (1024, 512, 256, 128) if n % t == 0):
        for tm in (t for t in range(16, m + 1, 16) if m % t == 0):
            need = (2 * n_a * tm * k * a_bytes + (tm * k * 2 if cast else 0)
                    + 2 * k * tn * 2 + 2 * n_out * tm * tn * 4)
            if need <= MATMUL_VMEM_BUDGET and (best is None or (tm * tn, tm) > (best[0] * best[1], best[0])):
                best = (tm, tn)
    assert best is not None, (m, k, n)
    return best


def rmsnorm(x, g, out_dtype=F32):
    m, d = x.shape
    tm = _row_tile(m)
    return pl.pallas_call(
        _rmsnorm_kernel,
        grid=(m // tm,),
        in_specs=[pl.BlockSpec((tm, d), lambda i: (i, 0)), pl.BlockSpec((1, d), lambda i: (0, 0))],
        out_specs=pl.BlockSpec((tm, d), lambda i: (i, 0)),
        out_shape=jax.ShapeDtypeStruct((m, d), out_dtype),
        compiler_params=_cparams(("parallel",)),
        name="rmsnorm",
    )(x, g.reshape(1, d))


def _mm_kernel(a_ref, w_ref, *rest, has_res, cast, norm):
    rest = list(rest)
    abf_ref = rest.pop() if cast else a_ref
    o_ref = rest.pop()

    if cast:
        @pl.when(pl.program_id(1) == 0)
        def _():
            x = a_ref[...]
            if norm:
                x = x * lax.rsqrt(jnp.mean(x * x, axis=-1, keepdims=True) + EPS) * rest[0][...]
            abf_ref[...] = x.astype(BF16)

    acc = _dot(abf_ref[...], w_ref[0])
    if has_res:
        acc = rest[-1][...] + acc
    o_ref[...] = acc.astype(o_ref.dtype)


def matmul(a, w, layer, res=None, out_dtype=F32, norm_g=None, n_cols=None):
    m, k = a.shape
    n = w.shape[2] if n_cols is None else n_cols
    cast = a.dtype != BF16 or norm_g is not None
    tm, tn = _matmul_tiles(m, k, n, a_bytes=a.dtype.itemsize, n_a=1, cast=cast, n_out=2 if res is not None else 1)
    in_specs = [pl.BlockSpec((tm, k), lambda i, j: (i, 0)), pl.BlockSpec((1, k, tn), lambda i, j: (layer, 0, j))]
    args = [a, w]
    if norm_g is not None:
        in_specs.append(pl.BlockSpec((1, k), lambda i, j: (0, 0)))
        args.append(norm_g.reshape(1, k))
    if res is not None:
        in_specs.append(pl.BlockSpec((tm, tn), lambda i, j: (i, j)))
        args.append(res)
    return pl.pallas_call(
        functools.partial(_mm_kernel, has_res=res is not None, cast=cast, norm=norm_g is not None),
        grid=(m // tm, n // tn),
        in_specs=in_specs,
        out_specs=pl.BlockSpec((tm, tn), lambda i, j: (i, j)),
        out_shape=jax.ShapeDtypeStruct((m, n), out_dtype),
        scratch_shapes=[pltpu.VMEM((tm, k), BF16)] if cast else [],
        compiler_params=_cparams(("parallel", "arbitrary")),
        name="matmul_res" if res is not None else "matmul",
    )(*args)


def _mixmm_kernel(u_ref, p_ref, mu_ref, w_ref, o_ref, xm_ref, *, shift_rows):
    @pl.when(pl.program_id(2) == 0)
    def _():
        u = u_ref[...]
        if shift_rows:
            first = lax.broadcasted_iota(jnp.int32, u.shape, 0) == 0
            prev = jnp.where(first, p_ref[7:8, :], pltpu.roll(u, 1, 0))
        else:
            prev = p_ref[...]
        xm_ref[...] = (u + (prev - u) * mu_ref[0]).astype(BF16)

    o_ref[0] = _dot(xm_ref[...], w_ref[0]).astype(o_ref.dtype)


def mix_matmul(u, prev, mu, w, out_dtype=F32):
    m, k = u.shape
    s, _, n = w.shape
    shift_rows = prev is None
    tm, tn = _matmul_tiles(m, k, n, a_bytes=4, n_a=1 if shift_rows else 2, cast=True, n_out=1)
    if shift_rows:
        prev_spec = pl.BlockSpec((8, k), lambda i, si, j: (jnp.maximum(i * (tm // 8) - 1, 0), 0))
    else:
        prev_spec = pl.BlockSpec((tm, k), lambda i, si, j: (i, 0))
    return pl.pallas_call(
        functools.partial(_mixmm_kernel, shift_rows=shift_rows),
        grid=(m // tm, s, n // tn),
        in_specs=[
            pl.BlockSpec((tm, k), lambda i, si, j: (i, 0)),
            prev_spec,
            pl.BlockSpec((1, 1, k), lambda i, si, j: (si, 0, 0)),
            pl.BlockSpec((1, k, tn), lambda i, si, j: (si, 0, j)),
        ],
        out_specs=pl.BlockSpec((1, tm, tn), lambda i, si, j: (si, i, j)),
        out_shape=jax.ShapeDtypeStruct((s, m, n), out_dtype),
        scratch_shapes=[pltpu.VMEM((tm, k), BF16)],
        compiler_params=_cparams(("parallel", "arbitrary", "arbitrary")),
        name="mix_matmul",
    )(u, u if shift_rows else prev, mu.reshape(s, 1, k), w)


def _conv_silu(cur, car_ref, cols, w, b, lq, lb):
    car_ref[8:8 + lq, cols] = cur
    acc = b + cur * w[SSD_CONV - 1:SSD_CONV]
    for s in range(1, SSD_CONV):
        acc = acc + car_ref[8 - s:8 - s + lq, cols] * w[SSD_CONV - 1 - s:SSD_CONV - s]
    car_ref[0:8, cols] = cur[lb - 8:lb]
    return _silu(acc)


def _ssd_kernel(z_ref, x_ref, b_ref, c_ref, dt_ref, ci_ref, s0_ref, cw_ref, cb_ref, dtb_ref, alog_ref,
                d_ref, ng_ref, *rest, lq, lo, hi, nchunks, out_layer):
    y_ref, sout_ref, car, st = rest[-4:]
    c = pl.program_id(1)
    gw = SSD_HPG * HEAD_DIM

    @pl.when(c == 0)
    def _():
        car[0:8, :] = ci_ref[0]
        st[...] = s0_ref[0, 0].reshape(SSD_GROUPS * gw, SSD_STATE)

    gens = [_ssd_group(g, z_ref, x_ref, b_ref, c_ref, dt_ref, cw_ref, cb_ref, dtb_ref, alog_ref, d_ref,
                       ng_ref, y_ref, car, st, lb=lq, lo=lo, hi=hi) for g in range(SSD_GROUPS)]
    for _ in zip(*gens):
        pass

    @pl.when(c == nchunks - 1)
    def _():
        for layer in range(sout_ref.shape[0]):
            if layer == out_layer:
                sout_ref[layer, 0] = st[...].reshape(SSD_GROUPS * SSD_HPG, HEAD_DIM, SSD_STATE)
            else:
                sout_ref[layer, 0] = jnp.zeros(sout_ref.shape[2:], F32)


def _ssd_group(g, z_ref, x_ref, b_ref, c_ref, dt_ref, cw_ref, cb_ref, dtb_ref, alog_ref, d_ref, ng_ref,
               y_ref, car, st, *, lb, lo, hi):
    c = pl.program_id(1)
    lq = max(lb, MIN_CHUNK_ROWS)
    gw = SSD_HPG * HEAD_DIM
    d_inner = SSD_GROUPS * gw
    xs = slice(g * gw, (g + 1) * gw)
    ns = slice(g * SSD_STATE, (g + 1) * SSD_STATE)
    bs = slice(d_inner + g * SSD_STATE, d_inner + (g + 1) * SSD_STATE)
    cs_ = slice(d_inner + (SSD_GROUPS + g) * SSD_STATE, d_inner + (SSD_GROUPS + g + 1) * SSD_STATE)

    f32 = lambda x: _pad_rows(x.astype(F32), lq)
    xc = _conv_silu(f32(x_ref[:, xs]), car, xs, cw_ref[:, xs], cb_ref[:, xs], lq, lb)
    bc = _conv_silu(f32(b_ref[:, ns]), car, bs, cw_ref[:, bs], cb_ref[:, bs], lq, lb)
    cc = _conv_silu(f32(c_ref[:, ns]), car, cs_, cw_ref[:, cs_], cb_ref[:, cs_], lq, lb)

    ti = lax.broadcasted_iota(jnp.int32, (lq, 1), 0)
    pos = c * lb + ti
    valid = (ti < lb) & (pos >= lo) & (pos < hi)
    lane = lax.broadcasted_iota(jnp.int32, (lq, 128), 1)
    dt = _softplus(f32(dt_ref[:, ns]) + dtb_ref[:, ns])
    dt = jnp.where(valid & (lane < SSD_HPG), dt, 0.0)
    la = dt * (-jnp.exp(alog_ref[:, ns]))
    yield

    ri = lax.broadcasted_iota(jnp.int32, (lq, lq), 0)
    ci = lax.broadcasted_iota(jnp.int32, (lq, lq), 1)
    tril = (ri >= ci).astype(BF16)
    triu = (ri <= ci).astype(BF16)
    acum = _dot_split_r(tril, la)
    acum_t = sum(_tn(part, triu) for part in _split(la, 3))
    yield
    a_end = acum[lq - 1:lq]
    dec_end = jnp.exp(a_end - acum)
    e_in = jnp.exp(acum)
    cd = jnp.exp(a_end)

    bcp = bc.astype(BF16)
    ccb = cc.astype(BF16)
    g_sc = _nt(ccb, bcp)
    st_old = st[xs, :]
    y_in = _nt(ccb, st_old.astype(BF16))
    yield

    causal = ci <= ri
    lane_q = lax.broadcasted_iota(jnp.int32, (lq, 128), 1) < HEAD_DIM
    lane_k = lane_q
    row_k = lax.broadcasted_iota(jnp.int32, (128, 1), 0) < HEAD_DIM
    dvec = d_ref[:, ns]

    er = lax.broadcasted_iota(jnp.int32, (2 * LANES, gw), 0)
    ec = lax.broadcasted_iota(jnp.int32, (2 * LANES, gw), 1)
    spread_m = (_imod(er, LANES) == _idiv(ec, HEAD_DIM)).astype(BF16)
    spread = lambda f: _dot(jnp.concatenate(_split(f, 2), axis=1), spread_m)
    dt_x = spread(dt)
    e_x = spread(e_in)
    dec_x = spread(dec_end)
    yield

    ys = []
    for p in range(SSD_HPG // 2):
        h0, h1 = 2 * p, 2 * p + 1
        ps = slice(128 * p, 128 * (p + 1))
        xp = xc[:, ps]
        vp = xp * dt_x[:, ps]
        vpp = vp
        yp = y_in[:, ps] * e_x[:, ps]
        yp = yp + xp * jnp.where(lane_q, dvec[:, h0:h0 + 1], dvec[:, h1:h1 + 1])
        for hh, h in ((0, h0), (1, h1)):
            seg = acum[:, h:h + 1] - acum_t[h:h + 1, :]
            lm = jnp.exp(jnp.where(causal, seg, NEG))
            pm = (g_sc * lm).astype(BF16)
            vm = jnp.where(lane_k if hh == 0 else jnp.logical_not(lane_k), vpp, 0.0).astype(BF16)
            yp = yp + _dot(pm, vm)
        ys.append(yp)
        vend = vpp * dec_x[:, ps]
        upd = _tn(vend.astype(BF16), bcp)
        cdp = jnp.where(row_k, cd[:, h0:h0 + 1], cd[:, h1:h1 + 1])
        st[g * gw + 128 * p:g * gw + 128 * (p + 1), :] = st_old[128 * p:128 * (p + 1), :] * cdp + upd
        yield

    y = jnp.concatenate(ys, axis=1)
    y = y * _silu(f32(z_ref[:, xs]))
    ms = jnp.mean(y * y, axis=-1, keepdims=True)
    y = y * lax.rsqrt(ms + EPS) * ng_ref[:, xs]
    y_ref[:, xs] = jnp.where(valid, y, 0.0)[:lb].astype(y_ref.dtype)
    yield


def ssd_scan(proj, dt_raw, conv_init, s0_all, layer, s_buf, conv_w, conv_b, dtb, alog, dskip, norm_g,
             *, bsz, tp, lq, lo, hi):
    m = proj.shape[0]
    heads = SSD_GROUPS * SSD_HPG
    nlayers = s0_all.shape[0]
    state_spec = pl.BlockSpec((1, 1, heads, HEAD_DIM, SSD_STATE), lambda b, c: (layer, b, 0, 0, 0))
    if s_buf is not None:
        out_state_spec, out_layer = state_spec, 0
        extra_specs, extra_args, aliases = [pl.BlockSpec(memory_space=pl.ANY)], [s_buf], {13: 1}
    else:
        out_state_spec = pl.BlockSpec((nlayers, 1, heads, HEAD_DIM, SSD_STATE), lambda b, c: (0, b, 0, 0, 0))
        out_layer = layer
        extra_specs, extra_args, aliases = [], [], {}
    nch = tp // lq
    d_inner = heads * HEAD_DIM
    gn = SSD_GROUPS * SSD_STATE
    conv_dim = d_inner + 2 * gn
    row = lambda b, c: b * nch + c
    const = lambda b, c: (0, 0)
    kern = functools.partial(_ssd_kernel, lq=lq, lo=lo, hi=hi, nchunks=nch, out_layer=out_layer)
    return pl.pallas_call(
        kern,
        grid=(bsz, nch),
        in_specs=[
            pl.BlockSpec((lq, d_inner), lambda b, c: (row(b, c), 0)),
            pl.BlockSpec((lq, d_inner), lambda b, c: (row(b, c), 1)),
            pl.BlockSpec((lq, gn), lambda b, c: (row(b, c), 2 * d_inner // gn)),
            pl.BlockSpec((lq, gn), lambda b, c: (row(b, c), 2 * d_inner // gn + 1)),
            pl.BlockSpec((lq, gn), lambda b, c: (row(b, c), 0)),
            pl.BlockSpec((1, 8, conv_dim), lambda b, c: (b, 0, 0)),
            state_spec,
            pl.BlockSpec((SSD_CONV, conv_dim), const),
            pl.BlockSpec((1, conv_dim), const),
            pl.BlockSpec((1, gn), const),
            pl.BlockSpec((1, gn), const),
            pl.BlockSpec((1, gn), const),
            pl.BlockSpec((1, d_inner), const),
        ] + extra_specs,
        out_specs=[
            pl.BlockSpec((lq, d_inner), lambda b, c: (row(b, c), 0)),
            out_state_spec,
        ],
        out_shape=[
            jax.ShapeDtypeStruct((m, d_inner), proj.dtype),
            jax.ShapeDtypeStruct(s0_all.shape, F32),
        ],
        scratch_shapes=[pltpu.VMEM((8 + max(lq, MIN_CHUNK_ROWS), conv_dim), F32),
                        pltpu.VMEM((d_inner, SSD_STATE), F32)],
        input_output_aliases=aliases,
        compiler_params=_cparams(("parallel", "arbitrary")),
        name="ssd_scan",
    )(proj, proj, proj, proj, dt_raw, conv_init, s0_all, conv_w, conv_b.reshape(1, -1),
      dtb.reshape(1, -1), alog.reshape(1, -1), dskip.reshape(1, -1), norm_g.reshape(1, -1), *extra_args)


def _ret_kernel(q_ref, k_ref, v_ref, g_ref, cos_ref, sin_ref, lg_ref, s0_ref, ng_ref,
                y_ref, sout_ref, *, lq, lo, hi):
    c = pl.program_id(1)

    @pl.when(c == 0)
    def _():
        sout_ref[...] = s0_ref[...]

    gens = [_ret_head(h, q_ref, k_ref, v_ref, g_ref, cos_ref, sin_ref, lg_ref, ng_ref, y_ref, sout_ref,
                      lb=lq, lo=lo, hi=hi) for h in range(RET_HEADS)]
    for _ in zip(*gens):
        pass


def _ret_head(h, q_ref, k_ref, v_ref, g_ref, cos_ref, sin_ref, lg_ref, ng_ref, y_ref, st_ref, *, lb, lo, hi):
    c = pl.program_id(1)
    lq = max(lb, MIN_CHUNK_ROWS)
    f32 = lambda x: _pad_rows(x.astype(F32), lq)
    qs = slice(h * RET_QK, (h + 1) * RET_QK)
    vs = slice(h * RET_V, (h + 1) * RET_V)
    lg = lg_ref[h][0:1, 0:1]
    nv = float(hi - lo)

    def count(p):
        return jnp.clip((p + 1 - lo).astype(F32), 0.0, nv)

    base = c * lb
    ti = lax.broadcasted_iota(jnp.int32, (lq, 1), 0)
    pos_i = base + ti
    valid = (ti < lb) & (pos_i >= lo) & (pos_i < hi)
    cnt_i = count(pos_i)
    cnt_j = count(base + lax.broadcasted_iota(jnp.int32, (1, lq), 1))
    cnt_jc = cnt_i
    cnt0 = count(base - 1 + jnp.zeros((1, 1), jnp.int32))
    cnt_end = count(base + lb - 1 + jnp.zeros((1, 1), jnp.int32))

    cos = f32(cos_ref[...])
    sin = f32(sin_ref[...])
    half = RET_QK // 2

    def rot(x):
        x1, x2 = x[:, :half], x[:, half:]
        return jnp.concatenate([x1 * cos - x2 * sin, x1 * sin + x2 * cos], axis=1)

    qr = rot(f32(q_ref[:, qs])).astype(BF16)
    kr = jnp.where(valid, rot(f32(k_ref[:, qs])) * (RET_QK ** -0.5), 0.0)
    v = jnp.where(valid, f32(v_ref[:, vs]), 0.0)
    krp = kr.astype(BF16)
    vp = v
    yield

    sc = _nt(qr, krp)
    s_old = st_ref[0, h]
    y_in = _nt(qr, s_old.astype(BF16))
    yield
    qi = lax.broadcasted_iota(jnp.int32, (lq, lq), 0)
    kj = lax.broadcasted_iota(jnp.int32, (lq, lq), 1)
    dm = jnp.exp(jnp.where(kj <= qi, lg * (cnt_i - cnt_j), NEG))
    y = _dot((sc * dm).astype(BF16), vp.astype(BF16))
    y = y + y_in * jnp.exp(lg * (cnt_i - cnt0))
    vend = vp * jnp.exp(lg * (cnt_end - cnt_jc))
    st_ref[0, h] = s_old * jnp.exp(lg * (cnt_end - cnt0)) + _tn(vend.astype(BF16), krp)
    yield

    ms = jnp.mean(y * y, axis=-1, keepdims=True)
    y = y * lax.rsqrt(ms + EPS) * ng_ref[:, vs] * _silu(f32(g_ref[:, vs]))
    y_ref[:, vs] = jnp.where(valid, y, 0.0)[:lb].astype(y_ref.dtype)
    yield


def ret_scan(proj, cos, sin, lg, s0, norm_g, *, bsz, tp, lq, lo, hi):
    m = proj.shape[0]
    nch = tp // lq
    d_inner = RET_HEADS * RET_V
    d_qk = RET_HEADS * RET_QK
    row = lambda b, c: b * nch + c
    state_spec = pl.BlockSpec((1, RET_HEADS, RET_V, RET_QK), lambda b, c: (b, 0, 0, 0))
    kern = functools.partial(_ret_kernel, lq=lq, lo=lo, hi=hi)
    return pl.pallas_call(
        kern,
        grid=(bsz, nch),
        in_specs=[
            pl.BlockSpec((lq, d_qk), lambda b, c: (row(b, c), 0)),
            pl.BlockSpec((lq, d_qk), lambda b, c: (row(b, c), 1)),
            pl.BlockSpec((lq, d_inner), lambda b, c: (row(b, c), 2 * d_qk // d_inner)),
            pl.BlockSpec((lq, d_inner), lambda b, c: (row(b, c), 2 * d_qk // d_inner + 1)),
            pl.BlockSpec((lq, RET_QK // 2), lambda b, c: (c, 0)),
            pl.BlockSpec((lq, RET_QK // 2), lambda b, c: (c, 0)),
            pl.BlockSpec((RET_HEADS, 8, 128), lambda b, c: (0, 0, 0)),
            state_spec,
            pl.BlockSpec((1, d_inner), lambda b, c: (0, 0)),
        ],
        out_specs=[
            pl.BlockSpec((lq, d_inner), lambda b, c: (row(b, c), 0)),
            state_spec,
        ],
        out_shape=[
            jax.ShapeDtypeStruct((m, d_inner), proj.dtype),
            jax.ShapeDtypeStruct((bsz, RET_HEADS, RET_V, RET_QK), F32),
        ],
        compiler_params=_cparams(("parallel", "arbitrary")),
        name="ret_scan",
    )(proj, proj, proj, proj, cos, sin, lg, s0, norm_g.reshape(1, -1))


def _rwkv_kernel(r_ref, k_ref, v_ref, g_ref, lw_ref, la_ref, bw_ref, ba_ref, par_ref, s0_ref,
                 y_ref, sout_ref, st, *, lo, hi, nchunks, ngrp):
    c = pl.program_id(2)
    w4 = RWKV_HPB * HEAD_DIM
    r2 = lax.broadcasted_iota(jnp.int32, (w4, w4), 0)
    c2 = lax.broadcasted_iota(jnp.int32, (w4, w4), 1)
    blk = _idiv(r2, HEAD_DIM) == _idiv(c2, HEAD_DIM)

    @pl.when(c == 0)
    def _():
        tile = (lax.broadcasted_iota(jnp.int32, (HEAD_DIM, w4), 0)
                == _imod(lax.broadcasted_iota(jnp.int32, (HEAD_DIM, w4), 1), HEAD_DIM)).astype(BF16)
        for gi in range(ngrp):
            s0 = s0_ref[0, gi * RWKV_HPB:(gi + 1) * RWKV_HPB].reshape(w4, HEAD_DIM)
            st[gi] = jnp.where(blk, _dot_split_l(s0, tile), 0.0)

    has_rows = ((c + 1) * RWKV_CHUNK > lo) & (c * RWKV_CHUNK < hi)

    @pl.when(has_rows)
    def _():
        gens = [_rwkv_group(gi, r_ref, k_ref, v_ref, g_ref, lw_ref, la_ref, bw_ref, ba_ref, par_ref,
                            y_ref, st, lo=lo, hi=hi) for gi in range(ngrp)]
        for _ in zip(*gens):
            pass

    @pl.when(jnp.logical_not(has_rows))
    def _():
        y_ref[...] = jnp.zeros(y_ref.shape, y_ref.dtype)

    @pl.when(c == nchunks - 1)
    def _():
        tile_t = (_imod(lax.broadcasted_iota(jnp.int32, (w4, HEAD_DIM), 0), HEAD_DIM)
                  == lax.broadcasted_iota(jnp.int32, (w4, HEAD_DIM), 1)).astype(BF16)
        for gi in range(ngrp):
            sout_ref[0, gi * RWKV_HPB:(gi + 1) * RWKV_HPB] = _dot_split_l(st[gi], tile_t).reshape(
                RWKV_HPB, HEAD_DIM, HEAD_DIM)


def _rwkv_group(gi, r_ref, k_ref, v_ref, g_ref, lw_ref, la_ref, bw_ref, ba_ref, par_ref, y_ref, st,
                *, lo, hi):
    c = pl.program_id(2)
    cs = RWKV_CHUNK
    w4 = RWKV_HPB * HEAD_DIM
    sl = slice(gi * w4, (gi + 1) * w4)

    nr = RWKV_HPB * cs
    ones_bd = (_idiv(lax.broadcasted_iota(jnp.int32, (w4, w4), 0), HEAD_DIM)
               == _idiv(lax.broadcasted_iota(jnp.int32, (w4, w4), 1), HEAD_DIM)).astype(BF16)
    blk = (_idiv(lax.broadcasted_iota(jnp.int32, (nr, w4), 0), cs)
           == _idiv(lax.broadcasted_iota(jnp.int32, (nr, w4), 1), HEAD_DIM))
    r2 = lax.broadcasted_iota(jnp.int32, (nr, nr), 0)
    c2 = lax.broadcasted_iota(jnp.int32, (nr, nr), 1)
    same = _idiv(r2, cs) == _idiv(c2, cs)

    def segsum(x):
        return _dot_split_l(x, ones_bd, terms=2)

    par = par_ref[:, sl]
    w0, a0, k_k, k_a, r_k, ln_g, ln_b = (par[i:i + 1] for i in range(7))

    r = r_ref[0, :, sl].astype(F32)
    k = k_ref[0, :, sl].astype(F32)
    v = v_ref[0, :, sl].astype(F32)
    g = g_ref[0, :, sl].astype(F32)
    w_raw = w0 + _dot(jnp.tanh(lw_ref[0]).astype(BF16), bw_ref[0, :, sl])
    a = jax.nn.sigmoid(a0 + _dot(la_ref[0].astype(BF16), ba_ref[0, :, sl]))

    ti = lax.broadcasted_iota(jnp.int32, (cs, 1), 0)
    pos = c * cs + ti
    valid = (pos >= lo) & (pos < hi)

    lw = -jnp.exp(-_softplus(-w_raw) - 0.5)
    kk = k * k_k
    kk = kk / jnp.maximum(jnp.sqrt(segsum(kk * kk)), 1e-12)
    yield
    kp = k * (1.0 + (a - 1.0) * k_a)
    lw = jnp.where(valid, lw, 0.0)
    kk = jnp.where(valid, kk, 0.0)
    kp = jnp.where(valid, kp, 0.0)
    vm = jnp.where(valid, v, 0.0)

    tri = (lax.broadcasted_iota(jnp.int32, (cs, cs), 0) >= lax.broadcasted_iota(jnp.int32, (cs, cs), 1)).astype(BF16)
    cw = _dot_split_r(tri, lw)
    yield
    cwl = cw[cs - 1:cs]
    wt = jnp.exp(cw)
    wi = jnp.exp(-cw)
    wend = jnp.exp(cwl - cw)
    b = kk * a
    at = -kk * jnp.exp(cw - lw)
    rt = r * wt

    def bd(x):
        return jnp.where(blk, jnp.concatenate([x] * RWKV_HPB, axis=0), 0.0).astype(BF16)

    lhs = jnp.concatenate([bd(at), bd(rt)], axis=0)
    rhs = jnp.concatenate([bd(b * wi), bd(kp * wi)], axis=0)
    sc = _nt(lhs, rhs)
    yield
    tt = _imod(r2, cs)
    jj = _imod(c2, cs)
    strict = same & (tt > jj)
    incl = same & (tt >= jj)
    mab = jnp.where(strict, sc[:nr, :nr], 0.0)
    mak = jnp.where(strict, sc[:nr, nr:], 0.0)
    nrb = jnp.where(incl, sc[nr:, :nr], 0.0)
    nrk = jnp.where(incl, sc[nr:, nr:], 0.0)

    tinv = (r2 == c2).astype(F32) + jnp.where(
        (_idiv(r2, 2) == _idiv(c2, 2)) & (_imod(tt, 2) == 1) & (_imod(jj, 2) == 0), mab, 0.0)
    msz = 2
    while msz < cs:
        off = ((_idiv(r2, 2 * msz) == _idiv(c2, 2 * msz)) & (_imod(tt, 2 * msz) >= msz)
               & (_imod(jj, 2 * msz) < msz))
        tb = tinv.astype(BF16)
        tno = _dot(tb, jnp.where(off, mab, 0.0).astype(BF16)).astype(BF16)
        yield
        tinv = tinv + _dot(tno, tb)
        yield
        msz *= 2

    s_old = st[gi]
    x = _nt(lhs, s_old.astype(BF16))
    yield
    vbd = bd(vm)
    u = _dot(tinv.astype(BF16), (x[:nr] + _dot(mak.astype(BF16), vbd)).astype(BF16))
    yield
    ub = u.astype(BF16)
    yb = x[nr:] + _dot(nrb.astype(BF16), ub) + _dot(nrk.astype(BF16), vbd)
    yield
    s_new = s_old * wt[cs - 1:cs] + _tn(jnp.concatenate([ub, vbd], axis=0),
                                        jnp.concatenate([bd(b * wend), bd(kp * wend)], axis=0))
    st[gi] = s_new
    yield

    y = sum(yb[i * cs:(i + 1) * cs] for i in range(RWKV_HPB))
    inv = 1.0 / HEAD_DIM
    yc = y - segsum(y) * inv
    yield
    y = yc * lax.rsqrt(segsum(yc * yc) * inv + RWKV_LN_EPS) * ln_g + ln_b
    y = (y + segsum(r * kp * r_k) * v) * _silu(g)
    y_ref[:, sl] = jnp.where(valid, y, 0.0).astype(BF16)
    yield


def rwkv_scan(rkvg, lora1, lora_b, par, s0, *, bsz, tp, lo, hi):
    _, m, e = rkvg.shape
    cs = RWKV_CHUNK
    nch = tp // cs
    heads = e // HEAD_DIM
    ngrp = RWKV_GROUPS_PER_STEP
    hb = RWKV_HPB * ngrp
    wb = hb * HEAD_DIM
    row = lambda b, h, c: b * nch + c
    kern = functools.partial(_rwkv_kernel, lo=lo, hi=hi, nchunks=nch, ngrp=ngrp)
    proj_spec = lambda s: pl.BlockSpec((1, cs, wb), lambda b, h, c: (s, row(b, h, c), h))
    return pl.pallas_call(
        kern,
        grid=(bsz, heads // hb, nch),
        in_specs=[
            proj_spec(0), proj_spec(1), proj_spec(2), proj_spec(3),
            pl.BlockSpec((1, cs, RWKV_LORA_PAD), lambda b, h, c: (0, row(b, h, c), 0)),
            pl.BlockSpec((1, cs, RWKV_LORA_PAD), lambda b, h, c: (1, row(b, h, c), 0)),
            pl.BlockSpec((1, RWKV_LORA_PAD, wb), lambda b, h, c: (0, 0, h)),
            pl.BlockSpec((1, RWKV_LORA_PAD, wb), lambda b, h, c: (1, 0, h)),
            pl.BlockSpec((8, wb), lambda b, h, c: (0, h)),
            pl.BlockSpec((1, hb, HEAD_DIM, HEAD_DIM), lambda b, h, c: (b, h, 0, 0)),
        ],
        out_specs=[
            pl.BlockSpec((cs, wb), lambda b, h, c: (row(b, h, c), h)),
            pl.BlockSpec((1, hb, HEAD_DIM, HEAD_DIM), lambda b, h, c: (b, h, 0, 0)),
        ],
        out_shape=[
            jax.ShapeDtypeStruct((m, e), BF16),
            jax.ShapeDtypeStruct((bsz, heads, HEAD_DIM, HEAD_DIM), F32),
        ],
        scratch_shapes=[pltpu.VMEM((ngrp, RWKV_HPB * HEAD_DIM, RWKV_HPB * HEAD_DIM), F32)],
        compiler_params=_cparams(("parallel", "parallel", "arbitrary")),
        name="rwkv_scan",
    )(rkvg, rkvg, rkvg, rkvg, lora1, lora1, lora_b, lora_b, par, s0)


def _rwkv_short_kernel(r_ref, k_ref, v_ref, g_ref, lw_ref, la_ref, bw_ref, ba_ref, par_ref, s0_ref,
                       ones_ref, tile_ref, y_ref, sout_ref, *, nvalid, rows_out):
    ct = 8
    hp = RWKV_SHORT_HEADS
    wl = hp * HEAD_DIM
    nr = hp * ct
    ones_bd = ones_ref[...]
    tile_t = tile_ref[...]

    def segsum(x):
        return jnp.concatenate(
            [_dot_split_l(x[:, 256 * j:256 * (j + 1)], ones_bd, terms=2) for j in range(wl // 256)], axis=1)

    par = par_ref[...]
    w0, a0, k_k, k_a, r_k, ln_g, ln_b = (par[i:i + 1] for i in range(7))
    r, k, v, g = r_ref[0], k_ref[0], v_ref[0], g_ref[0]
    w_raw = w0 + _dot(jnp.tanh(lw_ref[0]).astype(BF16), bw_ref[0])
    a = jax.nn.sigmoid(a0 + _dot(la_ref[0].astype(BF16), ba_ref[0]))

    ti = lax.broadcasted_iota(jnp.int32, (ct, 1), 0)
    valid = ti < nvalid
    lw = jnp.where(valid, -jnp.exp(-_softplus(-w_raw) - 0.5), 0.0)
    kk = k * k_k
    kk = jnp.where(valid, kk / jnp.maximum(jnp.sqrt(segsum(kk * kk)), 1e-12), 0.0)
    kp = jnp.where(valid, k * (1.0 + (a - 1.0) * k_a), 0.0)
    vm = jnp.where(valid, v, 0.0)

    cw = lw
    for s in (1, 2, 4):
        cw = cw + jnp.where(ti >= s, pltpu.roll(cw, s, 0), 0.0)
    cwl = cw[ct - 1:ct]
    wend = jnp.exp(cwl - cw)
    wi = jnp.exp(-cw)
    b = kk * a
    at = -kk * jnp.exp(cw - lw)
    rt = r * jnp.exp(cw)

    rr = lax.broadcasted_iota(jnp.int32, (nr, wl), 0)
    cc = lax.broadcasted_iota(jnp.int32, (nr, wl), 1)
    blk = _idiv(rr, ct) == _idiv(cc, HEAD_DIM)

    def bd(x):
        return jnp.where(blk, jnp.concatenate([x] * hp, axis=0), 0.0).astype(BF16)

    lhs = jnp.concatenate([bd(at), bd(rt)], axis=0)
    rhs = jnp.concatenate([bd(b * wi), bd(kp * wi)], axis=0)
    sc = _nt(lhs, rhs)
    ri = lax.broadcasted_iota(jnp.int32, (2 * nr, 2 * nr), 0)
    ci = lax.broadcasted_iota(jnp.int32, (2 * nr, 2 * nr), 1)
    same = _idiv(_imod(ri, nr), ct) == _idiv(_imod(ci, nr), ct)
    tt = _imod(ri, ct)
    jj = _imod(ci, ct)
    sc = jnp.where(same & (tt + (ri >= nr).astype(jnp.int32) > jj), sc, 0.0)
    mab, mak, nrb, nrk = sc[:nr, :nr], sc[:nr, nr:], sc[nr:, :nr], sc[nr:, nr:]

    r1 = lax.broadcasted_iota(jnp.int32, (nr, nr), 0)
    c1 = lax.broadcasted_iota(jnp.int32, (nr, nr), 1)
    t1 = _imod(r1, ct)
    j1 = _imod(c1, ct)
    tinv = (r1 == c1).astype(F32) + jnp.where(
        (_idiv(r1, 2) == _idiv(c1, 2)) & (_imod(t1, 2) == 1) & (_imod(j1, 2) == 0), mab, 0.0)
    msz = 2
    while msz < nvalid:
        off = ((_idiv(r1, 2 * msz) == _idiv(c1, 2 * msz)) & (_imod(t1, 2 * msz) >= msz)
               & (_imod(j1, 2 * msz) < msz))
        tb = tinv.astype(BF16)
        tinv = tinv + _dot(_dot(tb, jnp.where(off, mab, 0.0).astype(BF16)).astype(BF16), tb)
        msz *= 2

    s_old = s0_ref[0].reshape(wl, HEAD_DIM)
    lhs_rows = _dot(lhs, tile_t).astype(BF16)
    x = _nt(lhs_rows, s_old.astype(BF16))
    xa = jnp.where(blk, x[:nr], 0.0)
    xr = jnp.where(blk, x[nr:], 0.0)
    vbd = bd(vm)
    u = _dot(tinv.astype(BF16), (xa + _dot(mak.astype(BF16), vbd)).astype(BF16))
    uv = jnp.concatenate([u.astype(BF16), vbd], axis=0)
    yb = xr + _dot(jnp.concatenate([nrb, nrk], axis=1).astype(BF16), uv)
    bk_rows = _dot(jnp.concatenate([bd(b * wend), bd(kp * wend)], axis=0), tile_t).astype(BF16)
    ds = _tn(uv, bk_rows)

    r16 = lax.broadcasted_iota(jnp.int32, (hp, wl), 0)
    c16 = lax.broadcasted_iota(jnp.int32, (hp, wl), 1)
    wc = jnp.exp(cwl)
    wc_rows = _dot_split_l(jnp.where(r16 == _idiv(c16, HEAD_DIM), wc, 0.0), tile_t)
    rsel = lax.broadcasted_iota(jnp.int32, (hp, HEAD_DIM), 0)
    for h in range(hp):
        rs = slice(h * HEAD_DIM, (h + 1) * HEAD_DIM)
        wc_h = jnp.sum(jnp.where(rsel == h, wc_rows, 0.0), axis=0, keepdims=True)
        sout_ref[0, h] = s_old[rs] * wc_h + ds[rs]

    y = yb[0:ct]
    for h in range(1, hp):
        y = y + yb[h * ct:(h + 1) * ct]
    inv = 1.0 / HEAD_DIM
    yc = y - segsum(y) * inv
    y = yc * lax.rsqrt(segsum(yc * yc) * inv + RWKV_LN_EPS) * ln_g + ln_b
    y = (y + segsum(r * kp * r_k) * v) * _silu(g)
    y = jnp.where(valid, y, 0.0)
    y_ref[...] = _pad_rows(y, rows_out).astype(y_ref.dtype)


def rwkv_short(rkvg, lora1, lora_b, par, s0, *, bsz, tp, hi):
    _, m, e = rkvg.shape
    heads = e // HEAD_DIM
    hp = RWKV_SHORT_HEADS
    wl = hp * HEAD_DIM
    rb = tp // 8
    w4 = RWKV_HPB * HEAD_DIM
    ones_bd = (jnp.arange(w4)[:, None] // HEAD_DIM == jnp.arange(w4)[None, :] // HEAD_DIM).astype(BF16)
    tile_t = (jnp.arange(wl)[:, None] % HEAD_DIM == jnp.arange(HEAD_DIM)[None, :]).astype(BF16)
    kern = functools.partial(_rwkv_short_kernel, nvalid=hi, rows_out=tp)
    proj_spec = lambda s: pl.BlockSpec((1, 8, wl), lambda b, h: (s, b * rb, h))
    return pl.pallas_call(
        kern,
        grid=(bsz, heads // hp),
        in_specs=[
            proj_spec(0), proj_spec(1), proj_spec(2), proj_spec(3),
            pl.BlockSpec((1, 8, RWKV_LORA_PAD), lambda b, h: (0, b * rb, 0)),
            pl.BlockSpec((1, 8, RWKV_LORA_PAD), lambda b, h: (1, b * rb, 0)),
            pl.BlockSpec((1, RWKV_LORA_PAD, wl), lambda b, h: (0, 0, h)),
            pl.BlockSpec((1, RWKV_LORA_PAD, wl), lambda b, h: (1, 0, h)),
            pl.BlockSpec((8, wl), lambda b, h: (0, h)),
            pl.BlockSpec((1, hp, HEAD_DIM, HEAD_DIM), lambda b, h: (b, h, 0, 0)),
            pl.BlockSpec((w4, w4), lambda b, h: (0, 0)),
            pl.BlockSpec((wl, HEAD_DIM), lambda b, h: (0, 0)),
        ],
        out_specs=[
            pl.BlockSpec((tp, wl), lambda b, h: (b, h)),
            pl.BlockSpec((1, hp, HEAD_DIM, HEAD_DIM), lambda b, h: (b, h, 0, 0)),
        ],
        out_shape=[
            jax.ShapeDtypeStruct((m, e), rkvg.dtype),
            jax.ShapeDtypeStruct((bsz, heads, HEAD_DIM, HEAD_DIM), F32),
        ],
        compiler_params=_cparams(("parallel", "parallel")),
        name="rwkv_short",
    )(rkvg, rkvg, rkvg, rkvg, lora1, lora1, lora_b, lora_b, par, s0, ones_bd, tile_t)


def _rwkv_lanes_kernel(r_ref, k_ref, v_ref, g_ref, lw_ref, la_ref, bw_ref, ba_ref, par_ref, s0_ref,
                       y_ref, sout_ref, tk, tw, tb, tq, tr, tv, ty, *, nt, nb):
    hp = RWKV_LANE_HEADS
    wl = hp * HEAD_DIM
    ri = lax.broadcasted_iota(jnp.int32, (wl, wl), 0)
    ci = lax.broadcasted_iota(jnp.int32, (wl, wl), 1)
    ones_bd = (_idiv(ri, HEAD_DIM) == _idiv(ci, HEAD_DIM)).astype(BF16)

    def segsum(x):
        return _dot_split_l(x, ones_bd, terms=2)

    par = par_ref[...]
    w0, a0, k_k, k_a, r_k, ln_g, ln_b = (par[i:i + 1] for i in range(7))
    r, k, v, g = (x[0].astype(F32) for x in (r_ref, k_ref, v_ref, g_ref))
    w_raw = w0 + _dot(jnp.tanh(lw_ref[0]).astype(BF16), bw_ref[0])
    a = jax.nn.sigmoid(a0 + _dot(la_ref[0].astype(BF16), ba_ref[0]))
    decay = jnp.exp(-jnp.exp(-_softplus(-w_raw) - 0.5))
    kk = k * k_k
    kk = kk / jnp.maximum(jnp.sqrt(segsum(kk * kk)), 1e-12)
    kp = k * (1.0 + (a - 1.0) * k_a)
    bb = kk * a

    for t in range(nt):
        rows = slice(t * nb, (t + 1) * nb)
        tk[t] = (-kk[rows]).T
        tw[t] = decay[rows].T
        tb[t] = bb[rows].T
        tq[t] = kp[rows].T
        tr[t] = r[rows].T
        tv[t] = v[rows].T

    for hh in range(hp):
        ks = slice(hh * HEAD_DIM, (hh + 1) * HEAD_DIM)

        def body(vi, carry, hh=hh, ks=ks):
            sv = s0_ref[hh, vi]
            row = hh * HEAD_DIM + vi
            for t in range(nt):
                sa = jnp.sum(sv * tk[t, ks, :], axis=0, keepdims=True)
                sv = sv * tw[t, ks, :] + sa * tb[t, ks, :] + tv[t, pl.ds(row, 1), :] * tq[t, ks, :]
                ty[t, pl.ds(row, 1), :] = jnp.sum(sv * tr[t, ks, :], axis=0, keepdims=True)
            sout_ref[hh, vi] = sv
            return carry

        lax.fori_loop(0, HEAD_DIM, body, 0, unroll=4)

    inv = 1.0 / HEAD_DIM
    for t in range(nt):
        rows = slice(t * nb, (t + 1) * nb)
        y = ty[t].T
        yc = y - segsum(y) * inv
        y = yc * lax.rsqrt(segsum(yc * yc) * inv + RWKV_LN_EPS) * ln_g + ln_b
        y = (y + segsum(r[rows] * kp[rows] * r_k) * v[rows]) * _silu(g[rows])
        y_ref[rows, :] = y.astype(BF16)


def rwkv_lanes(rkvg, lora1, lora_b, par, s0t, *, nt, nb):
    _, m, e = rkvg.shape
    heads = e // HEAD_DIM
    hp = RWKV_LANE_HEADS
    wl = hp * HEAD_DIM
    kern = functools.partial(_rwkv_lanes_kernel, nt=nt, nb=nb)
    proj_spec = lambda s: pl.BlockSpec((1, m, wl), lambda h: (s, 0, h))
    state_spec = pl.BlockSpec((hp, HEAD_DIM, HEAD_DIM, nb), lambda h: (h, 0, 0, 0))
    tile = pltpu.VMEM((nt, wl, nb), F32)
    return pl.pallas_call(
        kern,
        grid=(heads // hp,),
        in_specs=[
            proj_spec(0), proj_spec(1), proj_spec(2), proj_spec(3),
            pl.BlockSpec((1, m, RWKV_LORA_PAD), lambda h: (0, 0, 0)),
            pl.BlockSpec((1, m, RWKV_LORA_PAD), lambda h: (1, 0, 0)),
            pl.BlockSpec((1, RWKV_LORA_PAD, wl), lambda h: (0, 0, h)),
            pl.BlockSpec((1, RWKV_LORA_PAD, wl), lambda h: (1, 0, h)),
            pl.BlockSpec((8, wl), lambda h: (0, h)),
            state_spec,
        ],
        out_specs=[pl.BlockSpec((m, wl), lambda h: (0, h)), state_spec],
        out_shape=[jax.ShapeDtypeStruct((m, e), BF16), jax.ShapeDtypeStruct(s0t.shape, F32)],
        scratch_shapes=[tile] * 7,
        compiler_params=_cparams(("parallel",)),
        name="rwkv_lanes",
    )(rkvg, rkvg, rkvg, rkvg, lora1, lora1, lora_b, lora_b, par, s0t)


def _prep_weights(norm_g, ssd_w_in, ssd_dt_bias, ssd_a_log, ssd_d, ssd_w_out,
                  rwkv_w_rkvg, rwkv_w_lora_a, rwkv_w_lora_b, rwkv_a_lora_a, rwkv_a_lora_b,
                  rwkv_w0, rwkv_a0, rwkv_k_k, rwkv_k_a, rwkv_r_k, rwkv_ln_g, rwkv_ln_b, rwkv_w_out,
                  ret_w_in, ret_w_out):
    d_inner = ssd_w_out.shape[1]
    n_main = d_inner + d_inner + 2 * SSD_GROUPS * SSD_STATE
    ns = ssd_w_in.shape[0]
    d_model = ssd_w_in.shape[1]

    def head_lanes(p):
        p = p.reshape(ns, SSD_GROUPS, 1, SSD_HPG)
        return jnp.pad(p, ((0, 0), (0, 0), (0, 0), (0, 128 - SSD_HPG)))

    w_dt = ssd_w_in[:, :, n_main:].reshape(ns, d_model, SSD_GROUPS, SSD_HPG)
    w_dt = jnp.pad(w_dt, ((0, 0), (0, 0), (0, 0), (0, 128 - SSD_HPG))).reshape(ns, d_model, SSD_GROUPS * 128)
    rank = rwkv_w_lora_a.shape[2]
    lora_a = jnp.stack([rwkv_w_lora_a, rwkv_a_lora_a], axis=1)
    lora_a = jnp.pad(lora_a, ((0, 0), (0, 0), (0, 0), (0, RWKV_LORA_PAD - rank)))
    lora_b = jnp.stack([rwkv_w_lora_b, rwkv_a_lora_b], axis=1)
    lora_b = jnp.pad(lora_b, ((0, 0), (0, 0), (0, RWKV_LORA_PAD - rank), (0, 0)))
    nr = rwkv_w0.shape[0]
    par = jnp.stack([rwkv_w0, rwkv_a0, rwkv_k_k, rwkv_k_a, rwkv_r_k.reshape(nr, -1), rwkv_ln_g, rwkv_ln_b,
                     jnp.zeros_like(rwkv_w0)], axis=1)
    return dict(
        ssd_w_in=ssd_w_in.astype(BF16), ssd_n_main=n_main, ssd_w_dt=w_dt.astype(BF16),
        ssd_dtb=head_lanes(ssd_dt_bias), ssd_alog=head_lanes(ssd_a_log), ssd_dskip=head_lanes(ssd_d),
        ssd_w_out=ssd_w_out.astype(BF16),
        rwkv_w=rwkv_w_rkvg.astype(BF16), rwkv_lora_a=lora_a.astype(BF16), rwkv_lora_b=lora_b.astype(BF16),
        rwkv_par=par, rwkv_w_out=rwkv_w_out.astype(BF16),
        ret_w_in=ret_w_in.astype(BF16), ret_w_out=ret_w_out.astype(BF16),
    )


def _trunk(h, conv_st, ssd_st, shift_st, wkv_st, ret_st, pos, *, bsz, tp, lq, lo, hi, depth,
           norm_g, final_norm_g, wts, ssd_conv_w, ssd_conv_b, ssd_norm_g, rwkv_mu, ret_norm_g):
    d_model = h.shape[1]
    geo = dict(bsz=bsz, tp=tp, lo=lo, hi=hi)
    new_conv, new_shift, new_wkv, new_ret = [], [], [], []
    new_ssd = None
    act = BF16 if lq % 16 == 0 else F32

    half = RET_QK // 2
    inv_freq = 1.0 / (RET_THETA_BASE ** jnp.linspace(0.0, 1.0, half, dtype=F32))
    ang = pos.astype(F32)[:, None] * inv_freq
    cos, sin = jnp.cos(ang), jnp.sin(ang)
    log_gamma = jnp.log1p(-jnp.exp2(-5.0 - jnp.arange(RET_HEADS, dtype=F32)))
    lg = jnp.broadcast_to(log_gamma[:, None, None], (RET_HEADS, 8, 128))

    for layer in range(depth):
        kind, j = layer % 3, layer // 3
        ng = norm_g[layer]
        if kind == 0:
            n_main = wts["ssd_n_main"]
            proj = matmul(h, wts["ssd_w_in"], j, out_dtype=act, norm_g=ng, n_cols=n_main)
            dt_raw = matmul(h, wts["ssd_w_dt"], j, norm_g=ng)
            conv_init = jnp.pad(conv_st[j], ((0, 0), (8 - (SSD_CONV - 1), 0), (0, 0)))
            y, new_ssd = ssd_scan(proj, dt_raw, conv_init, ssd_st, j, new_ssd, ssd_conv_w[j], ssd_conv_b[j],
                                  wts["ssd_dtb"][j], wts["ssd_alog"][j], wts["ssd_dskip"][j], ssd_norm_g[j],
                                  lq=lq, **geo)
            nk = SSD_CONV - 1
            last = h.reshape(bsz, tp, d_model)[:, hi - nk:hi].reshape(bsz * nk, d_model)
            last = jnp.pad(last, ((0, -(bsz * nk) % 16), (0, 0)))
            xbc = matmul(last, wts["ssd_w_in"], j, norm_g=ng, n_cols=n_main)[:bsz * nk, y.shape[1]:]
            new_conv.append(xbc.reshape(bsz, nk, -1))
            h = matmul(y, wts["ssd_w_out"], j, res=h)
        elif kind == 1 and lo == 0 and hi <= 8 and bsz % LANES == 0:
            u = rmsnorm(h, ng)
            tmajor = lambda x: jnp.swapaxes(x.reshape(bsz, tp, -1)[:, :hi], 0, 1)
            uc = tmajor(u)
            prev = jnp.concatenate([shift_st[j][None], uc[:-1]], axis=0).reshape(hi * bsz, d_model)
            uc = uc.reshape(hi * bsz, d_model)
            rkvg = mix_matmul(uc, prev, rwkv_mu[j][:4], wts["rwkv_w"][j], out_dtype=BF16)
            lora1 = mix_matmul(uc, prev, rwkv_mu[j][4:], wts["rwkv_lora_a"][j])
            y, s_t = rwkv_lanes(rkvg, lora1, wts["rwkv_lora_b"][j], wts["rwkv_par"][j],
                                jnp.transpose(wkv_st[j], (1, 2, 3, 0)), nt=hi, nb=bsz)
            new_shift.append(u.reshape(bsz, tp, d_model)[:, hi - 1])
            new_wkv.append(jnp.transpose(s_t, (3, 0, 1, 2)))
            hc = matmul(y, wts["rwkv_w_out"], j, res=tmajor(h).reshape(hi * bsz, d_model))
            hc = jnp.swapaxes(hc.reshape(hi, bsz, d_model), 0, 1)
            h = jnp.pad(hc, ((0, 0), (0, tp - hi), (0, 0))).reshape(bsz * tp, d_model)
        elif kind == 1:
            u = rmsnorm(h, ng)
            u3 = u.reshape(bsz, tp, d_model)
            if lo > 0:
                prev = None
            else:
                prev = jnp.concatenate([shift_st[j][:, None, :], u3[:, :-1]], axis=1).reshape(bsz * tp, d_model)
            rkvg = mix_matmul(u, prev, rwkv_mu[j][:4], wts["rwkv_w"][j], out_dtype=BF16)
            lora1 = mix_matmul(u, prev, rwkv_mu[j][4:], wts["rwkv_lora_a"][j])
            if lo == 0 and hi <= 8 and tp <= 16:
                y, s_new = rwkv_short(rkvg.astype(F32), lora1, wts["rwkv_lora_b"][j], wts["rwkv_par"][j], wkv_st[j],
                                      bsz=bsz, tp=tp, hi=hi)
            else:
                y, s_new = rwkv_scan(rkvg, lora1, wts["rwkv_lora_b"][j], wts["rwkv_par"][j], wkv_st[j], **geo)
            new_shift.append(u3[:, hi - 1])
            new_wkv.append(s_new)
            h = matmul(y, wts["rwkv_w_out"], j, res=h)
        else:
            proj = matmul(h, wts["ret_w_in"], j, out_dtype=act, norm_g=ng)
            y, s_new = ret_scan(proj, cos, sin, lg, ret_st[j], ret_norm_g[j], lq=lq, **geo)
            new_ret.append(s_new)
            h = matmul(y, wts["ret_w_out"], j, res=h)
    y = rmsnorm(h, final_norm_g)
    return (y, jnp.stack(new_conv), new_ssd, jnp.stack(new_shift), jnp.stack(new_wkv), jnp.stack(new_ret))


def kernel(x_prompt, x_sample, state_ssd_conv, state_ssd, state_rwkv_shift, state_rwkv_wkv, state_ret, meta_tokens, norm_g, final_norm_g, ssd_w_in, ssd_conv_w, ssd_conv_b, ssd_dt_bias, ssd_a_log, ssd_d, ssd_norm_g, ssd_w_out, rwkv_mu, rwkv_w_rkvg, rwkv_w0, rwkv_w_lora_a, rwkv_w_lora_b, rwkv_a0, rwkv_a_lora_a, rwkv_a_lora_b, rwkv_k_k, rwkv_k_a, rwkv_r_k, rwkv_ln_g, rwkv_ln_b, rwkv_w_out, ret_w_in, ret_norm_g, ret_w_out):
    depth = norm_g.shape[0]
    d_model = x_prompt.shape[2]
    wts = _prep_weights(norm_g, ssd_w_in, ssd_dt_bias, ssd_a_log, ssd_d, ssd_w_out,
                        rwkv_w_rkvg, rwkv_w_lora_a, rwkv_w_lora_b, rwkv_a_lora_a, rwkv_a_lora_b,
                        rwkv_w0, rwkv_a0, rwkv_k_k, rwkv_k_a, rwkv_r_k, rwkv_ln_g, rwkv_ln_b, rwkv_w_out,
                        ret_w_in, ret_w_out)
    common = dict(depth=depth, norm_g=norm_g, final_norm_g=final_norm_g, wts=wts, ssd_conv_w=ssd_conv_w,
                  ssd_conv_b=ssd_conv_b, ssd_norm_g=ssd_norm_g, rwkv_mu=rwkv_mu, ret_norm_g=ret_norm_g)

    bp, seq, _ = x_prompt.shape
    lq_p = 128
    lo_p = lq_p - N_META
    tp_p = lo_p + N_META + seq
    h_p = jnp.concatenate([jnp.zeros((bp, lo_p, d_model), F32),
                           jnp.broadcast_to(meta_tokens[None], (bp, N_META, d_model)), x_prompt], axis=1)
    zeros_like_b = lambda s: jnp.zeros((s.shape[0], bp) + s.shape[2:], F32)
    pos_p = jnp.maximum(jnp.arange(tp_p) - lo_p, 0)
    outs_p = _trunk(h_p.reshape(bp * tp_p, d_model), zeros_like_b(state_ssd_conv), zeros_like_b(state_ssd),
                    zeros_like_b(state_rwkv_shift), zeros_like_b(state_rwkv_wkv), zeros_like_b(state_ret), pos_p,
                    bsz=bp, tp=tp_p, lq=lq_p, lo=lo_p, hi=tp_p, **common)
    y_prompt = outs_p[0].reshape(bp, tp_p, d_model)[:, lo_p + N_META:]

    bs, ds, _ = x_sample.shape
    tp_s = 8
    h_s = jnp.concatenate([x_sample, jnp.zeros((bs, tp_s - ds, d_model), F32)], axis=1)
    pos_s = PAST_LEN + jnp.arange(tp_s)
    outs_s = _trunk(h_s.reshape(bs * tp_s, d_model), state_ssd_conv, state_ssd, state_rwkv_shift, state_rwkv_wkv,
                    state_ret, pos_s, bsz=bs, tp=tp_s, lq=tp_s, lo=0, hi=ds, **common)
    y_sample = outs_s[0].reshape(bs, tp_s, d_model)[:, :ds]

    return (y_prompt, y_sample) + tuple(outs_p[1:]) + tuple(outs_s[1:])
```
